```python
import math
import jax
import jax.numpy as jnp
from jax import lax
import numpy as np

D_MODEL = 1024
BATCH = 8
SEQ = 2048
DEPTH = 2
DEC_BATCH = 32
DEC_SEQ = 1
PAST_LEN = 8192
PAGE_SIZE = 128

HG_HEADS = 4
HG_DK = 128
HG_DV = 128
HG_WIDTH = HG_HEADS * HG_DV
HG_CHUNK = 64
SB_HEADS = 8
SB_HEAD_DIM = 64
SB_WIDTH = SB_HEADS * SB_HEAD_DIM
SB_BLOCK = 128
ML_HEADS = 4
ML_HEAD_DIM = 128
ML_WIDTH = ML_HEADS * ML_HEAD_DIM
ML_CHUNK = 128
ML_CONV = 4
D_FF = 2816
FFN_RES = 0.5
N_SUB = 3
NORM_EPS = 1e-6
IN_COLS = 2 * HG_HEADS * HG_DK + 2 * HG_WIDTH + 3 * SB_WIDTH + 2 * ML_WIDTH + 3 * D_MODEL

kernel_name = 'hybrid_hgrn2_stickbreak_mlstm_step'


def _in_split_points():
    sizes = (HG_HEADS * HG_DK, HG_HEADS * HG_DK, HG_WIDTH, HG_WIDTH,
             SB_WIDTH, SB_WIDTH, SB_WIDTH, ML_WIDTH, ML_WIDTH,
             D_MODEL, D_MODEL, D_MODEL)
    pts, acc = [], 0
    for s in sizes[:-1]:
        acc += s
        pts.append(acc)
    return pts


def _rmsnorm(x, g):
    xf = x.astype(jnp.float32)
    y = xf * lax.rsqrt(jnp.mean(xf * xf, axis=-1, keepdims=True) + NORM_EPS)
    return (y * g.astype(jnp.float32)).astype(x.dtype)


def _head_norm(o, gain):
    B, T, H, d = o.shape
    o = o * lax.rsqrt(jnp.mean(o * o, axis=-1, keepdims=True) + NORM_EPS)
    return o.reshape(B, T, H * d) * gain.astype(jnp.float32)


def _swiglu(h, w_in, w_down):
    gate, up = jnp.split(h @ w_in, 2, axis=-1)
    return (jax.nn.silu(gate) * up) @ w_down


def _to_chunks(a, nc):
    return jnp.moveaxis(a.reshape(a.shape[0], nc, a.shape[1] // nc, *a.shape[2:]), 1, 0)


def _from_chunks(a):
    a = jnp.moveaxis(a, 0, 1)
    return a.reshape(a.shape[0], a.shape[1] * a.shape[2], *a.shape[3:])


def _hgrn_lower_bound(lb_logits, l):
    p = jax.nn.softmax(lb_logits.astype(jnp.float32), axis=0)
    cum = jnp.cumsum(p, axis=0)
    return cum[l] - cum[0]


def _hgrn_chunk(S, inp):
    q, logf, k, i = inp
    L = q.shape[1]
    b = jnp.cumsum(logf, axis=1)
    causal = jnp.tril(jnp.ones((L, L), dtype=bool))
    diff = b[:, :, None] - b[:, None, :]
    decay = jnp.exp(jnp.where(causal[None, :, :, None, None], diff, -jnp.inf))
    scores = jnp.einsum('bthc,btshc,bshc->bhts', q, decay, k)
    o = (jnp.einsum('bthc,bhcv->bthv', q * jnp.exp(b), S)
         + jnp.einsum('bhts,bshv->bthv', scores, i))
    kd = jnp.exp(b[:, -1:] - b) * k
    S_new = jnp.exp(b[:, -1])[..., None] * S + jnp.einsum('bshc,bshv->bhcv', kd, i)
    return S_new, o


def _hgrn2(hq, hf, hi, hg, lb, S0, gain, chunk):
    B, T, _ = hq.shape
    f32 = jnp.float32
    q = hq.astype(f32).reshape(B, T, HG_HEADS, HG_DK)
    z = hf.astype(f32).reshape(B, T, HG_HEADS, HG_DK)
    i = hi.astype(f32).reshape(B, T, HG_HEADS, HG_DV)
    lbh = lb.reshape(HG_HEADS, HG_DK)
    logf = jnp.logaddexp(jnp.log(lbh), jnp.log1p(-lbh) + jax.nn.log_sigmoid(z))
    k = (1.0 - lbh) * jax.nn.sigmoid(-z)
    nc = T // chunk
    S, o = lax.scan(_hgrn_chunk, S0.astype(f32),
                    (_to_chunks(q, nc), _to_chunks(logf, nc), _to_chunks(k, nc), _to_chunks(i, nc)))
    o = _head_norm(_from_chunks(o), gain) * jax.nn.silu(hg.astype(f32))
    return o, S


def _sb_attend(q, k, v, q_pos, k_pos, bias):
    f32 = jnp.float32
    z = (jnp.einsum('bqhd,bkhd->bhqk', q.astype(f32), k.astype(f32)) * (SB_HEAD_DIM ** -0.5)
         + bias.astype(f32)[None, :, None, None])
    mask = k_pos[None, :] < q_pos[:, None]
    log_1m = jnp.where(mask, jax.nn.log_sigmoid(-z), 0.0)
    suffix = lax.cumsum(log_1m, axis=3, reverse=True) - log_1m
    A = jnp.where(mask, jnp.exp(jax.nn.log_sigmoid(z) + suffix), 0.0)
    return jnp.einsum('bhqk,bkhd->bqhd', A, v.astype(f32))


def _sb_prompt(q, k, v, bias):
    B, T, H, d = q.shape
    nb = T // SB_BLOCK
    qb = jnp.moveaxis(q.reshape(B, nb, SB_BLOCK, H, d), 1, 0)
    starts = jnp.arange(nb) * SB_BLOCK
    k_pos = jnp.arange(T)

    def block(args):
        qi, s = args
        return _sb_attend(qi, k, v, s + jnp.arange(SB_BLOCK), k_pos, bias)

    o = lax.map(block, (qb, starts))
    return jnp.moveaxis(o, 0, 1).reshape(B, T, H, d)


def _mlstm_chunk(carry, inp):
    C, n, m = carry
    q, k, v, it, logf = inp
    L = q.shape[1]
    b = jnp.cumsum(logf, axis=1)
    causal = jnp.tril(jnp.ones((L, L), dtype=bool))
    logD = b[:, :, None, :] - b[:, None, :, :] + it[:, None, :, :]
    logD = jnp.where(causal[None, :, :, None], logD, -jnp.inf)
    m_t = jnp.maximum(b + m[:, None, :], jnp.max(logD, axis=2))
    Dm = jnp.exp(logD - m_t[:, :, None, :])
    inter = jnp.exp(b + m[:, None, :] - m_t)
    W = jnp.einsum('bthd,bshd->btsh', q, k) * Dm
    numer = (inter[..., None] * jnp.einsum('bhvd,bthd->bthv', C, q)
             + jnp.einsum('btsh,bshv->bthv', W, v))
    denom = inter * jnp.einsum('bhd,bthd->bth', n, q) + jnp.sum(W, axis=2)
    h = numer / jnp.maximum(jnp.abs(denom), jnp.exp(-m_t))[..., None]
    m_new = m_t[:, -1]
    w_end = jnp.exp(b[:, -1:] - b + it - m_new[:, None])
    dec = jnp.exp(b[:, -1] + m - m_new)
    C_new = dec[..., None, None] * C + jnp.einsum('bsh,bshv,bshd->bhvd', w_end, v, k)
    n_new = dec[..., None] * n + jnp.einsum('bsh,bshd->bhd', w_end, k)
    return (C_new, n_new, m_new), h


def _mlstm(hx, ho, buf0, l, prm, C0, n0, m0, chunk):
    B, T, _ = hx.shape
    f32 = jnp.float32
    x = hx.astype(f32)
    xpad = jnp.concatenate([buf0.astype(f32), x], axis=1)
    w = prm['ml_conv_w'][l].astype(f32)
    xc = jax.nn.silu(sum(xpad[:, j:j + T] * w[j] for j in range(ML_CONV)) + prm['ml_conv_b'][l])
    buf_new = xpad[:, T:]
    xch = xc.reshape(B, T, ML_HEADS, ML_HEAD_DIM)
    xh = x.reshape(B, T, ML_HEADS, ML_HEAD_DIM)
    q = jnp.einsum('bthd,hde->bthe', xch, prm['ml_wq'][l])
    k = jnp.einsum('bthd,hde->bthe', xch, prm['ml_wk'][l])
    v = jnp.einsum('bthd,hde->bthe', xh, prm['ml_wv'][l])
    qkv = jnp.concatenate([q, k, v], axis=2).reshape(B, T, 3 * ML_WIDTH)
    gates = (qkv @ prm['ml_w_if'][l] + prm['ml_b_if'][l]).astype(f32)
    it = gates[..., :ML_HEADS]
    logf = jax.nn.log_sigmoid(gates[..., ML_HEADS:])
    ks = k * (ML_HEAD_DIM ** -0.5)
    nc = T // chunk
    (C, n, m), hh = lax.scan(
        _mlstm_chunk, (C0.astype(f32), n0.astype(f32), m0.astype(f32)),
        (_to_chunks(q, nc), _to_chunks(ks, nc), _to_chunks(v, nc), _to_chunks(it, nc), _to_chunks(logf, nc)))
    o = _head_norm(_from_chunks(hh), prm['ml_gain'][l]) * jax.nn.sigmoid(ho.astype(f32))
    return o, (C, n, m, buf_new)


def _mix(h, l, prm, past):
    B, T, _ = h.shape
    f32 = jnp.float32
    proj = h @ prm['w_in'][l]
    hq, hf, hi, hg, sq, sk, sv, mx, mo, ga, gb, gc = jnp.split(proj, _in_split_points(), axis=-1)
    if past is None:
        S0 = jnp.zeros((B, HG_HEADS, HG_DK, HG_DV), f32)
        C0 = jnp.zeros((B, ML_HEADS, ML_HEAD_DIM, ML_HEAD_DIM), f32)
        n0 = jnp.zeros((B, ML_HEADS, ML_HEAD_DIM), f32)
        m0 = jnp.zeros((B, ML_HEADS), f32)
        buf0 = jnp.zeros((B, ML_CONV - 1, ML_WIDTH), f32)
        hg_chunk, ml_chunk = HG_CHUNK, ML_CHUNK
    else:
        S0 = past['hgrn'][:, l]
        C0 = past['mc'][:, l]
        n0 = past['mn'][:, l]
        m0 = past['mm'][:, l]
        buf0 = past['mconv'][:, l]
        hg_chunk, ml_chunk = T, T
    lb = _hgrn_lower_bound(prm['hg_lb_logits'], l)
    o_a, S_new = _hgrn2(hq, hf, hi, hg, lb, S0, prm['hg_gain'][l], hg_chunk)
    sbb = prm['sb_bias'][l]
    q = sq.reshape(B, T, SB_HEADS, SB_HEAD_DIM)
    k = sk.reshape(B, T, SB_HEADS, SB_HEAD_DIM)
    v = sv.reshape(B, T, SB_HEADS, SB_HEAD_DIM)
    if past is None:
        o_b = _sb_prompt(q, k, v, sbb)
    else:
        pk = past['k'][past['page_table'], l].reshape(B, -1, SB_HEADS, SB_HEAD_DIM)
        pv = past['v'][past['page_table'], l].reshape(B, -1, SB_HEADS, SB_HEAD_DIM)
        P = pk.shape[1]
        o_b = _sb_attend(q, jnp.concatenate([pk.astype(k.dtype), k], axis=1),
                         jnp.concatenate([pv.astype(v.dtype), v], axis=1),
                         P + jnp.arange(T), jnp.arange(P + T), sbb)
    o_b = o_b.reshape(B, T, SB_WIDTH)
    o_c, (C_new, n_new, m_new, buf_new) = _mlstm(mx, mo, buf0, l, prm, C0, n0, m0, ml_chunk)
    merged = (jax.nn.sigmoid(ga.astype(f32)) * (o_a @ prm['w_hg_proj'][l])
              + jax.nn.sigmoid(gb.astype(f32)) * (o_b @ prm['w_sb_proj'][l])
              + jax.nn.sigmoid(gc.astype(f32)) * (o_c @ prm['w_ml_proj'][l]))
    y = (merged @ prm['w_out'][l]).astype(h.dtype)
    st = dict(k=k, v=v, hgrn=S_new, mc=C_new, mn=n_new, mm=m_new, mconv=buf_new)
    return y, st


def _layer(x, c, l, prm, past):
    B, _, D = x.shape
    ada = jax.nn.silu(c) @ prm['w_ada'][l] + prm['b_ada'][l]
    ada = ada.reshape(B, N_SUB, 3, D)[:, :, :, None, :]

    def pre(j, z):
        return _rmsnorm(z, prm['g_pre'][l, j]) * (1 + ada[:, j, 1]) + ada[:, j, 0]

    def post(j, z, y, w):
        return (z + w * (1 + ada[:, j, 2]) * _rmsnorm(y, prm['g_post'][l, j])).astype(z.dtype)

    x = post(0, x, _swiglu(pre(0, x), prm['w_ffn_in'][l, 0], prm['w_ffn_down'][l, 0]), FFN_RES)
    y, st = _mix(pre(1, x), l, prm, past)
    x = post(1, x, y, 1.0)
    x = post(2, x, _swiglu(pre(2, x), prm['w_ffn_in'][l, 1], prm['w_ffn_down'][l, 1]), FFN_RES)
    return x, st


def _run(x, c, prm, past):
    outs = []
    for l in range(DEPTH):
        x, st = _layer(x, c, l, prm, past)
        outs.append(st)
    stacked = {name: jnp.stack([o[name] for o in outs], axis=1).astype(x.dtype) for name in outs[0]}
    return x, stacked


def setup_inputs(seed: int = 0) -> dict:
    key = jax.random.key(seed)
    keys = jax.random.split(key, 48)
    cnt = [0]

    def nk():
        cnt[0] += 1
        return keys[cnt[0] - 1]

    def nrm(shape, scale):
        return jax.random.normal(nk(), shape, jnp.float32) * scale

    n_pages = PAST_LEN // PAGE_SIZE
    n_phys = (DEC_BATCH * n_pages * 5) // 4
    D = D_MODEL
    inp = {}
    inp['x_prompt'] = nrm((BATCH, SEQ, D), 1.0)
    inp['x_sample'] = nrm((DEC_BATCH, DEC_SEQ, D), 1.0)
    inp['cache_sb_k'] = nrm((n_phys, DEPTH, PAGE_SIZE, SB_HEADS, SB_HEAD_DIM), 1.0)
    inp['cache_sb_v'] = nrm((n_phys, DEPTH, PAGE_SIZE, SB_HEADS, SB_HEAD_DIM), 1.0)
    inp['state_hgrn'] = nrm((DEC_BATCH, DEPTH, HG_HEADS, HG_DK, HG_DV), 1.0)
    inp['state_mlstm_c'] = nrm((DEC_BATCH, DEPTH, ML_HEADS, ML_HEAD_DIM, ML_HEAD_DIM), 0.3)
    inp['state_mlstm_n'] = nrm((DEC_BATCH, DEPTH, ML_HEADS, ML_HEAD_DIM), 0.3)
    inp['state_mlstm_m'] = nrm((DEC_BATCH, DEPTH, ML_HEADS), 1.0)
    inp['state_mlstm_conv'] = nrm((DEC_BATCH, DEPTH, ML_CONV - 1, ML_WIDTH), 1.0)
    perm = jax.random.permutation(nk(), n_phys)
    inp['page_table'] = perm[:DEC_BATCH * n_pages].reshape(DEC_BATCH, n_pages).astype(jnp.int32)
    inp['c_prompt'] = nrm((BATCH, D), 1.0)
    inp['c_sample'] = nrm((DEC_BATCH, D), 1.0)
    inp['w_ada'] = nrm((DEPTH, D, N_SUB * 3 * D), 0.2 * D ** -0.5)
    inp['b_ada'] = nrm((DEPTH, N_SUB * 3 * D), 0.01)
    inp['g_pre'] = 1.0 + nrm((DEPTH, N_SUB, D), 0.05)
    inp['g_post'] = 1.0 + nrm((DEPTH, N_SUB, D), 0.05)
    inp['w_ffn_in'] = nrm((DEPTH, 2, D, 2 * D_FF), D ** -0.5)
    inp['w_ffn_down'] = nrm((DEPTH, 2, D_FF, D), D_FF ** -0.5)
    inp['w_in'] = nrm((DEPTH, D, IN_COLS), D ** -0.5)
    inp['sb_bias'] = -jnp.linspace(4.0, 9.0, SB_HEADS)[None, :] + nrm((DEPTH, SB_HEADS), 0.1)
    inp['hg_lb_logits'] = nrm((DEPTH, HG_HEADS * HG_DK), 1.0)
    inp['hg_gain'] = 1.0 + nrm((DEPTH, HG_WIDTH), 0.05)
    inp['ml_conv_w'] = nrm((DEPTH, ML_CONV, ML_WIDTH), ML_CONV ** -0.5)
    inp['ml_conv_b'] = nrm((DEPTH, ML_WIDTH), 0.01)
    inp['ml_wq'] = nrm((DEPTH, ML_HEADS, ML_HEAD_DIM, ML_HEAD_DIM), ML_HEAD_DIM ** -0.5)
    inp['ml_wk'] = nrm((DEPTH, ML_HEADS, ML_HEAD_DIM, ML_HEAD_DIM), ML_HEAD_DIM ** -0.5)
    inp['ml_wv'] = nrm((DEPTH, ML_HEADS, ML_HEAD_DIM, ML_HEAD_DIM), ML_HEAD_DIM ** -0.5)
    inp['ml_w_if'] = nrm((DEPTH, 3 * ML_WIDTH, 2 * ML_HEADS), 0.1 * (3 * ML_WIDTH) ** -0.5)
    b_i = nrm((DEPTH, ML_HEADS), 0.1)
    b_f = jnp.linspace(3.0, 6.0, ML_HEADS)[None, :] + nrm((DEPTH, ML_HEADS), 0.1)
    inp['ml_b_if'] = jnp.concatenate([b_i, b_f], axis=-1)
    inp['ml_gain'] = 1.0 + nrm((DEPTH, ML_WIDTH), 0.05)
    inp['w_hg_proj'] = nrm((DEPTH, HG_WIDTH, D), HG_WIDTH ** -0.5)
    inp['w_sb_proj'] = nrm((DEPTH, SB_WIDTH, D), SB_WIDTH ** -0.5)
    inp['w_ml_proj'] = nrm((DEPTH, ML_WIDTH, D), ML_WIDTH ** -0.5)
    inp['w_out'] = nrm((DEPTH, D, D), D ** -0.5)
    return inp


def reference(x_prompt, x_sample, cache_sb_k, cache_sb_v, state_hgrn, state_mlstm_c, state_mlstm_n,
              state_mlstm_m, state_mlstm_conv, page_table, c_prompt, c_sample, w_ada, b_ada, g_pre, g_post,
              w_ffn_in, w_ffn_down, w_in, sb_bias, hg_lb_logits, hg_gain, ml_conv_w, ml_conv_b, ml_wq, ml_wk,
              ml_wv, ml_w_if, ml_b_if, ml_gain, w_hg_proj, w_sb_proj, w_ml_proj, w_out):
    prm = dict(w_ada=w_ada, b_ada=b_ada, g_pre=g_pre, g_post=g_post, w_ffn_in=w_ffn_in,
               w_ffn_down=w_ffn_down, w_in=w_in, sb_bias=sb_bias, hg_lb_logits=hg_lb_logits,
               hg_gain=hg_gain, ml_conv_w=ml_conv_w, ml_conv_b=ml_conv_b, ml_wq=ml_wq, ml_wk=ml_wk,
               ml_wv=ml_wv, ml_w_if=ml_w_if, ml_b_if=ml_b_if, ml_gain=ml_gain, w_hg_proj=w_hg_proj,
               w_sb_proj=w_sb_proj, w_ml_proj=w_ml_proj, w_out=w_out)
    y_prompt, sp = _run(x_prompt, c_prompt, prm, None)
    past = dict(k=cache_sb_k, v=cache_sb_v, page_table=page_table, hgrn=state_hgrn, mc=state_mlstm_c,
                mn=state_mlstm_n, mm=state_mlstm_m, mconv=state_mlstm_conv)
    y_sample, ss = _run(x_sample, c_sample, prm, past)
    return (y_prompt, y_sample, sp['k'], sp['v'], ss['k'], ss['v'], sp['hgrn'], ss['hgrn'],
            sp['mc'], ss['mc'], sp['mn'], ss['mn'], sp['mm'], ss['mm'], sp['mconv'], ss['mconv'])
```

```python
import functools

import jax
import jax.numpy as jnp
from jax import lax
from jax.experimental import pallas as pl
from jax.experimental.pallas import tpu as pltpu

F32 = jnp.float32
BF16 = jnp.bfloat16

D_MODEL = 1024
DEPTH = 2
PAGE_SIZE = 128
HG_HEADS = 4
HG_DK = 128
HG_WIDTH = 512
HG_CHUNK = 64
HG_SUB = 16
SB_HEADS = 8
SB_HEAD_DIM = 64
SB_WIDTH = 512
SB_TILE = 128
ML_HEADS = 4
ML_HEAD_DIM = 128
ML_WIDTH = 512
ML_CHUNK = 128
ML_CONV = 4
D_FF = 2816
FF_CHUNK = 256
FFN_RES = 0.5
N_SUB = 3
NORM_EPS = 1e-6
IN_COLS = 7680
COL = 512
V7X_VMEM_LIMIT = 56 * 1024 * 1024
NEG_INF = float("-inf")


def _dot(a, b):
    return jnp.dot(a, b, preferred_element_type=F32)


def _dot_nt(a, b):
    return lax.dot_general(a, b, (((1,), (1,)), ((), ())), preferred_element_type=F32)


def _split3(x):
    x1 = x.astype(BF16)
    r1 = x - x1.astype(F32)
    x2 = r1.astype(BF16)
    x3 = (r1 - x2.astype(F32)).astype(BF16)
    return x1, x2, x3


def _tri_dot(tri, x):
    x1, x2, x3 = _split3(x)
    return _dot(tri, x1) + _dot(tri, x2) + _dot(tri, x3)


def _dot_tri(x, tri):
    x1, x2, x3 = _split3(x)
    return _dot(x1, tri) + _dot(x2, tri) + _dot(x3, tri)


def _sigmoid(x):
    return 1.0 / (1.0 + jnp.exp(-x))


def _silu(x):
    return x * _sigmoid(x)


def _log_sigmoid(x):
    return jnp.minimum(x, 0.0) - jnp.log1p(jnp.exp(-jnp.abs(x)))


def _logaddexp(a, b):
    amax = jnp.maximum(a, b)
    delta = a - b
    return jnp.where(delta != delta, a + b, amax + jnp.log1p(jnp.exp(-jnp.abs(delta))))


def _rms(x, g):
    return x * lax.rsqrt(jnp.mean(x * x, axis=-1, keepdims=True) + NORM_EPS) * g


def _head_norm(o, gain, heads, width):
    parts = []
    for h in range(heads):
        oh = o[:, h * width:(h + 1) * width]
        parts.append(oh * lax.rsqrt(jnp.mean(oh * oh, axis=-1, keepdims=True) + NORM_EPS))
    return jnp.concatenate(parts, axis=-1) * gain


def _params(sem, vmem=None):
    return pltpu.CompilerParams(dimension_semantics=sem, vmem_limit_bytes=vmem)


def _ada_kernel(c_ref, w_ref, b_ref, o_ref):
    s = _silu(c_ref[...]).astype(BF16)
    o_ref[...] = _dot(s, w_ref[...].astype(BF16)) + b_ref[...]


def _ada(c_all, w_ada, b_ada):
    n, d = c_all.shape
    cols = w_ada.shape[-1]
    tn = 1536
    return pl.pallas_call(
        _ada_kernel,
        grid=(DEPTH, cols // tn),
        in_specs=[pl.BlockSpec((n, d), lambda l, j: (0, 0)),
                  pl.BlockSpec((None, d, tn), lambda l, j: (l, 0, j)),
                  pl.BlockSpec((None, 1, tn), lambda l, j: (l, 0, j))],
        out_specs=pl.BlockSpec((None, n, tn), lambda l, j: (l, 0, j)),
        out_shape=jax.ShapeDtypeStruct((DEPTH, n, cols), F32),
        compiler_params=_params(("parallel", "parallel"), V7X_VMEM_LIMIT),
    )(c_all, w_ada, b_ada.reshape(DEPTH, 1, cols))


def _mod_spec(ada, tm, col, extra_grid=0):
    rm = ada.shape[1]
    if extra_grid:
        if rm == 1:
            return pl.BlockSpec((None, 1, D_MODEL), lambda g, i, n: (g, 0, col))
        return pl.BlockSpec((None, tm, D_MODEL), lambda g, i, n: (g, i, col))
    if rm == 1:
        return pl.BlockSpec((None, 1, D_MODEL), lambda g, i: (g, 0, col))
    return pl.BlockSpec((None, tm, D_MODEL), lambda g, i: (g, i, col))


def _ffn_kernel(x_ref, sh_ref, sc_ref, gt_ref, gpre_ref, gpost_ref, wg_ref, wu_ref, wd_ref,
                o_ref, h_ref, acc_ref):
    x = x_ref[...]
    h_ref[...] = (_rms(x, gpre_ref[...]) * (1.0 + sc_ref[...]) + sh_ref[...]).astype(BF16)
    acc_ref[...] = jnp.zeros_like(acc_ref)

    def body(f, carry):
        h = h_ref[...]
        g = _dot(h, wg_ref[f])
        u = _dot(h, wu_ref[f])
        acc_ref[...] += _dot((_silu(g) * u).astype(BF16), wd_ref[f])
        return carry

    lax.fori_loop(0, wg_ref.shape[0], body, 0)
    o_ref[...] = x + FFN_RES * (1.0 + gt_ref[...]) * _rms(acc_ref[...], gpost_ref[...])


def _ffn(x, ada, j, gpre, gpost, wg, wu, wd, tm):
    g, r, d = x.shape
    nf = wg.shape[0]
    xspec = pl.BlockSpec((None, tm, d), lambda a, i: (a, i, 0))
    vec = pl.BlockSpec((1, d), lambda a, i: (0, 0))
    res = lambda shape: pl.BlockSpec(shape, lambda a, i: (0, 0, 0), pipeline_mode=pl.Buffered(1))
    return pl.pallas_call(
        _ffn_kernel,
        grid=(g, r // tm),
        in_specs=[xspec, _mod_spec(ada, tm, 3 * j), _mod_spec(ada, tm, 3 * j + 1),
                  _mod_spec(ada, tm, 3 * j + 2), vec, vec,
                  res((nf, d, FF_CHUNK)), res((nf, d, FF_CHUNK)), res((nf, FF_CHUNK, d))],
        out_specs=xspec,
        out_shape=jax.ShapeDtypeStruct(x.shape, F32),
        scratch_shapes=[pltpu.VMEM((tm, d), BF16), pltpu.VMEM((tm, d), F32)],
        compiler_params=_params(("parallel", "parallel"), V7X_VMEM_LIMIT),
    )(x, ada, ada, ada, gpre.reshape(1, d), gpost.reshape(1, d), wg, wu, wd)


def _mixin_kernel(x_ref, sh_ref, sc_ref, gpre_ref, w_ref, o_ref, h_ref):
    @pl.when(pl.program_id(2) == 0)
    def _():
        h_ref[...] = (_rms(x_ref[...], gpre_ref[...]) * (1.0 + sc_ref[...]) + sh_ref[...]).astype(BF16)

    o_ref[...] = _dot(h_ref[...], w_ref[...])


def _mixin(x, ada, gpre, w_in, tm):
    g, r, d = x.shape
    tn = 1536
    return pl.pallas_call(
        _mixin_kernel,
        grid=(g, r // tm, IN_COLS // tn),
        in_specs=[pl.BlockSpec((None, tm, d), lambda a, i, n: (a, i, 0)),
                  _mod_spec(ada, tm, 3, 1), _mod_spec(ada, tm, 4, 1),
                  pl.BlockSpec((1, d), lambda a, i, n: (0, 0)),
                  pl.BlockSpec((d, tn), lambda a, i, n: (0, n))],
        out_specs=pl.BlockSpec((None, tm, tn), lambda a, i, n: (a, i, n)),
        out_shape=jax.ShapeDtypeStruct((g, r, IN_COLS), F32),
        scratch_shapes=[pltpu.VMEM((tm, d), BF16)],
        compiler_params=_params(("parallel", "parallel", "arbitrary"), V7X_VMEM_LIMIT),
    )(x, ada, ada, gpre.reshape(1, d), w_in)


def _merge_kernel(x_ref, gt_ref, gpost_ref, oa_ref, hg_ref, ob_ref, oc_ref, mo_ref,
                  g0, g1, g2, g3, g4, g5, hgain_ref, mgain_ref,
                  whg_ref, wsb_ref, wml_ref, wout_ref, o_ref):
    o_a = (_head_norm(oa_ref[...], hgain_ref[...], HG_HEADS, HG_DK) * _silu(hg_ref[...])).astype(BF16)
    o_b = ob_ref[...].astype(BF16)
    o_c = (_head_norm(oc_ref[...], mgain_ref[...], ML_HEADS, ML_HEAD_DIM)
           * _sigmoid(mo_ref[...])).astype(BF16)
    gates = ((g0, g2, g4), (g1, g3, g5))
    y = None
    for c in range(2):
        cs = slice(c * COL, (c + 1) * COL)
        ga, gb, gc = gates[c]
        m = (_sigmoid(ga[...]) * _dot(o_a, whg_ref[:, cs])
             + _sigmoid(gb[...]) * _dot(o_b, wsb_ref[:, cs])
             + _sigmoid(gc[...]) * _dot(o_c, wml_ref[:, cs]))
        part = _dot(m.astype(BF16), wout_ref[cs, :])
        y = part if y is None else y + part
    o_ref[...] = x_ref[...] + (1.0 + gt_ref[...]) * _rms(y, gpost_ref[...])


def _merge(x, ada, gpost, proj, o_a, o_b, o_c, hgain, mgain, whg, wsb, wml, wout, tm):
    g, r, d = x.shape
    row = lambda col: pl.BlockSpec((None, tm, COL), lambda a, i: (a, i, col))
    vec = lambda n: pl.BlockSpec((1, n), lambda a, i: (0, 0))
    wsp = lambda shape: pl.BlockSpec(shape, lambda a, i: (0, 0))
    xspec = pl.BlockSpec((None, tm, d), lambda a, i: (a, i, 0))
    return pl.pallas_call(
        _merge_kernel,
        grid=(g, r // tm),
        in_specs=[xspec, _mod_spec(ada, tm, 5), vec(d),
                  row(0), row(3), row(0), row(0), row(8),
                  row(9), row(10), row(11), row(12), row(13), row(14),
                  vec(HG_WIDTH), vec(ML_WIDTH),
                  wsp((HG_WIDTH, d)), wsp((SB_WIDTH, d)), wsp((ML_WIDTH, d)), wsp((d, d))],
        out_specs=xspec,
        out_shape=jax.ShapeDtypeStruct(x.shape, F32),
        compiler_params=_params(("parallel", "parallel"), V7X_VMEM_LIMIT),
    )(x, ada, gpost.reshape(1, d), o_a, proj, o_b, o_c, proj,
      proj, proj, proj, proj, proj, proj,
      hgain.reshape(1, HG_WIDTH), mgain.reshape(1, ML_WIDTH), whg, wsb, wml, wout)


def _hgrn_lb(lbl, layer):
    e = jnp.exp(lbl - jnp.max(lbl, axis=0, keepdims=True))
    p = e / jnp.sum(e, axis=0, keepdims=True)
    lb = jnp.zeros_like(p[0:1])
    for r in range(1, layer + 1):
        lb = lb + p[r:r + 1]
    return lb


def _hgrn_gates(z, lb):
    logf = _logaddexp(jnp.log(lb), jnp.log1p(-lb) + _log_sigmoid(z))
    k = (1.0 - lb) * _sigmoid(-z)
    return logf, k


def _hgrn_prompt_kernel(lbl_ref, q_ref, z_ref, i_ref, o_ref, s_ref, st_ref, b_ref, k_ref, *, layer):
    tc = q_ref.shape[0]
    nchunk = tc // HG_CHUNK
    nsub = HG_CHUNK // HG_SUB

    @pl.when(pl.program_id(1) == 0)
    def _():
        st_ref[...] = jnp.zeros_like(st_ref)

    lb = _hgrn_lb(lbl_ref[...], layer)[0]
    logf, k = _hgrn_gates(z_ref[...], lb)
    k_ref[...] = k
    ri = lax.broadcasted_iota(jnp.int32, (tc, tc), 0)
    ci = lax.broadcasted_iota(jnp.int32, (tc, tc), 1)
    tri = jnp.where((ci <= ri) & (ri // HG_CHUNK == ci // HG_CHUNK), 1.0, 0.0).astype(BF16)
    b_ref[...] = _tri_dot(tri, logf)

    rows64 = lax.broadcasted_iota(jnp.int32, (HG_CHUNK, 1), 0)
    rows16 = lax.broadcasted_iota(jnp.int32, (HG_SUB, 1), 0)

    def chunk_body(n, carry):
        r0 = pl.multiple_of(n * HG_CHUNK, HG_CHUNK)
        rows = pl.ds(r0, HG_CHUNK)
        q = q_ref[rows, :]
        i = i_ref[rows, :]
        b = b_ref[rows, :]
        kk = k_ref[rows, :]
        blast = b[HG_CHUNK - 1:HG_CHUNK, :]
        eb = jnp.exp(b)
        kd = kk * jnp.exp(blast - b)
        o_parts = []
        for h in range(HG_HEADS):
            cs = slice(h * HG_DK, (h + 1) * HG_DK)
            st = st_ref[h]
            o_h = _dot_nt((q[:, cs] * eb[:, cs]).astype(BF16), st.astype(BF16))
            p_rows = [jnp.zeros((HG_SUB, HG_CHUNK), F32)]
            for s_i in range(1, nsub):
                i0 = s_i * HG_SUB
                r = b[i0 - 1:i0, cs]
                qt = q[i0:i0 + HG_SUB, cs] * jnp.exp(b[i0:i0 + HG_SUB, cs] - r)
                kt = jnp.where(rows64 < i0, kk[:, cs] * jnp.exp(jnp.minimum(r - b[:, cs], 0.0)), 0.0)
                p_rows.append(_dot_nt(qt.astype(BF16), kt.astype(BF16)))
            p = jnp.concatenate(p_rows, axis=0)
            o_h = o_h + _dot(p.astype(BF16), i[:, cs].astype(BF16))
            o_parts.append(o_h)
            st_ref[h] = st * jnp.exp(blast[:, cs]) + _dot(i[:, cs].T.astype(BF16), kd[:, cs].astype(BF16))
        o_ref[rows, :] = jnp.concatenate(o_parts, axis=-1)

        for s_i in range(nsub):
            base = r0 + s_i * HG_SUB
            srows = pl.ds(base, HG_SUB)
            q_i = q_ref[srows, :]
            b_i = b_ref[srows, :]

            def s_body(s, acc):
                b_s = b_ref[pl.ds(base + s, 1), :]
                k_s = k_ref[pl.ds(base + s, 1), :]
                i_s = i_ref[pl.ds(base + s, 1), :]
                e = jnp.where(rows16 >= s, jnp.exp(jnp.minimum(b_i - b_s, 0.0)), 0.0)
                pr = q_i * e * k_s
                outs = []
                for h in range(HG_HEADS):
                    cs = slice(h * HG_DK, (h + 1) * HG_DK)
                    outs.append(jnp.sum(pr[:, cs], axis=1, keepdims=True) * i_s[:, cs])
                return acc + jnp.concatenate(outs, axis=-1)

            acc = lax.fori_loop(0, HG_SUB, s_body, jnp.zeros((HG_SUB, HG_WIDTH), F32))
            o_ref[srows, :] += acc
        return carry

    lax.fori_loop(0, nchunk, chunk_body, 0)

    @pl.when(pl.program_id(1) == pl.num_programs(1) - 1)
    def _():
        for h in range(HG_HEADS):
            s_ref[h] = st_ref[h].T


def _hgrn_prompt(proj, lbl, layer, tc):
    b, t, _ = proj.shape
    row = lambda col: pl.BlockSpec((None, tc, COL), lambda a, i: (a, i, col))
    return pl.pallas_call(
        functools.partial(_hgrn_prompt_kernel, layer=layer),
        grid=(b, t // tc),
        in_specs=[pl.BlockSpec((DEPTH, 1, HG_WIDTH), lambda a, i: (0, 0, 0)), row(0), row(1), row(2)],
        out_specs=[pl.BlockSpec((None, tc, HG_WIDTH), lambda a, i: (a, i, 0)),
                   pl.BlockSpec((None, HG_HEADS, HG_DK, HG_DK), lambda a, i: (a, 0, 0, 0))],
        out_shape=[jax.ShapeDtypeStruct((b, t, HG_WIDTH), F32),
                   jax.ShapeDtypeStruct((b, HG_HEADS, HG_DK, HG_DK), F32)],
        scratch_shapes=[pltpu.VMEM((HG_HEADS, HG_DK, HG_DK), F32),
                        pltpu.VMEM((tc, HG_WIDTH), F32), pltpu.VMEM((tc, HG_WIDTH), F32)],
        compiler_params=_params(("parallel", "arbitrary"), V7X_VMEM_LIMIT),
    )(lbl.reshape(DEPTH, 1, HG_WIDTH), proj, proj, proj)


def _hgrn_step_kernel(lbl_ref, qc_ref, zc_ref, ir_ref, s_ref, o_ref, so_ref, *, layer):
    lb = _hgrn_lb(lbl_ref[...], layer)[0]
    logf, k = _hgrn_gates(zc_ref[...], lb)
    outs = []
    for h in range(HG_HEADS):
        cs = slice(h * HG_DK, (h + 1) * HG_DK)
        s_new = jnp.exp(logf[h]) * s_ref[h] + k[h] * ir_ref[:, cs]
        so_ref[h] = s_new
        outs.append(jnp.sum(qc_ref[h] * s_new, axis=0, keepdims=True))
    o_ref[...] = jnp.concatenate(outs, axis=-1)


def _hgrn_step(proj, state, lbl, layer):
    b = proj.shape[0]
    qc = proj[:, 0, 0:COL].reshape(b, HG_HEADS, HG_DK, 1)
    zc = proj[:, 0, COL:2 * COL].reshape(b, HG_HEADS, HG_DK, 1)
    col = pl.BlockSpec((None, HG_HEADS, HG_DK, 1), lambda a: (a, 0, 0, 0))
    return pl.pallas_call(
        functools.partial(_hgrn_step_kernel, layer=layer),
        grid=(b,),
        in_specs=[pl.BlockSpec((DEPTH, HG_HEADS, HG_DK, 1), lambda a: (0, 0, 0, 0)), col, col,
                  pl.BlockSpec((None, 1, COL), lambda a: (a, 0, 2)),
                  pl.BlockSpec((None, None, HG_HEADS, HG_DK, HG_DK), lambda a: (a, layer, 0, 0, 0))],
        out_specs=[pl.BlockSpec((None, 1, HG_WIDTH), lambda a: (a, 0, 0)),
                   pl.BlockSpec((None, HG_HEADS, HG_DK, HG_DK), lambda a: (a, 0, 0, 0))],
        out_shape=[jax.ShapeDtypeStruct((b, 1, HG_WIDTH), F32),
                   jax.ShapeDtypeStruct((b, HG_HEADS, HG_DK, HG_DK), F32)],
        compiler_params=_params(("parallel",)),
    )(lbl.reshape(DEPTH, HG_HEADS, HG_DK, 1), qc, zc, proj, state)


def _softplus(z):
    return jnp.maximum(z, 0.0) + jnp.log1p(jnp.exp(-jnp.abs(z)))


def _sb_prompt_kernel(bias_ref, q_ref, k_ref, v_ref, o_ref):
    tq = q_ref.shape[0]
    qi = pl.program_id(1)
    ri = lax.broadcasted_iota(jnp.int32, (tq, tq), 0)
    ci = lax.broadcasted_iota(jnp.int32, (tq, tq), 1)
    upper = jnp.where(ri > ci, 1.0, 0.0).astype(BF16)
    causal = ci < ri

    def tile(qh, kj, vj, bias, c, acc, mask):
        z = _dot_nt(qh, kj.astype(BF16)) + bias
        sp = _softplus(z)
        l1m = -sp
        if mask is not None:
            l1m = jnp.where(mask, l1m, 0.0)
        hi = l1m.astype(BF16)
        lo = (l1m - hi.astype(F32)).astype(BF16)
        suf = _dot(hi, upper) + _dot(lo, upper)
        a = jnp.exp(z - sp + suf + c)
        if mask is not None:
            a = jnp.where(mask, a, 0.0)
        acc = acc + _dot(a.astype(BF16), vj.astype(BF16))
        c = c + suf[:, 0:1] + l1m[:, 0:1]
        return c, acc

    for h in range(SB_HEADS):
        cs = slice(h * SB_HEAD_DIM, (h + 1) * SB_HEAD_DIM)
        bias = bias_ref[h]
        qh = (q_ref[:, cs] * (SB_HEAD_DIM ** -0.5)).astype(BF16)
        r0 = pl.multiple_of(qi * tq, tq)
        c, acc = tile(qh, k_ref[pl.ds(r0, tq), cs], v_ref[pl.ds(r0, tq), cs], bias,
                      jnp.zeros((tq, 1), F32), jnp.zeros((tq, SB_HEAD_DIM), F32), causal)

        def body(step, carry):
            c, acc = carry
            rj = pl.multiple_of((qi - 1 - step) * tq, tq)
            return tile(qh, k_ref[pl.ds(rj, tq), cs], v_ref[pl.ds(rj, tq), cs], bias, c, acc, None)

        c, acc = lax.fori_loop(0, qi, body, (c, acc))
        o_ref[:, cs] = acc


def _sb_prompt(proj, bias):
    b, t, _ = proj.shape
    tq = SB_TILE
    seq = lambda col: pl.BlockSpec((None, t, COL), lambda a, i: (a, 0, col))
    return pl.pallas_call(
        _sb_prompt_kernel,
        grid=(b, t // tq),
        in_specs=[pl.BlockSpec(memory_space=pltpu.SMEM),
                  pl.BlockSpec((None, tq, COL), lambda a, i: (a, i, 4)), seq(5), seq(6)],
        out_specs=pl.BlockSpec((None, tq, SB_WIDTH), lambda a, i: (a, i, 0)),
        out_shape=jax.ShapeDtypeStruct((b, t, SB_WIDTH), F32),
        compiler_params=_params(("parallel", "arbitrary"), V7X_VMEM_LIMIT),
    )(bias, proj, proj, proj)


SB_PAGES_PER_STEP = 8


def _sb_decode_kernel(pt_ref, q_ref, bias_ref, *refs):
    npg = SB_PAGES_PER_STEP
    k_refs = refs[:npg]
    v_refs = refs[npg:2 * npg]
    o_ref, c_ref, acc_ref = refs[2 * npg:]
    g = pl.program_id(1)

    @pl.when(g == 0)
    def _():
        c_ref[...] = jnp.zeros_like(c_ref)
        acc_ref[...] = jnp.zeros_like(acc_ref)

    ri = lax.broadcasted_iota(jnp.int32, (PAGE_SIZE, PAGE_SIZE), 0)
    ci = lax.broadcasted_iota(jnp.int32, (PAGE_SIZE, PAGE_SIZE), 1)
    upper = jnp.where(ri > ci, 1.0, 0.0).astype(BF16)
    q = q_ref[...]
    bias = bias_ref[...]
    c = c_ref[:, 0:1]
    acc = acc_ref[...]
    for p in range(npg - 1, -1, -1):
        z = _dot_nt(q, k_refs[p][...].astype(BF16)) + bias
        sp = _softplus(z)
        l1m = -sp
        hi = l1m.astype(BF16)
        lo = (l1m - hi.astype(F32)).astype(BF16)
        suf = _dot(hi, upper) + _dot(lo, upper)
        a = jnp.exp(z - sp + suf + c)
        acc = acc + _dot(a.astype(BF16), v_refs[p][...].astype(BF16))
        c = c + suf[:, 0:1] + l1m[:, 0:1]
    c_ref[...] = jnp.broadcast_to(c, c_ref.shape)
    acc_ref[...] = acc

    @pl.when(g == pl.num_programs(1) - 1)
    def _():
        hrow = lax.broadcasted_iota(jnp.int32, (SB_HEADS, SB_WIDTH), 0)
        hcol = lax.broadcasted_iota(jnp.int32, (SB_HEADS, SB_WIDTH), 1) // SB_HEAD_DIM
        o_ref[...] = jnp.sum(jnp.where(hrow == hcol, acc, 0.0), axis=0, keepdims=True)


def _sb_decode(sq, bias, cache_k, cache_v, page_table, layer):
    b = sq.shape[0]
    n_pages = page_table.shape[1]
    npg = SB_PAGES_PER_STEP
    ng = n_pages // npg
    hsel = (jnp.arange(SB_WIDTH)[None, :] // SB_HEAD_DIM) == jnp.arange(SB_HEADS)[:, None]
    qbd = jnp.where(hsel[None], sq[:, None, :] * (SB_HEAD_DIM ** -0.5), 0.0).astype(BF16)

    def page_spec(p):
        return pl.BlockSpec((None, None, PAGE_SIZE, SB_WIDTH),
                            lambda a, g, pt: (pt[a, (ng - 1 - g) * npg + p], layer, 0, 0))

    grid_spec = pltpu.PrefetchScalarGridSpec(
        num_scalar_prefetch=1,
        grid=(b, ng),
        in_specs=[pl.BlockSpec((None, SB_HEADS, SB_WIDTH), lambda a, g, pt: (a, 0, 0)),
                  pl.BlockSpec((SB_HEADS, 1), lambda a, g, pt: (0, 0))]
                 + [page_spec(p) for p in range(npg)] + [page_spec(p) for p in range(npg)],
        out_specs=pl.BlockSpec((None, 1, SB_WIDTH), lambda a, g, pt: (a, 0, 0)),
        scratch_shapes=[pltpu.VMEM((SB_HEADS, 128), F32), pltpu.VMEM((SB_HEADS, SB_WIDTH), F32)],
    )
    return pl.pallas_call(
        _sb_decode_kernel,
        grid_spec=grid_spec,
        out_shape=jax.ShapeDtypeStruct((b, 1, SB_WIDTH), F32),
        compiler_params=_params(("parallel", "arbitrary"), V7X_VMEM_LIMIT),
    )(page_table, qbd, bias.reshape(SB_HEADS, 1), *([cache_k] * npg), *([cache_v] * npg))


def _ml_pre_kernel(x0, x1, x2, x3, cw_ref, cb_ref, wq_ref, wk_ref, wv_ref, wif_ref, bif_ref,
                   q_ref, k_ref, v_ref, g_ref):
    x = x3[...]
    cw = cw_ref[...]
    xc = _silu(x0[...] * cw[0:1] + x1[...] * cw[1:2] + x2[...] * cw[2:3] + x * cw[3:4] + cb_ref[...])
    xcb = xc.astype(BF16)
    xb = x.astype(BF16)
    qs, ks, vs = [], [], []
    for h in range(ML_HEADS):
        cs = slice(h * ML_HEAD_DIM, (h + 1) * ML_HEAD_DIM)
        qs.append(_dot(xcb[:, cs], wq_ref[h]))
        ks.append(_dot(xcb[:, cs], wk_ref[h]))
        vs.append(_dot(xb[:, cs], wv_ref[h]))
    q = jnp.concatenate(qs, axis=-1)
    k = jnp.concatenate(ks, axis=-1)
    v = jnp.concatenate(vs, axis=-1)
    q_ref[...] = q
    k_ref[...] = k
    v_ref[...] = v
    g_ref[...] = (_dot(q.astype(BF16), wif_ref[0:ML_WIDTH, :])
                  + _dot(k.astype(BF16), wif_ref[ML_WIDTH:2 * ML_WIDTH, :])
                  + _dot(v.astype(BF16), wif_ref[2 * ML_WIDTH:3 * ML_WIDTH, :]) + bif_ref[...])


def _ml_pre(xs, cw, cb, wq, wk, wv, wif, bif, tm):
    g, r, w = xs[0].shape
    row = pl.BlockSpec((None, tm, w), lambda a, i: (a, i, 0))
    full = lambda shape: pl.BlockSpec(shape, lambda a, i: (0,) * len(shape))
    out = jax.ShapeDtypeStruct((g, r, w), F32)
    return pl.pallas_call(
        _ml_pre_kernel,
        grid=(g, r // tm),
        in_specs=[row, row, row, row, full((ML_CONV, w)), full((1, w)),
                  full(wq.shape), full(wk.shape), full(wv.shape), full(wif.shape), full((1, 128))],
        out_specs=[row, row, row, pl.BlockSpec((None, tm, 128), lambda a, i: (a, i, 0))],
        out_shape=[out, out, out, jax.ShapeDtypeStruct((g, r, 128), F32)],
        compiler_params=_params(("parallel", "parallel"), V7X_VMEM_LIMIT),
    )(*xs, cw, cb.reshape(1, w), wq, wk, wv, wif, bif)


def _ml_prompt_kernel(q_ref, k_ref, v_ref, g_ref, h_ref, c_out, n_out, m_out, c_ref, n_ref, m_ref):
    L = ML_CHUNK

    @pl.when(pl.program_id(1) == 0)
    def _():
        c_ref[...] = jnp.zeros_like(c_ref)
        n_ref[...] = jnp.zeros_like(n_ref)
        m_ref[...] = jnp.zeros_like(m_ref)

    ri = lax.broadcasted_iota(jnp.int32, (L, L), 0)
    ci = lax.broadcasted_iota(jnp.int32, (L, L), 1)
    causal = ci <= ri
    tril = jnp.where(causal, 1.0, 0.0).astype(BF16)
    triu = jnp.where(ri <= ci, 1.0, 0.0).astype(BF16)

    g = g_ref[...]
    lg = _log_sigmoid(g)
    b_cols = _tri_dot(tril, lg)
    gt = g.T
    b_rows = _dot_tri(_log_sigmoid(gt), triu)
    outs = []
    for h in range(ML_HEADS):
        cs = slice(h * ML_HEAD_DIM, (h + 1) * ML_HEAD_DIM)
        q = q_ref[:, cs]
        ks = k_ref[:, cs] * (ML_HEAD_DIM ** -0.5)
        v = v_ref[:, cs]
        it_col = g[:, h:h + 1]
        it_row = gt[h:h + 1, :]
        b_col = b_cols[:, ML_HEADS + h:ML_HEADS + h + 1]
        b_row = b_rows[ML_HEADS + h:ML_HEADS + h + 1, :]
        m0 = m_ref[h:h + 1, 0:1]
        c0 = c_ref[h]
        n0 = n_ref[h:h + 1, :]
        log_d = jnp.where(causal, b_col - b_row + it_row, NEG_INF)
        m_t = jnp.maximum(b_col + m0, jnp.max(log_d, axis=1, keepdims=True))
        dm = jnp.exp(log_d - m_t)
        inter = jnp.exp(b_col + m0 - m_t)
        qb = q.astype(BF16)
        ksb = ks.astype(BF16)
        w = _dot_nt(qb, ksb) * dm
        numer = inter * _dot_nt(qb, c0.astype(BF16)) + _dot(w.astype(BF16), v.astype(BF16))
        denom = inter * jnp.sum(q * n0, axis=1, keepdims=True) + jnp.sum(w, axis=1, keepdims=True)
        outs.append(numer / jnp.maximum(jnp.abs(denom), jnp.exp(-m_t)))
        m_new = m_t[L - 1:L, :]
        b_last = b_col[L - 1:L, :]
        w_end = jnp.exp(b_last - b_col + it_col - m_new)
        dec = jnp.exp(b_last + m0 - m_new)
        c_ref[h] = dec * c0 + _dot((w_end * v).T.astype(BF16), ksb)
        n_ref[h:h + 1, :] = dec * n0 + jnp.sum(w_end * ks, axis=0, keepdims=True)
        m_ref[h:h + 1, :] = jnp.broadcast_to(m_new, (1, m_ref.shape[1]))
    h_ref[...] = jnp.concatenate(outs, axis=-1)

    @pl.when(pl.program_id(1) == pl.num_programs(1) - 1)
    def _():
        c_out[...] = c_ref[...]
        n_out[...] = n_ref[...]
        m_out[...] = m_ref[...]


def _ml_prompt(q, k, v, gates):
    b, t, w = q.shape
    row = pl.BlockSpec((None, ML_CHUNK, w), lambda a, i: (a, i, 0))
    return pl.pallas_call(
        _ml_prompt_kernel,
        grid=(b, t // ML_CHUNK),
        in_specs=[row, row, row, pl.BlockSpec((None, ML_CHUNK, 128), lambda a, i: (a, i, 0))],
        out_specs=[row,
                   pl.BlockSpec((None, ML_HEADS, ML_HEAD_DIM, ML_HEAD_DIM), lambda a, i: (a, 0, 0, 0)),
                   pl.BlockSpec((None, ML_HEADS, ML_HEAD_DIM), lambda a, i: (a, 0, 0)),
                   pl.BlockSpec((None, ML_HEADS, 128), lambda a, i: (a, 0, 0))],
        out_shape=[jax.ShapeDtypeStruct((b, t, w), F32),
                   jax.ShapeDtypeStruct((b, ML_HEADS, ML_HEAD_DIM, ML_HEAD_DIM), F32),
                   jax.ShapeDtypeStruct((b, ML_HEADS, ML_HEAD_DIM), F32),
                   jax.ShapeDtypeStruct((b, ML_HEADS, 128), F32)],
        scratch_shapes=[pltpu.VMEM((ML_HEADS, ML_HEAD_DIM, ML_HEAD_DIM), F32),
                        pltpu.VMEM((ML_HEADS, ML_HEAD_DIM), F32), pltpu.VMEM((ML_HEADS, 128), F32)],
        compiler_params=_params(("parallel", "arbitrary"), V7X_VMEM_LIMIT),
    )(q, k, v, gates)


def _ml_step_kernel(q_ref, k_ref, vc_ref, g_ref, c_ref, n_ref, m_ref, h_ref, c_out, n_out, m_out):
    g = g_ref[...]
    lane = lax.broadcasted_iota(jnp.int32, (1, 128), 1)
    m_row = jnp.zeros((1, 128), F32)
    for h in range(ML_HEADS):
        cs = slice(h * ML_HEAD_DIM, (h + 1) * ML_HEAD_DIM)
        q = q_ref[:, cs]
        ks = k_ref[:, cs] * (ML_HEAD_DIM ** -0.5)
        v = vc_ref[h]
        it = g[:, h:h + 1]
        logf = _log_sigmoid(g[:, ML_HEADS + h:ML_HEADS + h + 1])
        m0 = m_ref[:, h:h + 1]
        c0 = c_ref[h]
        n0 = n_ref[h:h + 1, :]
        m_t = jnp.maximum(logf + m0, it)
        dm = jnp.exp(it - m_t)
        inter = jnp.exp(logf + m0 - m_t)
        w = jnp.sum(q * ks, axis=1, keepdims=True) * dm
        numer = inter * jnp.sum(c0 * q, axis=1, keepdims=True) + w * v
        denom = inter * jnp.sum(q * n0, axis=1, keepdims=True) + w
        h_ref[h] = numer / jnp.maximum(jnp.abs(denom), jnp.exp(-m_t))
        w_end = jnp.exp(it - m_t)
        dec = jnp.exp(logf + m0 - m_t)
        c_out[h] = dec * c0 + (w_end * v) * ks
        n_out[h:h + 1, :] = dec * n0 + w_end * ks
        m_row = jnp.where(lane == h, m_t, m_row)
    m_out[...] = m_row


def _ml_step(q, k, v, gates, state_c, state_n, state_m, layer):
    b = q.shape[0]
    vc = v.reshape(b, ML_HEADS, ML_HEAD_DIM, 1)
    m0 = state_m[:, layer].reshape(b, 1, ML_HEADS)
    row = pl.BlockSpec((None, 1, ML_WIDTH), lambda a: (a, 0, 0))
    col = pl.BlockSpec((None, ML_HEADS, ML_HEAD_DIM, 1), lambda a: (a, 0, 0, 0))
    cspec = pl.BlockSpec((None, ML_HEADS, ML_HEAD_DIM, ML_HEAD_DIM), lambda a: (a, 0, 0, 0))
    nspec = pl.BlockSpec((None, ML_HEADS, ML_HEAD_DIM), lambda a: (a, 0, 0))
    return pl.pallas_call(
        _ml_step_kernel,
        grid=(b,),
        in_specs=[row, row, col, pl.BlockSpec((None, 1, 128), lambda a: (a, 0, 0)),
                  pl.BlockSpec((None, None, ML_HEADS, ML_HEAD_DIM, ML_HEAD_DIM),
                               lambda a: (a, layer, 0, 0, 0)),
                  pl.BlockSpec((None, None, ML_HEADS, ML_HEAD_DIM), lambda a: (a, layer, 0, 0)),
                  pl.BlockSpec((None, 1, ML_HEADS), lambda a: (a, 0, 0))],
        out_specs=[col, cspec, nspec, pl.BlockSpec((None, 1, 128), lambda a: (a, 0, 0))],
        out_shape=[jax.ShapeDtypeStruct((b, ML_HEADS, ML_HEAD_DIM, 1), F32),
                   jax.ShapeDtypeStruct((b, ML_HEADS, ML_HEAD_DIM, ML_HEAD_DIM), F32),
                   jax.ShapeDtypeStruct((b, ML_HEADS, ML_HEAD_DIM), F32),
                   jax.ShapeDtypeStruct((b, 1, 128), F32)],
        compiler_params=_params(("parallel",)),
    )(q, k, vc, gates, state_c, state_n, m0)


def _prep_weights(p):
    bf = lambda a: a.astype(BF16)
    w = {}
    fin = p["w_ffn_in"]
    nf = D_FF // FF_CHUNK
    chunked = lambda a: bf(a.reshape(DEPTH, 2, D_MODEL, nf, FF_CHUNK).transpose(0, 1, 3, 2, 4))
    w["wg"] = chunked(fin[..., :D_FF])
    w["wu"] = chunked(fin[..., D_FF:])
    w["wd"] = bf(p["w_ffn_down"].reshape(DEPTH, 2, nf, FF_CHUNK, D_MODEL))
    w["w_in"] = bf(p["w_in"])
    for name in ("w_hg_proj", "w_sb_proj", "w_ml_proj", "w_out", "ml_wq", "ml_wk", "ml_wv"):
        w[name] = bf(p[name])
    w["wif"] = bf(jnp.pad(p["ml_w_if"], ((0, 0), (0, 0), (0, 128 - 2 * ML_HEADS))))
    w["bif"] = jnp.pad(p["ml_b_if"], ((0, 0), (0, 128 - 2 * ML_HEADS))).reshape(DEPTH, 1, 128)
    return w


def _layer(x, ada, l, p, w, tm, past):
    ffn = lambda xx, j, s: _ffn(xx, ada, j, p["g_pre"][l, j], p["g_post"][l, j],
                                w["wg"][l, s], w["wu"][l, s], w["wd"][l, s], tm["ffn"])
    x = ffn(x, 0, 0)
    proj = _mixin(x, ada, p["g_pre"][l, 1], w["w_in"][l], tm["mixin"])
    st = {}
    mx = proj[..., 7 * COL:8 * COL]
    if past is None:
        b, t, _ = x.shape
        o_a, st["hgrn"] = _hgrn_prompt(proj, p["hg_lb_logits"], l, tm["hgrn"])
        o_b = _sb_prompt(proj, p["sb_bias"][l])
        taps = [jnp.pad(mx, ((0, 0), (ML_CONV - 1 - j, 0), (0, 0)))[:, :t] for j in range(ML_CONV - 1)]
        q, k, v, gates = _ml_pre(taps + [mx], p["ml_conv_w"][l], p["ml_conv_b"][l], w["ml_wq"][l],
                                 w["ml_wk"][l], w["ml_wv"][l], w["wif"][l], w["bif"][l], tm["mlpre"])
        o_c, st["mc"], st["mn"], m_pad = _ml_prompt(q, k, v, gates)
        st["mm"] = m_pad[:, :, 0]
        st["mconv"] = mx[:, t - (ML_CONV - 1):]
        st["k"] = proj[..., 5 * COL:6 * COL].reshape(b, t, SB_HEADS, SB_HEAD_DIM)
        st["v"] = proj[..., 6 * COL:7 * COL].reshape(b, t, SB_HEADS, SB_HEAD_DIM)
    else:
        b = x.shape[1]
        projs = proj.reshape(b, 1, IN_COLS)
        o_a, st["hgrn"] = _hgrn_step(projs, past["hgrn"], p["hg_lb_logits"], l)
        o_b = _sb_decode(projs[:, 0, 4 * COL:5 * COL], p["sb_bias"][l], past["k"], past["v"],
                         past["page_table"], l)
        buf = past["mconv"][:, l]
        taps = [buf[:, j].reshape(1, b, ML_WIDTH) for j in range(ML_CONV - 1)]
        q, k, v, gates = _ml_pre(taps + [mx], p["ml_conv_w"][l], p["ml_conv_b"][l], w["ml_wq"][l],
                                 w["ml_wk"][l], w["ml_wv"][l], w["wif"][l], w["bif"][l], b)
        hc, st["mc"], st["mn"], m_pad = _ml_step(q.reshape(b, 1, ML_WIDTH), k.reshape(b, 1, ML_WIDTH),
                                                 v.reshape(b, 1, ML_WIDTH), gates.reshape(b, 1, 128),
                                                 past["mc"], past["mn"], past["mm"], l)
        o_a = o_a.reshape(1, b, HG_WIDTH)
        o_b = o_b.reshape(1, b, SB_WIDTH)
        o_c = hc.reshape(1, b, ML_WIDTH)
        st["mm"] = m_pad[:, 0, :ML_HEADS]
        st["mconv"] = jnp.concatenate([buf[:, 1:], mx.reshape(b, 1, ML_WIDTH)], axis=1)
        st["k"] = projs[:, :, 5 * COL:6 * COL].reshape(b, 1, SB_HEADS, SB_HEAD_DIM)
        st["v"] = projs[:, :, 6 * COL:7 * COL].reshape(b, 1, SB_HEADS, SB_HEAD_DIM)
    x = _merge(x, ada, p["g_post"][l, 1], proj, o_a, o_b, o_c, p["hg_gain"][l], p["ml_gain"][l],
               w["w_hg_proj"][l], w["w_sb_proj"][l], w["w_ml_proj"][l], w["w_out"][l], tm["merge"])
    x = ffn(x, 2, 1)
    return x, st


def _run(x, ada_all, p, w, tm, past):
    outs = []
    for l in range(DEPTH):
        x, st = _layer(x, ada_all[l], l, p, w, tm, past)
        outs.append(st)
    return x, {name: jnp.stack([o[name] for o in outs], axis=1) for name in outs[0]}


def kernel(x_prompt, x_sample, cache_sb_k, cache_sb_v, state_hgrn, state_mlstm_c, state_mlstm_n,
           state_mlstm_m, state_mlstm_conv, page_table, c_prompt, c_sample, w_ada, b_ada, g_pre, g_post,
           w_ffn_in, w_ffn_down, w_in, sb_bias, hg_lb_logits, hg_gain, ml_conv_w, ml_conv_b, ml_wq, ml_wk,
           ml_wv, ml_w_if, ml_b_if, ml_gain, w_hg_proj, w_sb_proj, w_ml_proj, w_out):
    p = dict(g_pre=g_pre, g_post=g_post, w_ffn_in=w_ffn_in, w_ffn_down=w_ffn_down, w_in=w_in,
             sb_bias=sb_bias, hg_lb_logits=hg_lb_logits, hg_gain=hg_gain, ml_conv_w=ml_conv_w,
             ml_conv_b=ml_conv_b, ml_wq=ml_wq, ml_wk=ml_wk, ml_wv=ml_wv, ml_w_if=ml_w_if,
             ml_b_if=ml_b_if, ml_gain=ml_gain, w_hg_proj=w_hg_proj, w_sb_proj=w_sb_proj,
             w_ml_proj=w_ml_proj, w_out=w_out)
    w = _prep_weights(p)
    bp, t, d = x_prompt.shape
    bs = x_sample.shape[0]
    ada = _ada(jnp.concatenate([c_prompt, c_sample], axis=0), w_ada, b_ada)
    ada_p = ada[:, :bp].reshape(DEPTH, bp, 1, N_SUB * 3 * d)
    ada_s = ada[:, bp:].reshape(DEPTH, 1, bs, N_SUB * 3 * d)

    tm_p = dict(ffn=min(t, 512), mixin=min(t, 1024), merge=min(t, 256), hgrn=min(t, 512),
                mlpre=min(t, 512))
    y_p, sp = _run(x_prompt, ada_p, p, w, tm_p, None)

    n_phys = cache_sb_k.shape[0]
    past = dict(k=cache_sb_k.reshape(n_phys, DEPTH, PAGE_SIZE, SB_WIDTH),
                v=cache_sb_v.reshape(n_phys, DEPTH, PAGE_SIZE, SB_WIDTH),
                page_table=page_table, hgrn=state_hgrn, mc=state_mlstm_c, mn=state_mlstm_n,
                mm=state_mlstm_m, mconv=state_mlstm_conv)
    tm_s = dict(ffn=bs, mixin=bs, merge=bs)
    y_s, ss = _run(x_sample.reshape(1, bs, d), ada_s, p, w, tm_s, past)
    y_s = y_s.reshape(bs, 1, d)
    return (y_p, y_s, sp["k"], sp["v"], ss["k"], ss["v"], sp["hgrn"], ss["hgrn"],
            sp["mc"], ss["mc"], sp["mn"], ss["mn"], sp["mm"], ss["mm"], sp["mconv"], ss["mconv"])
```

```python
import functools

import jax
import jax.numpy as jnp
from jax import lax
from jax.experimental import pallas as pl
from jax.experimental.pallas import tpu as pltpu

F32 = jnp.float32
BF16 = jnp.bfloat16

D_MODEL = 1024
DEPTH = 2
PAGE_SIZE = 128
HG_HEADS = 4
HG_DK = 128
HG_WIDTH = 512
HG_CHUNK = 64
HG_SUB = 16
HG_SAFE_DECAY = 80.0
SB_HEADS = 8
SB_HEAD_DIM = 64
SB_WIDTH = 512
SB_TQ = 128
SB_TK = 256
SB_HEAD_GROUP = 8
ML_HEADS = 4
ML_HEAD_DIM = 128
ML_WIDTH = 512
ML_CHUNK = 128
ML_CONV = 4
D_FF = 2816
FF_CHUNK = 256
FFN_RES = 0.5
N_SUB = 3
NORM_EPS = 1e-6
IN_COLS = 7680
COL = 512
V7X_VMEM_LIMIT = 56 * 1024 * 1024
NEG_INF = float("-inf")


def _dot(a, b):
    return jnp.dot(a, b, preferred_element_type=F32)


def _dot_nt(a, b):
    return lax.dot_general(a, b, (((1,), (1,)), ((), ())), preferred_element_type=F32)


def _split3(x):
    x1 = x.astype(BF16)
    r1 = x - x1.astype(F32)
    x2 = r1.astype(BF16)
    x3 = (r1 - x2.astype(F32)).astype(BF16)
    return x1, x2, x3


def _tri_dot(tri, x):
    x1, x2, x3 = _split3(x)
    return _dot(tri, x1) + _dot(tri, x2) + _dot(tri, x3)


def _dot_tri(x, tri):
    x1, x2, x3 = _split3(x)
    return _dot(x1, tri) + _dot(x2, tri) + _dot(x3, tri)


def _sigmoid(x):
    return 1.0 / (1.0 + jnp.exp(-x))


def _silu(x):
    return x * _sigmoid(x)


def _log_sigmoid(x):
    return jnp.minimum(x, 0.0) - jnp.log1p(jnp.exp(-jnp.abs(x)))


def _logaddexp(a, b):
    amax = jnp.maximum(a, b)
    delta = a - b
    return jnp.where(delta != delta, a + b, amax + jnp.log1p(jnp.exp(-jnp.abs(delta))))


def _rms(x, g):
    return x * lax.rsqrt(jnp.mean(x * x, axis=-1, keepdims=True) + NORM_EPS) * g


def _head_norm(o, gain, heads, width):
    parts = []
    for h in range(heads):
        oh = o[:, h * width:(h + 1) * width]
        parts.append(oh * lax.rsqrt(jnp.mean(oh * oh, axis=-1, keepdims=True) + NORM_EPS))
    return jnp.concatenate(parts, axis=-1) * gain


def _params(sem, vmem=None):
    return pltpu.CompilerParams(dimension_semantics=sem, vmem_limit_bytes=vmem)


def _ada_kernel(c_ref, w_ref, b_ref, o_ref):
    s = _silu(c_ref[...]).astype(BF16)
    o_ref[...] = _dot(s, w_ref[...].astype(BF16)) + b_ref[...]


def _ada(c_all, w_ada, b_ada):
    n, d = c_all.shape
    cols = w_ada.shape[-1]
    tn = 1536
    return pl.pallas_call(
        _ada_kernel,
        grid=(DEPTH, cols // tn),
        in_specs=[pl.BlockSpec((n, d), lambda l, j: (0, 0)),
                  pl.BlockSpec((None, d, tn), lambda l, j: (l, 0, j)),
                  pl.BlockSpec((None, 1, tn), lambda l, j: (l, 0, j))],
        out_specs=pl.BlockSpec((None, n, tn), lambda l, j: (l, 0, j)),
        out_shape=jax.ShapeDtypeStruct((DEPTH, n, cols), F32),
        compiler_params=_params(("parallel", "parallel"), V7X_VMEM_LIMIT),
    )(c_all, w_ada, b_ada.reshape(DEPTH, 1, cols))


def _mod_spec(ada, tm, col, extra_grid=0):
    rm = ada.shape[1]
    if extra_grid:
        if rm == 1:
            return pl.BlockSpec((None, 1, D_MODEL), lambda g, i, n: (g, 0, col))
        return pl.BlockSpec((None, tm, D_MODEL), lambda g, i, n: (g, i, col))
    if rm == 1:
        return pl.BlockSpec((None, 1, D_MODEL), lambda g, i: (g, 0, col))
    return pl.BlockSpec((None, tm, D_MODEL), lambda g, i: (g, i, col))


def _ffn_kernel(x_ref, sh_ref, sc_ref, gt_ref, gpre_ref, gpost_ref, wg_ref, wu_ref, wd_ref,
                o_ref, h_ref, acc_ref):
    x = x_ref[...]
    h_ref[...] = (_rms(x, gpre_ref[...]) * (1.0 + sc_ref[...]) + sh_ref[...]).astype(BF16)
    acc_ref[...] = jnp.zeros_like(acc_ref)

    def body(f, carry):
        h = h_ref[...]
        g = _dot(h, wg_ref[f])
        u = _dot(h, wu_ref[f])
        acc_ref[...] += _dot((_silu(g) * u).astype(BF16), wd_ref[f])
        return carry

    lax.fori_loop(0, wg_ref.shape[0], body, 0)
    o_ref[...] = x + FFN_RES * (1.0 + gt_ref[...]) * _rms(acc_ref[...], gpost_ref[...])


def _ffn(x, ada, j, gpre, gpost, wg, wu, wd, tm):
    g, r, d = x.shape
    nf = wg.shape[0]
    xspec = pl.BlockSpec((None, tm, d), lambda a, i: (a, i, 0))
    vec = pl.BlockSpec((1, d), lambda a, i: (0, 0))
    res = lambda shape: pl.BlockSpec(shape, lambda a, i: (0, 0, 0), pipeline_mode=pl.Buffered(1))
    return pl.pallas_call(
        _ffn_kernel,
        grid=(g, r // tm),
        in_specs=[xspec, _mod_spec(ada, tm, 3 * j), _mod_spec(ada, tm, 3 * j + 1),
                  _mod_spec(ada, tm, 3 * j + 2), vec, vec,
                  res((nf, d, FF_CHUNK)), res((nf, d, FF_CHUNK)), res((nf, FF_CHUNK, d))],
        out_specs=xspec,
        out_shape=jax.ShapeDtypeStruct(x.shape, F32),
        scratch_shapes=[pltpu.VMEM((tm, d), BF16), pltpu.VMEM((tm, d), F32)],
        compiler_params=_params(("parallel", "parallel"), V7X_VMEM_LIMIT),
    )(x, ada, ada, ada, gpre.reshape(1, d), gpost.reshape(1, d), wg, wu, wd)


def _mixin_kernel(x_ref, sh_ref, sc_ref, gpre_ref, w_ref, o_ref, h_ref):
    @pl.when(pl.program_id(2) == 0)
    def _():
        h_ref[...] = (_rms(x_ref[...], gpre_ref[...]) * (1.0 + sc_ref[...]) + sh_ref[...]).astype(BF16)

    o_ref[...] = _dot(h_ref[...], w_ref[...])


def _mixin(x, ada, gpre, w_in, tm):
    g, r, d = x.shape
    tn = 1536
    return pl.pallas_call(
        _mixin_kernel,
        grid=(g, r // tm, IN_COLS // tn),
        in_specs=[pl.BlockSpec((None, tm, d), lambda a, i, n: (a, i, 0)),
                  _mod_spec(ada, tm, 3, 1), _mod_spec(ada, tm, 4, 1),
                  pl.BlockSpec((1, d), lambda a, i, n: (0, 0)),
                  pl.BlockSpec((d, tn), lambda a, i, n: (0, n))],
        out_specs=pl.BlockSpec((None, tm, tn), lambda a, i, n: (a, i, n)),
        out_shape=jax.ShapeDtypeStruct((g, r, IN_COLS), F32),
        scratch_shapes=[pltpu.VMEM((tm, d), BF16)],
        compiler_params=_params(("parallel", "parallel", "arbitrary"), V7X_VMEM_LIMIT),
    )(x, ada, ada, gpre.reshape(1, d), w_in)


def _merge_kernel(x_ref, gt_ref, gpost_ref, oa_ref, hg_ref, ob_ref, oc_ref, mo_ref,
                  g0, g1, g2, g3, g4, g5, hgain_ref, mgain_ref,
                  whg_ref, wsb_ref, wml_ref, wout_ref, o_ref):
    o_a = (_head_norm(oa_ref[...], hgain_ref[...], HG_HEADS, HG_DK) * _silu(hg_ref[...])).astype(BF16)
    o_b = ob_ref[...].astype(BF16)
    o_c = (_head_norm(oc_ref[...], mgain_ref[...], ML_HEADS, ML_HEAD_DIM)
           * _sigmoid(mo_ref[...])).astype(BF16)
    gates = ((g0, g2, g4), (g1, g3, g5))
    y = None
    for c in range(2):
        cs = slice(c * COL, (c + 1) * COL)
        ga, gb, gc = gates[c]
        m = (_sigmoid(ga[...]) * _dot(o_a, whg_ref[:, cs])
             + _sigmoid(gb[...]) * _dot(o_b, wsb_ref[:, cs])
             + _sigmoid(gc[...]) * _dot(o_c, wml_ref[:, cs]))
        part = _dot(m.astype(BF16), wout_ref[cs, :])
        y = part if y is None else y + part
    o_ref[...] = x_ref[...] + (1.0 + gt_ref[...]) * _rms(y, gpost_ref[...])


def _merge(x, ada, gpost, proj, o_a, o_b, o_c, hgain, mgain, whg, wsb, wml, wout, tm):
    g, r, d = x.shape
    row = lambda col: pl.BlockSpec((None, tm, COL), lambda a, i: (a, i, col))
    vec = lambda n: pl.BlockSpec((1, n), lambda a, i: (0, 0))
    wsp = lambda shape: pl.BlockSpec(shape, lambda a, i: (0, 0))
    xspec = pl.BlockSpec((None, tm, d), lambda a, i: (a, i, 0))
    return pl.pallas_call(
        _merge_kernel,
        grid=(g, r // tm),
        in_specs=[xspec, _mod_spec(ada, tm, 5), vec(d),
                  row(0), row(3), row(0), row(0), row(8),
                  row(9), row(10), row(11), row(12), row(13), row(14),
                  vec(HG_WIDTH), vec(ML_WIDTH),
                  wsp((HG_WIDTH, d)), wsp((SB_WIDTH, d)), wsp((ML_WIDTH, d)), wsp((d, d))],
        out_specs=xspec,
        out_shape=jax.ShapeDtypeStruct(x.shape, F32),
        compiler_params=_params(("parallel", "parallel"), V7X_VMEM_LIMIT),
    )(x, ada, gpost.reshape(1, d), o_a, proj, o_b, o_c, proj,
      proj, proj, proj, proj, proj, proj,
      hgain.reshape(1, HG_WIDTH), mgain.reshape(1, ML_WIDTH), whg, wsb, wml, wout)


def _hgrn_lb(lbl, layer):
    e = jnp.exp(lbl - jnp.max(lbl, axis=0, keepdims=True))
    p = e / jnp.sum(e, axis=0, keepdims=True)
    lb = jnp.zeros_like(p[0:1])
    for r in range(1, layer + 1):
        lb = lb + p[r:r + 1]
    return lb


def _hgrn_gates(z, lb):
    logf = _logaddexp(jnp.log(lb), jnp.log1p(-lb) + _log_sigmoid(z))
    k = (1.0 - lb) * _sigmoid(-z)
    return logf, k


def _hgrn_prompt_kernel(lbl_ref, q_ref, z_ref, i_ref, o_ref, s_ref, st_ref, b_ref, k_ref, *, layer):
    tc = q_ref.shape[0]
    nchunk = tc // HG_CHUNK
    nsub = HG_CHUNK // HG_SUB

    @pl.when(pl.program_id(1) == 0)
    def _():
        st_ref[...] = jnp.zeros_like(st_ref)

    lb = _hgrn_lb(lbl_ref[...], layer)[0]
    logf, k = _hgrn_gates(z_ref[...], lb)
    k_ref[...] = k
    ri = lax.broadcasted_iota(jnp.int32, (tc, tc), 0)
    ci = lax.broadcasted_iota(jnp.int32, (tc, tc), 1)
    tri = jnp.where((ci <= ri) & (ri // HG_CHUNK == ci // HG_CHUNK), 1.0, 0.0).astype(BF16)
    b_ref[...] = _tri_dot(tri, logf)
    block_decay = -jnp.sum(logf.reshape(tc // HG_SUB, HG_SUB, HG_WIDTH), axis=1)
    safe = jnp.max(block_decay) < HG_SAFE_DECAY

    rows64 = lax.broadcasted_iota(jnp.int32, (HG_CHUNK, 1), 0)
    rows16 = lax.broadcasted_iota(jnp.int32, (HG_SUB, 1), 0)
    r64 = lax.broadcasted_iota(jnp.int32, (HG_CHUNK, HG_CHUNK), 0)
    c64 = lax.broadcasted_iota(jnp.int32, (HG_CHUNK, HG_CHUNK), 1)
    causal64 = c64 <= r64

    def intra_scores(q, b, kk, h, include_diag):
        cs = slice(h * HG_DK, (h + 1) * HG_DK)
        p_rows = []
        for s_i in range(nsub):
            i0 = s_i * HG_SUB
            if s_i == 0 and not include_diag:
                p_rows.append(jnp.zeros((HG_SUB, HG_CHUNK), F32))
                continue
            r = b[i0 - 1:i0, cs] if s_i else jnp.zeros((1, HG_DK), F32)
            hi = i0 + HG_SUB if include_diag else i0
            cap = HG_SAFE_DECAY if include_diag else 0.0
            qt = q[i0:i0 + HG_SUB, cs] * jnp.exp(b[i0:i0 + HG_SUB, cs] - r)
            kt = jnp.where(rows64 < hi, kk[:, cs] * jnp.exp(jnp.minimum(r - b[:, cs], cap)), 0.0)
            p_rows.append(_dot_nt(qt.astype(BF16), kt.astype(BF16)))
        p = jnp.concatenate(p_rows, axis=0)
        return jnp.where(causal64, p, 0.0) if include_diag else p

    def chunk_body(n, carry):
        r0 = pl.multiple_of(n * HG_CHUNK, HG_CHUNK)
        rows = pl.ds(r0, HG_CHUNK)
        q = q_ref[rows, :]
        i = i_ref[rows, :]
        b = b_ref[rows, :]
        kk = k_ref[rows, :]
        blast = b[HG_CHUNK - 1:HG_CHUNK, :]
        eb = jnp.exp(b)
        kd = kk * jnp.exp(blast - b)
        o_parts = []
        for h in range(HG_HEADS):
            cs = slice(h * HG_DK, (h + 1) * HG_DK)
            st = st_ref[h]
            o_parts.append(_dot_nt((q[:, cs] * eb[:, cs]).astype(BF16), st.astype(BF16)))
            st_ref[h] = st * jnp.exp(blast[:, cs]) + _dot(i[:, cs].T.astype(BF16), kd[:, cs].astype(BF16))
        o_ref[rows, :] = jnp.concatenate(o_parts, axis=-1)

        @pl.when(safe)
        def _():
            outs = []
            for h in range(HG_HEADS):
                cs = slice(h * HG_DK, (h + 1) * HG_DK)
                p = intra_scores(q, b, kk, h, True)
                outs.append(_dot(p.astype(BF16), i[:, cs].astype(BF16)))
            o_ref[rows, :] += jnp.concatenate(outs, axis=-1)

        @pl.when(jnp.logical_not(safe))
        def _():
            outs = []
            for h in range(HG_HEADS):
                cs = slice(h * HG_DK, (h + 1) * HG_DK)
                p = intra_scores(q, b, kk, h, False)
                outs.append(_dot(p.astype(BF16), i[:, cs].astype(BF16)))
            o_ref[rows, :] += jnp.concatenate(outs, axis=-1)
            for s_i in range(nsub):
                base = r0 + s_i * HG_SUB
                srows = pl.ds(base, HG_SUB)
                q_i = q_ref[srows, :]
                b_i = b_ref[srows, :]

                def s_body(s, acc):
                    b_s = b_ref[pl.ds(base + s, 1), :]
                    k_s = k_ref[pl.ds(base + s, 1), :]
                    i_s = i_ref[pl.ds(base + s, 1), :]
                    e = jnp.where(rows16 >= s, jnp.exp(jnp.minimum(b_i - b_s, 0.0)), 0.0)
                    pr = q_i * e * k_s
                    parts = []
                    for h in range(HG_HEADS):
                        cs = slice(h * HG_DK, (h + 1) * HG_DK)
                        parts.append(jnp.sum(pr[:, cs], axis=1, keepdims=True) * i_s[:, cs])
                    return acc + jnp.concatenate(parts, axis=-1)

                acc = lax.fori_loop(0, HG_SUB, s_body, jnp.zeros((HG_SUB, HG_WIDTH), F32))
                o_ref[srows, :] += acc

        return carry

    lax.fori_loop(0, nchunk, chunk_body, 0)

    @pl.when(pl.program_id(1) == pl.num_programs(1) - 1)
    def _():
        for h in range(HG_HEADS):
            s_ref[h] = st_ref[h].T


def _hgrn_prompt(proj, lbl, layer, tc):
    b, t, _ = proj.shape
    row = lambda col: pl.BlockSpec((None, tc, COL), lambda a, i: (a, i, col))
    return pl.pallas_call(
        functools.partial(_hgrn_prompt_kernel, layer=layer),
        grid=(b, t // tc),
        in_specs=[pl.BlockSpec((DEPTH, 1, HG_WIDTH), lambda a, i: (0, 0, 0)), row(0), row(1), row(2)],
        out_specs=[pl.BlockSpec((None, tc, HG_WIDTH), lambda a, i: (a, i, 0)),
                   pl.BlockSpec((None, HG_HEADS, HG_DK, HG_DK), lambda a, i: (a, 0, 0, 0))],
        out_shape=[jax.ShapeDtypeStruct((b, t, HG_WIDTH), F32),
                   jax.ShapeDtypeStruct((b, HG_HEADS, HG_DK, HG_DK), F32)],
        scratch_shapes=[pltpu.VMEM((HG_HEADS, HG_DK, HG_DK), F32),
                        pltpu.VMEM((tc, HG_WIDTH), F32), pltpu.VMEM((tc, HG_WIDTH), F32)],
        compiler_params=_params(("parallel", "arbitrary"), V7X_VMEM_LIMIT),
    )(lbl.reshape(DEPTH, 1, HG_WIDTH), proj, proj, proj)


def _hgrn_step_kernel(lbl_ref, qc_ref, zc_ref, ir_ref, s_ref, o_ref, so_ref, *, layer):
    lb = _hgrn_lb(lbl_ref[...], layer)[0]
    logf, k = _hgrn_gates(zc_ref[...], lb)
    outs = []
    for h in range(HG_HEADS):
        cs = slice(h * HG_DK, (h + 1) * HG_DK)
        s_new = jnp.exp(logf[h]) * s_ref[h] + k[h] * ir_ref[:, cs]
        so_ref[h] = s_new
        outs.append(jnp.sum(qc_ref[h] * s_new, axis=0, keepdims=True))
    o_ref[...] = jnp.concatenate(outs, axis=-1)


def _hgrn_step(proj, state, lbl, layer):
    b = proj.shape[0]
    qc = proj[:, 0, 0:COL].reshape(b, HG_HEADS, HG_DK, 1)
    zc = proj[:, 0, COL:2 * COL].reshape(b, HG_HEADS, HG_DK, 1)
    col = pl.BlockSpec((None, HG_HEADS, HG_DK, 1), lambda a: (a, 0, 0, 0))
    return pl.pallas_call(
        functools.partial(_hgrn_step_kernel, layer=layer),
        grid=(b,),
        in_specs=[pl.BlockSpec((DEPTH, HG_HEADS, HG_DK, 1), lambda a: (0, 0, 0, 0)), col, col,
                  pl.BlockSpec((None, 1, COL), lambda a: (a, 0, 2)),
                  pl.BlockSpec((None, None, HG_HEADS, HG_DK, HG_DK), lambda a: (a, layer, 0, 0, 0))],
        out_specs=[pl.BlockSpec((None, 1, HG_WIDTH), lambda a: (a, 0, 0)),
                   pl.BlockSpec((None, HG_HEADS, HG_DK, HG_DK), lambda a: (a, 0, 0, 0))],
        out_shape=[jax.ShapeDtypeStruct((b, 1, HG_WIDTH), F32),
                   jax.ShapeDtypeStruct((b, HG_HEADS, HG_DK, HG_DK), F32)],
        compiler_params=_params(("parallel",)),
    )(lbl.reshape(DEPTH, HG_HEADS, HG_DK, 1), qc, zc, proj, state)


def _softplus(z):
    return jnp.maximum(z, 0.0) + jnp.log1p(jnp.exp(-jnp.abs(z)))


def _sb_prompt_kernel(bias_ref, q_ref, k_ref, v_ref, o_ref, qh_ref, kt_ref, vh_ref, acc_ref, c_ref):
    tq, tk = SB_TQ, SB_TK
    nk = k_ref.shape[0] // tk
    qi = pl.program_id(1)

    @pl.when(qi == 0)
    def _():
        for n in range(nk):
            kt_full = k_ref[n * tk:(n + 1) * tk, :].T
            for h in range(SB_HEADS):
                cs = slice(h * SB_HEAD_DIM, (h + 1) * SB_HEAD_DIM)
                kt_ref[h, n] = kt_full[cs, :].astype(BF16)
                vh_ref[h, n] = v_ref[n * tk:(n + 1) * tk, cs].astype(BF16)

    for h in range(SB_HEADS):
        cs = slice(h * SB_HEAD_DIM, (h + 1) * SB_HEAD_DIM)
        qh_ref[h] = (q_ref[:, cs] * (SB_HEAD_DIM ** -0.5)).astype(BF16)
    acc_ref[...] = jnp.zeros_like(acc_ref)
    c_ref[...] = jnp.zeros_like(c_ref)

    ri = lax.broadcasted_iota(jnp.int32, (tk, tk), 0)
    ci = lax.broadcasted_iota(jnp.int32, (tk, tk), 1)
    upper = jnp.where(ri > ci, 1.0, 0.0).astype(BF16)
    jd = (qi * tq) // tk
    t_pos = qi * tq + lax.broadcasted_iota(jnp.int32, (tq, tk), 0)
    s_pos = jd * tk + lax.broadcasted_iota(jnp.int32, (tq, tk), 1)
    causal = s_pos < t_pos

    def tile(j, mask):
        for g0 in range(0, SB_HEADS, SB_HEAD_GROUP):
            heads = range(g0, g0 + SB_HEAD_GROUP)
            zs = [_dot(qh_ref[h], kt_ref[h, j]) + bias_ref[h] for h in heads]
            lbs, l1ms, splits = [], [], []
            for z in zs:
                sp = jnp.maximum(z, 0.0) + jnp.log(1.0 + jnp.exp(-jnp.abs(z)))
                l1m = -sp
                if mask is not None:
                    l1m = jnp.where(mask, l1m, 0.0)
                hi = l1m.astype(BF16)
                lo = (l1m - hi.astype(F32)).astype(BF16)
                lbs.append(z - sp)
                l1ms.append(l1m)
                splits.append(jnp.concatenate([hi, lo], axis=0))
            rs = [_dot(s, upper) for s in splits]
            weights = []
            for h, lb, l1m, r in zip(heads, lbs, l1ms, rs):
                suf = r[:tq] + r[tq:]
                c = c_ref[h]
                a = jnp.exp(lb + suf + c)
                if mask is not None:
                    a = jnp.where(mask, a, 0.0)
                weights.append(a.astype(BF16))
                c_ref[h] = c + suf[:, 0:1] + l1m[:, 0:1]
            for h, a in zip(heads, weights):
                acc_ref[h] += _dot(a, vh_ref[h, j])

    tile(jd, causal)

    def body(step, carry):
        tile(jd - 1 - step, None)
        return carry

    lax.fori_loop(0, jd, body, 0)
    for h in range(SB_HEADS):
        o_ref[:, h * SB_HEAD_DIM:(h + 1) * SB_HEAD_DIM] = acc_ref[h]


def _sb_prompt(proj, bias):
    b, t, _ = proj.shape
    tq = min(SB_TQ, t)
    nk = t // SB_TK
    seq = lambda col: pl.BlockSpec((None, t, COL), lambda a, i: (a, 0, col))
    return pl.pallas_call(
        _sb_prompt_kernel,
        grid=(b, t // tq),
        in_specs=[pl.BlockSpec(memory_space=pltpu.SMEM),
                  pl.BlockSpec((None, tq, COL), lambda a, i: (a, i, 4)), seq(5), seq(6)],
        out_specs=pl.BlockSpec((None, tq, SB_WIDTH), lambda a, i: (a, i, 0)),
        out_shape=jax.ShapeDtypeStruct((b, t, SB_WIDTH), F32),
        scratch_shapes=[pltpu.VMEM((SB_HEADS, tq, SB_HEAD_DIM), BF16),
                        pltpu.VMEM((SB_HEADS, nk, SB_HEAD_DIM, SB_TK), BF16),
                        pltpu.VMEM((SB_HEADS, nk, SB_TK, SB_HEAD_DIM), BF16),
                        pltpu.VMEM((SB_HEADS, tq, SB_HEAD_DIM), F32),
                        pltpu.VMEM((SB_HEADS, tq, 1), F32)],
        compiler_params=_params(("parallel", "arbitrary"), V7X_VMEM_LIMIT),
    )(bias, proj, proj, proj)


SB_PAGES_PER_STEP = 16


def _sb_decode_kernel(pt_ref, q_ref, bias_ref, *refs):
    npg = SB_PAGES_PER_STEP
    k_refs = refs[:npg]
    v_refs = refs[npg:2 * npg]
    o_ref, qb_ref, c_ref, acc_ref = refs[2 * npg:]
    g = pl.program_id(1)

    @pl.when(g == 0)
    def _():
        qb_ref[...] = jnp.broadcast_to(q_ref[...], qb_ref.shape)
        c_ref[...] = jnp.zeros_like(c_ref)
        acc_ref[...] = jnp.zeros_like(acc_ref)

    ri = lax.broadcasted_iota(jnp.int32, (PAGE_SIZE, PAGE_SIZE), 0)
    ci = lax.broadcasted_iota(jnp.int32, (PAGE_SIZE, PAGE_SIZE), 1)
    upper = jnp.where(ri > ci, 1.0, 0.0).astype(BF16)
    bias = bias_ref[...]
    c = c_ref[:, 0:1]
    weights = [None] * npg
    for p in range(npg - 1, -1, -1):
        z = jnp.sum(k_refs[p][...] * qb_ref[...], axis=1) + bias
        sp = _softplus(z)
        l1m = -sp
        hi = l1m.astype(BF16)
        lo = (l1m - hi.astype(F32)).astype(BF16)
        r = _dot(jnp.concatenate([hi, lo], axis=0), upper)
        suf = r[:SB_HEADS] + r[SB_HEADS:]
        weights[p] = jnp.exp(z - sp + suf + c)
        c = c + suf[:, 0:1] + l1m[:, 0:1]
    c_ref[...] = jnp.broadcast_to(c, c_ref.shape)
    for h in range(SB_HEADS):
        part = weights[0][h:h + 1, :] * v_refs[0][h]
        for p in range(1, npg):
            part = part + weights[p][h:h + 1, :] * v_refs[p][h]
        acc_ref[h] += part

    @pl.when(g == pl.num_programs(1) - 1)
    def _():
        o_ref[...] = jnp.sum(acc_ref[...], axis=2, keepdims=True)


def _sb_decode(sq, bias, cache_kt, cache_vt, page_table, layer):
    b = sq.shape[0]
    n_pages = page_table.shape[1]
    npg = SB_PAGES_PER_STEP
    ng = n_pages // npg
    qcol = (sq * (SB_HEAD_DIM ** -0.5)).reshape(b, SB_HEADS, SB_HEAD_DIM, 1)

    def page_spec(p):
        return pl.BlockSpec((None, None, SB_HEADS, SB_HEAD_DIM, PAGE_SIZE),
                            lambda a, g, pt: (pt[a, (ng - 1 - g) * npg + p], layer, 0, 0, 0))

    hd = pl.BlockSpec((None, SB_HEADS, SB_HEAD_DIM, 1), lambda a, g, pt: (a, 0, 0, 0))
    grid_spec = pltpu.PrefetchScalarGridSpec(
        num_scalar_prefetch=1,
        grid=(b, ng),
        in_specs=[hd, pl.BlockSpec((SB_HEADS, 1), lambda a, g, pt: (0, 0))]
                 + [page_spec(p) for p in range(npg)] + [page_spec(p) for p in range(npg)],
        out_specs=hd,
        scratch_shapes=[pltpu.VMEM((SB_HEADS, SB_HEAD_DIM, PAGE_SIZE), F32),
                        pltpu.VMEM((SB_HEADS, 128), F32),
                        pltpu.VMEM((SB_HEADS, SB_HEAD_DIM, PAGE_SIZE), F32)],
    )
    out = pl.pallas_call(
        _sb_decode_kernel,
        grid_spec=grid_spec,
        out_shape=jax.ShapeDtypeStruct((b, SB_HEADS, SB_HEAD_DIM, 1), F32),
        compiler_params=_params(("parallel", "arbitrary"), V7X_VMEM_LIMIT),
    )(page_table, qcol, bias.reshape(SB_HEADS, 1), *([cache_kt] * npg), *([cache_vt] * npg))
    return out.reshape(b, 1, SB_WIDTH)


def _ml_pre_kernel(x0, x1, x2, x3, cw_ref, cb_ref, wq_ref, wk_ref, wv_ref, wif_ref, bif_ref,
                   q_ref, k_ref, v_ref, g_ref):
    x = x3[...]
    cw = cw_ref[...]
    xc = _silu(x0[...] * cw[0:1] + x1[...] * cw[1:2] + x2[...] * cw[2:3] + x * cw[3:4] + cb_ref[...])
    xcb = xc.astype(BF16)
    xb = x.astype(BF16)
    qs, ks, vs = [], [], []
    for h in range(ML_HEADS):
        cs = slice(h * ML_HEAD_DIM, (h + 1) * ML_HEAD_DIM)
        qs.append(_dot(xcb[:, cs], wq_ref[h]))
        ks.append(_dot(xcb[:, cs], wk_ref[h]))
        vs.append(_dot(xb[:, cs], wv_ref[h]))
    q = jnp.concatenate(qs, axis=-1)
    k = jnp.concatenate(ks, axis=-1)
    v = jnp.concatenate(vs, axis=-1)
    q_ref[...] = q
    k_ref[...] = k
    v_ref[...] = v
    g_ref[...] = (_dot(q.astype(BF16), wif_ref[0:ML_WIDTH, :])
                  + _dot(k.astype(BF16), wif_ref[ML_WIDTH:2 * ML_WIDTH, :])
                  + _dot(v.astype(BF16), wif_ref[2 * ML_WIDTH:3 * ML_WIDTH, :]) + bif_ref[...])


def _ml_pre(xs, cw, cb, wq, wk, wv, wif, bif, tm):
    g, r, w = xs[0].shape
    row = pl.BlockSpec((None, tm, w), lambda a, i: (a, i, 0))
    full = lambda shape: pl.BlockSpec(shape, lambda a, i: (0,) * len(shape))
    out = jax.ShapeDtypeStruct((g, r, w), F32)
    return pl.pallas_call(
        _ml_pre_kernel,
        grid=(g, r // tm),
        in_specs=[row, row, row, row, full((ML_CONV, w)), full((1, w)),
                  full(wq.shape), full(wk.shape), full(wv.shape), full(wif.shape), full((1, 128))],
        out_specs=[row, row, row, pl.BlockSpec((None, tm, 128), lambda a, i: (a, i, 0))],
        out_shape=[out, out, out, jax.ShapeDtypeStruct((g, r, 128), F32)],
        compiler_params=_params(("parallel", "parallel"), V7X_VMEM_LIMIT),
    )(*xs, cw, cb.reshape(1, w), wq, wk, wv, wif, bif)


def _ml_prompt_kernel(q_ref, k_ref, v_ref, g_ref, h_ref, c_out, n_out, m_out, c_ref, n_ref, m_ref):
    L = ML_CHUNK

    @pl.when(pl.program_id(1) == 0)
    def _():
        c_ref[...] = jnp.zeros_like(c_ref)
        n_ref[...] = jnp.zeros_like(n_ref)
        m_ref[...] = jnp.zeros_like(m_ref)

    ri = lax.broadcasted_iota(jnp.int32, (L, L), 0)
    ci = lax.broadcasted_iota(jnp.int32, (L, L), 1)
    causal = ci <= ri
    tril = jnp.where(causal, 1.0, 0.0).astype(BF16)
    triu = jnp.where(ri <= ci, 1.0, 0.0).astype(BF16)

    g = g_ref[...]
    lg = _log_sigmoid(g)
    b_cols = _tri_dot(tril, lg)
    gt = g.T
    b_rows = _dot_tri(_log_sigmoid(gt), triu)
    outs = []
    for h in range(ML_HEADS):
        cs = slice(h * ML_HEAD_DIM, (h + 1) * ML_HEAD_DIM)
        q = q_ref[:, cs]
        ks = k_ref[:, cs] * (ML_HEAD_DIM ** -0.5)
        v = v_ref[:, cs]
        it_col = g[:, h:h + 1]
        it_row = gt[h:h + 1, :]
        b_col = b_cols[:, ML_HEADS + h:ML_HEADS + h + 1]
        b_row = b_rows[ML_HEADS + h:ML_HEADS + h + 1, :]
        m0 = m_ref[h:h + 1, 0:1]
        c0 = c_ref[h]
        n0 = n_ref[h:h + 1, :]
        log_d = jnp.where(causal, b_col - b_row + it_row, NEG_INF)
        m_t = jnp.maximum(b_col + m0, jnp.max(log_d, axis=1, keepdims=True))
        dm = jnp.exp(log_d - m_t)
        inter = jnp.exp(b_col + m0 - m_t)
        qb = q.astype(BF16)
        ksb = ks.astype(BF16)
        w = _dot_nt(qb, ksb) * dm
        numer = inter * _dot_nt(qb, c0.astype(BF16)) + _dot(w.astype(BF16), v.astype(BF16))
        denom = inter * jnp.sum(q * n0, axis=1, keepdims=True) + jnp.sum(w, axis=1, keepdims=True)
        outs.append(numer / jnp.maximum(jnp.abs(denom), jnp.exp(-m_t)))
        m_new = m_t[L - 1:L, :]
        b_last = b_col[L - 1:L, :]
        w_end = jnp.exp(b_last - b_col + it_col - m_new)
        dec = jnp.exp(b_last + m0 - m_new)
        c_ref[h] = dec * c0 + _dot((w_end * v).T.astype(BF16), ksb)
        n_ref[h:h + 1, :] = dec * n0 + jnp.sum(w_end * ks, axis=0, keepdims=True)
        m_ref[h:h + 1, :] = jnp.broadcast_to(m_new, (1, m_ref.shape[1]))
    h_ref[...] = jnp.concatenate(outs, axis=-1)

    @pl.when(pl.program_id(1) == pl.num_programs(1) - 1)
    def _():
        c_out[...] = c_ref[...]
        n_out[...] = n_ref[...]
        m_out[...] = m_ref[...]


def _ml_prompt(q, k, v, gates):
    b, t, w = q.shape
    row = pl.BlockSpec((None, ML_CHUNK, w), lambda a, i: (a, i, 0))
    return pl.pallas_call(
        _ml_prompt_kernel,
        grid=(b, t // ML_CHUNK),
        in_specs=[row, row, row, pl.BlockSpec((None, ML_CHUNK, 128), lambda a, i: (a, i, 0))],
        out_specs=[row,
                   pl.BlockSpec((None, ML_HEADS, ML_HEAD_DIM, ML_HEAD_DIM), lambda a, i: (a, 0, 0, 0)),
                   pl.BlockSpec((None, ML_HEADS, ML_HEAD_DIM), lambda a, i: (a, 0, 0)),
                   pl.BlockSpec((None, ML_HEADS, 128), lambda a, i: (a, 0, 0))],
        out_shape=[jax.ShapeDtypeStruct((b, t, w), F32),
                   jax.ShapeDtypeStruct((b, ML_HEADS, ML_HEAD_DIM, ML_HEAD_DIM), F32),
                   jax.ShapeDtypeStruct((b, ML_HEADS, ML_HEAD_DIM), F32),
                   jax.ShapeDtypeStruct((b, ML_HEADS, 128), F32)],
        scratch_shapes=[pltpu.VMEM((ML_HEADS, ML_HEAD_DIM, ML_HEAD_DIM), F32),
                        pltpu.VMEM((ML_HEADS, ML_HEAD_DIM), F32), pltpu.VMEM((ML_HEADS, 128), F32)],
        compiler_params=_params(("parallel", "arbitrary"), V7X_VMEM_LIMIT),
    )(q, k, v, gates)


def _ml_step_kernel(q_ref, k_ref, vc_ref, g_ref, c_ref, n_ref, m_ref, h_ref, c_out, n_out, m_out):
    g = g_ref[...]
    lane = lax.broadcasted_iota(jnp.int32, (1, 128), 1)
    m_row = jnp.zeros((1, 128), F32)
    for h in range(ML_HEADS):
        cs = slice(h * ML_HEAD_DIM, (h + 1) * ML_HEAD_DIM)
        q = q_ref[:, cs]
        ks = k_ref[:, cs] * (ML_HEAD_DIM ** -0.5)
        v = vc_ref[h]
        it = g[:, h:h + 1]
        logf = _log_sigmoid(g[:, ML_HEADS + h:ML_HEADS + h + 1])
        m0 = m_ref[:, h:h + 1]
        c0 = c_ref[h]
        n0 = n_ref[h:h + 1, :]
        m_t = jnp.maximum(logf + m0, it)
        dm = jnp.exp(it - m_t)
        inter = jnp.exp(logf + m0 - m_t)
        w = jnp.sum(q * ks, axis=1, keepdims=True) * dm
        numer = inter * jnp.sum(c0 * q, axis=1, keepdims=True) + w * v
        denom = inter * jnp.sum(q * n0, axis=1, keepdims=True) + w
        h_ref[h] = numer / jnp.maximum(jnp.abs(denom), jnp.exp(-m_t))
        w_end = jnp.exp(it - m_t)
        dec = jnp.exp(logf + m0 - m_t)
        c_out[h] = dec * c0 + (w_end * v) * ks
        n_out[h:h + 1, :] = dec * n0 + w_end * ks
        m_row = jnp.where(lane == h, m_t, m_row)
    m_out[...] = m_row


def _ml_step(q, k, v, gates, state_c, state_n, state_m, layer):
    b = q.shape[0]
    vc = v.reshape(b, ML_HEADS, ML_HEAD_DIM, 1)
    m0 = state_m[:, layer].reshape(b, 1, ML_HEADS)
    row = pl.BlockSpec((None, 1, ML_WIDTH), lambda a: (a, 0, 0))
    col = pl.BlockSpec((None, ML_HEADS, ML_HEAD_DIM, 1), lambda a: (a, 0, 0, 0))
    cspec = pl.BlockSpec((None, ML_HEADS, ML_HEAD_DIM, ML_HEAD_DIM), lambda a: (a, 0, 0, 0))
    nspec = pl.BlockSpec((None, ML_HEADS, ML_HEAD_DIM), lambda a: (a, 0, 0))
    return pl.pallas_call(
        _ml_step_kernel,
        grid=(b,),
        in_specs=[row, row, col, pl.BlockSpec((None, 1, 128), lambda a: (a, 0, 0)),
                  pl.BlockSpec((None, None, ML_HEADS, ML_HEAD_DIM, ML_HEAD_DIM),
                               lambda a: (a, layer, 0, 0, 0)),
                  pl.BlockSpec((None, None, ML_HEADS, ML_HEAD_DIM), lambda a: (a, layer, 0, 0)),
                  pl.BlockSpec((None, 1, ML_HEADS), lambda a: (a, 0, 0))],
        out_specs=[col, cspec, nspec, pl.BlockSpec((None, 1, 128), lambda a: (a, 0, 0))],
        out_shape=[jax.ShapeDtypeStruct((b, ML_HEADS, ML_HEAD_DIM, 1), F32),
                   jax.ShapeDtypeStruct((b, ML_HEADS, ML_HEAD_DIM, ML_HEAD_DIM), F32),
                   jax.ShapeDtypeStruct((b, ML_HEADS, ML_HEAD_DIM), F32),
                   jax.ShapeDtypeStruct((b, 1, 128), F32)],
        compiler_params=_params(("parallel",)),
    )(q, k, vc, gates, state_c, state_n, m0)


def _prep_weights(p):
    bf = lambda a: a.astype(BF16)
    w = {}
    fin = p["w_ffn_in"]
    nf = D_FF // FF_CHUNK
    chunked = lambda a: bf(a.reshape(DEPTH, 2, D_MODEL, nf, FF_CHUNK).transpose(0, 1, 3, 2, 4))
    w["wg"] = chunked(fin[..., :D_FF])
    w["wu"] = chunked(fin[..., D_FF:])
    w["wd"] = bf(p["w_ffn_down"].reshape(DEPTH, 2, nf, FF_CHUNK, D_MODEL))
    w["w_in"] = bf(p["w_in"])
    for name in ("w_hg_proj", "w_sb_proj", "w_ml_proj", "w_out", "ml_wq", "ml_wk", "ml_wv"):
        w[name] = bf(p[name])
    w["wif"] = bf(jnp.pad(p["ml_w_if"], ((0, 0), (0, 0), (0, 128 - 2 * ML_HEADS))))
    w["bif"] = jnp.pad(p["ml_b_if"], ((0, 0), (0, 128 - 2 * ML_HEADS))).reshape(DEPTH, 1, 128)
    return w


def _layer(x, ada, l, p, w, tm, past):
    ffn = lambda xx, j, s: _ffn(xx, ada, j, p["g_pre"][l, j], p["g_post"][l, j],
                                w["wg"][l, s], w["wu"][l, s], w["wd"][l, s], tm["ffn"])
    x = ffn(x, 0, 0)
    proj = _mixin(x, ada, p["g_pre"][l, 1], w["w_in"][l], tm["mixin"])
    st = {}
    mx = proj[..., 7 * COL:8 * COL]
    if past is None:
        b, t, _ = x.shape
        o_a, st["hgrn"] = _hgrn_prompt(proj, p["hg_lb_logits"], l, tm["hgrn"])
        o_b = _sb_prompt(proj, p["sb_bias"][l])
        taps = [jnp.pad(mx, ((0, 0), (ML_CONV - 1 - j, 0), (0, 0)))[:, :t] for j in range(ML_CONV - 1)]
        q, k, v, gates = _ml_pre(taps + [mx], p["ml_conv_w"][l], p["ml_conv_b"][l], w["ml_wq"][l],
                                 w["ml_wk"][l], w["ml_wv"][l], w["wif"][l], w["bif"][l], tm["mlpre"])
        o_c, st["mc"], st["mn"], m_pad = _ml_prompt(q, k, v, gates)
        st["mm"] = m_pad[:, :, 0]
        st["mconv"] = mx[:, t - (ML_CONV - 1):]
        st["k"] = proj[..., 5 * COL:6 * COL].reshape(b, t, SB_HEADS, SB_HEAD_DIM)
        st["v"] = proj[..., 6 * COL:7 * COL].reshape(b, t, SB_HEADS, SB_HEAD_DIM)
    else:
        b = x.shape[1]
        projs = proj.reshape(b, 1, IN_COLS)
        o_a, st["hgrn"] = _hgrn_step(projs, past["hgrn"], p["hg_lb_logits"], l)
        o_b = _sb_decode(projs[:, 0, 4 * COL:5 * COL], p["sb_bias"][l], past["k"], past["v"],
                         past["page_table"], l)
        buf = past["mconv"][:, l]
        taps = [buf[:, j].reshape(1, b, ML_WIDTH) for j in range(ML_CONV - 1)]
        q, k, v, gates = _ml_pre(taps + [mx], p["ml_conv_w"][l], p["ml_conv_b"][l], w["ml_wq"][l],
                                 w["ml_wk"][l], w["ml_wv"][l], w["wif"][l], w["bif"][l], b)
        hc, st["mc"], st["mn"], m_pad = _ml_step(q.reshape(b, 1, ML_WIDTH), k.reshape(b, 1, ML_WIDTH),
                                                 v.reshape(b, 1, ML_WIDTH), gates.reshape(b, 1, 128),
                                                 past["mc"], past["mn"], past["mm"], l)
        o_a = o_a.reshape(1, b, HG_WIDTH)
        o_b = o_b.reshape(1, b, SB_WIDTH)
        o_c = hc.reshape(1, b, ML_WIDTH)
        st["mm"] = m_pad[:, 0, :ML_HEADS]
        st["mconv"] = jnp.concatenate([buf[:, 1:], mx.reshape(b, 1, ML_WIDTH)], axis=1)
        st["k"] = projs[:, :, 5 * COL:6 * COL].reshape(b, 1, SB_HEADS, SB_HEAD_DIM)
        st["v"] = projs[:, :, 6 * COL:7 * COL].reshape(b, 1, SB_HEADS, SB_HEAD_DIM)
    x = _merge(x, ada, p["g_post"][l, 1], proj, o_a, o_b, o_c, p["hg_gain"][l], p["ml_gain"][l],
               w["w_hg_proj"][l], w["w_sb_proj"][l], w["w_ml_proj"][l], w["w_out"][l], tm["merge"])
    x = ffn(x, 2, 1)
    return x, st


def _run(x, ada_all, p, w, tm, past):
    outs = []
    for l in range(DEPTH):
        x, st = _layer(x, ada_all[l], l, p, w, tm, past)
        outs.append(st)
    return x, {name: jnp.stack([o[name] for o in outs], axis=1) for name in outs[0]}


def kernel(x_prompt, x_sample, cache_sb_k, cache_sb_v, state_hgrn, state_mlstm_c, state_mlstm_n,
           state_mlstm_m, state_mlstm_conv, page_table, c_prompt, c_sample, w_ada, b_ada, g_pre, g_post,
           w_ffn_in, w_ffn_down, w_in, sb_bias, hg_lb_logits, hg_gain, ml_conv_w, ml_conv_b, ml_wq, ml_wk,
           ml_wv, ml_w_if, ml_b_if, ml_gain, w_hg_proj, w_sb_proj, w_ml_proj, w_out):
    p = dict(g_pre=g_pre, g_post=g_post, w_ffn_in=w_ffn_in, w_ffn_down=w_ffn_down, w_in=w_in,
             sb_bias=sb_bias, hg_lb_logits=hg_lb_logits, hg_gain=hg_gain, ml_conv_w=ml_conv_w,
             ml_conv_b=ml_conv_b, ml_wq=ml_wq, ml_wk=ml_wk, ml_wv=ml_wv, ml_w_if=ml_w_if,
             ml_b_if=ml_b_if, ml_gain=ml_gain, w_hg_proj=w_hg_proj, w_sb_proj=w_sb_proj,
             w_ml_proj=w_ml_proj, w_out=w_out)
    w = _prep_weights(p)
    bp, t, d = x_prompt.shape
    bs = x_sample.shape[0]
    ada = _ada(jnp.concatenate([c_prompt, c_sample], axis=0), w_ada, b_ada)
    ada_p = ada[:, :bp].reshape(DEPTH, bp, 1, N_SUB * 3 * d)
    ada_s = ada[:, bp:].reshape(DEPTH, 1, bs, N_SUB * 3 * d)

    tm_p = dict(ffn=min(t, 512), mixin=min(t, 1024), merge=min(t, 256), hgrn=min(t, 512),
                mlpre=min(t, 512))
    y_p, sp = _run(x_prompt, ada_p, p, w, tm_p, None)

    past = dict(k=jnp.transpose(cache_sb_k, (0, 1, 3, 4, 2)), v=jnp.transpose(cache_sb_v, (0, 1, 3, 4, 2)),
                page_table=page_table, hgrn=state_hgrn, mc=state_mlstm_c, mn=state_mlstm_n,
                mm=state_mlstm_m, mconv=state_mlstm_conv)
    tm_s = dict(ffn=bs, mixin=bs, merge=bs)
    y_s, ss = _run(x_sample.reshape(1, bs, d), ada_s, p, w, tm_s, past)
    y_s = y_s.reshape(bs, 1, d)
    return (y_p, y_s, sp["k"], sp["v"], ss["k"], ss["v"], sp["hgrn"], ss["hgrn"],
            sp["mc"], ss["mc"], sp["mn"], ss["mn"], sp["mm"], ss["mm"], sp["mconv"], ss["mconv"])
```

```python
import functools

import jax
import jax.numpy as jnp
from jax import lax
from jax.experimental import pallas as pl
from jax.experimental.pallas import tpu as pltpu

F32 = jnp.float32
BF16 = jnp.bfloat16

D_MODEL = 1024
DEPTH = 2
PAGE_SIZE = 128
HG_HEADS = 4
HG_DK = 128
HG_WIDTH = 512
HG_CHUNK = 64
HG_SUB = 16
HG_SAFE_DECAY = 80.0
SB_HEADS = 8
SB_HEAD_DIM = 64
SB_WIDTH = 512
SB_TQ = 128
SB_TK = 256
SB_HEAD_GROUP = 8
ML_HEADS = 4
ML_HEAD_DIM = 128
ML_WIDTH = 512
ML_CHUNK = 128
ML_CONV = 4
D_FF = 2816
FF_CHUNK = 256
FFN_RES = 0.5
N_SUB = 3
NORM_EPS = 1e-6
IN_COLS = 7680
COL = 512
V7X_VMEM_LIMIT = 56 * 1024 * 1024
NEG_INF = float("-inf")


def _dot(a, b):
    return jnp.dot(a, b, preferred_element_type=F32)


def _dot_nt(a, b):
    return lax.dot_general(a, b, (((1,), (1,)), ((), ())), preferred_element_type=F32)


def _split3(x):
    x1 = x.astype(BF16)
    r1 = x - x1.astype(F32)
    x2 = r1.astype(BF16)
    x3 = (r1 - x2.astype(F32)).astype(BF16)
    return x1, x2, x3


def _tri_dot(tri, x):
    x1, x2, x3 = _split3(x)
    return _dot(tri, x1) + _dot(tri, x2) + _dot(tri, x3)


def _dot_tri(x, tri):
    x1, x2, x3 = _split3(x)
    return _dot(x1, tri) + _dot(x2, tri) + _dot(x3, tri)


def _sigmoid(x):
    return 1.0 / (1.0 + jnp.exp(-x))


def _silu(x):
    return x * _sigmoid(x)


def _log_sigmoid(x):
    return jnp.minimum(x, 0.0) - jnp.log1p(jnp.exp(-jnp.abs(x)))


def _logaddexp(a, b):
    amax = jnp.maximum(a, b)
    delta = a - b
    return jnp.where(delta != delta, a + b, amax + jnp.log1p(jnp.exp(-jnp.abs(delta))))


def _rms(x, g):
    return x * lax.rsqrt(jnp.mean(x * x, axis=-1, keepdims=True) + NORM_EPS) * g


def _head_norm(o, gain, heads, width):
    parts = []
    for h in range(heads):
        oh = o[:, h * width:(h + 1) * width]
        parts.append(oh * lax.rsqrt(jnp.mean(oh * oh, axis=-1, keepdims=True) + NORM_EPS))
    return jnp.concatenate(parts, axis=-1) * gain


def _params(sem, vmem=None):
    return pltpu.CompilerParams(dimension_semantics=sem, vmem_limit_bytes=vmem)


def _ada_kernel(c_ref, w_ref, b_ref, o_ref):
    s = _silu(c_ref[...]).astype(BF16)
    o_ref[...] = _dot(s, w_ref[...].astype(BF16)) + b_ref[...]


def _ada(c_all, w_ada, b_ada):
    n, d = c_all.shape
    cols = w_ada.shape[-1]
    tn = 1536
    return pl.pallas_call(
        _ada_kernel,
        grid=(DEPTH, cols // tn),
        in_specs=[pl.BlockSpec((n, d), lambda l, j: (0, 0)),
                  pl.BlockSpec((None, d, tn), lambda l, j: (l, 0, j)),
                  pl.BlockSpec((None, 1, tn), lambda l, j: (l, 0, j))],
        out_specs=pl.BlockSpec((None, n, tn), lambda l, j: (l, 0, j)),
        out_shape=jax.ShapeDtypeStruct((DEPTH, n, cols), F32),
        compiler_params=_params(("parallel", "parallel"), V7X_VMEM_LIMIT),
    )(c_all, w_ada, b_ada.reshape(DEPTH, 1, cols))


def _mod_spec(ada, tm, col, extra_grid=0):
    rm = ada.shape[1]
    if extra_grid:
        if rm == 1:
            return pl.BlockSpec((None, 1, D_MODEL), lambda g, i, n: (g, 0, col))
        return pl.BlockSpec((None, tm, D_MODEL), lambda g, i, n: (g, i, col))
    if rm == 1:
        return pl.BlockSpec((None, 1, D_MODEL), lambda g, i: (g, 0, col))
    return pl.BlockSpec((None, tm, D_MODEL), lambda g, i: (g, i, col))


def _ffn_kernel(x_ref, sh_ref, sc_ref, gt_ref, gpre_ref, gpost_ref, win_ref, wd_ref, o_ref, h_ref, a_ref):
    x = x_ref[...]
    h_ref[...] = (_rms(x, gpre_ref[...]) * (1.0 + sc_ref[...]) + sh_ref[...]).astype(BF16)
    for f in range(D_FF // FF_CHUNK):
        cols = slice(f * FF_CHUNK, (f + 1) * FF_CHUNK)
        h = h_ref[...]
        g = _dot(h, win_ref[:, cols])
        u = _dot(h, win_ref[:, D_FF + f * FF_CHUNK:D_FF + (f + 1) * FF_CHUNK])
        a_ref[:, cols] = (_silu(g) * u).astype(BF16)
    y = _dot(a_ref[...], wd_ref[...])
    o_ref[...] = x + FFN_RES * (1.0 + gt_ref[...]) * _rms(y, gpost_ref[...])


def _ffn(x, ada, j, gpre, gpost, w_in, w_down, l, s, tm):
    g, r, d = x.shape
    xspec = pl.BlockSpec((None, tm, d), lambda a, i: (a, i, 0))
    vec = pl.BlockSpec((1, d), lambda a, i: (0, 0))
    res = lambda shape: pl.BlockSpec((None, None) + shape, lambda a, i: (l, s, 0, 0),
                                     pipeline_mode=pl.Buffered(1))
    return pl.pallas_call(
        _ffn_kernel,
        grid=(g, r // tm),
        in_specs=[xspec, _mod_spec(ada, tm, 3 * j), _mod_spec(ada, tm, 3 * j + 1),
                  _mod_spec(ada, tm, 3 * j + 2), vec, vec, res((d, 2 * D_FF)), res((D_FF, d))],
        out_specs=xspec,
        out_shape=jax.ShapeDtypeStruct(x.shape, F32),
        scratch_shapes=[pltpu.VMEM((tm, d), BF16), pltpu.VMEM((tm, D_FF), BF16)],
        compiler_params=_params(("parallel", "parallel"), V7X_VMEM_LIMIT),
    )(x, ada, ada, ada, gpre.reshape(1, d), gpost.reshape(1, d), w_in, w_down)


def _mixin_kernel(x_ref, sh_ref, sc_ref, gpre_ref, w_ref, o_ref, h_ref):
    @pl.when(pl.program_id(2) == 0)
    def _():
        h_ref[...] = (_rms(x_ref[...], gpre_ref[...]) * (1.0 + sc_ref[...]) + sh_ref[...]).astype(BF16)

    o_ref[...] = _dot(h_ref[...], w_ref[...])


def _mixin(x, ada, gpre, w_in, tm):
    g, r, d = x.shape
    tn = 1536
    return pl.pallas_call(
        _mixin_kernel,
        grid=(g, r // tm, IN_COLS // tn),
        in_specs=[pl.BlockSpec((None, tm, d), lambda a, i, n: (a, i, 0)),
                  _mod_spec(ada, tm, 3, 1), _mod_spec(ada, tm, 4, 1),
                  pl.BlockSpec((1, d), lambda a, i, n: (0, 0)),
                  pl.BlockSpec((d, tn), lambda a, i, n: (0, n))],
        out_specs=pl.BlockSpec((None, tm, tn), lambda a, i, n: (a, i, n)),
        out_shape=jax.ShapeDtypeStruct((g, r, IN_COLS), F32),
        scratch_shapes=[pltpu.VMEM((tm, d), BF16)],
        compiler_params=_params(("parallel", "parallel", "arbitrary"), V7X_VMEM_LIMIT),
    )(x, ada, ada, gpre.reshape(1, d), w_in)


def _merge_kernel(x_ref, gt_ref, gpost_ref, oa_ref, hg_ref, ob_ref, oc_ref, mo_ref,
                  g0, g1, g2, g3, g4, g5, hgain_ref, mgain_ref,
                  whg_ref, wsb_ref, wml_ref, wout_ref, o_ref):
    o_a = (_head_norm(oa_ref[...], hgain_ref[...], HG_HEADS, HG_DK) * _silu(hg_ref[...])).astype(BF16)
    o_b = ob_ref[...].astype(BF16)
    o_c = (_head_norm(oc_ref[...], mgain_ref[...], ML_HEADS, ML_HEAD_DIM)
           * _sigmoid(mo_ref[...])).astype(BF16)
    gates = ((g0, g2, g4), (g1, g3, g5))
    y = None
    for c in range(2):
        cs = slice(c * COL, (c + 1) * COL)
        ga, gb, gc = gates[c]
        m = (_sigmoid(ga[...]) * _dot(o_a, whg_ref[:, cs])
             + _sigmoid(gb[...]) * _dot(o_b, wsb_ref[:, cs])
             + _sigmoid(gc[...]) * _dot(o_c, wml_ref[:, cs]))
        part = _dot(m.astype(BF16), wout_ref[cs, :])
        y = part if y is None else y + part
    o_ref[...] = x_ref[...] + (1.0 + gt_ref[...]) * _rms(y, gpost_ref[...])


def _merge(x, ada, gpost, proj, o_a, o_b, o_c, hgain, mgain, whg, wsb, wml, wout, tm):
    g, r, d = x.shape
    row = lambda col: pl.BlockSpec((None, tm, COL), lambda a, i: (a, i, col))
    vec = lambda n: pl.BlockSpec((1, n), lambda a, i: (0, 0))
    wsp = lambda shape: pl.BlockSpec(shape, lambda a, i: (0, 0))
    xspec = pl.BlockSpec((None, tm, d), lambda a, i: (a, i, 0))
    return pl.pallas_call(
        _merge_kernel,
        grid=(g, r // tm),
        in_specs=[xspec, _mod_spec(ada, tm, 5), vec(d),
                  row(0), row(3), row(0), row(0), row(8),
                  row(9), row(10), row(11), row(12), row(13), row(14),
                  vec(HG_WIDTH), vec(ML_WIDTH),
                  wsp((HG_WIDTH, d)), wsp((SB_WIDTH, d)), wsp((ML_WIDTH, d)), wsp((d, d))],
        out_specs=xspec,
        out_shape=jax.ShapeDtypeStruct(x.shape, F32),
        compiler_params=_params(("parallel", "parallel"), V7X_VMEM_LIMIT),
    )(x, ada, gpost.reshape(1, d), o_a, proj, o_b, o_c, proj,
      proj, proj, proj, proj, proj, proj,
      hgain.reshape(1, HG_WIDTH), mgain.reshape(1, ML_WIDTH), whg, wsb, wml, wout)


def _hgrn_lb(lbl, layer):
    e = jnp.exp(lbl - jnp.max(lbl, axis=0, keepdims=True))
    p = e / jnp.sum(e, axis=0, keepdims=True)
    lb = jnp.zeros_like(p[0:1])
    for r in range(1, layer + 1):
        lb = lb + p[r:r + 1]
    return lb


def _hgrn_gates(z, lb):
    logf = _logaddexp(jnp.log(lb), jnp.log1p(-lb) + _log_sigmoid(z))
    k = (1.0 - lb) * _sigmoid(-z)
    return logf, k


def _hgrn_prompt_kernel(lbl_ref, q_ref, z_ref, i_ref, o_ref, s_ref, st_ref, b_ref, k_ref, *, layer):
    tc = q_ref.shape[0]
    nchunk = tc // HG_CHUNK
    nsub = HG_CHUNK // HG_SUB

    @pl.when(pl.program_id(1) == 0)
    def _():
        st_ref[...] = jnp.zeros_like(st_ref)

    lb = _hgrn_lb(lbl_ref[...], layer)[0]
    logf, k = _hgrn_gates(z_ref[...], lb)
    k_ref[...] = k
    ri = lax.broadcasted_iota(jnp.int32, (tc, tc), 0)
    ci = lax.broadcasted_iota(jnp.int32, (tc, tc), 1)
    tri = jnp.where((ci <= ri) & (ri // HG_CHUNK == ci // HG_CHUNK), 1.0, 0.0).astype(BF16)
    b_ref[...] = _tri_dot(tri, logf)
    block_decay = -jnp.sum(logf.reshape(tc // HG_SUB, HG_SUB, HG_WIDTH), axis=1)
    safe = jnp.max(block_decay) < HG_SAFE_DECAY

    rows64 = lax.broadcasted_iota(jnp.int32, (HG_CHUNK, 1), 0)
    rows16 = lax.broadcasted_iota(jnp.int32, (HG_SUB, 1), 0)
    r64 = lax.broadcasted_iota(jnp.int32, (HG_CHUNK, HG_CHUNK), 0)
    c64 = lax.broadcasted_iota(jnp.int32, (HG_CHUNK, HG_CHUNK), 1)
    causal64 = c64 <= r64

    def intra_scores(q, b, kk, h, include_diag):
        cs = slice(h * HG_DK, (h + 1) * HG_DK)
        p_rows = []
        for s_i in range(nsub):
            i0 = s_i * HG_SUB
            if s_i == 0 and not include_diag:
                p_rows.append(jnp.zeros((HG_SUB, HG_CHUNK), F32))
                continue
            r = b[i0 - 1:i0, cs] if s_i else jnp.zeros((1, HG_DK), F32)
            hi = i0 + HG_SUB if include_diag else i0
            cap = HG_SAFE_DECAY if include_diag else 0.0
            qt = q[i0:i0 + HG_SUB, cs] * jnp.exp(b[i0:i0 + HG_SUB, cs] - r)
            kt = jnp.where(rows64 < hi, kk[:, cs] * jnp.exp(jnp.minimum(r - b[:, cs], cap)), 0.0)
            p_rows.append(_dot_nt(qt.astype(BF16), kt.astype(BF16)))
        p = jnp.concatenate(p_rows, axis=0)
        return jnp.where(causal64, p, 0.0) if include_diag else p

    def chunk_body(n, carry):
        r0 = pl.multiple_of(n * HG_CHUNK, HG_CHUNK)
        rows = pl.ds(r0, HG_CHUNK)
        q = q_ref[rows, :]
        i = i_ref[rows, :]
        b = b_ref[rows, :]
        kk = k_ref[rows, :]
        blast = b[HG_CHUNK - 1:HG_CHUNK, :]
        eb = jnp.exp(b)
        kd = kk * jnp.exp(blast - b)
        o_parts = []
        for h in range(HG_HEADS):
            cs = slice(h * HG_DK, (h + 1) * HG_DK)
            st = st_ref[h]
            o_parts.append(_dot_nt((q[:, cs] * eb[:, cs]).astype(BF16), st.astype(BF16)))
            st_ref[h] = st * jnp.exp(blast[:, cs]) + _dot(i[:, cs].T.astype(BF16), kd[:, cs].astype(BF16))
        o_ref[rows, :] = jnp.concatenate(o_parts, axis=-1)

        @pl.when(safe)
        def _():
            outs = []
            for h in range(HG_HEADS):
                cs = slice(h * HG_DK, (h + 1) * HG_DK)
                p = intra_scores(q, b, kk, h, True)
                outs.append(_dot(p.astype(BF16), i[:, cs].astype(BF16)))
            o_ref[rows, :] += jnp.concatenate(outs, axis=-1)

        @pl.when(jnp.logical_not(safe))
        def _():
            outs = []
            for h in range(HG_HEADS):
                cs = slice(h * HG_DK, (h + 1) * HG_DK)
                p = intra_scores(q, b, kk, h, False)
                outs.append(_dot(p.astype(BF16), i[:, cs].astype(BF16)))
            o_ref[rows, :] += jnp.concatenate(outs, axis=-1)
            for s_i in range(nsub):
                base = r0 + s_i * HG_SUB
                srows = pl.ds(base, HG_SUB)
                q_i = q_ref[srows, :]
                b_i = b_ref[srows, :]

                def s_body(s, acc):
                    b_s = b_ref[pl.ds(base + s, 1), :]
                    k_s = k_ref[pl.ds(base + s, 1), :]
                    i_s = i_ref[pl.ds(base + s, 1), :]
                    e = jnp.where(rows16 >= s, jnp.exp(jnp.minimum(b_i - b_s, 0.0)), 0.0)
                    pr = q_i * e * k_s
                    parts = []
                    for h in range(HG_HEADS):
                        cs = slice(h * HG_DK, (h + 1) * HG_DK)
                        parts.append(jnp.sum(pr[:, cs], axis=1, keepdims=True) * i_s[:, cs])
                    return acc + jnp.concatenate(parts, axis=-1)

                acc = lax.fori_loop(0, HG_SUB, s_body, jnp.zeros((HG_SUB, HG_WIDTH), F32))
                o_ref[srows, :] += acc

        return carry

    lax.fori_loop(0, nchunk, chunk_body, 0)

    @pl.when(pl.program_id(1) == pl.num_programs(1) - 1)
    def _():
        for h in range(HG_HEADS):
            s_ref[h] = st_ref[h].T


def _hgrn_prompt(proj, lbl, layer, tc):
    b, t, _ = proj.shape
    row = lambda col: pl.BlockSpec((None, tc, COL), lambda a, i: (a, i, col))
    return pl.pallas_call(
        functools.partial(_hgrn_prompt_kernel, layer=layer),
        grid=(b, t // tc),
        in_specs=[pl.BlockSpec((DEPTH, 1, HG_WIDTH), lambda a, i: (0, 0, 0)), row(0), row(1), row(2)],
        out_specs=[pl.BlockSpec((None, tc, HG_WIDTH), lambda a, i: (a, i, 0)),
                   pl.BlockSpec((None, HG_HEADS, HG_DK, HG_DK), lambda a, i: (a, 0, 0, 0))],
        out_shape=[jax.ShapeDtypeStruct((b, t, HG_WIDTH), F32),
                   jax.ShapeDtypeStruct((b, HG_HEADS, HG_DK, HG_DK), F32)],
        scratch_shapes=[pltpu.VMEM((HG_HEADS, HG_DK, HG_DK), F32),
                        pltpu.VMEM((tc, HG_WIDTH), F32), pltpu.VMEM((tc, HG_WIDTH), F32)],
        compiler_params=_params(("parallel", "arbitrary"), V7X_VMEM_LIMIT),
    )(lbl.reshape(DEPTH, 1, HG_WIDTH), proj, proj, proj)


def _hgrn_step_kernel(lbl_ref, qc_ref, zc_ref, ir_ref, s_ref, o_ref, so_ref, *, layer):
    lb = _hgrn_lb(lbl_ref[...], layer)[0]
    logf, k = _hgrn_gates(zc_ref[...], lb)
    outs = []
    for h in range(HG_HEADS):
        cs = slice(h * HG_DK, (h + 1) * HG_DK)
        s_new = jnp.exp(logf[h]) * s_ref[h] + k[h] * ir_ref[:, cs]
        so_ref[h] = s_new
        outs.append(jnp.sum(qc_ref[h] * s_new, axis=0, keepdims=True))
    o_ref[...] = jnp.concatenate(outs, axis=-1)


def _hgrn_step(proj, state, lbl, layer):
    b = proj.shape[0]
    qc = proj[:, 0, 0:COL].reshape(b, HG_HEADS, HG_DK, 1)
    zc = proj[:, 0, COL:2 * COL].reshape(b, HG_HEADS, HG_DK, 1)
    col = pl.BlockSpec((None, HG_HEADS, HG_DK, 1), lambda a: (a, 0, 0, 0))
    return pl.pallas_call(
        functools.partial(_hgrn_step_kernel, layer=layer),
        grid=(b,),
        in_specs=[pl.BlockSpec((DEPTH, HG_HEADS, HG_DK, 1), lambda a: (0, 0, 0, 0)), col, col,
                  pl.BlockSpec((None, 1, COL), lambda a: (a, 0, 2)),
                  pl.BlockSpec((None, None, HG_HEADS, HG_DK, HG_DK), lambda a: (a, layer, 0, 0, 0))],
        out_specs=[pl.BlockSpec((None, 1, HG_WIDTH), lambda a: (a, 0, 0)),
                   pl.BlockSpec((None, HG_HEADS, HG_DK, HG_DK), lambda a: (a, 0, 0, 0))],
        out_shape=[jax.ShapeDtypeStruct((b, 1, HG_WIDTH), F32),
                   jax.ShapeDtypeStruct((b, HG_HEADS, HG_DK, HG_DK), F32)],
        compiler_params=_params(("parallel",)),
    )(lbl.reshape(DEPTH, HG_HEADS, HG_DK, 1), qc, zc, proj, state)


def _softplus(z):
    return jnp.maximum(z, 0.0) + jnp.log1p(jnp.exp(-jnp.abs(z)))


def _sb_prompt_kernel(bias_ref, q_ref, k_ref, v_ref, o_ref, kt_out, vt_out,
                      qh_ref, kt_ref, vh_ref, acc_ref, c_ref):
    tq, tk = SB_TQ, SB_TK
    nk = k_ref.shape[0] // tk
    qi = pl.program_id(1)

    @pl.when(qi == 0)
    def _():
        for n in range(nk):
            ks = slice(n * tk, (n + 1) * tk)
            kt_full = k_ref[ks, :].T
            kt_out[:, ks] = kt_full
            vt_out[:, ks] = v_ref[ks, :].T
            for h in range(SB_HEADS):
                cs = slice(h * SB_HEAD_DIM, (h + 1) * SB_HEAD_DIM)
                kt_ref[h, n] = kt_full[cs, :].astype(BF16)
                vh_ref[h, n] = v_ref[ks, cs].astype(BF16)

    for h in range(SB_HEADS):
        cs = slice(h * SB_HEAD_DIM, (h + 1) * SB_HEAD_DIM)
        qh_ref[h] = (q_ref[:, cs] * (SB_HEAD_DIM ** -0.5)).astype(BF16)
    acc_ref[...] = jnp.zeros_like(acc_ref)
    c_ref[...] = jnp.zeros_like(c_ref)

    ri = lax.broadcasted_iota(jnp.int32, (tk, tk), 0)
    ci = lax.broadcasted_iota(jnp.int32, (tk, tk), 1)
    upper = jnp.where(ri > ci, 1.0, 0.0).astype(BF16)
    jd = (qi * tq) // tk
    t_pos = qi * tq + lax.broadcasted_iota(jnp.int32, (tq, tk), 0)
    s_pos = jd * tk + lax.broadcasted_iota(jnp.int32, (tq, tk), 1)
    causal = s_pos < t_pos

    def tile(j, mask):
        for g0 in range(0, SB_HEADS, SB_HEAD_GROUP):
            heads = range(g0, g0 + SB_HEAD_GROUP)
            zs = [_dot(qh_ref[h], kt_ref[h, j]) + bias_ref[h] for h in heads]
            lbs, l1ms = [], []
            for z in zs:
                sp = jnp.maximum(z, 0.0) + jnp.log(1.0 + jnp.exp(-jnp.abs(z)))
                l1m = -sp
                if mask is not None:
                    l1m = jnp.where(mask, l1m, 0.0)
                lbs.append(z - sp)
                l1ms.append(l1m)
            sufs = [_dot(l1m.astype(BF16), upper) for l1m in l1ms]
            weights = []
            for h, lb, l1m, suf in zip(heads, lbs, l1ms, sufs):
                c = c_ref[h]
                a = jnp.exp(lb + suf + c)
                if mask is not None:
                    a = jnp.where(mask, a, 0.0)
                weights.append(a.astype(BF16))
                c_ref[h] = c + suf[:, 0:1] + l1m[:, 0:1]
            for h, a in zip(heads, weights):
                acc_ref[h] += _dot(a, vh_ref[h, j])

    tile(jd, causal)

    def body(step, carry):
        tile(jd - 1 - step, None)
        return carry

    lax.fori_loop(0, jd, body, 0)
    for h in range(SB_HEADS):
        o_ref[:, h * SB_HEAD_DIM:(h + 1) * SB_HEAD_DIM] = acc_ref[h]


def _sb_prompt(proj, bias):
    b, t, _ = proj.shape
    tq = min(SB_TQ, t)
    nk = t // SB_TK
    seq = lambda col: pl.BlockSpec((None, t, COL), lambda a, i: (a, 0, col))
    return pl.pallas_call(
        _sb_prompt_kernel,
        grid=(b, t // tq),
        in_specs=[pl.BlockSpec(memory_space=pltpu.SMEM),
                  pl.BlockSpec((None, tq, COL), lambda a, i: (a, i, 4)), seq(5), seq(6)],
        out_specs=[pl.BlockSpec((None, tq, SB_WIDTH), lambda a, i: (a, i, 0)),
                   pl.BlockSpec((None, SB_WIDTH, t), lambda a, i: (a, 0, 0)),
                   pl.BlockSpec((None, SB_WIDTH, t), lambda a, i: (a, 0, 0))],
        out_shape=[jax.ShapeDtypeStruct((b, t, SB_WIDTH), F32),
                   jax.ShapeDtypeStruct((b, SB_WIDTH, t), F32),
                   jax.ShapeDtypeStruct((b, SB_WIDTH, t), F32)],
        scratch_shapes=[pltpu.VMEM((SB_HEADS, tq, SB_HEAD_DIM), BF16),
                        pltpu.VMEM((SB_HEADS, nk, SB_HEAD_DIM, SB_TK), BF16),
                        pltpu.VMEM((SB_HEADS, nk, SB_TK, SB_HEAD_DIM), BF16),
                        pltpu.VMEM((SB_HEADS, tq, SB_HEAD_DIM), F32),
                        pltpu.VMEM((SB_HEADS, tq, 1), F32)],
        compiler_params=_params(("parallel", "arbitrary"), V7X_VMEM_LIMIT),
    )(bias, proj, proj, proj)


SB_PAGES_PER_STEP = 16


def _sb_decode_kernel(pt_ref, q_ref, bias_ref, *refs):
    npg = SB_PAGES_PER_STEP
    k_refs = refs[:npg]
    v_refs = refs[npg:2 * npg]
    o_ref, qb_ref, c_ref, acc_ref = refs[2 * npg:]
    g = pl.program_id(1)

    @pl.when(g == 0)
    def _():
        qb_ref[...] = jnp.broadcast_to(q_ref[...], qb_ref.shape)
        c_ref[...] = jnp.zeros_like(c_ref)
        acc_ref[...] = jnp.zeros_like(acc_ref)

    ri = lax.broadcasted_iota(jnp.int32, (PAGE_SIZE, PAGE_SIZE), 0)
    ci = lax.broadcasted_iota(jnp.int32, (PAGE_SIZE, PAGE_SIZE), 1)
    upper = jnp.where(ri > ci, 1.0, 0.0).astype(BF16)
    bias = bias_ref[...]
    c = c_ref[:, 0:1]
    weights = [None] * npg
    for p in range(npg - 1, -1, -1):
        z = jnp.sum(k_refs[p][...] * qb_ref[...], axis=1) + bias
        sp = _softplus(z)
        l1m = -sp
        suf = _dot(l1m.astype(BF16), upper)
        weights[p] = jnp.exp(z - sp + suf + c)
        c = c + suf[:, 0:1] + l1m[:, 0:1]
    c_ref[...] = jnp.broadcast_to(c, c_ref.shape)
    for h in range(SB_HEADS):
        part = weights[0][h:h + 1, :] * v_refs[0][h]
        for p in range(1, npg):
            part = part + weights[p][h:h + 1, :] * v_refs[p][h]
        acc_ref[h] += part

    @pl.when(g == pl.num_programs(1) - 1)
    def _():
        o_ref[...] = jnp.sum(acc_ref[...], axis=2, keepdims=True)


def _sb_decode(sq, bias, cache_kt, cache_vt, page_table, layer):
    b = sq.shape[0]
    n_pages = page_table.shape[1]
    npg = SB_PAGES_PER_STEP
    ng = n_pages // npg
    qcol = (sq * (SB_HEAD_DIM ** -0.5)).reshape(b, SB_HEADS, SB_HEAD_DIM, 1)

    def page_spec(p):
        return pl.BlockSpec((None, None, SB_HEADS, SB_HEAD_DIM, PAGE_SIZE),
                            lambda a, g, pt: (pt[a, (ng - 1 - g) * npg + p], layer, 0, 0, 0))

    hd = pl.BlockSpec((None, SB_HEADS, SB_HEAD_DIM, 1), lambda a, g, pt: (a, 0, 0, 0))
    grid_spec = pltpu.PrefetchScalarGridSpec(
        num_scalar_prefetch=1,
        grid=(b, ng),
        in_specs=[hd, pl.BlockSpec((SB_HEADS, 1), lambda a, g, pt: (0, 0))]
                 + [page_spec(p) for p in range(npg)] + [page_spec(p) for p in range(npg)],
        out_specs=hd,
        scratch_shapes=[pltpu.VMEM((SB_HEADS, SB_HEAD_DIM, PAGE_SIZE), F32),
                        pltpu.VMEM((SB_HEADS, 128), F32),
                        pltpu.VMEM((SB_HEADS, SB_HEAD_DIM, PAGE_SIZE), F32)],
    )
    out = pl.pallas_call(
        _sb_decode_kernel,
        grid_spec=grid_spec,
        out_shape=jax.ShapeDtypeStruct((b, SB_HEADS, SB_HEAD_DIM, 1), F32),
        compiler_params=_params(("parallel", "arbitrary"), V7X_VMEM_LIMIT),
    )(page_table, qcol, bias.reshape(SB_HEADS, 1), *([cache_kt] * npg), *([cache_vt] * npg))
    return out.reshape(b, 1, SB_WIDTH)


def _ml_pre_kernel(x0, x1, x2, x3, *refs):
    _ml_pre_body(x0[...], x1[...], x2[...], x3[...], *refs)


def _ml_pre_seq_kernel(prev_ref, x_ref, cw_ref, cb_ref, wq_ref, wk_ref, wv_ref, wif_ref, bif_ref,
                       q_ref, k_ref, v_ref, g_ref, xs_ref):
    tm = x_ref.shape[0]
    halo = prev_ref.shape[0]
    x = x_ref[...]
    xs_ref[halo:, :] = x
    xs_ref[:halo, :] = jnp.where(pl.program_id(1) == 0, 0.0, prev_ref[...])
    taps = [xs_ref[pl.ds(halo - (ML_CONV - 1) + j, tm), :] for j in range(ML_CONV - 1)]
    _ml_pre_body(*taps, x, cw_ref, cb_ref, wq_ref, wk_ref, wv_ref, wif_ref, bif_ref,
                 q_ref, k_ref, v_ref, g_ref)


def _ml_pre_body(x0, x1, x2, x, cw_ref, cb_ref, wq_ref, wk_ref, wv_ref, wif_ref, bif_ref,
                 q_ref, k_ref, v_ref, g_ref):
    cw = cw_ref[...]
    xc = _silu(x0 * cw[0:1] + x1 * cw[1:2] + x2 * cw[2:3] + x * cw[3:4] + cb_ref[...])
    xcb = xc.astype(BF16)
    xb = x.astype(BF16)
    qs, ks, vs = [], [], []
    for h in range(ML_HEADS):
        cs = slice(h * ML_HEAD_DIM, (h + 1) * ML_HEAD_DIM)
        qs.append(_dot(xcb[:, cs], wq_ref[h]))
        ks.append(_dot(xcb[:, cs], wk_ref[h]))
        vs.append(_dot(xb[:, cs], wv_ref[h]))
    q = jnp.concatenate(qs, axis=-1)
    k = jnp.concatenate(ks, axis=-1)
    v = jnp.concatenate(vs, axis=-1)
    q_ref[...] = q
    k_ref[...] = k
    v_ref[...] = v
    g_ref[...] = (_dot(q.astype(BF16), wif_ref[0:ML_WIDTH, :])
                  + _dot(k.astype(BF16), wif_ref[ML_WIDTH:2 * ML_WIDTH, :])
                  + _dot(v.astype(BF16), wif_ref[2 * ML_WIDTH:3 * ML_WIDTH, :]) + bif_ref[...])


def _ml_pre(xs, cw, cb, wq, wk, wv, wif, bif, tm):
    g, r, w = xs[0].shape
    row = pl.BlockSpec((None, tm, w), lambda a, i: (a, i, 0))
    full = lambda shape: pl.BlockSpec(shape, lambda a, i: (0,) * len(shape))
    out = jax.ShapeDtypeStruct((g, r, w), F32)
    return pl.pallas_call(
        _ml_pre_kernel,
        grid=(g, r // tm),
        in_specs=[row, row, row, row, full((ML_CONV, w)), full((1, w)),
                  full(wq.shape), full(wk.shape), full(wv.shape), full(wif.shape), full((1, 128))],
        out_specs=[row, row, row, pl.BlockSpec((None, tm, 128), lambda a, i: (a, i, 0))],
        out_shape=[out, out, out, jax.ShapeDtypeStruct((g, r, 128), F32)],
        compiler_params=_params(("parallel", "parallel"), V7X_VMEM_LIMIT),
    )(*xs, cw, cb.reshape(1, w), wq, wk, wv, wif, bif)


ML_HALO = 8


def _ml_pre_seq(proj, cw, cb, wq, wk, wv, wif, bif, tm):
    b, t, _ = proj.shape
    w = ML_WIDTH
    per = tm // ML_HALO
    row = pl.BlockSpec((None, tm, w), lambda a, i: (a, i, 0))
    full = lambda shape: pl.BlockSpec(shape, lambda a, i: (0,) * len(shape))
    out = jax.ShapeDtypeStruct((b, t, w), F32)
    return pl.pallas_call(
        _ml_pre_seq_kernel,
        grid=(b, t // tm),
        in_specs=[pl.BlockSpec((None, ML_HALO, w), lambda a, i: (a, jnp.maximum(i * per - 1, 0), 7)),
                  pl.BlockSpec((None, tm, w), lambda a, i: (a, i, 7)),
                  full((ML_CONV, w)), full((1, w)),
                  full(wq.shape), full(wk.shape), full(wv.shape), full(wif.shape), full((1, 128))],
        out_specs=[row, row, row, pl.BlockSpec((None, tm, 128), lambda a, i: (a, i, 0))],
        out_shape=[out, out, out, jax.ShapeDtypeStruct((b, t, 128), F32)],
        scratch_shapes=[pltpu.VMEM((tm + ML_HALO, w), F32)],
        compiler_params=_params(("parallel", "parallel"), V7X_VMEM_LIMIT),
    )(proj, proj, cw, cb.reshape(1, w), wq, wk, wv, wif, bif)


def _ml_prompt_kernel(q_ref, k_ref, v_ref, g_ref, h_ref, c_out, n_out, m_out, c_ref, n_ref, m_ref):
    L = ML_CHUNK

    @pl.when(pl.program_id(1) == 0)
    def _():
        c_ref[...] = jnp.zeros_like(c_ref)
        n_ref[...] = jnp.zeros_like(n_ref)
        m_ref[...] = jnp.zeros_like(m_ref)

    ri = lax.broadcasted_iota(jnp.int32, (L, L), 0)
    ci = lax.broadcasted_iota(jnp.int32, (L, L), 1)
    causal = ci <= ri
    tril = jnp.where(causal, 1.0, 0.0).astype(BF16)
    triu = jnp.where(ri <= ci, 1.0, 0.0).astype(BF16)

    g = g_ref[...]
    lg = _log_sigmoid(g)
    b_cols = _tri_dot(tril, lg)
    gt = g.T
    b_rows = _dot_tri(_log_sigmoid(gt), triu)
    heads = range(ML_HEADS)
    sl = [slice(h * ML_HEAD_DIM, (h + 1) * ML_HEAD_DIM) for h in heads]
    q = [q_ref[:, cs] for cs in sl]
    ks = [k_ref[:, cs] * (ML_HEAD_DIM ** -0.5) for cs in sl]
    v = [v_ref[:, cs] for cs in sl]
    qb = [x.astype(BF16) for x in q]
    ksb = [x.astype(BF16) for x in ks]
    c0 = [c_ref[h] for h in heads]
    n0 = [n_ref[h:h + 1, :] for h in heads]
    m0 = [m_ref[h:h + 1, 0:1] for h in heads]
    qk = [_dot_nt(qb[h], ksb[h]) for h in heads]
    qc = [_dot_nt(qb[h], c0[h].astype(BF16)) for h in heads]
    b_col = [b_cols[:, ML_HEADS + h:ML_HEADS + h + 1] for h in heads]
    m_t, inter, w = [], [], []
    for h in heads:
        b_row = b_rows[ML_HEADS + h:ML_HEADS + h + 1, :]
        log_d = jnp.where(causal, b_col[h] - b_row + gt[h:h + 1, :], NEG_INF)
        m_h = jnp.maximum(b_col[h] + m0[h], jnp.max(log_d, axis=1, keepdims=True))
        m_t.append(m_h)
        inter.append(jnp.exp(b_col[h] + m0[h] - m_h))
        w.append(qk[h] * jnp.exp(log_d - m_h))
    wv = [_dot(w[h].astype(BF16), v[h].astype(BF16)) for h in heads]
    outs, wend_v = [], []
    for h in heads:
        numer = inter[h] * qc[h] + wv[h]
        denom = inter[h] * jnp.sum(q[h] * n0[h], axis=1, keepdims=True) + jnp.sum(w[h], axis=1, keepdims=True)
        outs.append(numer / jnp.maximum(jnp.abs(denom), jnp.exp(-m_t[h])))
        m_new = m_t[h][L - 1:L, :]
        b_last = b_col[h][L - 1:L, :]
        w_end = jnp.exp(b_last - b_col[h] + g[:, h:h + 1] - m_new)
        dec = jnp.exp(b_last + m0[h] - m_new)
        wend_v.append((dec, (w_end * v[h]).T.astype(BF16)))
        n_ref[h:h + 1, :] = dec * n0[h] + jnp.sum(w_end * ks[h], axis=0, keepdims=True)
        m_ref[h:h + 1, :] = jnp.broadcast_to(m_new, (1, m_ref.shape[1]))
    for h in heads:
        dec, wvt = wend_v[h]
        c_ref[h] = dec * c0[h] + _dot(wvt, ksb[h])
    h_ref[...] = jnp.concatenate(outs, axis=-1)

    @pl.when(pl.program_id(1) == pl.num_programs(1) - 1)
    def _():
        c_out[...] = c_ref[...]
        n_out[...] = n_ref[...]
        m_out[...] = m_ref[...]


def _ml_prompt(q, k, v, gates):
    b, t, w = q.shape
    row = pl.BlockSpec((None, ML_CHUNK, w), lambda a, i: (a, i, 0))
    return pl.pallas_call(
        _ml_prompt_kernel,
        grid=(b, t // ML_CHUNK),
        in_specs=[row, row, row, pl.BlockSpec((None, ML_CHUNK, 128), lambda a, i: (a, i, 0))],
        out_specs=[row,
                   pl.BlockSpec((None, ML_HEADS, ML_HEAD_DIM, ML_HEAD_DIM), lambda a, i: (a, 0, 0, 0)),
                   pl.BlockSpec((None, ML_HEADS, ML_HEAD_DIM), lambda a, i: (a, 0, 0)),
                   pl.BlockSpec((None, ML_HEADS, 128), lambda a, i: (a, 0, 0))],
        out_shape=[jax.ShapeDtypeStruct((b, t, w), F32),
                   jax.ShapeDtypeStruct((b, ML_HEADS, ML_HEAD_DIM, ML_HEAD_DIM), F32),
                   jax.ShapeDtypeStruct((b, ML_HEADS, ML_HEAD_DIM), F32),
                   jax.ShapeDtypeStruct((b, ML_HEADS, 128), F32)],
        scratch_shapes=[pltpu.VMEM((ML_HEADS, ML_HEAD_DIM, ML_HEAD_DIM), F32),
                        pltpu.VMEM((ML_HEADS, ML_HEAD_DIM), F32), pltpu.VMEM((ML_HEADS, 128), F32)],
        compiler_params=_params(("parallel", "arbitrary"), V7X_VMEM_LIMIT),
    )(q, k, v, gates)


def _ml_step_kernel(q_ref, k_ref, vc_ref, g_ref, c_ref, n_ref, m_ref, h_ref, c_out, n_out, m_out):
    g = g_ref[...]
    lane = lax.broadcasted_iota(jnp.int32, (1, 128), 1)
    m_row = jnp.zeros((1, 128), F32)
    for h in range(ML_HEADS):
        cs = slice(h * ML_HEAD_DIM, (h + 1) * ML_HEAD_DIM)
        q = q_ref[:, cs]
        ks = k_ref[:, cs] * (ML_HEAD_DIM ** -0.5)
        v = vc_ref[h]
        it = g[:, h:h + 1]
        logf = _log_sigmoid(g[:, ML_HEADS + h:ML_HEADS + h + 1])
        m0 = m_ref[:, h:h + 1]
        c0 = c_ref[h]
        n0 = n_ref[h:h + 1, :]
        m_t = jnp.maximum(logf + m0, it)
        dm = jnp.exp(it - m_t)
        inter = jnp.exp(logf + m0 - m_t)
        w = jnp.sum(q * ks, axis=1, keepdims=True) * dm
        numer = inter * jnp.sum(c0 * q, axis=1, keepdims=True) + w * v
        denom = inter * jnp.sum(q * n0, axis=1, keepdims=True) + w
        h_ref[h] = numer / jnp.maximum(jnp.abs(denom), jnp.exp(-m_t))
        w_end = jnp.exp(it - m_t)
        dec = jnp.exp(logf + m0 - m_t)
        c_out[h] = dec * c0 + (w_end * v) * ks
        n_out[h:h + 1, :] = dec * n0 + w_end * ks
        m_row = jnp.where(lane == h, m_t, m_row)
    m_out[...] = m_row


def _ml_step(q, k, v, gates, state_c, state_n, state_m, layer):
    b = q.shape[0]
    vc = v.reshape(b, ML_HEADS, ML_HEAD_DIM, 1)
    m0 = state_m[:, layer].reshape(b, 1, ML_HEADS)
    row = pl.BlockSpec((None, 1, ML_WIDTH), lambda a: (a, 0, 0))
    col = pl.BlockSpec((None, ML_HEADS, ML_HEAD_DIM, 1), lambda a: (a, 0, 0, 0))
    cspec = pl.BlockSpec((None, ML_HEADS, ML_HEAD_DIM, ML_HEAD_DIM), lambda a: (a, 0, 0, 0))
    nspec = pl.BlockSpec((None, ML_HEADS, ML_HEAD_DIM), lambda a: (a, 0, 0))
    return pl.pallas_call(
        _ml_step_kernel,
        grid=(b,),
        in_specs=[row, row, col, pl.BlockSpec((None, 1, 128), lambda a: (a, 0, 0)),
                  pl.BlockSpec((None, None, ML_HEADS, ML_HEAD_DIM, ML_HEAD_DIM),
                               lambda a: (a, layer, 0, 0, 0)),
                  pl.BlockSpec((None, None, ML_HEADS, ML_HEAD_DIM), lambda a: (a, layer, 0, 0)),
                  pl.BlockSpec((None, 1, ML_HEADS), lambda a: (a, 0, 0))],
        out_specs=[col, cspec, nspec, pl.BlockSpec((None, 1, 128), lambda a: (a, 0, 0))],
        out_shape=[jax.ShapeDtypeStruct((b, ML_HEADS, ML_HEAD_DIM, 1), F32),
                   jax.ShapeDtypeStruct((b, ML_HEADS, ML_HEAD_DIM, ML_HEAD_DIM), F32),
                   jax.ShapeDtypeStruct((b, ML_HEADS, ML_HEAD_DIM), F32),
                   jax.ShapeDtypeStruct((b, 1, 128), F32)],
        compiler_params=_params(("parallel",)),
    )(q, k, vc, gates, state_c, state_n, m0)


def _prep_weights(p):
    bf = lambda a: a.astype(BF16)
    w = {}
    for name in ("w_ffn_in", "w_ffn_down", "w_in", "w_hg_proj", "w_sb_proj", "w_ml_proj", "w_out", "ml_wq", "ml_wk", "ml_wv"):
        w[name] = bf(p[name])
    w["wif"] = bf(jnp.pad(p["ml_w_if"], ((0, 0), (0, 0), (0, 128 - 2 * ML_HEADS))))
    w["bif"] = jnp.pad(p["ml_b_if"], ((0, 0), (0, 128 - 2 * ML_HEADS))).reshape(DEPTH, 1, 128)
    return w


def _layer(x, ada, l, p, w, tm, past):
    ffn = lambda xx, j, s: _ffn(xx, ada, j, p["g_pre"][l, j], p["g_post"][l, j],
                                w["w_ffn_in"], w["w_ffn_down"], l, s, tm["ffn"])
    x = ffn(x, 0, 0)
    proj = _mixin(x, ada, p["g_pre"][l, 1], w["w_in"][l], tm["mixin"])
    st = {}
    mx = proj[..., 7 * COL:8 * COL]
    if past is None:
        b, t, _ = x.shape
        o_a, st["hgrn"] = _hgrn_prompt(proj, p["hg_lb_logits"], l, tm["hgrn"])
        o_b, kt, vt = _sb_prompt(proj, p["sb_bias"][l])
        q, k, v, gates = _ml_pre_seq(proj, p["ml_conv_w"][l], p["ml_conv_b"][l], w["ml_wq"][l],
                                     w["ml_wk"][l], w["ml_wv"][l], w["wif"][l], w["bif"][l], tm["mlpre"])
        o_c, st["mc"], st["mn"], m_pad = _ml_prompt(q, k, v, gates)
        st["mm"] = m_pad[:, :, 0]
        st["mconv"] = proj[:, t - (ML_CONV - 1):, 7 * COL:8 * COL]
        st["k"] = kt.reshape(b, SB_HEADS, SB_HEAD_DIM, t)
        st["v"] = vt.reshape(b, SB_HEADS, SB_HEAD_DIM, t)
    else:
        b = x.shape[1]
        projs = proj.reshape(b, 1, IN_COLS)
        o_a, st["hgrn"] = _hgrn_step(projs, past["hgrn"], p["hg_lb_logits"], l)
        o_b = _sb_decode(projs[:, 0, 4 * COL:5 * COL], p["sb_bias"][l], past["k"], past["v"],
                         past["page_table"], l)
        buf = past["mconv"][:, l]
        taps = [buf[:, j].reshape(1, b, ML_WIDTH) for j in range(ML_CONV - 1)]
        q, k, v, gates = _ml_pre(taps + [mx], p["ml_conv_w"][l], p["ml_conv_b"][l], w["ml_wq"][l],
                                 w["ml_wk"][l], w["ml_wv"][l], w["wif"][l], w["bif"][l], b)
        hc, st["mc"], st["mn"], m_pad = _ml_step(q.reshape(b, 1, ML_WIDTH), k.reshape(b, 1, ML_WIDTH),
                                                 v.reshape(b, 1, ML_WIDTH), gates.reshape(b, 1, 128),
                                                 past["mc"], past["mn"], past["mm"], l)
        o_a = o_a.reshape(1, b, HG_WIDTH)
        o_b = o_b.reshape(1, b, SB_WIDTH)
        o_c = hc.reshape(1, b, ML_WIDTH)
        st["mm"] = m_pad[:, 0, :ML_HEADS]
        st["mconv"] = jnp.concatenate([buf[:, 1:], mx.reshape(b, 1, ML_WIDTH)], axis=1)
        st["k"] = projs[:, :, 5 * COL:6 * COL].reshape(b, 1, SB_HEADS, SB_HEAD_DIM)
        st["v"] = projs[:, :, 6 * COL:7 * COL].reshape(b, 1, SB_HEADS, SB_HEAD_DIM)
    x = _merge(x, ada, p["g_post"][l, 1], proj, o_a, o_b, o_c, p["hg_gain"][l], p["ml_gain"][l],
               w["w_hg_proj"][l], w["w_sb_proj"][l], w["w_ml_proj"][l], w["w_out"][l], tm["merge"])
    x = ffn(x, 2, 1)
    return x, st


def _run(x, ada_all, p, w, tm, past):
    outs = []
    for l in range(DEPTH):
        x, st = _layer(x, ada_all[l], l, p, w, tm, past)
        outs.append(st)
    return x, {name: jnp.stack([o[name] for o in outs], axis=1) for name in outs[0]}


def kernel(x_prompt, x_sample, cache_sb_k, cache_sb_v, state_hgrn, state_mlstm_c, state_mlstm_n,
           state_mlstm_m, state_mlstm_conv, page_table, c_prompt, c_sample, w_ada, b_ada, g_pre, g_post,
           w_ffn_in, w_ffn_down, w_in, sb_bias, hg_lb_logits, hg_gain, ml_conv_w, ml_conv_b, ml_wq, ml_wk,
           ml_wv, ml_w_if, ml_b_if, ml_gain, w_hg_proj, w_sb_proj, w_ml_proj, w_out):
    p = dict(g_pre=g_pre, g_post=g_post, w_ffn_in=w_ffn_in, w_ffn_down=w_ffn_down, w_in=w_in,
             sb_bias=sb_bias, hg_lb_logits=hg_lb_logits, hg_gain=hg_gain, ml_conv_w=ml_conv_w,
             ml_conv_b=ml_conv_b, ml_wq=ml_wq, ml_wk=ml_wk, ml_wv=ml_wv, ml_w_if=ml_w_if,
             ml_b_if=ml_b_if, ml_gain=ml_gain, w_hg_proj=w_hg_proj, w_sb_proj=w_sb_proj,
             w_ml_proj=w_ml_proj, w_out=w_out)
    w = _prep_weights(p)
    bp, t, d = x_prompt.shape
    bs = x_sample.shape[0]
    ada = _ada(jnp.concatenate([c_prompt, c_sample], axis=0), w_ada, b_ada)
    ada_p = ada[:, :bp].reshape(DEPTH, bp, 1, N_SUB * 3 * d)
    ada_s = ada[:, bp:].reshape(DEPTH, 1, bs, N_SUB * 3 * d)

    tm_p = dict(ffn=min(t, 512), mixin=min(t, 1024), merge=min(t, 256), hgrn=min(t, 512),
                mlpre=min(t, 512))
    y_p, sp = _run(x_prompt, ada_p, p, w, tm_p, None)

    past = dict(k=jnp.transpose(cache_sb_k, (0, 1, 3, 4, 2)), v=jnp.transpose(cache_sb_v, (0, 1, 3, 4, 2)),
                page_table=page_table, hgrn=state_hgrn, mc=state_mlstm_c, mn=state_mlstm_n,
                mm=state_mlstm_m, mconv=state_mlstm_conv)
    tm_s = dict(ffn=bs, mixin=bs, merge=bs)
    y_s, ss = _run(x_sample.reshape(1, bs, d), ada_s, p, w, tm_s, past)
    y_s = y_s.reshape(bs, 1, d)
    k_p = jnp.transpose(sp["k"], (0, 1, 4, 2, 3))
    v_p = jnp.transpose(sp["v"], (0, 1, 4, 2, 3))
    return (y_p, y_s, k_p, v_p, ss["k"], ss["v"], sp["hgrn"], ss["hgrn"],
            sp["mc"], ss["mc"], sp["mn"], ss["mn"], sp["mm"], ss["mm"], sp["mconv"], ss["mconv"])
```

```python
import functools

import jax
import jax.numpy as jnp
from jax import lax
from jax.experimental import pallas as pl
from jax.experimental.pallas import tpu as pltpu

F32 = jnp.float32
BF16 = jnp.bfloat16

D_MODEL = 1024
DEPTH = 2
PAGE_SIZE = 128
HG_HEADS = 4
HG_DK = 128
HG_WIDTH = 512
HG_CHUNK = 64
HG_SUB = 16
HG_SAFE_DECAY = 80.0
SB_HEADS = 8
SB_HEAD_DIM = 64
SB_WIDTH = 512
SB_TQ = 128
SB_TK = 256
SB_HEAD_GROUP = 8
ML_HEADS = 4
ML_HEAD_DIM = 128
ML_WIDTH = 512
ML_CHUNK = 128
ML_CONV = 4
D_FF = 2816
FF_CHUNK = 256
FFN_RES = 0.5
N_SUB = 3
NORM_EPS = 1e-6
IN_COLS = 7680
COL = 512
V7X_VMEM_LIMIT = 56 * 1024 * 1024
NEG_INF = float("-inf")


def _dot(a, b):
    return jnp.dot(a, b, preferred_element_type=F32)


def _dot_nt(a, b):
    return lax.dot_general(a, b, (((1,), (1,)), ((), ())), preferred_element_type=F32)


def _split3(x):
    x1 = x.astype(BF16)
    r1 = x - x1.astype(F32)
    x2 = r1.astype(BF16)
    x3 = (r1 - x2.astype(F32)).astype(BF16)
    return x1, x2, x3


def _tri_dot(tri, x):
    x1, x2, x3 = _split3(x)
    return _dot(tri, x1) + _dot(tri, x2) + _dot(tri, x3)


def _dot_tri(x, tri):
    x1, x2, x3 = _split3(x)
    return _dot(x1, tri) + _dot(x2, tri) + _dot(x3, tri)


def _sigmoid(x):
    return 1.0 / (1.0 + jnp.exp(-x))


def _silu(x):
    return x * _sigmoid(x)


def _log_sigmoid(x):
    return jnp.minimum(x, 0.0) - jnp.log1p(jnp.exp(-jnp.abs(x)))


def _logaddexp(a, b):
    amax = jnp.maximum(a, b)
    delta = a - b
    return jnp.where(delta != delta, a + b, amax + jnp.log1p(jnp.exp(-jnp.abs(delta))))


def _rms(x, g):
    return x * lax.rsqrt(jnp.mean(x * x, axis=-1, keepdims=True) + NORM_EPS) * g


def _head_norm(o, gain, heads, width):
    parts = []
    for h in range(heads):
        oh = o[:, h * width:(h + 1) * width]
        parts.append(oh * lax.rsqrt(jnp.mean(oh * oh, axis=-1, keepdims=True) + NORM_EPS))
    return jnp.concatenate(parts, axis=-1) * gain


def _params(sem, vmem=None):
    return pltpu.CompilerParams(dimension_semantics=sem, vmem_limit_bytes=vmem)


def _ada_kernel(c_ref, w_ref, b_ref, o_ref):
    s = _silu(c_ref[...]).astype(BF16)
    o_ref[...] = _dot(s, w_ref[...].astype(BF16)) + b_ref[...]


def _ada(c_all, w_ada, b_ada):
    n, d = c_all.shape
    cols = w_ada.shape[-1]
    tn = 1536
    return pl.pallas_call(
        _ada_kernel,
        grid=(DEPTH, cols // tn),
        in_specs=[pl.BlockSpec((n, d), lambda l, j: (0, 0)),
                  pl.BlockSpec((None, d, tn), lambda l, j: (l, 0, j)),
                  pl.BlockSpec((None, 1, tn), lambda l, j: (l, 0, j))],
        out_specs=pl.BlockSpec((None, n, tn), lambda l, j: (l, 0, j)),
        out_shape=jax.ShapeDtypeStruct((DEPTH, n, cols), F32),
        compiler_params=_params(("parallel", "parallel"), V7X_VMEM_LIMIT),
    )(c_all, w_ada, b_ada.reshape(DEPTH, 1, cols))


def _mod_spec(ada, tm, col, extra_grid=0):
    rm = ada.shape[1]
    if extra_grid:
        if rm == 1:
            return pl.BlockSpec((None, 1, D_MODEL), lambda g, i, n: (g, 0, col))
        return pl.BlockSpec((None, tm, D_MODEL), lambda g, i, n: (g, i, col))
    if rm == 1:
        return pl.BlockSpec((None, 1, D_MODEL), lambda g, i: (g, 0, col))
    return pl.BlockSpec((None, tm, D_MODEL), lambda g, i: (g, i, col))


def _ffn_kernel(x_ref, sh_ref, sc_ref, gt_ref, gpre_ref, gpost_ref, win_ref, wd_ref, o_ref, h_ref, a_ref):
    x = x_ref[...]
    h_ref[...] = (_rms(x, gpre_ref[...]) * (1.0 + sc_ref[...]) + sh_ref[...]).astype(BF16)
    for f in range(D_FF // FF_CHUNK):
        cols = slice(f * FF_CHUNK, (f + 1) * FF_CHUNK)
        h = h_ref[...]
        g = _dot(h, win_ref[:, cols])
        u = _dot(h, win_ref[:, D_FF + f * FF_CHUNK:D_FF + (f + 1) * FF_CHUNK])
        a_ref[:, cols] = (_silu(g) * u).astype(BF16)
    y = _dot(a_ref[...], wd_ref[...])
    o_ref[...] = x + FFN_RES * (1.0 + gt_ref[...]) * _rms(y, gpost_ref[...])


def _ffn(x, ada, j, gpre, gpost, w_in, w_down, l, s, tm):
    g, r, d = x.shape
    xspec = pl.BlockSpec((None, tm, d), lambda a, i: (a, i, 0))
    vec = pl.BlockSpec((1, d), lambda a, i: (0, 0))
    res = lambda shape: pl.BlockSpec((None, None) + shape, lambda a, i: (l, s, 0, 0),
                                     pipeline_mode=pl.Buffered(1))
    return pl.pallas_call(
        _ffn_kernel,
        grid=(g, r // tm),
        in_specs=[xspec, _mod_spec(ada, tm, 3 * j), _mod_spec(ada, tm, 3 * j + 1),
                  _mod_spec(ada, tm, 3 * j + 2), vec, vec, res((d, 2 * D_FF)), res((D_FF, d))],
        out_specs=xspec,
        out_shape=jax.ShapeDtypeStruct(x.shape, F32),
        scratch_shapes=[pltpu.VMEM((tm, d), BF16), pltpu.VMEM((tm, D_FF), BF16)],
        compiler_params=_params(("parallel", "parallel"), V7X_VMEM_LIMIT),
    )(x, ada, ada, ada, gpre.reshape(1, d), gpost.reshape(1, d), w_in, w_down)


def _mixin_kernel(x_ref, sh_ref, sc_ref, gpre_ref, w_ref, o_ref, h_ref):
    @pl.when(pl.program_id(2) == 0)
    def _():
        h_ref[...] = (_rms(x_ref[...], gpre_ref[...]) * (1.0 + sc_ref[...]) + sh_ref[...]).astype(BF16)

    o_ref[...] = _dot(h_ref[...], w_ref[...])


def _mixin(x, ada, gpre, w_in, tm):
    g, r, d = x.shape
    tn = 1536
    return pl.pallas_call(
        _mixin_kernel,
        grid=(g, r // tm, IN_COLS // tn),
        in_specs=[pl.BlockSpec((None, tm, d), lambda a, i, n: (a, i, 0)),
                  _mod_spec(ada, tm, 3, 1), _mod_spec(ada, tm, 4, 1),
                  pl.BlockSpec((1, d), lambda a, i, n: (0, 0)),
                  pl.BlockSpec((d, tn), lambda a, i, n: (0, n))],
        out_specs=pl.BlockSpec((None, tm, tn), lambda a, i, n: (a, i, n)),
        out_shape=jax.ShapeDtypeStruct((g, r, IN_COLS), F32),
        scratch_shapes=[pltpu.VMEM((tm, d), BF16)],
        compiler_params=_params(("parallel", "parallel", "arbitrary"), V7X_VMEM_LIMIT),
    )(x, ada, ada, gpre.reshape(1, d), w_in)


def _merge_kernel(x_ref, gt_ref, gpost_ref, oa_ref, hg_ref, ob_ref, oc_ref, mo_ref,
                  g0, g1, g2, g3, g4, g5, hgain_ref, mgain_ref,
                  whg_ref, wsb_ref, wml_ref, wout_ref, o_ref):
    o_a = (_head_norm(oa_ref[...], hgain_ref[...], HG_HEADS, HG_DK) * _silu(hg_ref[...])).astype(BF16)
    o_b = ob_ref[...].astype(BF16)
    o_c = (_head_norm(oc_ref[...], mgain_ref[...], ML_HEADS, ML_HEAD_DIM)
           * _sigmoid(mo_ref[...])).astype(BF16)
    gates = ((g0, g2, g4), (g1, g3, g5))
    y = None
    for c in range(2):
        cs = slice(c * COL, (c + 1) * COL)
        ga, gb, gc = gates[c]
        m = (_sigmoid(ga[...]) * _dot(o_a, whg_ref[:, cs])
             + _sigmoid(gb[...]) * _dot(o_b, wsb_ref[:, cs])
             + _sigmoid(gc[...]) * _dot(o_c, wml_ref[:, cs]))
        part = _dot(m.astype(BF16), wout_ref[cs, :])
        y = part if y is None else y + part
    o_ref[...] = x_ref[...] + (1.0 + gt_ref[...]) * _rms(y, gpost_ref[...])


def _merge(x, ada, gpost, proj, o_a, o_b, o_c, hgain, mgain, whg, wsb, wml, wout, tm):
    g, r, d = x.shape
    row = lambda col: pl.BlockSpec((None, tm, COL), lambda a, i: (a, i, col))
    vec = lambda n: pl.BlockSpec((1, n), lambda a, i: (0, 0))
    wsp = lambda shape: pl.BlockSpec(shape, lambda a, i: (0, 0))
    xspec = pl.BlockSpec((None, tm, d), lambda a, i: (a, i, 0))
    return pl.pallas_call(
        _merge_kernel,
        grid=(g, r // tm),
        in_specs=[xspec, _mod_spec(ada, tm, 5), vec(d),
                  row(0), row(3), row(0), row(0), row(8),
                  row(9), row(10), row(11), row(12), row(13), row(14),
                  vec(HG_WIDTH), vec(ML_WIDTH),
                  wsp((HG_WIDTH, d)), wsp((SB_WIDTH, d)), wsp((ML_WIDTH, d)), wsp((d, d))],
        out_specs=xspec,
        out_shape=jax.ShapeDtypeStruct(x.shape, F32),
        compiler_params=_params(("parallel", "parallel"), V7X_VMEM_LIMIT),
    )(x, ada, gpost.reshape(1, d), o_a, proj, o_b, o_c, proj,
      proj, proj, proj, proj, proj, proj,
      hgain.reshape(1, HG_WIDTH), mgain.reshape(1, ML_WIDTH), whg, wsb, wml, wout)


def _hgrn_lb(lbl, layer):
    e = jnp.exp(lbl - jnp.max(lbl, axis=0, keepdims=True))
    p = e / jnp.sum(e, axis=0, keepdims=True)
    lb = jnp.zeros_like(p[0:1])
    for r in range(1, layer + 1):
        lb = lb + p[r:r + 1]
    return lb


def _hgrn_gates(z, lb):
    logf = _logaddexp(jnp.log(lb), jnp.log1p(-lb) + _log_sigmoid(z))
    k = (1.0 - lb) * _sigmoid(-z)
    return logf, k


def _hgrn_prompt_kernel(lbl_ref, q_ref, z_ref, i_ref, o_ref, s_ref, st_ref, b_ref, k_ref, *, layer):
    tc = q_ref.shape[0]
    nchunk = tc // HG_CHUNK
    nsub = HG_CHUNK // HG_SUB

    @pl.when(pl.program_id(1) == 0)
    def _():
        st_ref[...] = jnp.zeros_like(st_ref)

    lb = _hgrn_lb(lbl_ref[...], layer)[0]
    logf, k = _hgrn_gates(z_ref[...], lb)
    k_ref[...] = k
    ri = lax.broadcasted_iota(jnp.int32, (tc, tc), 0)
    ci = lax.broadcasted_iota(jnp.int32, (tc, tc), 1)
    tri = jnp.where((ci <= ri) & (ri // HG_CHUNK == ci // HG_CHUNK), 1.0, 0.0).astype(BF16)
    b_ref[...] = _tri_dot(tri, logf)
    block_decay = -jnp.sum(logf.reshape(tc // HG_SUB, HG_SUB, HG_WIDTH), axis=1)
    safe = jnp.max(block_decay) < HG_SAFE_DECAY

    rows64 = lax.broadcasted_iota(jnp.int32, (HG_CHUNK, 1), 0)
    rows16 = lax.broadcasted_iota(jnp.int32, (HG_SUB, 1), 0)
    r64 = lax.broadcasted_iota(jnp.int32, (HG_CHUNK, HG_CHUNK), 0)
    c64 = lax.broadcasted_iota(jnp.int32, (HG_CHUNK, HG_CHUNK), 1)
    causal64 = c64 <= r64

    heads = range(HG_HEADS)
    sl = [slice(h * HG_DK, (h + 1) * HG_DK) for h in heads]

    def score_operands(q, b, kk, cs, s_i, include_diag):
        i0 = s_i * HG_SUB
        r = b[i0 - 1:i0, cs] if s_i else jnp.zeros((1, HG_DK), F32)
        hi = i0 + HG_SUB if include_diag else i0
        cap = HG_SAFE_DECAY if include_diag else 0.0
        qt = q[i0:i0 + HG_SUB, cs] * jnp.exp(b[i0:i0 + HG_SUB, cs] - r)
        kt = jnp.where(rows64 < hi, kk[:, cs] * jnp.exp(jnp.minimum(r - b[:, cs], cap)), 0.0)
        return qt.astype(BF16), kt.astype(BF16)

    def chunk(r0):
        rows = pl.ds(r0, HG_CHUNK)
        q = q_ref[rows, :]
        i = i_ref[rows, :]
        b = b_ref[rows, :]
        kk = k_ref[rows, :]
        blast = b[HG_CHUNK - 1:HG_CHUNK, :]
        eb = jnp.exp(b)
        kd = kk * jnp.exp(blast - b)
        ib = [i[:, cs].astype(BF16) for cs in sl]
        ops = [[score_operands(q, b, kk, cs, s_i, True) for s_i in range(nsub)] for cs in sl]
        p = [jnp.concatenate([_dot_nt(qt, kt) for qt, kt in ops[h]], axis=0) for h in heads]
        st = [st_ref[h] for h in heads]
        o_state = [_dot_nt((q[:, sl[h]] * eb[:, sl[h]]).astype(BF16), st[h].astype(BF16)) for h in heads]
        pm = [jnp.where(causal64 & safe, p[h], 0.0).astype(BF16) for h in heads]
        o_intra = [_dot(pm[h], ib[h]) for h in heads]
        upd = [_dot(i[:, sl[h]].T.astype(BF16), kd[:, sl[h]].astype(BF16)) for h in heads]
        o_ref[rows, :] = jnp.concatenate([o_state[h] + o_intra[h] for h in heads], axis=-1)
        for h in heads:
            st_ref[h] = st[h] * jnp.exp(blast[:, sl[h]]) + upd[h]

        @pl.when(jnp.logical_not(safe))
        def _():
            outs = []
            for h in heads:
                rows_p = [jnp.zeros((HG_SUB, HG_CHUNK), F32)]
                for s_i in range(1, nsub):
                    qt, kt = score_operands(q, b, kk, sl[h], s_i, False)
                    rows_p.append(_dot_nt(qt, kt))
                outs.append(_dot(jnp.concatenate(rows_p, axis=0).astype(BF16), ib[h]))
            o_ref[rows, :] += jnp.concatenate(outs, axis=-1)
            for s_i in range(nsub):
                base = r0 + s_i * HG_SUB
                srows = pl.ds(base, HG_SUB)
                q_i = q_ref[srows, :]
                b_i = b_ref[srows, :]

                def s_body(s, acc):
                    b_s = b_ref[pl.ds(base + s, 1), :]
                    k_s = k_ref[pl.ds(base + s, 1), :]
                    i_s = i_ref[pl.ds(base + s, 1), :]
                    e = jnp.where(rows16 >= s, jnp.exp(jnp.minimum(b_i - b_s, 0.0)), 0.0)
                    pr = q_i * e * k_s
                    parts = [jnp.sum(pr[:, cs], axis=1, keepdims=True) * i_s[:, cs] for cs in sl]
                    return acc + jnp.concatenate(parts, axis=-1)

                acc = lax.fori_loop(0, HG_SUB, s_body, jnp.zeros((HG_SUB, HG_WIDTH), F32))
                o_ref[srows, :] += acc

    def pair_body(n, carry):
        r0 = pl.multiple_of(n * (2 * HG_CHUNK), 2 * HG_CHUNK)
        chunk(r0)
        chunk(r0 + HG_CHUNK)
        return carry

    lax.fori_loop(0, nchunk // 2, pair_body, 0)

    @pl.when(pl.program_id(1) == pl.num_programs(1) - 1)
    def _():
        for h in range(HG_HEADS):
            s_ref[h] = st_ref[h].T


def _hgrn_prompt(proj, lbl, layer, tc):
    b, t, _ = proj.shape
    row = lambda col: pl.BlockSpec((None, tc, COL), lambda a, i: (a, i, col))
    return pl.pallas_call(
        functools.partial(_hgrn_prompt_kernel, layer=layer),
        grid=(b, t // tc),
        in_specs=[pl.BlockSpec((DEPTH, 1, HG_WIDTH), lambda a, i: (0, 0, 0)), row(0), row(1), row(2)],
        out_specs=[pl.BlockSpec((None, tc, HG_WIDTH), lambda a, i: (a, i, 0)),
                   pl.BlockSpec((None, HG_HEADS, HG_DK, HG_DK), lambda a, i: (a, 0, 0, 0))],
        out_shape=[jax.ShapeDtypeStruct((b, t, HG_WIDTH), F32),
                   jax.ShapeDtypeStruct((b, HG_HEADS, HG_DK, HG_DK), F32)],
        scratch_shapes=[pltpu.VMEM((HG_HEADS, HG_DK, HG_DK), F32),
                        pltpu.VMEM((tc, HG_WIDTH), F32), pltpu.VMEM((tc, HG_WIDTH), F32)],
        compiler_params=_params(("parallel", "arbitrary"), V7X_VMEM_LIMIT),
    )(lbl.reshape(DEPTH, 1, HG_WIDTH), proj, proj, proj)


def _hgrn_step_kernel(lbl_ref, qc_ref, zc_ref, ir_ref, s_ref, o_ref, so_ref, *, layer):
    lb = _hgrn_lb(lbl_ref[...], layer)[0]
    logf, k = _hgrn_gates(zc_ref[...], lb)
    outs = []
    for h in range(HG_HEADS):
        cs = slice(h * HG_DK, (h + 1) * HG_DK)
        s_new = jnp.exp(logf[h]) * s_ref[h] + k[h] * ir_ref[:, cs]
        so_ref[h] = s_new
        outs.append(jnp.sum(qc_ref[h] * s_new, axis=0, keepdims=True))
    o_ref[...] = jnp.concatenate(outs, axis=-1)


def _hgrn_step(proj, state, lbl, layer):
    b = proj.shape[0]
    qc = proj[:, 0, 0:COL].reshape(b, HG_HEADS, HG_DK, 1)
    zc = proj[:, 0, COL:2 * COL].reshape(b, HG_HEADS, HG_DK, 1)
    col = pl.BlockSpec((None, HG_HEADS, HG_DK, 1), lambda a: (a, 0, 0, 0))
    return pl.pallas_call(
        functools.partial(_hgrn_step_kernel, layer=layer),
        grid=(b,),
        in_specs=[pl.BlockSpec((DEPTH, HG_HEADS, HG_DK, 1), lambda a: (0, 0, 0, 0)), col, col,
                  pl.BlockSpec((None, 1, COL), lambda a: (a, 0, 2)),
                  pl.BlockSpec((None, None, HG_HEADS, HG_DK, HG_DK), lambda a: (a, layer, 0, 0, 0))],
        out_specs=[pl.BlockSpec((None, 1, HG_WIDTH), lambda a: (a, 0, 0)),
                   pl.BlockSpec((None, HG_HEADS, HG_DK, HG_DK), lambda a: (a, 0, 0, 0))],
        out_shape=[jax.ShapeDtypeStruct((b, 1, HG_WIDTH), F32),
                   jax.ShapeDtypeStruct((b, HG_HEADS, HG_DK, HG_DK), F32)],
        compiler_params=_params(("parallel",)),
    )(lbl.reshape(DEPTH, HG_HEADS, HG_DK, 1), qc, zc, proj, state)


def _softplus(z):
    return jnp.maximum(z, 0.0) + jnp.log1p(jnp.exp(-jnp.abs(z)))


def _sb_prompt_kernel(bias_ref, q_ref, k_ref, v_ref, o_ref, kt_out, vt_out,
                      qh_ref, kt_ref, vh_ref, acc_ref, c_ref):
    tq, tk = SB_TQ, SB_TK
    nk = k_ref.shape[0] // tk
    qi = pl.program_id(1)

    @pl.when(qi == 0)
    def _():
        for n in range(nk):
            ks = slice(n * tk, (n + 1) * tk)
            kt_full = k_ref[ks, :].T
            kt_out[:, ks] = kt_full
            vt_out[:, ks] = v_ref[ks, :].T
            for h in range(SB_HEADS):
                cs = slice(h * SB_HEAD_DIM, (h + 1) * SB_HEAD_DIM)
                kt_ref[h, n] = kt_full[cs, :].astype(BF16)
                vh_ref[h, n] = v_ref[ks, cs].astype(BF16)

    for h in range(SB_HEADS):
        cs = slice(h * SB_HEAD_DIM, (h + 1) * SB_HEAD_DIM)
        qh_ref[h] = (q_ref[:, cs] * (SB_HEAD_DIM ** -0.5)).astype(BF16)
    acc_ref[...] = jnp.zeros_like(acc_ref)
    c_ref[...] = jnp.zeros_like(c_ref)

    ri = lax.broadcasted_iota(jnp.int32, (tk, tk), 0)
    ci = lax.broadcasted_iota(jnp.int32, (tk, tk), 1)
    upper = jnp.where(ri > ci, 1.0, 0.0).astype(BF16)
    jd = (qi * tq) // tk
    t_pos = qi * tq + lax.broadcasted_iota(jnp.int32, (tq, tk), 0)
    s_pos = jd * tk + lax.broadcasted_iota(jnp.int32, (tq, tk), 1)
    causal = s_pos < t_pos

    def tile(j, mask):
        for g0 in range(0, SB_HEADS, SB_HEAD_GROUP):
            heads = range(g0, g0 + SB_HEAD_GROUP)
            zs = [_dot(qh_ref[h], kt_ref[h, j]) + bias_ref[h] for h in heads]
            lbs, l1ms = [], []
            for z in zs:
                sp = jnp.maximum(z, 0.0) + jnp.log(1.0 + jnp.exp(-jnp.abs(z)))
                l1m = -sp
                if mask is not None:
                    l1m = jnp.where(mask, l1m, 0.0)
                lbs.append(z - sp)
                l1ms.append(l1m)
            sufs = [_dot(l1m.astype(BF16), upper) for l1m in l1ms]
            weights = []
            for h, lb, l1m, suf in zip(heads, lbs, l1ms, sufs):
                c = c_ref[h]
                a = jnp.exp(lb + suf + c)
                if mask is not None:
                    a = jnp.where(mask, a, 0.0)
                weights.append(a.astype(BF16))
                c_ref[h] = c + suf[:, 0:1] + l1m[:, 0:1]
            for h, a in zip(heads, weights):
                acc_ref[h] += _dot(a, vh_ref[h, j])

    tile(jd, causal)

    def body(step, carry):
        tile(jd - 1 - step, None)
        return carry

    lax.fori_loop(0, jd, body, 0)
    for h in range(SB_HEADS):
        o_ref[:, h * SB_HEAD_DIM:(h + 1) * SB_HEAD_DIM] = acc_ref[h]


def _sb_prompt(proj, bias):
    b, t, _ = proj.shape
    tq = min(SB_TQ, t)
    nk = t // SB_TK
    seq = lambda col: pl.BlockSpec((None, t, COL), lambda a, i: (a, 0, col))
    return pl.pallas_call(
        _sb_prompt_kernel,
        grid=(b, t // tq),
        in_specs=[pl.BlockSpec(memory_space=pltpu.SMEM),
                  pl.BlockSpec((None, tq, COL), lambda a, i: (a, i, 4)), seq(5), seq(6)],
        out_specs=[pl.BlockSpec((None, tq, SB_WIDTH), lambda a, i: (a, i, 0)),
                   pl.BlockSpec((None, SB_WIDTH, t), lambda a, i: (a, 0, 0)),
                   pl.BlockSpec((None, SB_WIDTH, t), lambda a, i: (a, 0, 0))],
        out_shape=[jax.ShapeDtypeStruct((b, t, SB_WIDTH), F32),
                   jax.ShapeDtypeStruct((b, SB_WIDTH, t), F32),
                   jax.ShapeDtypeStruct((b, SB_WIDTH, t), F32)],
        scratch_shapes=[pltpu.VMEM((SB_HEADS, tq, SB_HEAD_DIM), BF16),
                        pltpu.VMEM((SB_HEADS, nk, SB_HEAD_DIM, SB_TK), BF16),
                        pltpu.VMEM((SB_HEADS, nk, SB_TK, SB_HEAD_DIM), BF16),
                        pltpu.VMEM((SB_HEADS, tq, SB_HEAD_DIM), F32),
                        pltpu.VMEM((SB_HEADS, tq, 1), F32)],
        compiler_params=_params(("parallel", "arbitrary"), V7X_VMEM_LIMIT),
    )(bias, proj, proj, proj)


SB_PAGES_PER_STEP = 16


def _sb_decode_kernel(pt_ref, q_ref, bias_ref, *refs):
    npg = SB_PAGES_PER_STEP
    k_refs = refs[:npg]
    v_refs = refs[npg:2 * npg]
    o_ref, qb_ref, c_ref, acc_ref = refs[2 * npg:]
    g = pl.program_id(1)

    @pl.when(g == 0)
    def _():
        qb_ref[...] = jnp.broadcast_to(q_ref[...], qb_ref.shape)
        c_ref[...] = jnp.zeros_like(c_ref)
        acc_ref[...] = jnp.zeros_like(acc_ref)

    ri = lax.broadcasted_iota(jnp.int32, (PAGE_SIZE, PAGE_SIZE), 0)
    ci = lax.broadcasted_iota(jnp.int32, (PAGE_SIZE, PAGE_SIZE), 1)
    upper = jnp.where(ri > ci, 1.0, 0.0).astype(BF16)
    bias = bias_ref[...]
    c = c_ref[:, 0:1]
    weights = [None] * npg
    for p in range(npg - 1, -1, -1):
        z = jnp.sum(k_refs[p][...] * qb_ref[...], axis=1) + bias
        sp = _softplus(z)
        l1m = -sp
        suf = _dot(l1m.astype(BF16), upper)
        weights[p] = jnp.exp(z - sp + suf + c)
        c = c + suf[:, 0:1] + l1m[:, 0:1]
    c_ref[...] = jnp.broadcast_to(c, c_ref.shape)
    for h in range(SB_HEADS):
        part = weights[0][h:h + 1, :] * v_refs[0][h]
        for p in range(1, npg):
            part = part + weights[p][h:h + 1, :] * v_refs[p][h]
        acc_ref[h] += part

    @pl.when(g == pl.num_programs(1) - 1)
    def _():
        o_ref[...] = jnp.sum(acc_ref[...], axis=2, keepdims=True)


def _sb_decode(sq, bias, cache_kt, cache_vt, page_table, layer):
    b = sq.shape[0]
    n_pages = page_table.shape[1]
    npg = SB_PAGES_PER_STEP
    ng = n_pages // npg
    qcol = (sq * (SB_HEAD_DIM ** -0.5)).reshape(b, SB_HEADS, SB_HEAD_DIM, 1)

    def page_spec(p):
        return pl.BlockSpec((None, None, SB_HEADS, SB_HEAD_DIM, PAGE_SIZE),
                            lambda a, g, pt: (pt[a, (ng - 1 - g) * npg + p], layer, 0, 0, 0))

    hd = pl.BlockSpec((None, SB_HEADS, SB_HEAD_DIM, 1), lambda a, g, pt: (a, 0, 0, 0))
    grid_spec = pltpu.PrefetchScalarGridSpec(
        num_scalar_prefetch=1,
        grid=(b, ng),
        in_specs=[hd, pl.BlockSpec((SB_HEADS, 1), lambda a, g, pt: (0, 0))]
                 + [page_spec(p) for p in range(npg)] + [page_spec(p) for p in range(npg)],
        out_specs=hd,
        scratch_shapes=[pltpu.VMEM((SB_HEADS, SB_HEAD_DIM, PAGE_SIZE), F32),
                        pltpu.VMEM((SB_HEADS, 128), F32),
                        pltpu.VMEM((SB_HEADS, SB_HEAD_DIM, PAGE_SIZE), F32)],
    )
    out = pl.pallas_call(
        _sb_decode_kernel,
        grid_spec=grid_spec,
        out_shape=jax.ShapeDtypeStruct((b, SB_HEADS, SB_HEAD_DIM, 1), F32),
        compiler_params=_params(("parallel", "arbitrary"), V7X_VMEM_LIMIT),
    )(page_table, qcol, bias.reshape(SB_HEADS, 1), *([cache_kt] * npg), *([cache_vt] * npg))
    return out.reshape(b, 1, SB_WIDTH)


def _ml_pre_kernel(x0, x1, x2, x3, *refs):
    _ml_pre_body(x0[...], x1[...], x2[...], x3[...], *refs)


def _ml_pre_seq_kernel(prev_ref, x_ref, cw_ref, cb_ref, wq_ref, wk_ref, wv_ref, wif_ref, bif_ref,
                       q_ref, k_ref, v_ref, g_ref, xs_ref):
    tm = x_ref.shape[0]
    halo = prev_ref.shape[0]
    x = x_ref[...]
    xs_ref[halo:, :] = x
    xs_ref[:halo, :] = jnp.where(pl.program_id(1) == 0, 0.0, prev_ref[...])
    taps = [xs_ref[pl.ds(halo - (ML_CONV - 1) + j, tm), :] for j in range(ML_CONV - 1)]
    _ml_pre_body(*taps, x, cw_ref, cb_ref, wq_ref, wk_ref, wv_ref, wif_ref, bif_ref,
                 q_ref, k_ref, v_ref, g_ref)


def _ml_pre_body(x0, x1, x2, x, cw_ref, cb_ref, wq_ref, wk_ref, wv_ref, wif_ref, bif_ref,
                 q_ref, k_ref, v_ref, g_ref):
    cw = cw_ref[...]
    xc = _silu(x0 * cw[0:1] + x1 * cw[1:2] + x2 * cw[2:3] + x * cw[3:4] + cb_ref[...])
    xcb = xc.astype(BF16)
    xb = x.astype(BF16)
    qs, ks, vs = [], [], []
    for h in range(ML_HEADS):
        cs = slice(h * ML_HEAD_DIM, (h + 1) * ML_HEAD_DIM)
        qs.append(_dot(xcb[:, cs], wq_ref[h]))
        ks.append(_dot(xcb[:, cs], wk_ref[h]))
        vs.append(_dot(xb[:, cs], wv_ref[h]))
    q = jnp.concatenate(qs, axis=-1)
    k = jnp.concatenate(ks, axis=-1)
    v = jnp.concatenate(vs, axis=-1)
    q_ref[...] = q
    k_ref[...] = k
    v_ref[...] = v
    g_ref[...] = (_dot(q.astype(BF16), wif_ref[0:ML_WIDTH, :])
                  + _dot(k.astype(BF16), wif_ref[ML_WIDTH:2 * ML_WIDTH, :])
                  + _dot(v.astype(BF16), wif_ref[2 * ML_WIDTH:3 * ML_WIDTH, :]) + bif_ref[...])


def _ml_pre(xs, cw, cb, wq, wk, wv, wif, bif, tm):
    g, r, w = xs[0].shape
    row = pl.BlockSpec((None, tm, w), lambda a, i: (a, i, 0))
    full = lambda shape: pl.BlockSpec(shape, lambda a, i: (0,) * len(shape))
    out = jax.ShapeDtypeStruct((g, r, w), F32)
    return pl.pallas_call(
        _ml_pre_kernel,
        grid=(g, r // tm),
        in_specs=[row, row, row, row, full((ML_CONV, w)), full((1, w)),
                  full(wq.shape), full(wk.shape), full(wv.shape), full(wif.shape), full((1, 128))],
        out_specs=[row, row, row, pl.BlockSpec((None, tm, 128), lambda a, i: (a, i, 0))],
        out_shape=[out, out, out, jax.ShapeDtypeStruct((g, r, 128), F32)],
        compiler_params=_params(("parallel", "parallel"), V7X_VMEM_LIMIT),
    )(*xs, cw, cb.reshape(1, w), wq, wk, wv, wif, bif)


ML_HALO = 8


def _ml_pre_seq(proj, cw, cb, wq, wk, wv, wif, bif, tm):
    b, t, _ = proj.shape
    w = ML_WIDTH
    per = tm // ML_HALO
    row = pl.BlockSpec((None, tm, w), lambda a, i: (a, i, 0))
    full = lambda shape: pl.BlockSpec(shape, lambda a, i: (0,) * len(shape))
    out = jax.ShapeDtypeStruct((b, t, w), F32)
    return pl.pallas_call(
        _ml_pre_seq_kernel,
        grid=(b, t // tm),
        in_specs=[pl.BlockSpec((None, ML_HALO, w), lambda a, i: (a, jnp.maximum(i * per - 1, 0), 7)),
                  pl.BlockSpec((None, tm, w), lambda a, i: (a, i, 7)),
                  full((ML_CONV, w)), full((1, w)),
                  full(wq.shape), full(wk.shape), full(wv.shape), full(wif.shape), full((1, 128))],
        out_specs=[row, row, row, pl.BlockSpec((None, tm, 128), lambda a, i: (a, i, 0))],
        out_shape=[out, out, out, jax.ShapeDtypeStruct((b, t, 128), F32)],
        scratch_shapes=[pltpu.VMEM((tm + ML_HALO, w), F32)],
        compiler_params=_params(("parallel", "parallel"), V7X_VMEM_LIMIT),
    )(proj, proj, cw, cb.reshape(1, w), wq, wk, wv, wif, bif)


def _ml_prompt_kernel(q_ref, k_ref, v_ref, g_ref, h_ref, c_out, n_out, m_out, c_ref, n_ref, m_ref):
    L = ML_CHUNK
    nch = q_ref.shape[0] // L

    @pl.when(pl.program_id(1) == 0)
    def _():
        c_ref[...] = jnp.zeros_like(c_ref)
        n_ref[...] = jnp.zeros_like(n_ref)
        m_ref[...] = jnp.zeros_like(m_ref)

    ri = lax.broadcasted_iota(jnp.int32, (L, L), 0)
    ci = lax.broadcasted_iota(jnp.int32, (L, L), 1)
    causal = ci <= ri
    tril = jnp.where(causal, 1.0, 0.0).astype(BF16)
    triu = jnp.where(ri <= ci, 1.0, 0.0).astype(BF16)
    heads = range(ML_HEADS)
    sl = [slice(h * ML_HEAD_DIM, (h + 1) * ML_HEAD_DIM) for h in heads]

    pre = []
    for j in range(nch):
        rows = slice(j * L, (j + 1) * L)
        g = g_ref[rows, :]
        gt = g.T
        b_cols = _tri_dot(tril, _log_sigmoid(g))
        b_rows = _dot_tri(_log_sigmoid(gt), triu)
        q = [q_ref[rows, cs] for cs in sl]
        ks = [k_ref[rows, cs] * (ML_HEAD_DIM ** -0.5) for cs in sl]
        v = [v_ref[rows, cs] for cs in sl]
        qb = [x.astype(BF16) for x in q]
        ksb = [x.astype(BF16) for x in ks]
        qk = [_dot_nt(qb[h], ksb[h]) for h in heads]
        b_col = [b_cols[:, ML_HEADS + h:ML_HEADS + h + 1] for h in heads]
        log_d = [jnp.where(causal, b_col[h] - b_rows[ML_HEADS + h:ML_HEADS + h + 1, :] + gt[h:h + 1, :], NEG_INF)
                 for h in heads]
        d_max = [jnp.max(x, axis=1, keepdims=True) for x in log_d]
        pre.append(dict(g=g, q=q, ks=ks, v=v, qb=qb, ksb=ksb, qk=qk, b_col=b_col, log_d=log_d, d_max=d_max))

    c0 = [c_ref[h] for h in heads]
    n0 = [n_ref[h:h + 1, :] for h in heads]
    m0 = [m_ref[h:h + 1, 0:1] for h in heads]
    for j, pj in enumerate(pre):
        rows = slice(j * L, (j + 1) * L)
        qc = [_dot_nt(pj["qb"][h], c0[h].astype(BF16)) for h in heads]
        m_t, inter, w = [], [], []
        for h in heads:
            m_h = jnp.maximum(pj["b_col"][h] + m0[h], pj["d_max"][h])
            m_t.append(m_h)
            inter.append(jnp.exp(pj["b_col"][h] + m0[h] - m_h))
            w.append(pj["qk"][h] * jnp.exp(pj["log_d"][h] - m_h))
        wv = [_dot(w[h].astype(BF16), pj["v"][h].astype(BF16)) for h in heads]
        outs, dec, wend_v, n_new, m_new = [], [], [], [], []
        for h in heads:
            numer = inter[h] * qc[h] + wv[h]
            denom = (inter[h] * jnp.sum(pj["q"][h] * n0[h], axis=1, keepdims=True)
                     + jnp.sum(w[h], axis=1, keepdims=True))
            outs.append(numer / jnp.maximum(jnp.abs(denom), jnp.exp(-m_t[h])))
            m_h = m_t[h][L - 1:L, :]
            b_last = pj["b_col"][h][L - 1:L, :]
            w_end = jnp.exp(b_last - pj["b_col"][h] + pj["g"][:, h:h + 1] - m_h)
            d_h = jnp.exp(b_last + m0[h] - m_h)
            dec.append(d_h)
            wend_v.append((w_end * pj["v"][h]).T.astype(BF16))
            n_new.append(d_h * n0[h] + jnp.sum(w_end * pj["ks"][h], axis=0, keepdims=True))
            m_new.append(m_h)
        c0 = [dec[h] * c0[h] + _dot(wend_v[h], pj["ksb"][h]) for h in heads]
        n0, m0 = n_new, m_new
        h_ref[rows, :] = jnp.concatenate(outs, axis=-1)
    for h in heads:
        c_ref[h] = c0[h]
        n_ref[h:h + 1, :] = n0[h]
        m_ref[h:h + 1, :] = jnp.broadcast_to(m0[h], (1, m_ref.shape[1]))

    @pl.when(pl.program_id(1) == pl.num_programs(1) - 1)
    def _():
        c_out[...] = c_ref[...]
        n_out[...] = n_ref[...]
        m_out[...] = m_ref[...]


ML_CHUNKS_PER_STEP = 4


def _ml_prompt(q, k, v, gates):
    b, t, w = q.shape
    tr = min(t, ML_CHUNKS_PER_STEP * ML_CHUNK)
    row = pl.BlockSpec((None, tr, w), lambda a, i: (a, i, 0))
    return pl.pallas_call(
        _ml_prompt_kernel,
        grid=(b, t // tr),
        in_specs=[row, row, row, pl.BlockSpec((None, tr, 128), lambda a, i: (a, i, 0))],
        out_specs=[row,
                   pl.BlockSpec((None, ML_HEADS, ML_HEAD_DIM, ML_HEAD_DIM), lambda a, i: (a, 0, 0, 0)),
                   pl.BlockSpec((None, ML_HEADS, ML_HEAD_DIM), lambda a, i: (a, 0, 0)),
                   pl.BlockSpec((None, ML_HEADS, 128), lambda a, i: (a, 0, 0))],
        out_shape=[jax.ShapeDtypeStruct((b, t, w), F32),
                   jax.ShapeDtypeStruct((b, ML_HEADS, ML_HEAD_DIM, ML_HEAD_DIM), F32),
                   jax.ShapeDtypeStruct((b, ML_HEADS, ML_HEAD_DIM), F32),
                   jax.ShapeDtypeStruct((b, ML_HEADS, 128), F32)],
        scratch_shapes=[pltpu.VMEM((ML_HEADS, ML_HEAD_DIM, ML_HEAD_DIM), F32),
                        pltpu.VMEM((ML_HEADS, ML_HEAD_DIM), F32), pltpu.VMEM((ML_HEADS, 128), F32)],
        compiler_params=_params(("parallel", "arbitrary"), V7X_VMEM_LIMIT),
    )(q, k, v, gates)


def _ml_step_kernel(q_ref, k_ref, vc_ref, g_ref, c_ref, n_ref, m_ref, h_ref, c_out, n_out, m_out):
    g = g_ref[...]
    lane = lax.broadcasted_iota(jnp.int32, (1, 128), 1)
    m_row = jnp.zeros((1, 128), F32)
    for h in range(ML_HEADS):
        cs = slice(h * ML_HEAD_DIM, (h + 1) * ML_HEAD_DIM)
        q = q_ref[:, cs]
        ks = k_ref[:, cs] * (ML_HEAD_DIM ** -0.5)
        v = vc_ref[h]
        it = g[:, h:h + 1]
        logf = _log_sigmoid(g[:, ML_HEADS + h:ML_HEADS + h + 1])
        m0 = m_ref[:, h:h + 1]
        c0 = c_ref[h]
        n0 = n_ref[h:h + 1, :]
        m_t = jnp.maximum(logf + m0, it)
        dm = jnp.exp(it - m_t)
        inter = jnp.exp(logf + m0 - m_t)
        w = jnp.sum(q * ks, axis=1, keepdims=True) * dm
        numer = inter * jnp.sum(c0 * q, axis=1, keepdims=True) + w * v
        denom = inter * jnp.sum(q * n0, axis=1, keepdims=True) + w
        h_ref[h] = numer / jnp.maximum(jnp.abs(denom), jnp.exp(-m_t))
        w_end = jnp.exp(it - m_t)
        dec = jnp.exp(logf + m0 - m_t)
        c_out[h] = dec * c0 + (w_end * v) * ks
        n_out[h:h + 1, :] = dec * n0 + w_end * ks
        m_row = jnp.where(lane == h, m_t, m_row)
    m_out[...] = m_row


def _ml_step(q, k, v, gates, state_c, state_n, state_m, layer):
    b = q.shape[0]
    vc = v.reshape(b, ML_HEADS, ML_HEAD_DIM, 1)
    m0 = state_m[:, layer].reshape(b, 1, ML_HEADS)
    row = pl.BlockSpec((None, 1, ML_WIDTH), lambda a: (a, 0, 0))
    col = pl.BlockSpec((None, ML_HEADS, ML_HEAD_DIM, 1), lambda a: (a, 0, 0, 0))
    cspec = pl.BlockSpec((None, ML_HEADS, ML_HEAD_DIM, ML_HEAD_DIM), lambda a: (a, 0, 0, 0))
    nspec = pl.BlockSpec((None, ML_HEADS, ML_HEAD_DIM), lambda a: (a, 0, 0))
    return pl.pallas_call(
        _ml_step_kernel,
        grid=(b,),
        in_specs=[row, row, col, pl.BlockSpec((None, 1, 128), lambda a: (a, 0, 0)),
                  pl.BlockSpec((None, None, ML_HEADS, ML_HEAD_DIM, ML_HEAD_DIM),
                               lambda a: (a, layer, 0, 0, 0)),
                  pl.BlockSpec((None, None, ML_HEADS, ML_HEAD_DIM), lambda a: (a, layer, 0, 0)),
                  pl.BlockSpec((None, 1, ML_HEADS), lambda a: (a, 0, 0))],
        out_specs=[col, cspec, nspec, pl.BlockSpec((None, 1, 128), lambda a: (a, 0, 0))],
        out_shape=[jax.ShapeDtypeStruct((b, ML_HEADS, ML_HEAD_DIM, 1), F32),
                   jax.ShapeDtypeStruct((b, ML_HEADS, ML_HEAD_DIM, ML_HEAD_DIM), F32),
                   jax.ShapeDtypeStruct((b, ML_HEADS, ML_HEAD_DIM), F32),
                   jax.ShapeDtypeStruct((b, 1, 128), F32)],
        compiler_params=_params(("parallel",)),
    )(q, k, vc, gates, state_c, state_n, m0)


def _prep_weights(p):
    bf = lambda a: a.astype(BF16)
    w = {}
    for name in ("w_ffn_in", "w_ffn_down", "w_in", "w_hg_proj", "w_sb_proj", "w_ml_proj", "w_out", "ml_wq", "ml_wk", "ml_wv"):
        w[name] = bf(p[name])
    w["wif"] = bf(jnp.pad(p["ml_w_if"], ((0, 0), (0, 0), (0, 128 - 2 * ML_HEADS))))
    w["bif"] = jnp.pad(p["ml_b_if"], ((0, 0), (0, 128 - 2 * ML_HEADS))).reshape(DEPTH, 1, 128)
    return w


def _layer(x, ada, l, p, w, tm, past):
    ffn = lambda xx, j, s: _ffn(xx, ada, j, p["g_pre"][l, j], p["g_post"][l, j],
                                w["w_ffn_in"], w["w_ffn_down"], l, s, tm["ffn"])
    x = ffn(x, 0, 0)
    proj = _mixin(x, ada, p["g_pre"][l, 1], w["w_in"][l], tm["mixin"])
    st = {}
    mx = proj[..., 7 * COL:8 * COL]
    if past is None:
        b, t, _ = x.shape
        o_a, st["hgrn"] = _hgrn_prompt(proj, p["hg_lb_logits"], l, tm["hgrn"])
        o_b, kt, vt = _sb_prompt(proj, p["sb_bias"][l])
        q, k, v, gates = _ml_pre_seq(proj, p["ml_conv_w"][l], p["ml_conv_b"][l], w["ml_wq"][l],
                                     w["ml_wk"][l], w["ml_wv"][l], w["wif"][l], w["bif"][l], tm["mlpre"])
        o_c, st["mc"], st["mn"], m_pad = _ml_prompt(q, k, v, gates)
        st["mm"] = m_pad[:, :, 0]
        st["mconv"] = proj[:, t - (ML_CONV - 1):, 7 * COL:8 * COL]
        st["k"] = kt.reshape(b, SB_HEADS, SB_HEAD_DIM, t)
        st["v"] = vt.reshape(b, SB_HEADS, SB_HEAD_DIM, t)
    else:
        b = x.shape[1]
        projs = proj.reshape(b, 1, IN_COLS)
        o_a, st["hgrn"] = _hgrn_step(projs, past["hgrn"], p["hg_lb_logits"], l)
        o_b = _sb_decode(projs[:, 0, 4 * COL:5 * COL], p["sb_bias"][l], past["k"], past["v"],
                         past["page_table"], l)
        buf = past["mconv"][:, l]
        taps = [buf[:, j].reshape(1, b, ML_WIDTH) for j in range(ML_CONV - 1)]
        q, k, v, gates = _ml_pre(taps + [mx], p["ml_conv_w"][l], p["ml_conv_b"][l], w["ml_wq"][l],
                                 w["ml_wk"][l], w["ml_wv"][l], w["wif"][l], w["bif"][l], b)
        hc, st["mc"], st["mn"], m_pad = _ml_step(q.reshape(b, 1, ML_WIDTH), k.reshape(b, 1, ML_WIDTH),
                                                 v.reshape(b, 1, ML_WIDTH), gates.reshape(b, 1, 128),
                                                 past["mc"], past["mn"], past["mm"], l)
        o_a = o_a.reshape(1, b, HG_WIDTH)
        o_b = o_b.reshape(1, b, SB_WIDTH)
        o_c = hc.reshape(1, b, ML_WIDTH)
        st["mm"] = m_pad[:, 0, :ML_HEADS]
        st["mconv"] = jnp.concatenate([buf[:, 1:], mx.reshape(b, 1, ML_WIDTH)], axis=1)
        st["k"] = projs[:, :, 5 * COL:6 * COL].reshape(b, 1, SB_HEADS, SB_HEAD_DIM)
        st["v"] = projs[:, :, 6 * COL:7 * COL].reshape(b, 1, SB_HEADS, SB_HEAD_DIM)
    x = _merge(x, ada, p["g_post"][l, 1], proj, o_a, o_b, o_c, p["hg_gain"][l], p["ml_gain"][l],
               w["w_hg_proj"][l], w["w_sb_proj"][l], w["w_ml_proj"][l], w["w_out"][l], tm["merge"])
    x = ffn(x, 2, 1)
    return x, st


def _run(x, ada_all, p, w, tm, past):
    outs = []
    for l in range(DEPTH):
        x, st = _layer(x, ada_all[l], l, p, w, tm, past)
        outs.append(st)
    return x, {name: jnp.stack([o[name] for o in outs], axis=1) for name in outs[0]}


def kernel(x_prompt, x_sample, cache_sb_k, cache_sb_v, state_hgrn, state_mlstm_c, state_mlstm_n,
           state_mlstm_m, state_mlstm_conv, page_table, c_prompt, c_sample, w_ada, b_ada, g_pre, g_post,
           w_ffn_in, w_ffn_down, w_in, sb_bias, hg_lb_logits, hg_gain, ml_conv_w, ml_conv_b, ml_wq, ml_wk,
           ml_wv, ml_w_if, ml_b_if, ml_gain, w_hg_proj, w_sb_proj, w_ml_proj, w_out):
    p = dict(g_pre=g_pre, g_post=g_post, w_ffn_in=w_ffn_in, w_ffn_down=w_ffn_down, w_in=w_in,
             sb_bias=sb_bias, hg_lb_logits=hg_lb_logits, hg_gain=hg_gain, ml_conv_w=ml_conv_w,
             ml_conv_b=ml_conv_b, ml_wq=ml_wq, ml_wk=ml_wk, ml_wv=ml_wv, ml_w_if=ml_w_if,
             ml_b_if=ml_b_if, ml_gain=ml_gain, w_hg_proj=w_hg_proj, w_sb_proj=w_sb_proj,
             w_ml_proj=w_ml_proj, w_out=w_out)
    w = _prep_weights(p)
    bp, t, d = x_prompt.shape
    bs = x_sample.shape[0]
    ada = _ada(jnp.concatenate([c_prompt, c_sample], axis=0), w_ada, b_ada)
    ada_p = ada[:, :bp].reshape(DEPTH, bp, 1, N_SUB * 3 * d)
    ada_s = ada[:, bp:].reshape(DEPTH, 1, bs, N_SUB * 3 * d)

    tm_p = dict(ffn=min(t, 512), mixin=min(t, 1024), merge=min(t, 256), hgrn=min(t, 512),
                mlpre=min(t, 512))
    y_p, sp = _run(x_prompt, ada_p, p, w, tm_p, None)

    past = dict(k=jnp.transpose(cache_sb_k, (0, 1, 3, 4, 2)), v=jnp.transpose(cache_sb_v, (0, 1, 3, 4, 2)),
                page_table=page_table, hgrn=state_hgrn, mc=state_mlstm_c, mn=state_mlstm_n,
                mm=state_mlstm_m, mconv=state_mlstm_conv)
    tm_s = dict(ffn=bs, mixin=bs, merge=bs)
    y_s, ss = _run(x_sample.reshape(1, bs, d), ada_s, p, w, tm_s, past)
    y_s = y_s.reshape(bs, 1, d)
    k_p = jnp.transpose(sp["k"], (0, 1, 4, 2, 3))
    v_p = jnp.transpose(sp["v"], (0, 1, 4, 2, 3))
    return (y_p, y_s, k_p, v_p, ss["k"], ss["v"], sp["hgrn"], ss["hgrn"],
            sp["mc"], ss["mc"], sp["mn"], ss["mn"], sp["mm"], ss["mm"], sp["mconv"], ss["mconv"])
```

```python
import functools

import jax
import jax.numpy as jnp
from jax import lax
from jax.experimental import pallas as pl
from jax.experimental.pallas import tpu as pltpu

F32 = jnp.float32
BF16 = jnp.bfloat16

D_MODEL = 1024
DEPTH = 2
PAGE_SIZE = 128
HG_HEADS = 4
HG_DK = 128
HG_WIDTH = 512
HG_CHUNK = 64
HG_SUB = 16
HG_SAFE_DECAY = 80.0
SB_HEADS = 8
SB_HEAD_DIM = 64
SB_WIDTH = 512
SB_TQ = 128
SB_TK = 256
SB_HEAD_GROUP = 8
ML_HEADS = 4
ML_HEAD_DIM = 128
ML_WIDTH = 512
ML_CHUNK = 128
ML_CONV = 4
D_FF = 2816
FF_CHUNK = 256
FFN_RES = 0.5
N_SUB = 3
NORM_EPS = 1e-6
IN_COLS = 7680
COL = 512
PF_SRC = (1, 5, 6, 7)
PB_SRC = (0, 2, 3, 4, 8, 9, 10, 11, 12, 13, 14)
PF_COLS = len(PF_SRC) * COL
PB_COLS = len(PB_SRC) * COL
PF_Z, PF_K, PF_V, PF_MX = 0, 1, 2, 3
PB_Q, PB_I, PB_HG, PB_SQ, PB_MO, PB_GATES = 0, 1, 2, 3, 4, 5
V7X_VMEM_LIMIT = 56 * 1024 * 1024
NEG_INF = float("-inf")
LOG2E = 1.4426950408889634


def _dot(a, b):
    return jnp.dot(a, b, preferred_element_type=F32)


def _dot_nt(a, b):
    return lax.dot_general(a, b, (((1,), (1,)), ((), ())), preferred_element_type=F32)


def _split3(x):
    x1 = x.astype(BF16)
    r1 = x - x1.astype(F32)
    x2 = r1.astype(BF16)
    x3 = (r1 - x2.astype(F32)).astype(BF16)
    return x1, x2, x3


def _tri_dot(tri, x):
    x1, x2, x3 = _split3(x)
    return _dot(tri, x1) + _dot(tri, x2) + _dot(tri, x3)


def _dot_tri(x, tri):
    x1, x2, x3 = _split3(x)
    return _dot(x1, tri) + _dot(x2, tri) + _dot(x3, tri)


def _sigmoid(x):
    return 1.0 / (1.0 + jnp.exp(-x))


def _silu(x):
    return x * _sigmoid(x)


def _log_sigmoid(x):
    return jnp.minimum(x, 0.0) - jnp.log1p(jnp.exp(-jnp.abs(x)))


def _logaddexp(a, b):
    amax = jnp.maximum(a, b)
    delta = a - b
    return jnp.where(delta != delta, a + b, amax + jnp.log1p(jnp.exp(-jnp.abs(delta))))


def _rms(x, g):
    return x * lax.rsqrt(jnp.mean(x * x, axis=-1, keepdims=True) + NORM_EPS) * g


def _head_norm(o, gain, heads, width):
    parts = []
    for h in range(heads):
        oh = o[:, h * width:(h + 1) * width]
        parts.append(oh * lax.rsqrt(jnp.mean(oh * oh, axis=-1, keepdims=True) + NORM_EPS))
    return jnp.concatenate(parts, axis=-1) * gain


def _params(sem, vmem=None):
    return pltpu.CompilerParams(dimension_semantics=sem, vmem_limit_bytes=vmem)


def _ada_kernel(c_ref, w_ref, b_ref, o_ref):
    s = _silu(c_ref[...]).astype(BF16)
    o_ref[...] = _dot(s, w_ref[...].astype(BF16)) + b_ref[...]


def _ada(c_all, w_ada, b_ada):
    n, d = c_all.shape
    cols = w_ada.shape[-1]
    tn = 1536
    return pl.pallas_call(
        _ada_kernel,
        grid=(DEPTH, cols // tn),
        in_specs=[pl.BlockSpec((n, d), lambda l, j: (0, 0)),
                  pl.BlockSpec((None, d, tn), lambda l, j: (l, 0, j)),
                  pl.BlockSpec((None, 1, tn), lambda l, j: (l, 0, j))],
        out_specs=pl.BlockSpec((None, n, tn), lambda l, j: (l, 0, j)),
        out_shape=jax.ShapeDtypeStruct((DEPTH, n, cols), F32),
        compiler_params=_params(("parallel", "parallel"), V7X_VMEM_LIMIT),
    )(c_all, w_ada, b_ada.reshape(DEPTH, 1, cols))


def _mod_spec(ada, tm, col):
    rm = ada.shape[1]
    if rm == 1:
        return pl.BlockSpec((None, 1, D_MODEL), lambda g, i: (g, 0, col))
    return pl.BlockSpec((None, tm, D_MODEL), lambda g, i: (g, i, col))


def _ffn_kernel(x_ref, sh_ref, sc_ref, gt_ref, gpre_ref, gpost_ref, win_ref, wd_ref, o_ref, h_ref, a_ref):
    x = x_ref[...]
    h_ref[...] = (_rms(x, gpre_ref[...]) * (1.0 + sc_ref[...]) + sh_ref[...]).astype(BF16)
    for f in range(D_FF // FF_CHUNK):
        cols = slice(f * FF_CHUNK, (f + 1) * FF_CHUNK)
        h = h_ref[...]
        g = _dot(h, win_ref[:, cols])
        u = _dot(h, win_ref[:, D_FF + f * FF_CHUNK:D_FF + (f + 1) * FF_CHUNK])
        a_ref[:, cols] = (_silu(g) * u).astype(BF16)
    y = _dot(a_ref[...], wd_ref[...])
    o_ref[...] = x + FFN_RES * (1.0 + gt_ref[...]) * _rms(y, gpost_ref[...])


def _ffn(x, ada, j, gpre, gpost, w_in, w_down, l, s, tm):
    g, r, d = x.shape
    xspec = pl.BlockSpec((None, tm, d), lambda a, i: (a, i, 0))
    vec = pl.BlockSpec((1, d), lambda a, i: (0, 0))
    res = lambda shape: pl.BlockSpec((None, None) + shape, lambda a, i: (l, s, 0, 0),
                                     pipeline_mode=pl.Buffered(1))
    return pl.pallas_call(
        _ffn_kernel,
        grid=(g, r // tm),
        in_specs=[xspec, _mod_spec(ada, tm, 3 * j), _mod_spec(ada, tm, 3 * j + 1),
                  _mod_spec(ada, tm, 3 * j + 2), vec, vec, res((d, 2 * D_FF)), res((D_FF, d))],
        out_specs=xspec,
        out_shape=jax.ShapeDtypeStruct(x.shape, F32),
        scratch_shapes=[pltpu.VMEM((tm, d), BF16), pltpu.VMEM((tm, D_FF), BF16)],
        compiler_params=_params(("parallel", "parallel"), V7X_VMEM_LIMIT),
    )(x, ada, ada, ada, gpre.reshape(1, d), gpost.reshape(1, d), w_in, w_down)


def _mixin_kernel(x_ref, sh_ref, sc_ref, gpre_ref, w_ref, pf_ref, pb_ref, h_ref):
    h_ref[...] = (_rms(x_ref[...], gpre_ref[...]) * (1.0 + sc_ref[...]) + sh_ref[...]).astype(BF16)
    nf = PF_COLS // COL
    for c in range(IN_COLS // COL):
        y = _dot(h_ref[...], w_ref[:, c * COL:(c + 1) * COL])
        if c < nf:
            pf_ref[:, c * COL:(c + 1) * COL] = y
        else:
            pb_ref[:, (c - nf) * COL:(c - nf + 1) * COL] = y.astype(BF16)


def _mixin(x, ada, gpre, w_in, l, tm):
    g, r, d = x.shape
    return pl.pallas_call(
        _mixin_kernel,
        grid=(g, r // tm),
        in_specs=[pl.BlockSpec((None, tm, d), lambda a, i: (a, i, 0)),
                  _mod_spec(ada, tm, 3), _mod_spec(ada, tm, 4),
                  pl.BlockSpec((1, d), lambda a, i: (0, 0)),
                  pl.BlockSpec((None, d, IN_COLS), lambda a, i: (l, 0, 0), pipeline_mode=pl.Buffered(1))],
        out_specs=[pl.BlockSpec((None, tm, PF_COLS), lambda a, i: (a, i, 0)),
                   pl.BlockSpec((None, tm, PB_COLS), lambda a, i: (a, i, 0))],
        out_shape=[jax.ShapeDtypeStruct((g, r, PF_COLS), F32), jax.ShapeDtypeStruct((g, r, PB_COLS), BF16)],
        scratch_shapes=[pltpu.VMEM((tm, d), BF16)],
        compiler_params=_params(("parallel", "parallel"), V7X_VMEM_LIMIT),
    )(x, ada, ada, gpre.reshape(1, d), w_in)


def _merge_kernel(x_ref, gt_ref, gpost_ref, oa_ref, hg_ref, ob_ref, oc_ref, mo_ref,
                  g0, g1, g2, g3, g4, g5, hgain_ref, mgain_ref,
                  whg_ref, wsb_ref, wml_ref, wout_ref, o_ref):
    f32 = lambda ref: ref[...].astype(F32)
    o_a = (_head_norm(oa_ref[...], hgain_ref[...], HG_HEADS, HG_DK) * _silu(f32(hg_ref))).astype(BF16)
    o_b = ob_ref[...]
    o_c = (_head_norm(f32(oc_ref), mgain_ref[...], ML_HEADS, ML_HEAD_DIM) * _sigmoid(f32(mo_ref))).astype(BF16)
    gates = ((g0, g2, g4), (g1, g3, g5))
    y = None
    for c in range(2):
        cs = slice(c * COL, (c + 1) * COL)
        ga, gb, gc = gates[c]
        m = (_sigmoid(f32(ga)) * _dot(o_a, whg_ref[:, cs])
             + _sigmoid(f32(gb)) * _dot(o_b, wsb_ref[:, cs])
             + _sigmoid(f32(gc)) * _dot(o_c, wml_ref[:, cs]))
        part = _dot(m.astype(BF16), wout_ref[cs, :])
        y = part if y is None else y + part
    o_ref[...] = x_ref[...] + (1.0 + gt_ref[...]) * _rms(y, gpost_ref[...])


def _merge(x, ada, gpost, pb, o_a, o_b, o_c, hgain, mgain, whg, wsb, wml, wout, tm):
    g, r, d = x.shape
    row = lambda col: pl.BlockSpec((None, tm, COL), lambda a, i: (a, i, col))
    vec = lambda n: pl.BlockSpec((1, n), lambda a, i: (0, 0))
    wsp = lambda shape: pl.BlockSpec(shape, lambda a, i: (0, 0))
    xspec = pl.BlockSpec((None, tm, d), lambda a, i: (a, i, 0))
    return pl.pallas_call(
        _merge_kernel,
        grid=(g, r // tm),
        in_specs=[xspec, _mod_spec(ada, tm, 5), vec(d),
                  row(0), row(PB_HG), row(0), row(0), row(PB_MO)]
                 + [row(PB_GATES + c) for c in range(6)]
                 + [vec(HG_WIDTH), vec(ML_WIDTH),
                    wsp((HG_WIDTH, d)), wsp((SB_WIDTH, d)), wsp((ML_WIDTH, d)), wsp((d, d))],
        out_specs=xspec,
        out_shape=jax.ShapeDtypeStruct(x.shape, F32),
        compiler_params=_params(("parallel", "parallel"), V7X_VMEM_LIMIT),
    )(x, ada, gpost.reshape(1, d), o_a, pb, o_b, o_c, pb,
      pb, pb, pb, pb, pb, pb,
      hgain.reshape(1, HG_WIDTH), mgain.reshape(1, ML_WIDTH), whg, wsb, wml, wout)


def _hgrn_lb(lbl, layer):
    e = jnp.exp(lbl - jnp.max(lbl, axis=0, keepdims=True))
    p = e / jnp.sum(e, axis=0, keepdims=True)
    lb = jnp.zeros_like(p[0:1])
    for r in range(1, layer + 1):
        lb = lb + p[r:r + 1]
    return lb


def _hgrn_gates(z, lb):
    logf = _logaddexp(jnp.log(lb), jnp.log1p(-lb) + _log_sigmoid(z))
    k = (1.0 - lb) * _sigmoid(-z)
    return logf, k


def _hgrn_prompt_kernel(lbl_ref, qb_ref, z_ref, ib_ref, o_ref, s_ref, st_ref, b_ref, k_ref, q_ref, i_ref,
                        *, layer):
    tc = qb_ref.shape[0]
    q_ref[...] = qb_ref[...].astype(F32)
    i_ref[...] = ib_ref[...].astype(F32)
    nchunk = tc // HG_CHUNK
    nsub = HG_CHUNK // HG_SUB

    @pl.when(pl.program_id(1) == 0)
    def _():
        st_ref[...] = jnp.zeros_like(st_ref)

    lb = _hgrn_lb(lbl_ref[...], layer)[0]
    logf, k = _hgrn_gates(z_ref[...], lb)
    k_ref[...] = k
    ri = lax.broadcasted_iota(jnp.int32, (tc, tc), 0)
    ci = lax.broadcasted_iota(jnp.int32, (tc, tc), 1)
    tri = jnp.where((ci <= ri) & (ri // HG_CHUNK == ci // HG_CHUNK), 1.0, 0.0).astype(BF16)
    b_ref[...] = _tri_dot(tri, logf)
    block_decay = -jnp.sum(logf.reshape(tc // HG_SUB, HG_SUB, HG_WIDTH), axis=1)
    safe = jnp.max(block_decay) < HG_SAFE_DECAY

    rows64 = lax.broadcasted_iota(jnp.int32, (HG_CHUNK, 1), 0)
    rows16 = lax.broadcasted_iota(jnp.int32, (HG_SUB, 1), 0)
    r64 = lax.broadcasted_iota(jnp.int32, (HG_CHUNK, HG_CHUNK), 0)
    c64 = lax.broadcasted_iota(jnp.int32, (HG_CHUNK, HG_CHUNK), 1)
    causal64 = c64 <= r64

    heads = range(HG_HEADS)
    sl = [slice(h * HG_DK, (h + 1) * HG_DK) for h in heads]

    def score_operands(q, b, kk, cs, s_i, include_diag):
        i0 = s_i * HG_SUB
        r = b[i0 - 1:i0, cs] if s_i else jnp.zeros((1, HG_DK), F32)
        hi = i0 + HG_SUB if include_diag else i0
        cap = HG_SAFE_DECAY if include_diag else 0.0
        qt = q[i0:i0 + HG_SUB, cs] * jnp.exp(b[i0:i0 + HG_SUB, cs] - r)
        kt = jnp.where(rows64 < hi, kk[:, cs] * jnp.exp(jnp.minimum(r - b[:, cs], cap)), 0.0)
        return qt.astype(BF16), kt.astype(BF16)

    def chunk(r0):
        rows = pl.ds(r0, HG_CHUNK)
        q = q_ref[rows, :]
        i = i_ref[rows, :]
        b = b_ref[rows, :]
        kk = k_ref[rows, :]
        blast = b[HG_CHUNK - 1:HG_CHUNK, :]
        eb = jnp.exp(b)
        kd = kk * jnp.exp(blast - b)
        ib = [i[:, cs].astype(BF16) for cs in sl]
        ops = [[score_operands(q, b, kk, cs, s_i, True) for s_i in range(nsub)] for cs in sl]
        p = [jnp.concatenate([_dot_nt(qt, kt) for qt, kt in ops[h]], axis=0) for h in heads]
        st = [st_ref[h] for h in heads]
        o_state = [_dot_nt((q[:, sl[h]] * eb[:, sl[h]]).astype(BF16), st[h].astype(BF16)) for h in heads]
        pm = [jnp.where(causal64 & safe, p[h], 0.0).astype(BF16) for h in heads]
        o_intra = [_dot(pm[h], ib[h]) for h in heads]
        upd = [_dot(i[:, sl[h]].T.astype(BF16), kd[:, sl[h]].astype(BF16)) for h in heads]
        o_ref[rows, :] = jnp.concatenate([o_state[h] + o_intra[h] for h in heads], axis=-1)
        for h in heads:
            st_ref[h] = st[h] * jnp.exp(blast[:, sl[h]]) + upd[h]

        @pl.when(jnp.logical_not(safe))
        def _():
            outs = []
            for h in heads:
                rows_p = [jnp.zeros((HG_SUB, HG_CHUNK), F32)]
                for s_i in range(1, nsub):
                    qt, kt = score_operands(q, b, kk, sl[h], s_i, False)
                    rows_p.append(_dot_nt(qt, kt))
                outs.append(_dot(jnp.concatenate(rows_p, axis=0).astype(BF16), ib[h]))
            o_ref[rows, :] += jnp.concatenate(outs, axis=-1)
            for s_i in range(nsub):
                base = r0 + s_i * HG_SUB
                srows = pl.ds(base, HG_SUB)
                q_i = q_ref[srows, :]
                b_i = b_ref[srows, :]

                def s_body(s, acc):
                    b_s = b_ref[pl.ds(base + s, 1), :]
                    k_s = k_ref[pl.ds(base + s, 1), :]
                    i_s = i_ref[pl.ds(base + s, 1), :]
                    e = jnp.where(rows16 >= s, jnp.exp(jnp.minimum(b_i - b_s, 0.0)), 0.0)
                    pr = q_i * e * k_s
                    parts = [jnp.sum(pr[:, cs], axis=1, keepdims=True) * i_s[:, cs] for cs in sl]
                    return acc + jnp.concatenate(parts, axis=-1)

                acc = lax.fori_loop(0, HG_SUB, s_body, jnp.zeros((HG_SUB, HG_WIDTH), F32))
                o_ref[srows, :] += acc

    def pair_body(n, carry):
        r0 = pl.multiple_of(n * (2 * HG_CHUNK), 2 * HG_CHUNK)
        chunk(r0)
        chunk(r0 + HG_CHUNK)
        return carry

    lax.fori_loop(0, nchunk // 2, pair_body, 0)

    @pl.when(pl.program_id(1) == pl.num_programs(1) - 1)
    def _():
        for h in range(HG_HEADS):
            s_ref[h] = st_ref[h].T


def _hgrn_prompt(pf, pb, lbl, layer, tc):
    b, t, _ = pf.shape
    row = lambda col: pl.BlockSpec((None, tc, COL), lambda a, i: (a, i, col))
    return pl.pallas_call(
        functools.partial(_hgrn_prompt_kernel, layer=layer),
        grid=(b, t // tc),
        in_specs=[pl.BlockSpec((DEPTH, 1, HG_WIDTH), lambda a, i: (0, 0, 0)), row(PB_Q), row(PF_Z), row(PB_I)],
        out_specs=[pl.BlockSpec((None, tc, HG_WIDTH), lambda a, i: (a, i, 0)),
                   pl.BlockSpec((None, HG_HEADS, HG_DK, HG_DK), lambda a, i: (a, 0, 0, 0))],
        out_shape=[jax.ShapeDtypeStruct((b, t, HG_WIDTH), F32),
                   jax.ShapeDtypeStruct((b, HG_HEADS, HG_DK, HG_DK), F32)],
        scratch_shapes=[pltpu.VMEM((HG_HEADS, HG_DK, HG_DK), F32)] + [pltpu.VMEM((tc, HG_WIDTH), F32)] * 4,
        compiler_params=_params(("parallel", "arbitrary"), V7X_VMEM_LIMIT),
    )(lbl.reshape(DEPTH, 1, HG_WIDTH), pb, pf, pb)


def _hgrn_step_kernel(lbl_ref, qc_ref, zc_ref, ir_ref, s_ref, o_ref, so_ref, *, layer):
    lb = _hgrn_lb(lbl_ref[...], layer)[0]
    logf, k = _hgrn_gates(zc_ref[...], lb)
    outs = []
    for h in range(HG_HEADS):
        cs = slice(h * HG_DK, (h + 1) * HG_DK)
        s_new = jnp.exp(logf[h]) * s_ref[h] + k[h] * ir_ref[:, cs]
        so_ref[h] = s_new
        outs.append(jnp.sum(qc_ref[h] * s_new, axis=0, keepdims=True))
    o_ref[...] = jnp.concatenate(outs, axis=-1)


def _hgrn_step(q, z, i, state, lbl, layer):
    b = q.shape[0]
    qc = q.reshape(b, HG_HEADS, HG_DK, 1)
    zc = z.reshape(b, HG_HEADS, HG_DK, 1)
    col = pl.BlockSpec((None, HG_HEADS, HG_DK, 1), lambda a: (a, 0, 0, 0))
    return pl.pallas_call(
        functools.partial(_hgrn_step_kernel, layer=layer),
        grid=(b,),
        in_specs=[pl.BlockSpec((DEPTH, HG_HEADS, HG_DK, 1), lambda a: (0, 0, 0, 0)), col, col,
                  pl.BlockSpec((None, 1, COL), lambda a: (a, 0, 0)),
                  pl.BlockSpec((None, None, HG_HEADS, HG_DK, HG_DK), lambda a: (a, layer, 0, 0, 0))],
        out_specs=[pl.BlockSpec((None, 1, HG_WIDTH), lambda a: (a, 0, 0)),
                   pl.BlockSpec((None, HG_HEADS, HG_DK, HG_DK), lambda a: (a, 0, 0, 0))],
        out_shape=[jax.ShapeDtypeStruct((b, 1, HG_WIDTH), F32),
                   jax.ShapeDtypeStruct((b, HG_HEADS, HG_DK, HG_DK), F32)],
        compiler_params=_params(("parallel",)),
    )(lbl.reshape(DEPTH, HG_HEADS, HG_DK, 1), qc, zc, i.reshape(b, 1, COL), state)


def _softplus(z):
    return jnp.maximum(z, 0.0) + jnp.log1p(jnp.exp(-jnp.abs(z)))


def _sb_prompt_kernel(bias_ref, q_ref, k_ref, v_ref, o_ref, kt_out, vt_out,
                      qh_ref, kt_ref, vh_ref, acc_ref, c_ref):
    tq, tk = SB_TQ, SB_TK
    nk = k_ref.shape[0] // tk
    qi = pl.program_id(1)

    @pl.when(qi == 0)
    def _():
        for n in range(nk):
            ks = slice(n * tk, (n + 1) * tk)
            kt_full = k_ref[ks, :].T
            kt_out[:, ks] = kt_full
            vt_out[:, ks] = v_ref[ks, :].T
            for h in range(SB_HEADS):
                cs = slice(h * SB_HEAD_DIM, (h + 1) * SB_HEAD_DIM)
                kt_ref[h, n] = kt_full[cs, :].astype(BF16)
                vh_ref[h, n] = v_ref[ks, cs].astype(BF16)

    for h in range(SB_HEADS):
        cs = slice(h * SB_HEAD_DIM, (h + 1) * SB_HEAD_DIM)
        qh_ref[h] = (q_ref[:, cs].astype(F32) * (SB_HEAD_DIM ** -0.5 * LOG2E)).astype(BF16)
    acc_ref[...] = jnp.zeros_like(acc_ref)
    c_ref[...] = jnp.zeros_like(c_ref)

    ri = lax.broadcasted_iota(jnp.int32, (tk, tk), 0)
    ci = lax.broadcasted_iota(jnp.int32, (tk, tk), 1)
    upper = jnp.where(ri > ci, 1.0, 0.0).astype(BF16)
    jd = (qi * tq) // tk
    t_pos = qi * tq + lax.broadcasted_iota(jnp.int32, (tq, tk), 0)
    s_pos = jd * tk + lax.broadcasted_iota(jnp.int32, (tq, tk), 1)
    causal = s_pos < t_pos

    def tile(j, mask):
        for g0 in range(0, SB_HEADS, SB_HEAD_GROUP):
            heads = range(g0, g0 + SB_HEAD_GROUP)
            zs = [_dot(qh_ref[h], kt_ref[h, j]) + bias_ref[h] * LOG2E for h in heads]
            lbs, l1ms = [], []
            for z in zs:
                sp = jnp.maximum(z, 0.0) + jnp.log(1.0 + jnp.exp2(-jnp.abs(z))) * LOG2E
                l1m = -sp
                if mask is not None:
                    l1m = jnp.where(mask, l1m, 0.0)
                lbs.append(z - sp)
                l1ms.append(l1m)
            sufs = [_dot(l1m.astype(BF16), upper) for l1m in l1ms]
            weights = []
            for h, lb, l1m, suf in zip(heads, lbs, l1ms, sufs):
                c = c_ref[h]
                a = jnp.exp2(lb + suf + c)
                if mask is not None:
                    a = jnp.where(mask, a, 0.0)
                weights.append(a.astype(BF16))
                c_ref[h] = c + suf[:, 0:1] + l1m[:, 0:1]
            for h, a in zip(heads, weights):
                acc_ref[h] += _dot(a, vh_ref[h, j])

    tile(jd, causal)

    def body(step, carry):
        tile(jd - 1 - step, None)
        return carry

    lax.fori_loop(0, jd, body, 0)
    o_ref[...] = jnp.concatenate([acc_ref[h] for h in range(SB_HEADS)], axis=-1).astype(BF16)


def _sb_prompt(pf, pb, bias):
    b, t, _ = pf.shape
    tq = min(SB_TQ, t)
    nk = t // SB_TK
    seq = lambda col: pl.BlockSpec((None, t, COL), lambda a, i: (a, 0, col))
    return pl.pallas_call(
        _sb_prompt_kernel,
        grid=(b, t // tq),
        in_specs=[pl.BlockSpec(memory_space=pltpu.SMEM),
                  pl.BlockSpec((None, tq, COL), lambda a, i: (a, i, PB_SQ)), seq(PF_K), seq(PF_V)],
        out_specs=[pl.BlockSpec((None, tq, SB_WIDTH), lambda a, i: (a, i, 0)),
                   pl.BlockSpec((None, SB_WIDTH, t), lambda a, i: (a, 0, 0)),
                   pl.BlockSpec((None, SB_WIDTH, t), lambda a, i: (a, 0, 0))],
        out_shape=[jax.ShapeDtypeStruct((b, t, SB_WIDTH), BF16),
                   jax.ShapeDtypeStruct((b, SB_WIDTH, t), F32),
                   jax.ShapeDtypeStruct((b, SB_WIDTH, t), F32)],
        scratch_shapes=[pltpu.VMEM((SB_HEADS, tq, SB_HEAD_DIM), BF16),
                        pltpu.VMEM((SB_HEADS, nk, SB_HEAD_DIM, SB_TK), BF16),
                        pltpu.VMEM((SB_HEADS, nk, SB_TK, SB_HEAD_DIM), BF16),
                        pltpu.VMEM((SB_HEADS, tq, SB_HEAD_DIM), F32),
                        pltpu.VMEM((SB_HEADS, tq, 1), F32)],
        compiler_params=_params(("parallel", "arbitrary"), V7X_VMEM_LIMIT),
    )(bias, pb, pf, pf)


SB_PAGES_PER_STEP = 16


def _sb_decode_kernel(pt_ref, q_ref, bias_ref, *refs):
    npg = SB_PAGES_PER_STEP
    k_refs = refs[:npg]
    v_refs = refs[npg:2 * npg]
    o_ref, qb_ref, c_ref, acc_ref = refs[2 * npg:]
    g = pl.program_id(1)

    @pl.when(g == 0)
    def _():
        qb_ref[...] = jnp.broadcast_to(q_ref[...], qb_ref.shape)
        c_ref[...] = jnp.zeros_like(c_ref)
        acc_ref[...] = jnp.zeros_like(acc_ref)

    ri = lax.broadcasted_iota(jnp.int32, (PAGE_SIZE, PAGE_SIZE), 0)
    ci = lax.broadcasted_iota(jnp.int32, (PAGE_SIZE, PAGE_SIZE), 1)
    upper = jnp.where(ri > ci, 1.0, 0.0).astype(BF16)
    bias = bias_ref[...]
    c = c_ref[:, 0:1]
    weights = [None] * npg
    for p in range(npg - 1, -1, -1):
        z = jnp.sum(k_refs[p][...] * qb_ref[...], axis=1) + bias
        sp = _softplus(z)
        l1m = -sp
        suf = _dot(l1m.astype(BF16), upper)
        weights[p] = jnp.exp(z - sp + suf + c)
        c = c + suf[:, 0:1] + l1m[:, 0:1]
    c_ref[...] = jnp.broadcast_to(c, c_ref.shape)
    for h in range(SB_HEADS):
        part = weights[0][h:h + 1, :] * v_refs[0][h]
        for p in range(1, npg):
            part = part + weights[p][h:h + 1, :] * v_refs[p][h]
        acc_ref[h] += part

    @pl.when(g == pl.num_programs(1) - 1)
    def _():
        o_ref[...] = jnp.sum(acc_ref[...], axis=2, keepdims=True)


def _sb_decode(sq, bias, cache_kt, cache_vt, page_table, layer):
    b = sq.shape[0]
    n_pages = page_table.shape[1]
    npg = SB_PAGES_PER_STEP
    ng = n_pages // npg
    qcol = (sq * (SB_HEAD_DIM ** -0.5)).reshape(b, SB_HEADS, SB_HEAD_DIM, 1)

    def page_spec(p):
        return pl.BlockSpec((None, None, SB_HEADS, SB_HEAD_DIM, PAGE_SIZE),
                            lambda a, g, pt: (pt[a, (ng - 1 - g) * npg + p], layer, 0, 0, 0))

    hd = pl.BlockSpec((None, SB_HEADS, SB_HEAD_DIM, 1), lambda a, g, pt: (a, 0, 0, 0))
    grid_spec = pltpu.PrefetchScalarGridSpec(
        num_scalar_prefetch=1,
        grid=(b, ng),
        in_specs=[hd, pl.BlockSpec((SB_HEADS, 1), lambda a, g, pt: (0, 0))]
                 + [page_spec(p) for p in range(npg)] + [page_spec(p) for p in range(npg)],
        out_specs=hd,
        scratch_shapes=[pltpu.VMEM((SB_HEADS, SB_HEAD_DIM, PAGE_SIZE), F32),
                        pltpu.VMEM((SB_HEADS, 128), F32),
                        pltpu.VMEM((SB_HEADS, SB_HEAD_DIM, PAGE_SIZE), F32)],
    )
    out = pl.pallas_call(
        _sb_decode_kernel,
        grid_spec=grid_spec,
        out_shape=jax.ShapeDtypeStruct((b, SB_HEADS, SB_HEAD_DIM, 1), F32),
        compiler_params=_params(("parallel", "arbitrary"), V7X_VMEM_LIMIT),
    )(page_table, qcol, bias.reshape(SB_HEADS, 1), *([cache_kt] * npg), *([cache_vt] * npg))
    return out.reshape(b, 1, SB_WIDTH)


def _ml_pre_kernel(x0, x1, x2, x3, *refs):
    _ml_pre_body(x0[...], x1[...], x2[...], x3[...], *refs)


def _ml_pre_seq_kernel(prev_ref, x_ref, cw_ref, cb_ref, wq_ref, wk_ref, wv_ref, wif_ref, bif_ref,
                       q_ref, k_ref, v_ref, g_ref, xs_ref):
    tm = x_ref.shape[0]
    halo = prev_ref.shape[0]
    x = x_ref[...]
    xs_ref[halo:, :] = x
    xs_ref[:halo, :] = jnp.where(pl.program_id(1) == 0, 0.0, prev_ref[...])
    taps = [xs_ref[pl.ds(halo - (ML_CONV - 1) + j, tm), :] for j in range(ML_CONV - 1)]
    _ml_pre_body(*taps, x, cw_ref, cb_ref, wq_ref, wk_ref, wv_ref, wif_ref, bif_ref,
                 q_ref, k_ref, v_ref, g_ref)


def _ml_pre_body(x0, x1, x2, x, cw_ref, cb_ref, wq_ref, wk_ref, wv_ref, wif_ref, bif_ref,
                 q_ref, k_ref, v_ref, g_ref):
    cw = cw_ref[...]
    xc = _silu(x0 * cw[0:1] + x1 * cw[1:2] + x2 * cw[2:3] + x * cw[3:4] + cb_ref[...])
    xcb = xc.astype(BF16)
    xb = x.astype(BF16)
    qs, ks, vs = [], [], []
    for h in range(ML_HEADS):
        cs = slice(h * ML_HEAD_DIM, (h + 1) * ML_HEAD_DIM)
        qs.append(_dot(xcb[:, cs], wq_ref[h]))
        ks.append(_dot(xcb[:, cs], wk_ref[h]))
        vs.append(_dot(xb[:, cs], wv_ref[h]))
    q = jnp.concatenate(qs, axis=-1)
    k = jnp.concatenate(ks, axis=-1)
    v = jnp.concatenate(vs, axis=-1)
    q_ref[...] = q
    k_ref[...] = k
    v_ref[...] = v
    g_ref[...] = (_dot(q.astype(BF16), wif_ref[0:ML_WIDTH, :])
                  + _dot(k.astype(BF16), wif_ref[ML_WIDTH:2 * ML_WIDTH, :])
                  + _dot(v.astype(BF16), wif_ref[2 * ML_WIDTH:3 * ML_WIDTH, :]) + bif_ref[...])


def _ml_pre(xs, cw, cb, wq, wk, wv, wif, bif, tm):
    g, r, w = xs[0].shape
    row = pl.BlockSpec((None, tm, w), lambda a, i: (a, i, 0))
    full = lambda shape: pl.BlockSpec(shape, lambda a, i: (0,) * len(shape))
    out = jax.ShapeDtypeStruct((g, r, w), F32)
    return pl.pallas_call(
        _ml_pre_kernel,
        grid=(g, r // tm),
        in_specs=[row, row, row, row, full((ML_CONV, w)), full((1, w)),
                  full(wq.shape), full(wk.shape), full(wv.shape), full(wif.shape), full((1, 128))],
        out_specs=[row, row, row, pl.BlockSpec((None, tm, 128), lambda a, i: (a, i, 0))],
        out_shape=[out, out, out, jax.ShapeDtypeStruct((g, r, 128), F32)],
        compiler_params=_params(("parallel", "parallel"), V7X_VMEM_LIMIT),
    )(*xs, cw, cb.reshape(1, w), wq, wk, wv, wif, bif)


ML_HALO = 8


def _ml_pre_seq(pf, cw, cb, wq, wk, wv, wif, bif, tm):
    b, t, _ = pf.shape
    w = ML_WIDTH
    per = tm // ML_HALO
    row = pl.BlockSpec((None, tm, w), lambda a, i: (a, i, 0))
    full = lambda shape: pl.BlockSpec(shape, lambda a, i: (0,) * len(shape))
    out = jax.ShapeDtypeStruct((b, t, w), F32)
    return pl.pallas_call(
        _ml_pre_seq_kernel,
        grid=(b, t // tm),
        in_specs=[pl.BlockSpec((None, ML_HALO, w), lambda a, i: (a, jnp.maximum(i * per - 1, 0), PF_MX)),
                  pl.BlockSpec((None, tm, w), lambda a, i: (a, i, PF_MX)),
                  full((ML_CONV, w)), full((1, w)),
                  full(wq.shape), full(wk.shape), full(wv.shape), full(wif.shape), full((1, 128))],
        out_specs=[row, row, row, pl.BlockSpec((None, tm, 128), lambda a, i: (a, i, 0))],
        out_shape=[out, out, out, jax.ShapeDtypeStruct((b, t, 128), F32)],
        scratch_shapes=[pltpu.VMEM((tm + ML_HALO, w), F32)],
        compiler_params=_params(("parallel", "parallel"), V7X_VMEM_LIMIT),
    )(pf, pf, cw, cb.reshape(1, w), wq, wk, wv, wif, bif)


def _ml_prompt_kernel(q_ref, k_ref, v_ref, g_ref, h_ref, c_out, n_out, m_out, c_ref, n_ref, m_ref):
    L = ML_CHUNK
    nch = q_ref.shape[0] // L

    @pl.when(pl.program_id(1) == 0)
    def _():
        c_ref[...] = jnp.zeros_like(c_ref)
        n_ref[...] = jnp.zeros_like(n_ref)
        m_ref[...] = jnp.zeros_like(m_ref)

    ri = lax.broadcasted_iota(jnp.int32, (L, L), 0)
    ci = lax.broadcasted_iota(jnp.int32, (L, L), 1)
    causal = ci <= ri
    tril = jnp.where(causal, 1.0, 0.0).astype(BF16)
    triu = jnp.where(ri <= ci, 1.0, 0.0).astype(BF16)
    heads = range(ML_HEADS)
    sl = [slice(h * ML_HEAD_DIM, (h + 1) * ML_HEAD_DIM) for h in heads]

    pre = []
    for j in range(nch):
        rows = slice(j * L, (j + 1) * L)
        g = g_ref[rows, :]
        gt = g.T
        b_cols = _tri_dot(tril, _log_sigmoid(g))
        b_rows = _dot_tri(_log_sigmoid(gt), triu)
        q = [q_ref[rows, cs] for cs in sl]
        ks = [k_ref[rows, cs] * (ML_HEAD_DIM ** -0.5) for cs in sl]
        v = [v_ref[rows, cs] for cs in sl]
        qb = [x.astype(BF16) for x in q]
        ksb = [x.astype(BF16) for x in ks]
        qk = [_dot_nt(qb[h], ksb[h]) for h in heads]
        b_col = [b_cols[:, ML_HEADS + h:ML_HEADS + h + 1] for h in heads]
        log_d = [jnp.where(causal, b_col[h] - b_rows[ML_HEADS + h:ML_HEADS + h + 1, :] + gt[h:h + 1, :], NEG_INF)
                 for h in heads]
        d_max = [jnp.max(x, axis=1, keepdims=True) for x in log_d]
        pre.append(dict(g=g, q=q, ks=ks, v=v, qb=qb, ksb=ksb, qk=qk, b_col=b_col, log_d=log_d, d_max=d_max))

    c0 = [c_ref[h] for h in heads]
    n0 = [n_ref[h:h + 1, :] for h in heads]
    m0 = [m_ref[h:h + 1, 0:1] for h in heads]
    for j, pj in enumerate(pre):
        rows = slice(j * L, (j + 1) * L)
        qc = [_dot_nt(pj["qb"][h], c0[h].astype(BF16)) for h in heads]
        m_t, inter, w = [], [], []
        for h in heads:
            m_h = jnp.maximum(pj["b_col"][h] + m0[h], pj["d_max"][h])
            m_t.append(m_h)
            inter.append(jnp.exp(pj["b_col"][h] + m0[h] - m_h))
            w.append(pj["qk"][h] * jnp.exp(pj["log_d"][h] - m_h))
        wv = [_dot(w[h].astype(BF16), pj["v"][h].astype(BF16)) for h in heads]
        outs, dec, wend_v, n_new, m_new = [], [], [], [], []
        for h in heads:
            numer = inter[h] * qc[h] + wv[h]
            denom = (inter[h] * jnp.sum(pj["q"][h] * n0[h], axis=1, keepdims=True)
                     + jnp.sum(w[h], axis=1, keepdims=True))
            outs.append(numer / jnp.maximum(jnp.abs(denom), jnp.exp(-m_t[h])))
            m_h = m_t[h][L - 1:L, :]
            b_last = pj["b_col"][h][L - 1:L, :]
            w_end = jnp.exp(b_last - pj["b_col"][h] + pj["g"][:, h:h + 1] - m_h)
            d_h = jnp.exp(b_last + m0[h] - m_h)
            dec.append(d_h)
            wend_v.append((w_end * pj["v"][h]).T.astype(BF16))
            n_new.append(d_h * n0[h] + jnp.sum(w_end * pj["ks"][h], axis=0, keepdims=True))
            m_new.append(m_h)
        c0 = [dec[h] * c0[h] + _dot(wend_v[h], pj["ksb"][h]) for h in heads]
        n0, m0 = n_new, m_new
        h_ref[rows, :] = jnp.concatenate(outs, axis=-1).astype(BF16)
    for h in heads:
        c_ref[h] = c0[h]
        n_ref[h:h + 1, :] = n0[h]
        m_ref[h:h + 1, :] = jnp.broadcast_to(m0[h], (1, m_ref.shape[1]))

    @pl.when(pl.program_id(1) == pl.num_programs(1) - 1)
    def _():
        c_out[...] = c_ref[...]
        n_out[...] = n_ref[...]
        m_out[...] = m_ref[...]


ML_CHUNKS_PER_STEP = 4


def _ml_prompt(q, k, v, gates):
    b, t, w = q.shape
    tr = min(t, ML_CHUNKS_PER_STEP * ML_CHUNK)
    row = pl.BlockSpec((None, tr, w), lambda a, i: (a, i, 0))
    return pl.pallas_call(
        _ml_prompt_kernel,
        grid=(b, t // tr),
        in_specs=[row, row, row, pl.BlockSpec((None, tr, 128), lambda a, i: (a, i, 0))],
        out_specs=[row,
                   pl.BlockSpec((None, ML_HEADS, ML_HEAD_DIM, ML_HEAD_DIM), lambda a, i: (a, 0, 0, 0)),
                   pl.BlockSpec((None, ML_HEADS, ML_HEAD_DIM), lambda a, i: (a, 0, 0)),
                   pl.BlockSpec((None, ML_HEADS, 128), lambda a, i: (a, 0, 0))],
        out_shape=[jax.ShapeDtypeStruct((b, t, w), BF16),
                   jax.ShapeDtypeStruct((b, ML_HEADS, ML_HEAD_DIM, ML_HEAD_DIM), F32),
                   jax.ShapeDtypeStruct((b, ML_HEADS, ML_HEAD_DIM), F32),
                   jax.ShapeDtypeStruct((b, ML_HEADS, 128), F32)],
        scratch_shapes=[pltpu.VMEM((ML_HEADS, ML_HEAD_DIM, ML_HEAD_DIM), F32),
                        pltpu.VMEM((ML_HEADS, ML_HEAD_DIM), F32), pltpu.VMEM((ML_HEADS, 128), F32)],
        compiler_params=_params(("parallel", "arbitrary"), V7X_VMEM_LIMIT),
    )(q, k, v, gates)


def _ml_step_kernel(q_ref, k_ref, vc_ref, g_ref, c_ref, n_ref, m_ref, h_ref, c_out, n_out, m_out):
    g = g_ref[...]
    lane = lax.broadcasted_iota(jnp.int32, (1, 128), 1)
    m_row = jnp.zeros((1, 128), F32)
    for h in range(ML_HEADS):
        cs = slice(h * ML_HEAD_DIM, (h + 1) * ML_HEAD_DIM)
        q = q_ref[:, cs]
        ks = k_ref[:, cs] * (ML_HEAD_DIM ** -0.5)
        v = vc_ref[h]
        it = g[:, h:h + 1]
        logf = _log_sigmoid(g[:, ML_HEADS + h:ML_HEADS + h + 1])
        m0 = m_ref[:, h:h + 1]
        c0 = c_ref[h]
        n0 = n_ref[h:h + 1, :]
        m_t = jnp.maximum(logf + m0, it)
        dm = jnp.exp(it - m_t)
        inter = jnp.exp(logf + m0 - m_t)
        w = jnp.sum(q * ks, axis=1, keepdims=True) * dm
        numer = inter * jnp.sum(c0 * q, axis=1, keepdims=True) + w * v
        denom = inter * jnp.sum(q * n0, axis=1, keepdims=True) + w
        h_ref[h] = numer / jnp.maximum(jnp.abs(denom), jnp.exp(-m_t))
        w_end = jnp.exp(it - m_t)
        dec = jnp.exp(logf + m0 - m_t)
        c_out[h] = dec * c0 + (w_end * v) * ks
        n_out[h:h + 1, :] = dec * n0 + w_end * ks
        m_row = jnp.where(lane == h, m_t, m_row)
    m_out[...] = m_row


def _ml_step(q, k, v, gates, state_c, state_n, state_m, layer):
    b = q.shape[0]
    vc = v.reshape(b, ML_HEADS, ML_HEAD_DIM, 1)
    m0 = state_m[:, layer].reshape(b, 1, ML_HEADS)
    row = pl.BlockSpec((None, 1, ML_WIDTH), lambda a: (a, 0, 0))
    col = pl.BlockSpec((None, ML_HEADS, ML_HEAD_DIM, 1), lambda a: (a, 0, 0, 0))
    cspec = pl.BlockSpec((None, ML_HEADS, ML_HEAD_DIM, ML_HEAD_DIM), lambda a: (a, 0, 0, 0))
    nspec = pl.BlockSpec((None, ML_HEADS, ML_HEAD_DIM), lambda a: (a, 0, 0))
    return pl.pallas_call(
        _ml_step_kernel,
        grid=(b,),
        in_specs=[row, row, col, pl.BlockSpec((None, 1, 128), lambda a: (a, 0, 0)),
                  pl.BlockSpec((None, None, ML_HEADS, ML_HEAD_DIM, ML_HEAD_DIM),
                               lambda a: (a, layer, 0, 0, 0)),
                  pl.BlockSpec((None, None, ML_HEADS, ML_HEAD_DIM), lambda a: (a, layer, 0, 0)),
                  pl.BlockSpec((None, 1, ML_HEADS), lambda a: (a, 0, 0))],
        out_specs=[col, cspec, nspec, pl.BlockSpec((None, 1, 128), lambda a: (a, 0, 0))],
        out_shape=[jax.ShapeDtypeStruct((b, ML_HEADS, ML_HEAD_DIM, 1), F32),
                   jax.ShapeDtypeStruct((b, ML_HEADS, ML_HEAD_DIM, ML_HEAD_DIM), F32),
                   jax.ShapeDtypeStruct((b, ML_HEADS, ML_HEAD_DIM), F32),
                   jax.ShapeDtypeStruct((b, 1, 128), F32)],
        compiler_params=_params(("parallel",)),
    )(q, k, vc, gates, state_c, state_n, m0)


def _prep_weights(p):
    bf = lambda a: a.astype(BF16)
    w = {}
    blocks = p["w_in"].reshape(DEPTH, D_MODEL, IN_COLS // COL, COL)
    w["w_in"] = bf(jnp.concatenate([blocks[:, :, c] for c in PF_SRC + PB_SRC], axis=-1))
    for name in ("w_ffn_in", "w_ffn_down", "w_hg_proj", "w_sb_proj", "w_ml_proj", "w_out", "ml_wq", "ml_wk", "ml_wv"):
        w[name] = bf(p[name])
    w["wif"] = bf(jnp.pad(p["ml_w_if"], ((0, 0), (0, 0), (0, 128 - 2 * ML_HEADS))))
    w["bif"] = jnp.pad(p["ml_b_if"], ((0, 0), (0, 128 - 2 * ML_HEADS))).reshape(DEPTH, 1, 128)
    return w


def _layer(x, ada, l, p, w, tm, past):
    ffn = lambda xx, j, s: _ffn(xx, ada, j, p["g_pre"][l, j], p["g_post"][l, j],
                                w["w_ffn_in"], w["w_ffn_down"], l, s, tm["ffn"])
    x = ffn(x, 0, 0)
    pf, pb = _mixin(x, ada, p["g_pre"][l, 1], w["w_in"], l, tm["mixin"])
    st = {}
    blk = lambda arr, c: arr[..., c * COL:(c + 1) * COL]
    if past is None:
        b, t, _ = x.shape
        o_a, st["hgrn"] = _hgrn_prompt(pf, pb, p["hg_lb_logits"], l, tm["hgrn"])
        o_b, kt, vt = _sb_prompt(pf, pb, p["sb_bias"][l])
        q, k, v, gates = _ml_pre_seq(pf, p["ml_conv_w"][l], p["ml_conv_b"][l], w["ml_wq"][l],
                                     w["ml_wk"][l], w["ml_wv"][l], w["wif"][l], w["bif"][l], tm["mlpre"])
        o_c, st["mc"], st["mn"], m_pad = _ml_prompt(q, k, v, gates)
        st["mm"] = m_pad[:, :, 0]
        st["mconv"] = blk(pf, PF_MX)[:, t - (ML_CONV - 1):]
        st["k"] = kt.reshape(b, SB_HEADS, SB_HEAD_DIM, t)
        st["v"] = vt.reshape(b, SB_HEADS, SB_HEAD_DIM, t)
    else:
        b = x.shape[1]
        rows = lambda arr, c: blk(arr, c).reshape(b, COL).astype(F32)
        mx = blk(pf, PF_MX)
        o_a, st["hgrn"] = _hgrn_step(rows(pb, PB_Q), rows(pf, PF_Z), rows(pb, PB_I), past["hgrn"],
                                     p["hg_lb_logits"], l)
        o_b = _sb_decode(rows(pb, PB_SQ), p["sb_bias"][l], past["k"], past["v"], past["page_table"], l)
        buf = past["mconv"][:, l]
        taps = [buf[:, j].reshape(1, b, ML_WIDTH) for j in range(ML_CONV - 1)]
        q, k, v, gates = _ml_pre(taps + [mx], p["ml_conv_w"][l], p["ml_conv_b"][l], w["ml_wq"][l],
                                 w["ml_wk"][l], w["ml_wv"][l], w["wif"][l], w["bif"][l], b)
        hc, st["mc"], st["mn"], m_pad = _ml_step(q.reshape(b, 1, ML_WIDTH), k.reshape(b, 1, ML_WIDTH),
                                                 v.reshape(b, 1, ML_WIDTH), gates.reshape(b, 1, 128),
                                                 past["mc"], past["mn"], past["mm"], l)
        o_a = o_a.reshape(1, b, HG_WIDTH)
        o_b = o_b.reshape(1, b, SB_WIDTH).astype(BF16)
        o_c = hc.reshape(1, b, ML_WIDTH).astype(BF16)
        st["mm"] = m_pad[:, 0, :ML_HEADS]
        st["mconv"] = jnp.concatenate([buf[:, 1:], mx.reshape(b, 1, ML_WIDTH)], axis=1)
        st["k"] = blk(pf, PF_K).reshape(b, 1, SB_HEADS, SB_HEAD_DIM)
        st["v"] = blk(pf, PF_V).reshape(b, 1, SB_HEADS, SB_HEAD_DIM)
    x = _merge(x, ada, p["g_post"][l, 1], pb, o_a, o_b, o_c, p["hg_gain"][l], p["ml_gain"][l],
               w["w_hg_proj"][l], w["w_sb_proj"][l], w["w_ml_proj"][l], w["w_out"][l], tm["merge"])
    x = ffn(x, 2, 1)
    return x, st


def _run(x, ada_all, p, w, tm, past):
    outs = []
    for l in range(DEPTH):
        x, st = _layer(x, ada_all[l], l, p, w, tm, past)
        outs.append(st)
    return x, {name: jnp.stack([o[name] for o in outs], axis=1) for name in outs[0]}


def kernel(x_prompt, x_sample, cache_sb_k, cache_sb_v, state_hgrn, state_mlstm_c, state_mlstm_n,
           state_mlstm_m, state_mlstm_conv, page_table, c_prompt, c_sample, w_ada, b_ada, g_pre, g_post,
           w_ffn_in, w_ffn_down, w_in, sb_bias, hg_lb_logits, hg_gain, ml_conv_w, ml_conv_b, ml_wq, ml_wk,
           ml_wv, ml_w_if, ml_b_if, ml_gain, w_hg_proj, w_sb_proj, w_ml_proj, w_out):
    p = dict(g_pre=g_pre, g_post=g_post, w_ffn_in=w_ffn_in, w_ffn_down=w_ffn_down, w_in=w_in,
             sb_bias=sb_bias, hg_lb_logits=hg_lb_logits, hg_gain=hg_gain, ml_conv_w=ml_conv_w,
             ml_conv_b=ml_conv_b, ml_wq=ml_wq, ml_wk=ml_wk, ml_wv=ml_wv, ml_w_if=ml_w_if,
             ml_b_if=ml_b_if, ml_gain=ml_gain, w_hg_proj=w_hg_proj, w_sb_proj=w_sb_proj,
             w_ml_proj=w_ml_proj, w_out=w_out)
    w = _prep_weights(p)
    bp, t, d = x_prompt.shape
    bs = x_sample.shape[0]
    ada = _ada(jnp.concatenate([c_prompt, c_sample], axis=0), w_ada, b_ada)
    ada_p = ada[:, :bp].reshape(DEPTH, bp, 1, N_SUB * 3 * d)
    ada_s = ada[:, bp:].reshape(DEPTH, 1, bs, N_SUB * 3 * d)

    tm_p = dict(ffn=min(t, 512), mixin=min(t, 512), merge=min(t, 256), hgrn=min(t, 512),
                mlpre=min(t, 512))
    y_p, sp = _run(x_prompt, ada_p, p, w, tm_p, None)

    past = dict(k=jnp.transpose(cache_sb_k, (0, 1, 3, 4, 2)), v=jnp.transpose(cache_sb_v, (0, 1, 3, 4, 2)),
                page_table=page_table, hgrn=state_hgrn, mc=state_mlstm_c, mn=state_mlstm_n,
                mm=state_mlstm_m, mconv=state_mlstm_conv)
    tm_s = dict(ffn=bs, mixin=bs, merge=bs)
    y_s, ss = _run(x_sample.reshape(1, bs, d), ada_s, p, w, tm_s, past)
    y_s = y_s.reshape(bs, 1, d)
    k_p = jnp.transpose(sp["k"], (0, 1, 4, 2, 3))
    v_p = jnp.transpose(sp["v"], (0, 1, 4, 2, 3))
    return (y_p, y_s, k_p, v_p, ss["k"], ss["v"], sp["hgrn"], ss["hgrn"],
            sp["mc"], ss["mc"], sp["mn"], ss["mn"], sp["mm"], ss["mm"], sp["mconv"], ss["mconv"])
```

```python
import functools

import jax
import jax.numpy as jnp
from jax import lax
from jax.experimental import pallas as pl
from jax.experimental.pallas import tpu as pltpu

F32 = jnp.float32
BF16 = jnp.bfloat16

D_MODEL = 1024
DEPTH = 2
PAGE_SIZE = 128
HG_HEADS = 4
HG_DK = 128
HG_WIDTH = 512
HG_CHUNK = 64
HG_SUB = 16
HG_SAFE_DECAY = 80.0
SB_HEADS = 8
SB_HEAD_DIM = 64
SB_WIDTH = 512
SB_TQ = 128
SB_TK = 256
SB_HEAD_GROUP = 8
ML_HEADS = 4
ML_HEAD_DIM = 128
ML_WIDTH = 512
ML_CHUNK = 128
ML_CONV = 4
D_FF = 2816
FF_CHUNK = 256
FFN_RES = 0.5
N_SUB = 3
NORM_EPS = 1e-6
IN_COLS = 7680
COL = 512
PF_SRC = (1, 5, 6, 7)
PB_SRC = (0, 2, 3, 4, 8, 9, 10, 11, 12, 13, 14)
PF_COLS = len(PF_SRC) * COL
PB_COLS = len(PB_SRC) * COL
PF_Z, PF_K, PF_V, PF_MX = 0, 1, 2, 3
PB_Q, PB_I, PB_HG, PB_SQ, PB_MO, PB_GATES = 0, 1, 2, 3, 4, 5
V7X_VMEM_LIMIT = 56 * 1024 * 1024
NEG_INF = float("-inf")
LOG2E = 1.4426950408889634


def _dot(a, b):
    return jnp.dot(a, b, preferred_element_type=F32)


def _dot_nt(a, b):
    return lax.dot_general(a, b, (((1,), (1,)), ((), ())), preferred_element_type=F32)


def _split3(x):
    x1 = x.astype(BF16)
    r1 = x - x1.astype(F32)
    x2 = r1.astype(BF16)
    x3 = (r1 - x2.astype(F32)).astype(BF16)
    return x1, x2, x3


def _tri_dot(tri, x):
    x1, x2, x3 = _split3(x)
    return _dot(tri, x1) + _dot(tri, x2) + _dot(tri, x3)


def _dot_tri(x, tri):
    x1, x2, x3 = _split3(x)
    return _dot(x1, tri) + _dot(x2, tri) + _dot(x3, tri)


def _sigmoid(x):
    return 0.5 * jnp.tanh(0.5 * x) + 0.5


def _silu(x):
    return x * _sigmoid(x)


def _log_sigmoid(x):
    return jnp.minimum(x, 0.0) - jnp.log1p(jnp.exp(-jnp.abs(x)))


def _logaddexp(a, b):
    amax = jnp.maximum(a, b)
    delta = a - b
    return jnp.where(delta != delta, a + b, amax + jnp.log1p(jnp.exp(-jnp.abs(delta))))


def _rms(x, g):
    return x * lax.rsqrt(jnp.mean(x * x, axis=-1, keepdims=True) + NORM_EPS) * g


def _head_norm(o, gain, heads, width):
    parts = []
    for h in range(heads):
        oh = o[:, h * width:(h + 1) * width]
        parts.append(oh * lax.rsqrt(jnp.mean(oh * oh, axis=-1, keepdims=True) + NORM_EPS))
    return jnp.concatenate(parts, axis=-1) * gain


def _params(sem, vmem=None):
    return pltpu.CompilerParams(dimension_semantics=sem, vmem_limit_bytes=vmem)


def _ada_kernel(c_ref, w_ref, b_ref, o_ref):
    s = _silu(c_ref[...]).astype(BF16)
    o_ref[...] = _dot(s, w_ref[...].astype(BF16)) + b_ref[...]


def _ada(c_all, w_ada, b_ada):
    n, d = c_all.shape
    cols = w_ada.shape[-1]
    tn = 1536
    return pl.pallas_call(
        _ada_kernel,
        grid=(DEPTH, cols // tn),
        in_specs=[pl.BlockSpec((n, d), lambda l, j: (0, 0)),
                  pl.BlockSpec((None, d, tn), lambda l, j: (l, 0, j)),
                  pl.BlockSpec((None, 1, tn), lambda l, j: (l, 0, j))],
        out_specs=pl.BlockSpec((None, n, tn), lambda l, j: (l, 0, j)),
        out_shape=jax.ShapeDtypeStruct((DEPTH, n, cols), F32),
        compiler_params=_params(("parallel", "parallel"), V7X_VMEM_LIMIT),
    )(c_all, w_ada, b_ada.reshape(DEPTH, 1, cols))


def _mod_spec(ada, tm, col):
    rm = ada.shape[1]
    if rm == 1:
        return pl.BlockSpec((None, 1, D_MODEL), lambda g, i: (g, 0, col))
    return pl.BlockSpec((None, tm, D_MODEL), lambda g, i: (g, i, col))


def _ffn_kernel(x_ref, sh_ref, sc_ref, gt_ref, gpre_ref, gpost_ref, win_ref, wd_ref, o_ref, h_ref, a_ref):
    x = x_ref[...]
    h_ref[...] = (_rms(x, gpre_ref[...]) * (1.0 + sc_ref[...]) + sh_ref[...]).astype(BF16)
    for f in range(D_FF // FF_CHUNK):
        cols = slice(f * FF_CHUNK, (f + 1) * FF_CHUNK)
        h = h_ref[...]
        g = _dot(h, win_ref[:, cols])
        u = _dot(h, win_ref[:, D_FF + f * FF_CHUNK:D_FF + (f + 1) * FF_CHUNK])
        a_ref[:, cols] = (_silu(g) * u).astype(BF16)
    y = _dot(a_ref[...], wd_ref[...])
    o_ref[...] = x + FFN_RES * (1.0 + gt_ref[...]) * _rms(y, gpost_ref[...])


def _ffn(x, ada, j, gpre, gpost, w_in, w_down, l, s, tm):
    g, r, d = x.shape
    xspec = pl.BlockSpec((None, tm, d), lambda a, i: (a, i, 0))
    vec = pl.BlockSpec((1, d), lambda a, i: (0, 0))
    res = lambda shape: pl.BlockSpec((None, None) + shape, lambda a, i: (l, s, 0, 0),
                                     pipeline_mode=pl.Buffered(1))
    return pl.pallas_call(
        _ffn_kernel,
        grid=(g, r // tm),
        in_specs=[xspec, _mod_spec(ada, tm, 3 * j), _mod_spec(ada, tm, 3 * j + 1),
                  _mod_spec(ada, tm, 3 * j + 2), vec, vec, res((d, 2 * D_FF)), res((D_FF, d))],
        out_specs=xspec,
        out_shape=jax.ShapeDtypeStruct(x.shape, F32),
        scratch_shapes=[pltpu.VMEM((tm, d), BF16), pltpu.VMEM((tm, D_FF), BF16)],
        compiler_params=_params(("parallel", "parallel"), V7X_VMEM_LIMIT),
    )(x, ada, ada, ada, gpre.reshape(1, d), gpost.reshape(1, d), w_in, w_down)


def _mixin_kernel(x_ref, sh_ref, sc_ref, gpre_ref, w_ref, pf_ref, pb_ref, h_ref):
    h_ref[...] = (_rms(x_ref[...], gpre_ref[...]) * (1.0 + sc_ref[...]) + sh_ref[...]).astype(BF16)
    nf = PF_COLS // COL
    for c, src in enumerate(PF_SRC + PB_SRC):
        y = _dot(h_ref[...], w_ref[:, src * COL:(src + 1) * COL])
        if c < nf:
            pf_ref[:, c * COL:(c + 1) * COL] = y
        else:
            pb_ref[:, (c - nf) * COL:(c - nf + 1) * COL] = y.astype(BF16)


def _mixin(x, ada, gpre, w_in, l, tm):
    g, r, d = x.shape
    return pl.pallas_call(
        _mixin_kernel,
        grid=(g, r // tm),
        in_specs=[pl.BlockSpec((None, tm, d), lambda a, i: (a, i, 0)),
                  _mod_spec(ada, tm, 3), _mod_spec(ada, tm, 4),
                  pl.BlockSpec((1, d), lambda a, i: (0, 0)),
                  pl.BlockSpec((None, d, IN_COLS), lambda a, i: (l, 0, 0), pipeline_mode=pl.Buffered(1))],
        out_specs=[pl.BlockSpec((None, tm, PF_COLS), lambda a, i: (a, i, 0)),
                   pl.BlockSpec((None, tm, PB_COLS), lambda a, i: (a, i, 0))],
        out_shape=[jax.ShapeDtypeStruct((g, r, PF_COLS), F32), jax.ShapeDtypeStruct((g, r, PB_COLS), BF16)],
        scratch_shapes=[pltpu.VMEM((tm, d), BF16)],
        compiler_params=_params(("parallel", "parallel"), V7X_VMEM_LIMIT),
    )(x, ada, ada, gpre.reshape(1, d), w_in)


def _merge_kernel(x_ref, gt_ref, gpost_ref, oa_ref, hg_ref, ob_ref, oc_ref, mo_ref,
                  g0, g1, g2, g3, g4, g5, hgain_ref, mgain_ref,
                  whg_ref, wsb_ref, wml_ref, wout_ref, o_ref):
    f32 = lambda ref: ref[...].astype(F32)
    o_a = (_head_norm(oa_ref[...], hgain_ref[...], HG_HEADS, HG_DK) * _silu(f32(hg_ref))).astype(BF16)
    o_b = ob_ref[...]
    o_c = (_head_norm(f32(oc_ref), mgain_ref[...], ML_HEADS, ML_HEAD_DIM) * _sigmoid(f32(mo_ref))).astype(BF16)
    gates = ((g0, g2, g4), (g1, g3, g5))
    y = None
    for c in range(2):
        cs = slice(c * COL, (c + 1) * COL)
        ga, gb, gc = gates[c]
        m = (_sigmoid(f32(ga)) * _dot(o_a, whg_ref[:, cs])
             + _sigmoid(f32(gb)) * _dot(o_b, wsb_ref[:, cs])
             + _sigmoid(f32(gc)) * _dot(o_c, wml_ref[:, cs]))
        part = _dot(m.astype(BF16), wout_ref[cs, :])
        y = part if y is None else y + part
    o_ref[...] = x_ref[...] + (1.0 + gt_ref[...]) * _rms(y, gpost_ref[...])


def _merge(x, ada, gpost, pb, o_a, o_b, o_c, hgain, mgain, whg, wsb, wml, wout, tm):
    g, r, d = x.shape
    row = lambda col: pl.BlockSpec((None, tm, COL), lambda a, i: (a, i, col))
    vec = lambda n: pl.BlockSpec((1, n), lambda a, i: (0, 0))
    wsp = lambda shape: pl.BlockSpec(shape, lambda a, i: (0, 0))
    xspec = pl.BlockSpec((None, tm, d), lambda a, i: (a, i, 0))
    return pl.pallas_call(
        _merge_kernel,
        grid=(g, r // tm),
        in_specs=[xspec, _mod_spec(ada, tm, 5), vec(d),
                  row(0), row(PB_HG), row(0), row(0), row(PB_MO)]
                 + [row(PB_GATES + c) for c in range(6)]
                 + [vec(HG_WIDTH), vec(ML_WIDTH),
                    wsp((HG_WIDTH, d)), wsp((SB_WIDTH, d)), wsp((ML_WIDTH, d)), wsp((d, d))],
        out_specs=xspec,
        out_shape=jax.ShapeDtypeStruct(x.shape, F32),
        compiler_params=_params(("parallel", "parallel"), V7X_VMEM_LIMIT),
    )(x, ada, gpost.reshape(1, d), o_a, pb, o_b, o_c, pb,
      pb, pb, pb, pb, pb, pb,
      hgain.reshape(1, HG_WIDTH), mgain.reshape(1, ML_WIDTH), whg, wsb, wml, wout)


def _hgrn_lb(lbl, layer):
    e = jnp.exp(lbl - jnp.max(lbl, axis=0, keepdims=True))
    p = e / jnp.sum(e, axis=0, keepdims=True)
    lb = jnp.zeros_like(p[0:1])
    for r in range(1, layer + 1):
        lb = lb + p[r:r + 1]
    return lb


def _hgrn_gates(z, lb):
    logf = _logaddexp(jnp.log(lb), jnp.log1p(-lb) + _log_sigmoid(z))
    k = (1.0 - lb) * _sigmoid(-z)
    return logf, k


def _hgrn_prompt_kernel(lbl_ref, qb_ref, z_ref, ib_ref, o_ref, s_ref, st_ref, b_ref, k_ref, q_ref, i_ref,
                        *, layer):
    tc = qb_ref.shape[0]
    q_ref[...] = qb_ref[...].astype(F32)
    i_ref[...] = ib_ref[...].astype(F32)
    nchunk = tc // HG_CHUNK
    nsub = HG_CHUNK // HG_SUB

    @pl.when(pl.program_id(1) == 0)
    def _():
        st_ref[...] = jnp.zeros_like(st_ref)

    lb = _hgrn_lb(lbl_ref[...], layer)[0]
    logf, k = _hgrn_gates(z_ref[...], lb)
    k_ref[...] = k
    ri = lax.broadcasted_iota(jnp.int32, (tc, tc), 0)
    ci = lax.broadcasted_iota(jnp.int32, (tc, tc), 1)
    tri = jnp.where((ci <= ri) & (ri // HG_CHUNK == ci // HG_CHUNK), 1.0, 0.0).astype(BF16)
    b_ref[...] = _tri_dot(tri, logf)
    block_decay = -jnp.sum(logf.reshape(tc // HG_SUB, HG_SUB, HG_WIDTH), axis=1)
    safe = jnp.max(block_decay) < HG_SAFE_DECAY

    rows64 = lax.broadcasted_iota(jnp.int32, (HG_CHUNK, 1), 0)
    rows16 = lax.broadcasted_iota(jnp.int32, (HG_SUB, 1), 0)
    r64 = lax.broadcasted_iota(jnp.int32, (HG_CHUNK, HG_CHUNK), 0)
    c64 = lax.broadcasted_iota(jnp.int32, (HG_CHUNK, HG_CHUNK), 1)
    causal64 = c64 <= r64

    heads = range(HG_HEADS)
    sl = [slice(h * HG_DK, (h + 1) * HG_DK) for h in heads]

    def score_operands(q, b, kk, cs, s_i, include_diag):
        i0 = s_i * HG_SUB
        r = b[i0 - 1:i0, cs] if s_i else jnp.zeros((1, HG_DK), F32)
        hi = i0 + HG_SUB if include_diag else i0
        cap = HG_SAFE_DECAY if include_diag else 0.0
        qt = q[i0:i0 + HG_SUB, cs] * jnp.exp(b[i0:i0 + HG_SUB, cs] - r)
        kt = jnp.where(rows64 < hi, kk[:, cs] * jnp.exp(jnp.minimum(r - b[:, cs], cap)), 0.0)
        return qt.astype(BF16), kt.astype(BF16)

    def chunk(r0):
        rows = pl.ds(r0, HG_CHUNK)
        q = q_ref[rows, :]
        i = i_ref[rows, :]
        b = b_ref[rows, :]
        kk = k_ref[rows, :]
        blast = b[HG_CHUNK - 1:HG_CHUNK, :]
        eb = jnp.exp(b)
        kd = kk * jnp.exp(blast - b)
        ib = [i[:, cs].astype(BF16) for cs in sl]
        ops = [[score_operands(q, b, kk, cs, s_i, True) for s_i in range(nsub)] for cs in sl]
        p = [jnp.concatenate([_dot_nt(qt, kt) for qt, kt in ops[h]], axis=0) for h in heads]
        st = [st_ref[h] for h in heads]
        o_state = [_dot_nt((q[:, sl[h]] * eb[:, sl[h]]).astype(BF16), st[h].astype(BF16)) for h in heads]
        pm = [jnp.where(causal64 & safe, p[h], 0.0).astype(BF16) for h in heads]
        o_intra = [_dot(pm[h], ib[h]) for h in heads]
        upd = [_dot(i[:, sl[h]].T.astype(BF16), kd[:, sl[h]].astype(BF16)) for h in heads]
        o_ref[rows, :] = jnp.concatenate([o_state[h] + o_intra[h] for h in heads], axis=-1)
        for h in heads:
            st_ref[h] = st[h] * jnp.exp(blast[:, sl[h]]) + upd[h]

        @pl.when(jnp.logical_not(safe))
        def _():
            outs = []
            for h in heads:
                rows_p = [jnp.zeros((HG_SUB, HG_CHUNK), F32)]
                for s_i in range(1, nsub):
                    qt, kt = score_operands(q, b, kk, sl[h], s_i, False)
                    rows_p.append(_dot_nt(qt, kt))
                outs.append(_dot(jnp.concatenate(rows_p, axis=0).astype(BF16), ib[h]))
            o_ref[rows, :] += jnp.concatenate(outs, axis=-1)
            for s_i in range(nsub):
                base = r0 + s_i * HG_SUB
                srows = pl.ds(base, HG_SUB)
                q_i = q_ref[srows, :]
                b_i = b_ref[srows, :]

                def s_body(s, acc):
                    b_s = b_ref[pl.ds(base + s, 1), :]
                    k_s = k_ref[pl.ds(base + s, 1), :]
                    i_s = i_ref[pl.ds(base + s, 1), :]
                    e = jnp.where(rows16 >= s, jnp.exp(jnp.minimum(b_i - b_s, 0.0)), 0.0)
                    pr = q_i * e * k_s
                    parts = [jnp.sum(pr[:, cs], axis=1, keepdims=True) * i_s[:, cs] for cs in sl]
                    return acc + jnp.concatenate(parts, axis=-1)

                acc = lax.fori_loop(0, HG_SUB, s_body, jnp.zeros((HG_SUB, HG_WIDTH), F32))
                o_ref[srows, :] += acc

    def pair_body(n, carry):
        r0 = pl.multiple_of(n * (2 * HG_CHUNK), 2 * HG_CHUNK)
        chunk(r0)
        chunk(r0 + HG_CHUNK)
        return carry

    lax.fori_loop(0, nchunk // 2, pair_body, 0)

    @pl.when(pl.program_id(1) == pl.num_programs(1) - 1)
    def _():
        for h in range(HG_HEADS):
            s_ref[h] = st_ref[h].T


def _hgrn_prompt(pf, pb, lbl, layer, tc):
    b, t, _ = pf.shape
    row = lambda col: pl.BlockSpec((None, tc, COL), lambda a, i: (a, i, col))
    return pl.pallas_call(
        functools.partial(_hgrn_prompt_kernel, layer=layer),
        grid=(b, t // tc),
        in_specs=[pl.BlockSpec((DEPTH, 1, HG_WIDTH), lambda a, i: (0, 0, 0)), row(PB_Q), row(PF_Z), row(PB_I)],
        out_specs=[pl.BlockSpec((None, tc, HG_WIDTH), lambda a, i: (a, i, 0)),
                   pl.BlockSpec((None, HG_HEADS, HG_DK, HG_DK), lambda a, i: (a, 0, 0, 0))],
        out_shape=[jax.ShapeDtypeStruct((b, t, HG_WIDTH), F32),
                   jax.ShapeDtypeStruct((b, HG_HEADS, HG_DK, HG_DK), F32)],
        scratch_shapes=[pltpu.VMEM((HG_HEADS, HG_DK, HG_DK), F32)] + [pltpu.VMEM((tc, HG_WIDTH), F32)] * 4,
        compiler_params=_params(("parallel", "arbitrary"), V7X_VMEM_LIMIT),
    )(lbl.reshape(DEPTH, 1, HG_WIDTH), pb, pf, pb)


def _hgrn_step_kernel(lbl_ref, qc_ref, zc_ref, ir_ref, s_ref, o_ref, so_ref, *, layer):
    lb = _hgrn_lb(lbl_ref[...], layer)[0]
    for n in range(qc_ref.shape[0]):
        logf, k = _hgrn_gates(zc_ref[n], lb)
        outs = []
        for h in range(HG_HEADS):
            cs = slice(h * HG_DK, (h + 1) * HG_DK)
            s_new = jnp.exp(logf[h]) * s_ref[n, h] + k[h] * ir_ref[n, :, cs]
            so_ref[n, h] = s_new
            outs.append(jnp.sum(qc_ref[n, h] * s_new, axis=0, keepdims=True))
        o_ref[n] = jnp.concatenate(outs, axis=-1)


STEP_SEQS = 4


def _hgrn_step(q, z, i, state, lbl, layer):
    b = q.shape[0]
    qc = q.reshape(b, HG_HEADS, HG_DK, 1)
    zc = z.reshape(b, HG_HEADS, HG_DK, 1)
    ns = STEP_SEQS if b % STEP_SEQS == 0 else 1
    col = pl.BlockSpec((ns, HG_HEADS, HG_DK, 1), lambda a: (a, 0, 0, 0))
    return pl.pallas_call(
        functools.partial(_hgrn_step_kernel, layer=layer),
        grid=(b // ns,),
        in_specs=[pl.BlockSpec((DEPTH, HG_HEADS, HG_DK, 1), lambda a: (0, 0, 0, 0)), col, col,
                  pl.BlockSpec((ns, 1, COL), lambda a: (a, 0, 0)),
                  pl.BlockSpec((ns, None, HG_HEADS, HG_DK, HG_DK), lambda a: (a, layer, 0, 0, 0))],
        out_specs=[pl.BlockSpec((ns, 1, HG_WIDTH), lambda a: (a, 0, 0)),
                   pl.BlockSpec((ns, HG_HEADS, HG_DK, HG_DK), lambda a: (a, 0, 0, 0))],
        out_shape=[jax.ShapeDtypeStruct((b, 1, HG_WIDTH), F32),
                   jax.ShapeDtypeStruct((b, HG_HEADS, HG_DK, HG_DK), F32)],
        compiler_params=_params(("parallel",)),
    )(lbl.reshape(DEPTH, HG_HEADS, HG_DK, 1), qc, zc, i.reshape(b, 1, COL), state)


def _softplus(z):
    return jnp.maximum(z, 0.0) + jnp.log1p(jnp.exp(-jnp.abs(z)))


def _sb_prompt_kernel(bias_ref, q_ref, k_ref, v_ref, o_ref, kt_out, vt_out,
                      qh_ref, kt_ref, vh_ref, acc_ref, c_ref):
    tq, tk = SB_TQ, SB_TK
    nk = k_ref.shape[0] // tk
    qi = pl.program_id(1)

    @pl.when(qi == 0)
    def _():
        for n in range(nk):
            ks = slice(n * tk, (n + 1) * tk)
            kt_full = k_ref[ks, :].T
            kt_out[:, ks] = kt_full
            vt_out[:, ks] = v_ref[ks, :].T
            for h in range(SB_HEADS):
                cs = slice(h * SB_HEAD_DIM, (h + 1) * SB_HEAD_DIM)
                kt_ref[h, n] = kt_full[cs, :].astype(BF16)
                vh_ref[h, n] = v_ref[ks, cs].astype(BF16)

    for h in range(SB_HEADS):
        cs = slice(h * SB_HEAD_DIM, (h + 1) * SB_HEAD_DIM)
        qh_ref[h] = (q_ref[:, cs].astype(F32) * (SB_HEAD_DIM ** -0.5 * LOG2E)).astype(BF16)
    acc_ref[...] = jnp.zeros_like(acc_ref)
    c_ref[...] = jnp.zeros_like(c_ref)

    ri = lax.broadcasted_iota(jnp.int32, (tk, tk), 0)
    ci = lax.broadcasted_iota(jnp.int32, (tk, tk), 1)
    upper = jnp.where(ri > ci, 1.0, 0.0).astype(BF16)
    jd = (qi * tq) // tk
    t_pos = qi * tq + lax.broadcasted_iota(jnp.int32, (tq, tk), 0)
    s_pos = jd * tk + lax.broadcasted_iota(jnp.int32, (tq, tk), 1)
    causal = s_pos < t_pos

    def tile(j, mask, nkeys=SB_TK):
        for g0 in range(0, SB_HEADS, SB_HEAD_GROUP):
            heads = range(g0, g0 + SB_HEAD_GROUP)
            zs = [_dot(qh_ref[h], kt_ref[h, j, :, :nkeys]) + bias_ref[h] * LOG2E for h in heads]
            lbs, l1ms = [], []
            for z in zs:
                sp = jnp.maximum(z, 0.0) + jnp.log(1.0 + jnp.exp2(-jnp.abs(z))) * LOG2E
                l1m = -sp
                if mask is not None:
                    l1m = jnp.where(mask, l1m, 0.0)
                lbs.append(z - sp)
                l1ms.append(l1m)
            sufs = [_dot(l1m.astype(BF16), upper[:nkeys, :nkeys]) for l1m in l1ms]
            weights = []
            for h, lb, l1m, suf in zip(heads, lbs, l1ms, sufs):
                c = c_ref[h]
                a = jnp.exp2(lb + suf + c)
                if mask is not None:
                    a = jnp.where(mask, a, 0.0)
                weights.append(a.astype(BF16))
                c_ref[h] = c + suf[:, 0:1] + l1m[:, 0:1]
            for h, a in zip(heads, weights):
                acc_ref[h] += _dot(a, vh_ref[h, j, :nkeys, :])

    first_half = (qi * tq) % tk == 0

    @pl.when(first_half)
    def _():
        tile(jd, causal[:, :tq], tq)

    @pl.when(jnp.logical_not(first_half))
    def _():
        tile(jd, causal)

    def body(step, carry):
        tile(jd - 1 - step, None)
        return carry

    lax.fori_loop(0, jd, body, 0)
    o_ref[...] = jnp.concatenate([acc_ref[h] for h in range(SB_HEADS)], axis=-1).astype(BF16)


def _sb_prompt(pf, pb, bias):
    b, t, _ = pf.shape
    tq = min(SB_TQ, t)
    nk = t // SB_TK
    seq = lambda col: pl.BlockSpec((None, t, COL), lambda a, i: (a, 0, col))
    return pl.pallas_call(
        _sb_prompt_kernel,
        grid=(b, t // tq),
        in_specs=[pl.BlockSpec(memory_space=pltpu.SMEM),
                  pl.BlockSpec((None, tq, COL), lambda a, i: (a, i, PB_SQ)), seq(PF_K), seq(PF_V)],
        out_specs=[pl.BlockSpec((None, tq, SB_WIDTH), lambda a, i: (a, i, 0)),
                   pl.BlockSpec((None, SB_WIDTH, t), lambda a, i: (a, 0, 0)),
                   pl.BlockSpec((None, SB_WIDTH, t), lambda a, i: (a, 0, 0))],
        out_shape=[jax.ShapeDtypeStruct((b, t, SB_WIDTH), BF16),
                   jax.ShapeDtypeStruct((b, SB_WIDTH, t), F32),
                   jax.ShapeDtypeStruct((b, SB_WIDTH, t), F32)],
        scratch_shapes=[pltpu.VMEM((SB_HEADS, tq, SB_HEAD_DIM), BF16),
                        pltpu.VMEM((SB_HEADS, nk, SB_HEAD_DIM, SB_TK), BF16),
                        pltpu.VMEM((SB_HEADS, nk, SB_TK, SB_HEAD_DIM), BF16),
                        pltpu.VMEM((SB_HEADS, tq, SB_HEAD_DIM), F32),
                        pltpu.VMEM((SB_HEADS, tq, 1), F32)],
        compiler_params=_params(("parallel", "arbitrary"), V7X_VMEM_LIMIT),
    )(bias, pb, pf, pf)


SB_PAGES_PER_STEP = 16


def _sb_decode_kernel(pt_ref, q_ref, bias_ref, *refs):
    npg = SB_PAGES_PER_STEP
    k_refs = refs[:npg]
    v_refs = refs[npg:2 * npg]
    o_ref, qb_ref, c_ref, acc_ref = refs[2 * npg:]
    g = pl.program_id(1)

    @pl.when(g == 0)
    def _():
        qb_ref[...] = jnp.broadcast_to(q_ref[...], qb_ref.shape)
        c_ref[...] = jnp.zeros_like(c_ref)
        acc_ref[...] = jnp.zeros_like(acc_ref)

    ri = lax.broadcasted_iota(jnp.int32, (PAGE_SIZE, PAGE_SIZE), 0)
    ci = lax.broadcasted_iota(jnp.int32, (PAGE_SIZE, PAGE_SIZE), 1)
    upper = jnp.where(ri > ci, 1.0, 0.0).astype(BF16)
    bias = bias_ref[...]
    c = c_ref[:, 0:1]
    weights = [None] * npg
    for p in range(npg - 1, -1, -1):
        z = jnp.sum(k_refs[p][...] * qb_ref[...], axis=1) + bias
        sp = _softplus(z)
        l1m = -sp
        suf = _dot(l1m.astype(BF16), upper)
        weights[p] = jnp.exp(z - sp + suf + c)
        c = c + suf[:, 0:1] + l1m[:, 0:1]
    c_ref[...] = jnp.broadcast_to(c, c_ref.shape)
    for h in range(SB_HEADS):
        part = weights[0][h:h + 1, :] * v_refs[0][h]
        for p in range(1, npg):
            part = part + weights[p][h:h + 1, :] * v_refs[p][h]
        acc_ref[h] += part

    @pl.when(g == pl.num_programs(1) - 1)
    def _():
        o_ref[...] = jnp.sum(acc_ref[...], axis=2, keepdims=True)


def _sb_decode(sq, bias, cache_kt, cache_vt, page_table, layer):
    b = sq.shape[0]
    n_pages = page_table.shape[1]
    npg = SB_PAGES_PER_STEP
    ng = n_pages // npg
    qcol = (sq * (SB_HEAD_DIM ** -0.5)).reshape(b, SB_HEADS, SB_HEAD_DIM, 1)

    def page_spec(p):
        return pl.BlockSpec((None, None, SB_HEADS, SB_HEAD_DIM, PAGE_SIZE),
                            lambda a, g, pt: (pt[a, (ng - 1 - g) * npg + p], layer, 0, 0, 0))

    hd = pl.BlockSpec((None, SB_HEADS, SB_HEAD_DIM, 1), lambda a, g, pt: (a, 0, 0, 0))
    grid_spec = pltpu.PrefetchScalarGridSpec(
        num_scalar_prefetch=1,
        grid=(b, ng),
        in_specs=[hd, pl.BlockSpec((SB_HEADS, 1), lambda a, g, pt: (0, 0))]
                 + [page_spec(p) for p in range(npg)] + [page_spec(p) for p in range(npg)],
        out_specs=hd,
        scratch_shapes=[pltpu.VMEM((SB_HEADS, SB_HEAD_DIM, PAGE_SIZE), F32),
                        pltpu.VMEM((SB_HEADS, 128), F32),
                        pltpu.VMEM((SB_HEADS, SB_HEAD_DIM, PAGE_SIZE), F32)],
    )
    out = pl.pallas_call(
        _sb_decode_kernel,
        grid_spec=grid_spec,
        out_shape=jax.ShapeDtypeStruct((b, SB_HEADS, SB_HEAD_DIM, 1), F32),
        compiler_params=_params(("parallel", "arbitrary"), V7X_VMEM_LIMIT),
    )(page_table, qcol, bias.reshape(SB_HEADS, 1), *([cache_kt] * npg), *([cache_vt] * npg))
    return out.reshape(b, 1, SB_WIDTH)


def _ml_pre_kernel(x0, x1, x2, x3, *refs):
    _ml_pre_body(x0[...], x1[...], x2[...], x3[...], *refs)


def _ml_pre_seq_kernel(prev_ref, x_ref, cw_ref, cb_ref, wq_ref, wk_ref, wv_ref, wif_ref, bif_ref,
                       q_ref, k_ref, v_ref, g_ref, xs_ref):
    tm = x_ref.shape[0]
    halo = prev_ref.shape[0]
    x = x_ref[...]
    xs_ref[halo:, :] = x
    xs_ref[:halo, :] = jnp.where(pl.program_id(1) == 0, 0.0, prev_ref[...])
    taps = [xs_ref[pl.ds(halo - (ML_CONV - 1) + j, tm), :] for j in range(ML_CONV - 1)]
    _ml_pre_body(*taps, x, cw_ref, cb_ref, wq_ref, wk_ref, wv_ref, wif_ref, bif_ref,
                 q_ref, k_ref, v_ref, g_ref)


def _ml_pre_body(x0, x1, x2, x, cw_ref, cb_ref, wq_ref, wk_ref, wv_ref, wif_ref, bif_ref,
                 q_ref, k_ref, v_ref, g_ref):
    cw = cw_ref[...]
    xc = _silu(x0 * cw[0:1] + x1 * cw[1:2] + x2 * cw[2:3] + x * cw[3:4] + cb_ref[...])
    xcb = xc.astype(BF16)
    xb = x.astype(BF16)
    qs, ks, vs = [], [], []
    for h in range(ML_HEADS):
        cs = slice(h * ML_HEAD_DIM, (h + 1) * ML_HEAD_DIM)
        qs.append(_dot(xcb[:, cs], wq_ref[h]))
        ks.append(_dot(xcb[:, cs], wk_ref[h]))
        vs.append(_dot(xb[:, cs], wv_ref[h]))
    q = jnp.concatenate(qs, axis=-1)
    k = jnp.concatenate(ks, axis=-1)
    v = jnp.concatenate(vs, axis=-1)
    q_ref[...] = q
    k_ref[...] = k
    v_ref[...] = v
    g_ref[...] = (_dot(q.astype(BF16), wif_ref[0:ML_WIDTH, :])
                  + _dot(k.astype(BF16), wif_ref[ML_WIDTH:2 * ML_WIDTH, :])
                  + _dot(v.astype(BF16), wif_ref[2 * ML_WIDTH:3 * ML_WIDTH, :]) + bif_ref[...])


def _ml_pre(xs, cw, cb, wq, wk, wv, wif, bif, tm):
    g, r, w = xs[0].shape
    row = pl.BlockSpec((None, tm, w), lambda a, i: (a, i, 0))
    full = lambda shape: pl.BlockSpec(shape, lambda a, i: (0,) * len(shape))
    out = jax.ShapeDtypeStruct((g, r, w), F32)
    return pl.pallas_call(
        _ml_pre_kernel,
        grid=(g, r // tm),
        in_specs=[row, row, row, row, full((ML_CONV, w)), full((1, w)),
                  full(wq.shape), full(wk.shape), full(wv.shape), full(wif.shape), full((1, 128))],
        out_specs=[row, row, row, pl.BlockSpec((None, tm, 128), lambda a, i: (a, i, 0))],
        out_shape=[out, out, out, jax.ShapeDtypeStruct((g, r, 128), F32)],
        compiler_params=_params(("parallel", "parallel"), V7X_VMEM_LIMIT),
    )(*xs, cw, cb.reshape(1, w), wq, wk, wv, wif, bif)


ML_HALO = 8


def _ml_pre_seq(pf, cw, cb, wq, wk, wv, wif, bif, tm):
    b, t, _ = pf.shape
    w = ML_WIDTH
    per = tm // ML_HALO
    row = pl.BlockSpec((None, tm, w), lambda a, i: (a, i, 0))
    full = lambda shape: pl.BlockSpec(shape, lambda a, i: (0,) * len(shape))
    out = jax.ShapeDtypeStruct((b, t, w), F32)
    return pl.pallas_call(
        _ml_pre_seq_kernel,
        grid=(b, t // tm),
        in_specs=[pl.BlockSpec((None, ML_HALO, w), lambda a, i: (a, jnp.maximum(i * per - 1, 0), PF_MX)),
                  pl.BlockSpec((None, tm, w), lambda a, i: (a, i, PF_MX)),
                  full((ML_CONV, w)), full((1, w)),
                  full(wq.shape), full(wk.shape), full(wv.shape), full(wif.shape), full((1, 128))],
        out_specs=[row, row, row, pl.BlockSpec((None, tm, 128), lambda a, i: (a, i, 0))],
        out_shape=[out, out, out, jax.ShapeDtypeStruct((b, t, 128), F32)],
        scratch_shapes=[pltpu.VMEM((tm + ML_HALO, w), F32)],
        compiler_params=_params(("parallel", "parallel"), V7X_VMEM_LIMIT),
    )(pf, pf, cw, cb.reshape(1, w), wq, wk, wv, wif, bif)


def _ml_prompt_kernel(q_ref, k_ref, v_ref, g_ref, h_ref, c_out, n_out, m_out, c_ref, n_ref, m_ref):
    L = ML_CHUNK
    nch = q_ref.shape[0] // L

    @pl.when(pl.program_id(1) == 0)
    def _():
        c_ref[...] = jnp.zeros_like(c_ref)
        n_ref[...] = jnp.zeros_like(n_ref)
        m_ref[...] = jnp.zeros_like(m_ref)

    ri = lax.broadcasted_iota(jnp.int32, (L, L), 0)
    ci = lax.broadcasted_iota(jnp.int32, (L, L), 1)
    causal = ci <= ri
    tril = jnp.where(causal, 1.0, 0.0).astype(BF16)
    triu = jnp.where(ri <= ci, 1.0, 0.0).astype(BF16)
    heads = range(ML_HEADS)
    sl = [slice(h * ML_HEAD_DIM, (h + 1) * ML_HEAD_DIM) for h in heads]

    pre = []
    for j in range(nch):
        rows = slice(j * L, (j + 1) * L)
        g = g_ref[rows, :]
        gt = g.T
        b_cols = _tri_dot(tril, _log_sigmoid(g))
        b_rows = _dot_tri(_log_sigmoid(gt), triu)
        q = [q_ref[rows, cs] for cs in sl]
        ks = [k_ref[rows, cs] * (ML_HEAD_DIM ** -0.5) for cs in sl]
        v = [v_ref[rows, cs] for cs in sl]
        qb = [x.astype(BF16) for x in q]
        ksb = [x.astype(BF16) for x in ks]
        qk = [_dot_nt(qb[h], ksb[h]) for h in heads]
        b_col = [b_cols[:, ML_HEADS + h:ML_HEADS + h + 1] for h in heads]
        log_d = [jnp.where(causal, b_col[h] - b_rows[ML_HEADS + h:ML_HEADS + h + 1, :] + gt[h:h + 1, :], NEG_INF)
                 for h in heads]
        d_max = [jnp.max(x, axis=1, keepdims=True) for x in log_d]
        pre.append(dict(g=g, q=q, ks=ks, v=v, qb=qb, ksb=ksb, qk=qk, b_col=b_col, log_d=log_d, d_max=d_max))

    c0 = [c_ref[h] for h in heads]
    n0 = [n_ref[h:h + 1, :] for h in heads]
    m0 = [m_ref[h:h + 1, 0:1] for h in heads]
    for j, pj in enumerate(pre):
        rows = slice(j * L, (j + 1) * L)
        qc = [_dot_nt(pj["qb"][h], c0[h].astype(BF16)) for h in heads]
        m_t, inter, w = [], [], []
        for h in heads:
            m_h = jnp.maximum(pj["b_col"][h] + m0[h], pj["d_max"][h])
            m_t.append(m_h)
            inter.append(jnp.exp(pj["b_col"][h] + m0[h] - m_h))
            w.append(pj["qk"][h] * jnp.exp(pj["log_d"][h] - m_h))
        wv = [_dot(w[h].astype(BF16), pj["v"][h].astype(BF16)) for h in heads]
        outs, dec, wend_v, n_new, m_new = [], [], [], [], []
        for h in heads:
            numer = inter[h] * qc[h] + wv[h]
            denom = (inter[h] * jnp.sum(pj["q"][h] * n0[h], axis=1, keepdims=True)
                     + jnp.sum(w[h], axis=1, keepdims=True))
            outs.append(numer / jnp.maximum(jnp.abs(denom), jnp.exp(-m_t[h])))
            m_h = m_t[h][L - 1:L, :]
            b_last = pj["b_col"][h][L - 1:L, :]
            w_end = jnp.exp(b_last - pj["b_col"][h] + pj["g"][:, h:h + 1] - m_h)
            d_h = jnp.exp(b_last + m0[h] - m_h)
            dec.append(d_h)
            wend_v.append((w_end * pj["v"][h]).T.astype(BF16))
            n_new.append(d_h * n0[h] + jnp.sum(w_end * pj["ks"][h], axis=0, keepdims=True))
            m_new.append(m_h)
        c0 = [dec[h] * c0[h] + _dot(wend_v[h], pj["ksb"][h]) for h in heads]
        n0, m0 = n_new, m_new
        h_ref[rows, :] = jnp.concatenate(outs, axis=-1).astype(BF16)
    for h in heads:
        c_ref[h] = c0[h]
        n_ref[h:h + 1, :] = n0[h]
        m_ref[h:h + 1, :] = jnp.broadcast_to(m0[h], (1, m_ref.shape[1]))

    @pl.when(pl.program_id(1) == pl.num_programs(1) - 1)
    def _():
        c_out[...] = c_ref[...]
        n_out[...] = n_ref[...]
        m_out[...] = m_ref[...]


ML_CHUNKS_PER_STEP = 4


def _ml_prompt(q, k, v, gates):
    b, t, w = q.shape
    tr = min(t, ML_CHUNKS_PER_STEP * ML_CHUNK)
    row = pl.BlockSpec((None, tr, w), lambda a, i: (a, i, 0))
    return pl.pallas_call(
        _ml_prompt_kernel,
        grid=(b, t // tr),
        in_specs=[row, row, row, pl.BlockSpec((None, tr, 128), lambda a, i: (a, i, 0))],
        out_specs=[row,
                   pl.BlockSpec((None, ML_HEADS, ML_HEAD_DIM, ML_HEAD_DIM), lambda a, i: (a, 0, 0, 0)),
                   pl.BlockSpec((None, ML_HEADS, ML_HEAD_DIM), lambda a, i: (a, 0, 0)),
                   pl.BlockSpec((None, ML_HEADS, 128), lambda a, i: (a, 0, 0))],
        out_shape=[jax.ShapeDtypeStruct((b, t, w), BF16),
                   jax.ShapeDtypeStruct((b, ML_HEADS, ML_HEAD_DIM, ML_HEAD_DIM), F32),
                   jax.ShapeDtypeStruct((b, ML_HEADS, ML_HEAD_DIM), F32),
                   jax.ShapeDtypeStruct((b, ML_HEADS, 128), F32)],
        scratch_shapes=[pltpu.VMEM((ML_HEADS, ML_HEAD_DIM, ML_HEAD_DIM), F32),
                        pltpu.VMEM((ML_HEADS, ML_HEAD_DIM), F32), pltpu.VMEM((ML_HEADS, 128), F32)],
        compiler_params=_params(("parallel", "arbitrary"), V7X_VMEM_LIMIT),
    )(q, k, v, gates)


def _ml_step_kernel(q_ref, k_ref, vc_ref, g_ref, c_ref, n_ref, m_ref, h_ref, c_out, n_out, m_out):
    lane = lax.broadcasted_iota(jnp.int32, (1, 128), 1)
    for s in range(q_ref.shape[0]):
        g = g_ref[s]
        m_row = jnp.zeros((1, 128), F32)
        for h in range(ML_HEADS):
            cs = slice(h * ML_HEAD_DIM, (h + 1) * ML_HEAD_DIM)
            q = q_ref[s, :, cs]
            ks = k_ref[s, :, cs] * (ML_HEAD_DIM ** -0.5)
            v = vc_ref[s, h]
            it = g[:, h:h + 1]
            logf = _log_sigmoid(g[:, ML_HEADS + h:ML_HEADS + h + 1])
            m0 = m_ref[s, :, h:h + 1]
            c0 = c_ref[s, h]
            n0 = n_ref[s, h:h + 1, :]
            m_t = jnp.maximum(logf + m0, it)
            dm = jnp.exp(it - m_t)
            inter = jnp.exp(logf + m0 - m_t)
            w = jnp.sum(q * ks, axis=1, keepdims=True) * dm
            numer = inter * jnp.sum(c0 * q, axis=1, keepdims=True) + w * v
            denom = inter * jnp.sum(q * n0, axis=1, keepdims=True) + w
            h_ref[s, h] = numer / jnp.maximum(jnp.abs(denom), jnp.exp(-m_t))
            w_end = jnp.exp(it - m_t)
            dec = jnp.exp(logf + m0 - m_t)
            c_out[s, h] = dec * c0 + (w_end * v) * ks
            n_out[s, h:h + 1, :] = dec * n0 + w_end * ks
            m_row = jnp.where(lane == h, m_t, m_row)
        m_out[s] = m_row


def _ml_step(q, k, v, gates, state_c, state_n, state_m, layer):
    b = q.shape[0]
    vc = v.reshape(b, ML_HEADS, ML_HEAD_DIM, 1)
    m0 = state_m[:, layer].reshape(b, 1, ML_HEADS)
    ns = STEP_SEQS if b % STEP_SEQS == 0 else 1
    row = pl.BlockSpec((ns, 1, ML_WIDTH), lambda a: (a, 0, 0))
    col = pl.BlockSpec((ns, ML_HEADS, ML_HEAD_DIM, 1), lambda a: (a, 0, 0, 0))
    cspec = pl.BlockSpec((ns, ML_HEADS, ML_HEAD_DIM, ML_HEAD_DIM), lambda a: (a, 0, 0, 0))
    nspec = pl.BlockSpec((ns, ML_HEADS, ML_HEAD_DIM), lambda a: (a, 0, 0))
    return pl.pallas_call(
        _ml_step_kernel,
        grid=(b // ns,),
        in_specs=[row, row, col, pl.BlockSpec((ns, 1, 128), lambda a: (a, 0, 0)),
                  pl.BlockSpec((ns, None, ML_HEADS, ML_HEAD_DIM, ML_HEAD_DIM),
                               lambda a: (a, layer, 0, 0, 0)),
                  pl.BlockSpec((ns, None, ML_HEADS, ML_HEAD_DIM), lambda a: (a, layer, 0, 0)),
                  pl.BlockSpec((ns, 1, ML_HEADS), lambda a: (a, 0, 0))],
        out_specs=[col, cspec, nspec, pl.BlockSpec((ns, 1, 128), lambda a: (a, 0, 0))],
        out_shape=[jax.ShapeDtypeStruct((b, ML_HEADS, ML_HEAD_DIM, 1), F32),
                   jax.ShapeDtypeStruct((b, ML_HEADS, ML_HEAD_DIM, ML_HEAD_DIM), F32),
                   jax.ShapeDtypeStruct((b, ML_HEADS, ML_HEAD_DIM), F32),
                   jax.ShapeDtypeStruct((b, 1, 128), F32)],
        compiler_params=_params(("parallel",)),
    )(q, k, vc, gates, state_c, state_n, m0)


def _prep_weights(p):
    bf = lambda a: a.astype(BF16)
    w = {}
    for name in ("w_ffn_in", "w_ffn_down", "w_in", "w_hg_proj", "w_sb_proj", "w_ml_proj", "w_out", "ml_wq", "ml_wk", "ml_wv"):
        w[name] = bf(p[name])
    w["wif"] = bf(jnp.pad(p["ml_w_if"], ((0, 0), (0, 0), (0, 128 - 2 * ML_HEADS))))
    w["bif"] = jnp.pad(p["ml_b_if"], ((0, 0), (0, 128 - 2 * ML_HEADS))).reshape(DEPTH, 1, 128)
    return w


def _layer(x, ada, l, p, w, tm, past):
    ffn = lambda xx, j, s: _ffn(xx, ada, j, p["g_pre"][l, j], p["g_post"][l, j],
                                w["w_ffn_in"], w["w_ffn_down"], l, s, tm["ffn"])
    x = ffn(x, 0, 0)
    pf, pb = _mixin(x, ada, p["g_pre"][l, 1], w["w_in"], l, tm["mixin"])
    st = {}
    blk = lambda arr, c: arr[..., c * COL:(c + 1) * COL]
    if past is None:
        b, t, _ = x.shape
        o_a, st["hgrn"] = _hgrn_prompt(pf, pb, p["hg_lb_logits"], l, tm["hgrn"])
        o_b, kt, vt = _sb_prompt(pf, pb, p["sb_bias"][l])
        q, k, v, gates = _ml_pre_seq(pf, p["ml_conv_w"][l], p["ml_conv_b"][l], w["ml_wq"][l],
                                     w["ml_wk"][l], w["ml_wv"][l], w["wif"][l], w["bif"][l], tm["mlpre"])
        o_c, st["mc"], st["mn"], m_pad = _ml_prompt(q, k, v, gates)
        st["mm"] = m_pad[:, :, 0]
        st["mconv"] = blk(pf, PF_MX)[:, t - (ML_CONV - 1):]
        st["k"] = kt.reshape(b, SB_HEADS, SB_HEAD_DIM, t)
        st["v"] = vt.reshape(b, SB_HEADS, SB_HEAD_DIM, t)
    else:
        b = x.shape[1]
        rows = lambda arr, c: blk(arr, c).reshape(b, COL).astype(F32)
        mx = blk(pf, PF_MX)
        o_a, st["hgrn"] = _hgrn_step(rows(pb, PB_Q), rows(pf, PF_Z), rows(pb, PB_I), past["hgrn"],
                                     p["hg_lb_logits"], l)
        o_b = _sb_decode(rows(pb, PB_SQ), p["sb_bias"][l], past["k"], past["v"], past["page_table"], l)
        buf = past["mconv"][:, l]
        taps = [buf[:, j].reshape(1, b, ML_WIDTH) for j in range(ML_CONV - 1)]
        q, k, v, gates = _ml_pre(taps + [mx], p["ml_conv_w"][l], p["ml_conv_b"][l], w["ml_wq"][l],
                                 w["ml_wk"][l], w["ml_wv"][l], w["wif"][l], w["bif"][l], b)
        hc, st["mc"], st["mn"], m_pad = _ml_step(q.reshape(b, 1, ML_WIDTH), k.reshape(b, 1, ML_WIDTH),
                                                 v.reshape(b, 1, ML_WIDTH), gates.reshape(b, 1, 128),
                                                 past["mc"], past["mn"], past["mm"], l)
        o_a = o_a.reshape(1, b, HG_WIDTH)
        o_b = o_b.reshape(1, b, SB_WIDTH).astype(BF16)
        o_c = hc.reshape(1, b, ML_WIDTH).astype(BF16)
        st["mm"] = m_pad[:, 0, :ML_HEADS]
        st["mconv"] = jnp.concatenate([buf[:, 1:], mx.reshape(b, 1, ML_WIDTH)], axis=1)
        st["k"] = blk(pf, PF_K).reshape(b, 1, SB_HEADS, SB_HEAD_DIM)
        st["v"] = blk(pf, PF_V).reshape(b, 1, SB_HEADS, SB_HEAD_DIM)
    x = _merge(x, ada, p["g_post"][l, 1], pb, o_a, o_b, o_c, p["hg_gain"][l], p["ml_gain"][l],
               w["w_hg_proj"][l], w["w_sb_proj"][l], w["w_ml_proj"][l], w["w_out"][l], tm["merge"])
    x = ffn(x, 2, 1)
    return x, st


def _run(x, ada_all, p, w, tm, past):
    outs = []
    for l in range(DEPTH):
        x, st = _layer(x, ada_all[l], l, p, w, tm, past)
        outs.append(st)
    return x, {name: jnp.stack([o[name] for o in outs], axis=1) for name in outs[0]}


def kernel(x_prompt, x_sample, cache_sb_k, cache_sb_v, state_hgrn, state_mlstm_c, state_mlstm_n,
           state_mlstm_m, state_mlstm_conv, page_table, c_prompt, c_sample, w_ada, b_ada, g_pre, g_post,
           w_ffn_in, w_ffn_down, w_in, sb_bias, hg_lb_logits, hg_gain, ml_conv_w, ml_conv_b, ml_wq, ml_wk,
           ml_wv, ml_w_if, ml_b_if, ml_gain, w_hg_proj, w_sb_proj, w_ml_proj, w_out):
    p = dict(g_pre=g_pre, g_post=g_post, w_ffn_in=w_ffn_in, w_ffn_down=w_ffn_down, w_in=w_in,
             sb_bias=sb_bias, hg_lb_logits=hg_lb_logits, hg_gain=hg_gain, ml_conv_w=ml_conv_w,
             ml_conv_b=ml_conv_b, ml_wq=ml_wq, ml_wk=ml_wk, ml_wv=ml_wv, ml_w_if=ml_w_if,
             ml_b_if=ml_b_if, ml_gain=ml_gain, w_hg_proj=w_hg_proj, w_sb_proj=w_sb_proj,
             w_ml_proj=w_ml_proj, w_out=w_out)
    w = _prep_weights(p)
    bp, t, d = x_prompt.shape
    bs = x_sample.shape[0]
    ada = _ada(jnp.concatenate([c_prompt, c_sample], axis=0), w_ada, b_ada)
    ada_p = ada[:, :bp].reshape(DEPTH, bp, 1, N_SUB * 3 * d)
    ada_s = ada[:, bp:].reshape(DEPTH, 1, bs, N_SUB * 3 * d)

    tm_p = dict(ffn=min(t, 512), mixin=min(t, 512), merge=min(t, 256), hgrn=min(t, 512),
                mlpre=min(t, 512))
    y_p, sp = _run(x_prompt, ada_p, p, w, tm_p, None)

    past = dict(k=jnp.transpose(cache_sb_k, (0, 1, 3, 4, 2)), v=jnp.transpose(cache_sb_v, (0, 1, 3, 4, 2)),
                page_table=page_table, hgrn=state_hgrn, mc=state_mlstm_c, mn=state_mlstm_n,
                mm=state_mlstm_m, mconv=state_mlstm_conv)
    tm_s = dict(ffn=bs, mixin=bs, merge=bs)
    y_s, ss = _run(x_sample.reshape(1, bs, d), ada_s, p, w, tm_s, past)
    y_s = y_s.reshape(bs, 1, d)
    k_p = jnp.transpose(sp["k"], (0, 1, 4, 2, 3))
    v_p = jnp.transpose(sp["v"], (0, 1, 4, 2, 3))
    return (y_p, y_s, k_p, v_p, ss["k"], ss["v"], sp["hgrn"], ss["hgrn"],
            sp["mc"], ss["mc"], sp["mn"], ss["mn"], sp["mm"], ss["mm"], sp["mconv"], ss["mconv"])
```

```python
import functools

import jax
import jax.numpy as jnp
from jax import lax
from jax.experimental import pallas as pl
from jax.experimental.pallas import tpu as pltpu

F32 = jnp.float32
BF16 = jnp.bfloat16

D_MODEL = 1024
DEPTH = 2
PAGE_SIZE = 128
HG_HEADS = 4
HG_DK = 128
HG_WIDTH = 512
HG_CHUNK = 64
HG_SUB = 16
HG_SAFE_DECAY = 80.0
SB_HEADS = 8
SB_HEAD_DIM = 64
SB_WIDTH = 512
SB_TQ = 128
SB_TK = 256
SB_HEAD_GROUP = 8
ML_HEADS = 4
ML_HEAD_DIM = 128
ML_WIDTH = 512
ML_CHUNK = 128
ML_CONV = 4
D_FF = 2816
FF_CHUNK = 256
FFN_RES = 0.5
N_SUB = 3
NORM_EPS = 1e-6
IN_COLS = 7680
COL = 512
PF_SRC = (1, 5, 6, 7)
PB_SRC = (0, 2, 3, 4, 8, 9, 10, 11, 12, 13, 14)
PF_COLS = len(PF_SRC) * COL
PB_COLS = len(PB_SRC) * COL
PF_Z, PF_K, PF_V, PF_MX = 0, 1, 2, 3
PB_Q, PB_I, PB_HG, PB_SQ, PB_MO, PB_GATES = 0, 1, 2, 3, 4, 5
V7X_VMEM_LIMIT = 56 * 1024 * 1024
NEG_INF = float("-inf")
LOG2E = 1.4426950408889634


def _dot(a, b):
    return jnp.dot(a, b, preferred_element_type=F32)


def _dot_nt(a, b):
    return lax.dot_general(a, b, (((1,), (1,)), ((), ())), preferred_element_type=F32)


def _split3(x):
    x1 = x.astype(BF16)
    r1 = x - x1.astype(F32)
    x2 = r1.astype(BF16)
    x3 = (r1 - x2.astype(F32)).astype(BF16)
    return x1, x2, x3


def _tri_dot(tri, x):
    x1, x2, x3 = _split3(x)
    return _dot(tri, x1) + _dot(tri, x2) + _dot(tri, x3)


def _dot_tri(x, tri):
    x1, x2, x3 = _split3(x)
    return _dot(x1, tri) + _dot(x2, tri) + _dot(x3, tri)


def _sigmoid(x):
    return 0.5 * jnp.tanh(0.5 * x) + 0.5


def _silu(x):
    return x * _sigmoid(x)


def _log_sigmoid(x):
    return jnp.minimum(x, 0.0) - jnp.log1p(jnp.exp(-jnp.abs(x)))


def _logaddexp(a, b):
    amax = jnp.maximum(a, b)
    delta = a - b
    return jnp.where(delta != delta, a + b, amax + jnp.log1p(jnp.exp(-jnp.abs(delta))))


def _rms(x, g):
    return x * lax.rsqrt(jnp.mean(x * x, axis=-1, keepdims=True) + NORM_EPS) * g


def _head_norm(o, gain, heads, width):
    parts = []
    for h in range(heads):
        oh = o[:, h * width:(h + 1) * width]
        parts.append(oh * lax.rsqrt(jnp.mean(oh * oh, axis=-1, keepdims=True) + NORM_EPS))
    return jnp.concatenate(parts, axis=-1) * gain


def _params(sem, vmem=None):
    return pltpu.CompilerParams(dimension_semantics=sem, vmem_limit_bytes=vmem)


def _ada_kernel(c_ref, w_ref, b_ref, o_ref):
    s = _silu(c_ref[...]).astype(BF16)
    o_ref[...] = _dot(s, w_ref[...].astype(BF16)) + b_ref[...]


def _ada(c_all, w_ada, b_ada):
    n, d = c_all.shape
    cols = w_ada.shape[-1]
    tn = 1536
    return pl.pallas_call(
        _ada_kernel,
        grid=(DEPTH, cols // tn),
        in_specs=[pl.BlockSpec((n, d), lambda l, j: (0, 0)),
                  pl.BlockSpec((None, d, tn), lambda l, j: (l, 0, j)),
                  pl.BlockSpec((None, 1, tn), lambda l, j: (l, 0, j))],
        out_specs=pl.BlockSpec((None, n, tn), lambda l, j: (l, 0, j)),
        out_shape=jax.ShapeDtypeStruct((DEPTH, n, cols), F32),
        compiler_params=_params(("parallel", "parallel"), V7X_VMEM_LIMIT),
    )(c_all, w_ada, b_ada.reshape(DEPTH, 1, cols))


def _mod_spec(ada, tm, col):
    rm = ada.shape[1]
    if rm == 1:
        return pl.BlockSpec((None, 1, D_MODEL), lambda g, i: (g, 0, col))
    return pl.BlockSpec((None, tm, D_MODEL), lambda g, i: (g, i, col))


def _ffn_kernel(x_ref, sh_ref, sc_ref, gt_ref, gpre_ref, gpost_ref, win_ref, wd_ref, o_ref, h_ref, a_ref):
    x = x_ref[...]
    h_ref[...] = (_rms(x, gpre_ref[...]) * (1.0 + sc_ref[...]) + sh_ref[...]).astype(BF16)
    for f in range(D_FF // FF_CHUNK):
        cols = slice(f * FF_CHUNK, (f + 1) * FF_CHUNK)
        h = h_ref[...]
        g = _dot(h, win_ref[:, cols])
        u = _dot(h, win_ref[:, D_FF + f * FF_CHUNK:D_FF + (f + 1) * FF_CHUNK])
        a_ref[:, cols] = (_silu(g) * u).astype(BF16)
    y = _dot(a_ref[...], wd_ref[...])
    o_ref[...] = x + FFN_RES * (1.0 + gt_ref[...]) * _rms(y, gpost_ref[...])


def _ffn(x, ada, j, gpre, gpost, w_in, w_down, l, s, tm):
    g, r, d = x.shape
    xspec = pl.BlockSpec((None, tm, d), lambda a, i: (a, i, 0))
    vec = pl.BlockSpec((1, d), lambda a, i: (0, 0))
    res = lambda shape: pl.BlockSpec((None, None) + shape, lambda a, i: (l, s, 0, 0),
                                     pipeline_mode=pl.Buffered(1))
    return pl.pallas_call(
        _ffn_kernel,
        grid=(g, r // tm),
        in_specs=[xspec, _mod_spec(ada, tm, 3 * j), _mod_spec(ada, tm, 3 * j + 1),
                  _mod_spec(ada, tm, 3 * j + 2), vec, vec, res((d, 2 * D_FF)), res((D_FF, d))],
        out_specs=xspec,
        out_shape=jax.ShapeDtypeStruct(x.shape, F32),
        scratch_shapes=[pltpu.VMEM((tm, d), BF16), pltpu.VMEM((tm, D_FF), BF16)],
        compiler_params=_params(("parallel", "parallel"), V7X_VMEM_LIMIT),
    )(x, ada, ada, ada, gpre.reshape(1, d), gpost.reshape(1, d), w_in, w_down)


def _mixin_kernel(x_ref, sh_ref, sc_ref, gpre_ref, w_ref, pf_ref, pb_ref, h_ref):
    h_ref[...] = (_rms(x_ref[...], gpre_ref[...]) * (1.0 + sc_ref[...]) + sh_ref[...]).astype(BF16)
    nf = PF_COLS // COL
    for c, src in enumerate(PF_SRC + PB_SRC):
        y = _dot(h_ref[...], w_ref[:, src * COL:(src + 1) * COL])
        if c < nf:
            pf_ref[:, c * COL:(c + 1) * COL] = y
        else:
            pb_ref[:, (c - nf) * COL:(c - nf + 1) * COL] = y.astype(BF16)


def _mixin(x, ada, gpre, w_in, l, tm):
    g, r, d = x.shape
    return pl.pallas_call(
        _mixin_kernel,
        grid=(g, r // tm),
        in_specs=[pl.BlockSpec((None, tm, d), lambda a, i: (a, i, 0)),
                  _mod_spec(ada, tm, 3), _mod_spec(ada, tm, 4),
                  pl.BlockSpec((1, d), lambda a, i: (0, 0)),
                  pl.BlockSpec((None, d, IN_COLS), lambda a, i: (l, 0, 0), pipeline_mode=pl.Buffered(1))],
        out_specs=[pl.BlockSpec((None, tm, PF_COLS), lambda a, i: (a, i, 0)),
                   pl.BlockSpec((None, tm, PB_COLS), lambda a, i: (a, i, 0))],
        out_shape=[jax.ShapeDtypeStruct((g, r, PF_COLS), F32), jax.ShapeDtypeStruct((g, r, PB_COLS), BF16)],
        scratch_shapes=[pltpu.VMEM((tm, d), BF16)],
        compiler_params=_params(("parallel", "parallel"), V7X_VMEM_LIMIT),
    )(x, ada, ada, gpre.reshape(1, d), w_in)


def _merge_kernel(x_ref, gt_ref, gpost_ref, oa_ref, hg_ref, ob_ref, oc_ref, mo_ref,
                  g0, g1, g2, g3, g4, g5, hgain_ref, mgain_ref,
                  whg_ref, wsb_ref, wml_ref, wout_ref, o_ref):
    f32 = lambda ref: ref[...].astype(F32)
    o_a = (_head_norm(oa_ref[...], hgain_ref[...], HG_HEADS, HG_DK) * _silu(f32(hg_ref))).astype(BF16)
    o_b = ob_ref[...]
    o_c = (_head_norm(f32(oc_ref), mgain_ref[...], ML_HEADS, ML_HEAD_DIM) * _sigmoid(f32(mo_ref))).astype(BF16)
    gates = ((g0, g2, g4), (g1, g3, g5))
    y = None
    for c in range(2):
        cs = slice(c * COL, (c + 1) * COL)
        ga, gb, gc = gates[c]
        m = (_sigmoid(f32(ga)) * _dot(o_a, whg_ref[:, cs])
             + _sigmoid(f32(gb)) * _dot(o_b, wsb_ref[:, cs])
             + _sigmoid(f32(gc)) * _dot(o_c, wml_ref[:, cs]))
        part = _dot(m.astype(BF16), wout_ref[cs, :])
        y = part if y is None else y + part
    o_ref[...] = x_ref[...] + (1.0 + gt_ref[...]) * _rms(y, gpost_ref[...])


def _merge(x, ada, gpost, pb, o_a, o_b, o_c, hgain, mgain, whg, wsb, wml, wout, tm):
    g, r, d = x.shape
    row = lambda col: pl.BlockSpec((None, tm, COL), lambda a, i: (a, i, col))
    vec = lambda n: pl.BlockSpec((1, n), lambda a, i: (0, 0))
    wsp = lambda shape: pl.BlockSpec(shape, lambda a, i: (0, 0))
    xspec = pl.BlockSpec((None, tm, d), lambda a, i: (a, i, 0))
    return pl.pallas_call(
        _merge_kernel,
        grid=(g, r // tm),
        in_specs=[xspec, _mod_spec(ada, tm, 5), vec(d),
                  row(0), row(PB_HG), row(0), row(0), row(PB_MO)]
                 + [row(PB_GATES + c) for c in range(6)]
                 + [vec(HG_WIDTH), vec(ML_WIDTH),
                    wsp((HG_WIDTH, d)), wsp((SB_WIDTH, d)), wsp((ML_WIDTH, d)), wsp((d, d))],
        out_specs=xspec,
        out_shape=jax.ShapeDtypeStruct(x.shape, F32),
        compiler_params=_params(("parallel", "parallel"), V7X_VMEM_LIMIT),
    )(x, ada, gpost.reshape(1, d), o_a, pb, o_b, o_c, pb,
      pb, pb, pb, pb, pb, pb,
      hgain.reshape(1, HG_WIDTH), mgain.reshape(1, ML_WIDTH), whg, wsb, wml, wout)


def _hgrn_lb(lbl, layer):
    e = jnp.exp(lbl - jnp.max(lbl, axis=0, keepdims=True))
    p = e / jnp.sum(e, axis=0, keepdims=True)
    lb = jnp.zeros_like(p[0:1])
    for r in range(1, layer + 1):
        lb = lb + p[r:r + 1]
    return lb


def _hgrn_gates(z, lb):
    logf = _logaddexp(jnp.log(lb), jnp.log1p(-lb) + _log_sigmoid(z))
    k = (1.0 - lb) * _sigmoid(-z)
    return logf, k


def _hgrn_prompt_kernel(lbl_ref, qb_ref, z_ref, ib_ref, o_ref, s_ref, st_ref, b_ref, k_ref, q_ref, i_ref,
                        *, layer):
    tc = qb_ref.shape[0]
    q_ref[...] = qb_ref[...].astype(F32)
    i_ref[...] = ib_ref[...].astype(F32)
    nchunk = tc // HG_CHUNK
    nsub = HG_CHUNK // HG_SUB

    @pl.when(pl.program_id(1) == 0)
    def _():
        st_ref[...] = jnp.zeros_like(st_ref)

    lb = _hgrn_lb(lbl_ref[...], layer)[0]
    logf, k = _hgrn_gates(z_ref[...], lb)
    k_ref[...] = k
    ri = lax.broadcasted_iota(jnp.int32, (tc, tc), 0)
    ci = lax.broadcasted_iota(jnp.int32, (tc, tc), 1)
    tri = jnp.where((ci <= ri) & (ri // HG_CHUNK == ci // HG_CHUNK), 1.0, 0.0).astype(BF16)
    b_ref[...] = _tri_dot(tri, logf)
    block_decay = -jnp.sum(logf.reshape(tc // HG_SUB, HG_SUB, HG_WIDTH), axis=1)
    safe = jnp.max(block_decay) < HG_SAFE_DECAY

    rows64 = lax.broadcasted_iota(jnp.int32, (HG_CHUNK, 1), 0)
    rows16 = lax.broadcasted_iota(jnp.int32, (HG_SUB, 1), 0)
    r64 = lax.broadcasted_iota(jnp.int32, (HG_CHUNK, HG_CHUNK), 0)
    c64 = lax.broadcasted_iota(jnp.int32, (HG_CHUNK, HG_CHUNK), 1)
    causal64 = c64 <= r64

    heads = range(HG_HEADS)
    sl = [slice(h * HG_DK, (h + 1) * HG_DK) for h in heads]

    def score_operands(q, b, kk, cs, s_i, include_diag):
        i0 = s_i * HG_SUB
        r = b[i0 - 1:i0, cs] if s_i else jnp.zeros((1, HG_DK), F32)
        hi = i0 + HG_SUB if include_diag else i0
        cap = HG_SAFE_DECAY if include_diag else 0.0
        qt = q[i0:i0 + HG_SUB, cs] * jnp.exp(b[i0:i0 + HG_SUB, cs] - r)
        kt = jnp.where(rows64 < hi, kk[:, cs] * jnp.exp(jnp.minimum(r - b[:, cs], cap)), 0.0)
        return qt.astype(BF16), kt.astype(BF16)

    def chunk(r0):
        rows = pl.ds(r0, HG_CHUNK)
        q = q_ref[rows, :]
        i = i_ref[rows, :]
        b = b_ref[rows, :]
        kk = k_ref[rows, :]
        blast = b[HG_CHUNK - 1:HG_CHUNK, :]
        eb = jnp.exp(b)
        kd = kk * jnp.exp(blast - b)
        ib = [i[:, cs].astype(BF16) for cs in sl]
        ops = [[score_operands(q, b, kk, cs, s_i, True) for s_i in range(nsub)] for cs in sl]
        p = [jnp.concatenate([_dot_nt(qt, kt) for qt, kt in ops[h]], axis=0) for h in heads]
        st = [st_ref[h] for h in heads]
        o_state = [_dot_nt((q[:, sl[h]] * eb[:, sl[h]]).astype(BF16), st[h].astype(BF16)) for h in heads]
        pm = [jnp.where(causal64 & safe, p[h], 0.0).astype(BF16) for h in heads]
        o_intra = [_dot(pm[h], ib[h]) for h in heads]
        upd = [_dot(i[:, sl[h]].T.astype(BF16), kd[:, sl[h]].astype(BF16)) for h in heads]
        o_ref[rows, :] = jnp.concatenate([o_state[h] + o_intra[h] for h in heads], axis=-1)
        for h in heads:
            st_ref[h] = st[h] * jnp.exp(blast[:, sl[h]]) + upd[h]

        @pl.when(jnp.logical_not(safe))
        def _():
            outs = []
            for h in heads:
                rows_p = [jnp.zeros((HG_SUB, HG_CHUNK), F32)]
                for s_i in range(1, nsub):
                    qt, kt = score_operands(q, b, kk, sl[h], s_i, False)
                    rows_p.append(_dot_nt(qt, kt))
                outs.append(_dot(jnp.concatenate(rows_p, axis=0).astype(BF16), ib[h]))
            o_ref[rows, :] += jnp.concatenate(outs, axis=-1)
            for s_i in range(nsub):
                base = r0 + s_i * HG_SUB
                srows = pl.ds(base, HG_SUB)
                q_i = q_ref[srows, :]
                b_i = b_ref[srows, :]

                def s_body(s, acc):
                    b_s = b_ref[pl.ds(base + s, 1), :]
                    k_s = k_ref[pl.ds(base + s, 1), :]
                    i_s = i_ref[pl.ds(base + s, 1), :]
                    e = jnp.where(rows16 >= s, jnp.exp(jnp.minimum(b_i - b_s, 0.0)), 0.0)
                    pr = q_i * e * k_s
                    parts = [jnp.sum(pr[:, cs], axis=1, keepdims=True) * i_s[:, cs] for cs in sl]
                    return acc + jnp.concatenate(parts, axis=-1)

                acc = lax.fori_loop(0, HG_SUB, s_body, jnp.zeros((HG_SUB, HG_WIDTH), F32))
                o_ref[srows, :] += acc

    def pair_body(n, carry):
        r0 = pl.multiple_of(n * (2 * HG_CHUNK), 2 * HG_CHUNK)
        chunk(r0)
        chunk(r0 + HG_CHUNK)
        return carry

    lax.fori_loop(0, nchunk // 2, pair_body, 0)

    @pl.when(pl.program_id(1) == pl.num_programs(1) - 1)
    def _():
        for h in range(HG_HEADS):
            s_ref[h] = st_ref[h].T


def _hgrn_prompt(pf, pb, lbl, layer, tc):
    b, t, _ = pf.shape
    row = lambda col: pl.BlockSpec((None, tc, COL), lambda a, i: (a, i, col))
    return pl.pallas_call(
        functools.partial(_hgrn_prompt_kernel, layer=layer),
        grid=(b, t // tc),
        in_specs=[pl.BlockSpec((DEPTH, 1, HG_WIDTH), lambda a, i: (0, 0, 0)), row(PB_Q), row(PF_Z), row(PB_I)],
        out_specs=[pl.BlockSpec((None, tc, HG_WIDTH), lambda a, i: (a, i, 0)),
                   pl.BlockSpec((None, HG_HEADS, HG_DK, HG_DK), lambda a, i: (a, 0, 0, 0))],
        out_shape=[jax.ShapeDtypeStruct((b, t, HG_WIDTH), F32),
                   jax.ShapeDtypeStruct((b, HG_HEADS, HG_DK, HG_DK), F32)],
        scratch_shapes=[pltpu.VMEM((HG_HEADS, HG_DK, HG_DK), F32)] + [pltpu.VMEM((tc, HG_WIDTH), F32)] * 4,
        compiler_params=_params(("parallel", "arbitrary"), V7X_VMEM_LIMIT),
    )(lbl.reshape(DEPTH, 1, HG_WIDTH), pb, pf, pb)


def _hgrn_step_kernel(lbl_ref, qc_ref, zc_ref, ir_ref, s_ref, o_ref, so_ref, *, layer):
    lb = _hgrn_lb(lbl_ref[...], layer)[0]
    for n in range(qc_ref.shape[0]):
        logf, k = _hgrn_gates(zc_ref[n], lb)
        outs = []
        for h in range(HG_HEADS):
            cs = slice(h * HG_DK, (h + 1) * HG_DK)
            s_new = jnp.exp(logf[h]) * s_ref[n, h] + k[h] * ir_ref[n, :, cs]
            so_ref[n, h] = s_new
            outs.append(jnp.sum(qc_ref[n, h] * s_new, axis=0, keepdims=True))
        o_ref[n] = jnp.concatenate(outs, axis=-1)


STEP_SEQS = 4


def _hgrn_step(q, z, i, state, lbl, layer):
    b = q.shape[0]
    qc = q.reshape(b, HG_HEADS, HG_DK, 1)
    zc = z.reshape(b, HG_HEADS, HG_DK, 1)
    ns = STEP_SEQS if b % STEP_SEQS == 0 else 1
    col = pl.BlockSpec((ns, HG_HEADS, HG_DK, 1), lambda a: (a, 0, 0, 0))
    return pl.pallas_call(
        functools.partial(_hgrn_step_kernel, layer=layer),
        grid=(b // ns,),
        in_specs=[pl.BlockSpec((DEPTH, HG_HEADS, HG_DK, 1), lambda a: (0, 0, 0, 0)), col, col,
                  pl.BlockSpec((ns, 1, COL), lambda a: (a, 0, 0)),
                  pl.BlockSpec((ns, None, HG_HEADS, HG_DK, HG_DK), lambda a: (a, layer, 0, 0, 0))],
        out_specs=[pl.BlockSpec((ns, 1, HG_WIDTH), lambda a: (a, 0, 0)),
                   pl.BlockSpec((ns, HG_HEADS, HG_DK, HG_DK), lambda a: (a, 0, 0, 0))],
        out_shape=[jax.ShapeDtypeStruct((b, 1, HG_WIDTH), F32),
                   jax.ShapeDtypeStruct((b, HG_HEADS, HG_DK, HG_DK), F32)],
        compiler_params=_params(("parallel",)),
    )(lbl.reshape(DEPTH, HG_HEADS, HG_DK, 1), qc, zc, i.reshape(b, 1, COL), state)


def _softplus(z):
    return jnp.maximum(z, 0.0) + jnp.log1p(jnp.exp(-jnp.abs(z)))


def _sb_prompt_kernel(bias_ref, q_ref, k_ref, v_ref, o_ref, kt_out, vt_out,
                      qh_ref, kt_ref, vh_ref, acc_ref, c_ref):
    tq, tk = SB_TQ, SB_TK
    nk = k_ref.shape[0] // tk
    qi = pl.program_id(1)

    @pl.when(qi == 0)
    def _():
        for n in range(nk):
            ks = slice(n * tk, (n + 1) * tk)
            kt_full = k_ref[ks, :].T
            kt_out[:, ks] = kt_full
            vt_out[:, ks] = v_ref[ks, :].T
            for h in range(SB_HEADS):
                cs = slice(h * SB_HEAD_DIM, (h + 1) * SB_HEAD_DIM)
                kt_ref[h, n] = kt_full[cs, :].astype(BF16)
                vh_ref[h, n] = v_ref[ks, cs].astype(BF16)

    for h in range(SB_HEADS):
        cs = slice(h * SB_HEAD_DIM, (h + 1) * SB_HEAD_DIM)
        qh_ref[h] = (q_ref[:, cs].astype(F32) * (SB_HEAD_DIM ** -0.5 * LOG2E)).astype(BF16)
    acc_ref[...] = jnp.zeros_like(acc_ref)
    c_ref[...] = jnp.zeros_like(c_ref)

    ri = lax.broadcasted_iota(jnp.int32, (tk, tk), 0)
    ci = lax.broadcasted_iota(jnp.int32, (tk, tk), 1)
    upper = jnp.where(ri > ci, 1.0, 0.0).astype(BF16)
    jd = (qi * tq) // tk
    t_pos = qi * tq + lax.broadcasted_iota(jnp.int32, (tq, tk), 0)
    s_pos = jd * tk + lax.broadcasted_iota(jnp.int32, (tq, tk), 1)
    causal = s_pos < t_pos

    def tile(j, mask, nkeys=SB_TK):
        for g0 in range(0, SB_HEADS, SB_HEAD_GROUP):
            heads = range(g0, g0 + SB_HEAD_GROUP)
            zs = [_dot(qh_ref[h], kt_ref[h, j, :, :nkeys]) + bias_ref[h] * LOG2E for h in heads]
            lbs, l1ms = [], []
            for z in zs:
                sp = jnp.maximum(z, 0.0) + jnp.log(1.0 + jnp.exp2(-jnp.abs(z))) * LOG2E
                l1m = -sp
                if mask is not None:
                    l1m = jnp.where(mask, l1m, 0.0)
                lbs.append(z - sp)
                l1ms.append(l1m)
            sufs = [_dot(l1m.astype(BF16), upper[:nkeys, :nkeys]) for l1m in l1ms]
            weights = []
            for h, lb, l1m, suf in zip(heads, lbs, l1ms, sufs):
                c = c_ref[h]
                a = jnp.exp2(lb + suf + c)
                if mask is not None:
                    a = jnp.where(mask, a, 0.0)
                weights.append(a.astype(BF16))
                c_ref[h] = c + suf[:, 0:1] + l1m[:, 0:1]
            for h, a in zip(heads, weights):
                acc_ref[h] += _dot(a, vh_ref[h, j, :nkeys, :])

    first_half = (qi * tq) % tk == 0

    @pl.when(first_half)
    def _():
        tile(jd, causal[:, :tq], tq)

    @pl.when(jnp.logical_not(first_half))
    def _():
        tile(jd, causal)

    def body(step, carry):
        tile(jd - 1 - step, None)
        return carry

    lax.fori_loop(0, jd, body, 0)
    o_ref[...] = jnp.concatenate([acc_ref[h] for h in range(SB_HEADS)], axis=-1).astype(BF16)


def _sb_prompt(pf, pb, bias):
    b, t, _ = pf.shape
    tq = min(SB_TQ, t)
    nk = t // SB_TK
    seq = lambda col: pl.BlockSpec((None, t, COL), lambda a, i: (a, 0, col))
    return pl.pallas_call(
        _sb_prompt_kernel,
        grid=(b, t // tq),
        in_specs=[pl.BlockSpec(memory_space=pltpu.SMEM),
                  pl.BlockSpec((None, tq, COL), lambda a, i: (a, i, PB_SQ)), seq(PF_K), seq(PF_V)],
        out_specs=[pl.BlockSpec((None, tq, SB_WIDTH), lambda a, i: (a, i, 0)),
                   pl.BlockSpec((None, SB_WIDTH, t), lambda a, i: (a, 0, 0)),
                   pl.BlockSpec((None, SB_WIDTH, t), lambda a, i: (a, 0, 0))],
        out_shape=[jax.ShapeDtypeStruct((b, t, SB_WIDTH), BF16),
                   jax.ShapeDtypeStruct((b, SB_WIDTH, t), F32),
                   jax.ShapeDtypeStruct((b, SB_WIDTH, t), F32)],
        scratch_shapes=[pltpu.VMEM((SB_HEADS, tq, SB_HEAD_DIM), BF16),
                        pltpu.VMEM((SB_HEADS, nk, SB_HEAD_DIM, SB_TK), BF16),
                        pltpu.VMEM((SB_HEADS, nk, SB_TK, SB_HEAD_DIM), BF16),
                        pltpu.VMEM((SB_HEADS, tq, SB_HEAD_DIM), F32),
                        pltpu.VMEM((SB_HEADS, tq, 1), F32)],
        compiler_params=_params(("parallel", "arbitrary"), V7X_VMEM_LIMIT),
    )(bias, pb, pf, pf)


SB_PAGES_PER_STEP = 16


def _sb_decode_kernel(pt_ref, q_ref, bias_ref, *refs):
    npg = SB_PAGES_PER_STEP
    k_refs = refs[:npg]
    v_refs = refs[npg:2 * npg]
    o_ref, qb_ref, c_ref, acc_ref = refs[2 * npg:]
    g = pl.program_id(1)

    @pl.when(g == 0)
    def _():
        qb_ref[...] = jnp.broadcast_to(q_ref[...], qb_ref.shape)
        c_ref[...] = jnp.zeros_like(c_ref)
        acc_ref[...] = jnp.zeros_like(acc_ref)

    ri = lax.broadcasted_iota(jnp.int32, (PAGE_SIZE, PAGE_SIZE), 0)
    ci = lax.broadcasted_iota(jnp.int32, (PAGE_SIZE, PAGE_SIZE), 1)
    upper = jnp.where(ri > ci, 1.0, 0.0).astype(BF16)
    bias = bias_ref[...]
    c = c_ref[:, 0:1]
    weights = [None] * npg
    for p in range(npg - 1, -1, -1):
        zrows = []
        for h in range(SB_HEADS):
            part = k_refs[p][h, 0:8, :] * qb_ref[h, 0:8, :]
            for r in range(8, SB_HEAD_DIM, 8):
                part = part + k_refs[p][h, r:r + 8, :] * qb_ref[h, r:r + 8, :]
            zrows.append(jnp.sum(part, axis=0, keepdims=True))
        z = jnp.concatenate(zrows, axis=0) + bias
        sp = _softplus(z)
        l1m = -sp
        suf = _dot(l1m.astype(BF16), upper)
        weights[p] = jnp.exp(z - sp + suf + c)
        c = c + suf[:, 0:1] + l1m[:, 0:1]
    c_ref[...] = jnp.broadcast_to(c, c_ref.shape)
    for h in range(SB_HEADS):
        part = weights[0][h:h + 1, :] * v_refs[0][h]
        for p in range(1, npg):
            part = part + weights[p][h:h + 1, :] * v_refs[p][h]
        acc_ref[h] += part

    @pl.when(g == pl.num_programs(1) - 1)
    def _():
        o_ref[...] = jnp.sum(acc_ref[...], axis=2, keepdims=True)


def _sb_decode(sq, bias, cache_kt, cache_vt, page_table, layer):
    b = sq.shape[0]
    n_pages = page_table.shape[1]
    npg = SB_PAGES_PER_STEP
    ng = n_pages // npg
    qcol = (sq * (SB_HEAD_DIM ** -0.5)).reshape(b, SB_HEADS, SB_HEAD_DIM, 1)

    def page_spec(p):
        return pl.BlockSpec((None, None, SB_HEADS, SB_HEAD_DIM, PAGE_SIZE),
                            lambda a, g, pt: (pt[a, (ng - 1 - g) * npg + p], layer, 0, 0, 0))

    hd = pl.BlockSpec((None, SB_HEADS, SB_HEAD_DIM, 1), lambda a, g, pt: (a, 0, 0, 0))
    grid_spec = pltpu.PrefetchScalarGridSpec(
        num_scalar_prefetch=1,
        grid=(b, ng),
        in_specs=[hd, pl.BlockSpec((SB_HEADS, 1), lambda a, g, pt: (0, 0))]
                 + [page_spec(p) for p in range(npg)] + [page_spec(p) for p in range(npg)],
        out_specs=hd,
        scratch_shapes=[pltpu.VMEM((SB_HEADS, SB_HEAD_DIM, PAGE_SIZE), F32),
                        pltpu.VMEM((SB_HEADS, 128), F32),
                        pltpu.VMEM((SB_HEADS, SB_HEAD_DIM, PAGE_SIZE), F32)],
    )
    out = pl.pallas_call(
        _sb_decode_kernel,
        grid_spec=grid_spec,
        out_shape=jax.ShapeDtypeStruct((b, SB_HEADS, SB_HEAD_DIM, 1), F32),
        compiler_params=_params(("parallel", "arbitrary"), V7X_VMEM_LIMIT),
    )(page_table, qcol, bias.reshape(SB_HEADS, 1), *([cache_kt] * npg), *([cache_vt] * npg))
    return out.reshape(b, 1, SB_WIDTH)


def _ml_pre_kernel(x0, x1, x2, x3, *refs):
    _ml_pre_body(x0[...], x1[...], x2[...], x3[...], *refs)


def _ml_pre_seq_kernel(prev_ref, x_ref, cw_ref, cb_ref, wq_ref, wk_ref, wv_ref, wif_ref, bif_ref,
                       q_ref, k_ref, v_ref, g_ref, xs_ref):
    tm = x_ref.shape[0]
    halo = prev_ref.shape[0]
    x = x_ref[...]
    xs_ref[halo:, :] = x
    xs_ref[:halo, :] = jnp.where(pl.program_id(1) == 0, 0.0, prev_ref[...])
    taps = [xs_ref[pl.ds(halo - (ML_CONV - 1) + j, tm), :] for j in range(ML_CONV - 1)]
    _ml_pre_body(*taps, x, cw_ref, cb_ref, wq_ref, wk_ref, wv_ref, wif_ref, bif_ref,
                 q_ref, k_ref, v_ref, g_ref)


def _ml_pre_body(x0, x1, x2, x, cw_ref, cb_ref, wq_ref, wk_ref, wv_ref, wif_ref, bif_ref,
                 q_ref, k_ref, v_ref, g_ref):
    cw = cw_ref[...]
    xc = _silu(x0 * cw[0:1] + x1 * cw[1:2] + x2 * cw[2:3] + x * cw[3:4] + cb_ref[...])
    xcb = xc.astype(BF16)
    xb = x.astype(BF16)
    qs, ks, vs = [], [], []
    for h in range(ML_HEADS):
        cs = slice(h * ML_HEAD_DIM, (h + 1) * ML_HEAD_DIM)
        qs.append(_dot(xcb[:, cs], wq_ref[h]))
        ks.append(_dot(xcb[:, cs], wk_ref[h]))
        vs.append(_dot(xb[:, cs], wv_ref[h]))
    q = jnp.concatenate(qs, axis=-1)
    k = jnp.concatenate(ks, axis=-1)
    v = jnp.concatenate(vs, axis=-1)
    q_ref[...] = q
    k_ref[...] = k
    v_ref[...] = v
    g_ref[...] = (_dot(q.astype(BF16), wif_ref[0:ML_WIDTH, :])
                  + _dot(k.astype(BF16), wif_ref[ML_WIDTH:2 * ML_WIDTH, :])
                  + _dot(v.astype(BF16), wif_ref[2 * ML_WIDTH:3 * ML_WIDTH, :]) + bif_ref[...])


def _ml_pre(xs, cw, cb, wq, wk, wv, wif, bif, tm):
    g, r, w = xs[0].shape
    row = pl.BlockSpec((None, tm, w), lambda a, i: (a, i, 0))
    full = lambda shape: pl.BlockSpec(shape, lambda a, i: (0,) * len(shape))
    out = jax.ShapeDtypeStruct((g, r, w), F32)
    return pl.pallas_call(
        _ml_pre_kernel,
        grid=(g, r // tm),
        in_specs=[row, row, row, row, full((ML_CONV, w)), full((1, w)),
                  full(wq.shape), full(wk.shape), full(wv.shape), full(wif.shape), full((1, 128))],
        out_specs=[row, row, row, pl.BlockSpec((None, tm, 128), lambda a, i: (a, i, 0))],
        out_shape=[out, out, out, jax.ShapeDtypeStruct((g, r, 128), F32)],
        compiler_params=_params(("parallel", "parallel"), V7X_VMEM_LIMIT),
    )(*xs, cw, cb.reshape(1, w), wq, wk, wv, wif, bif)


ML_HALO = 8


def _ml_pre_seq(pf, cw, cb, wq, wk, wv, wif, bif, tm):
    b, t, _ = pf.shape
    w = ML_WIDTH
    per = tm // ML_HALO
    row = pl.BlockSpec((None, tm, w), lambda a, i: (a, i, 0))
    full = lambda shape: pl.BlockSpec(shape, lambda a, i: (0,) * len(shape))
    out = jax.ShapeDtypeStruct((b, t, w), F32)
    return pl.pallas_call(
        _ml_pre_seq_kernel,
        grid=(b, t // tm),
        in_specs=[pl.BlockSpec((None, ML_HALO, w), lambda a, i: (a, jnp.maximum(i * per - 1, 0), PF_MX)),
                  pl.BlockSpec((None, tm, w), lambda a, i: (a, i, PF_MX)),
                  full((ML_CONV, w)), full((1, w)),
                  full(wq.shape), full(wk.shape), full(wv.shape), full(wif.shape), full((1, 128))],
        out_specs=[row, row, row, pl.BlockSpec((None, tm, 128), lambda a, i: (a, i, 0))],
        out_shape=[out, out, out, jax.ShapeDtypeStruct((b, t, 128), F32)],
        scratch_shapes=[pltpu.VMEM((tm + ML_HALO, w), F32)],
        compiler_params=_params(("parallel", "parallel"), V7X_VMEM_LIMIT),
    )(pf, pf, cw, cb.reshape(1, w), wq, wk, wv, wif, bif)


def _ml_prompt_kernel(q_ref, k_ref, v_ref, g_ref, h_ref, c_out, n_out, m_out, c_ref, n_ref, m_ref):
    L = ML_CHUNK
    nch = q_ref.shape[0] // L

    @pl.when(pl.program_id(1) == 0)
    def _():
        c_ref[...] = jnp.zeros_like(c_ref)
        n_ref[...] = jnp.zeros_like(n_ref)
        m_ref[...] = jnp.zeros_like(m_ref)

    ri = lax.broadcasted_iota(jnp.int32, (L, L), 0)
    ci = lax.broadcasted_iota(jnp.int32, (L, L), 1)
    causal = ci <= ri
    tril = jnp.where(causal, 1.0, 0.0).astype(BF16)
    triu = jnp.where(ri <= ci, 1.0, 0.0).astype(BF16)
    heads = range(ML_HEADS)
    sl = [slice(h * ML_HEAD_DIM, (h + 1) * ML_HEAD_DIM) for h in heads]

    ones = jnp.ones((L, ML_HEAD_DIM), BF16)
    pre = []
    for j in range(nch):
        rows = slice(j * L, (j + 1) * L)
        g = g_ref[rows, :]
        gt = g.T
        b_cols = _tri_dot(tril, _log_sigmoid(g))
        b_rows = _dot_tri(_log_sigmoid(gt), triu)
        q = [q_ref[rows, cs] for cs in sl]
        ks = [k_ref[rows, cs] * (ML_HEAD_DIM ** -0.5) for cs in sl]
        qb = [x.astype(BF16) for x in q]
        ksb = [x.astype(BF16) for x in ks]
        vb = [v_ref[rows, cs].astype(BF16) for cs in sl]
        vt = [v_ref[rows, cs].T for cs in sl]
        qk = [_dot_nt(qb[h], ksb[h]) for h in heads]
        b_bc = [jnp.broadcast_to(b_cols[:, ML_HEADS + h:ML_HEADS + h + 1], (L, L)) for h in heads]
        u_row = [gt[h:h + 1, :] - b_rows[ML_HEADS + h:ML_HEADS + h + 1, :] for h in heads]
        u_max = [jnp.broadcast_to(jnp.max(jnp.where(causal, u_row[h], NEG_INF), axis=1, keepdims=True), (L, L))
                 for h in heads]
        pre.append(dict(qb=qb, ksb=ksb, vb=vb, vt=vt, qk=qk, b_bc=b_bc, u_row=u_row, u_max=u_max))

    c0 = [c_ref[h] for h in heads]
    n0 = [n_ref[h:h + 1, :] for h in heads]
    m0 = [m_ref[h:h + 1, 0:1] for h in heads]
    for j, pj in enumerate(pre):
        rows = slice(j * L, (j + 1) * L)
        qc = [_dot_nt(pj["qb"][h], c0[h].astype(BF16)) for h in heads]
        qn = [_dot_nt(pj["qb"][h], jnp.broadcast_to(n0[h], (L, ML_HEAD_DIM)).astype(BF16)) for h in heads]
        y, inter, w = [], [], []
        for h in heads:
            y_h = jnp.maximum(pj["u_max"][h], m0[h])
            y.append(y_h)
            inter.append(jnp.exp(m0[h] - y_h))
            w.append(pj["qk"][h] * jnp.where(causal, jnp.exp(pj["u_row"][h] - y_h), 0.0))
        wb = [w[h].astype(BF16) for h in heads]
        wv = [_dot(wb[h], pj["vb"][h]) for h in heads]
        wsum = [_dot(wb[h], ones) for h in heads]
        outs, dec, w_end, m_new = [], [], [], []
        for h in heads:
            m_t = pj["b_bc"][h] + y[h]
            numer = inter[h] * qc[h] + wv[h]
            denom = inter[h] * qn[h] + wsum[h]
            outs.append(numer / jnp.maximum(jnp.abs(denom), jnp.exp(-m_t)))
            m_h = m_t[L - 1:L, 0:1]
            b_last = pj["b_bc"][h][L - 1:L, 0:1]
            w_end.append(jnp.exp(pj["u_row"][h] + b_last - m_h))
            dec.append(jnp.exp(b_last + m0[h] - m_h))
            m_new.append(m_h)
        c_upd = [_dot((pj["vt"][h] * w_end[h]).astype(BF16), pj["ksb"][h]) for h in heads]
        n_upd = [_dot(w_end[h].astype(BF16), pj["ksb"][h]) for h in heads]
        c0 = [dec[h] * c0[h] + c_upd[h] for h in heads]
        n0 = [dec[h] * n0[h] + n_upd[h] for h in heads]
        m0 = m_new
        h_ref[rows, :] = jnp.concatenate(outs, axis=-1).astype(BF16)
    for h in heads:
        c_ref[h] = c0[h]
        n_ref[h:h + 1, :] = n0[h]
        m_ref[h:h + 1, :] = jnp.broadcast_to(m0[h], (1, m_ref.shape[1]))

    @pl.when(pl.program_id(1) == pl.num_programs(1) - 1)
    def _():
        c_out[...] = c_ref[...]
        n_out[...] = n_ref[...]
        m_out[...] = m_ref[...]


ML_CHUNKS_PER_STEP = 4


def _ml_prompt(q, k, v, gates):
    b, t, w = q.shape
    tr = min(t, ML_CHUNKS_PER_STEP * ML_CHUNK)
    row = pl.BlockSpec((None, tr, w), lambda a, i: (a, i, 0))
    return pl.pallas_call(
        _ml_prompt_kernel,
        grid=(b, t // tr),
        in_specs=[row, row, row, pl.BlockSpec((None, tr, 128), lambda a, i: (a, i, 0))],
        out_specs=[row,
                   pl.BlockSpec((None, ML_HEADS, ML_HEAD_DIM, ML_HEAD_DIM), lambda a, i: (a, 0, 0, 0)),
                   pl.BlockSpec((None, ML_HEADS, ML_HEAD_DIM), lambda a, i: (a, 0, 0)),
                   pl.BlockSpec((None, ML_HEADS, 128), lambda a, i: (a, 0, 0))],
        out_shape=[jax.ShapeDtypeStruct((b, t, w), BF16),
                   jax.ShapeDtypeStruct((b, ML_HEADS, ML_HEAD_DIM, ML_HEAD_DIM), F32),
                   jax.ShapeDtypeStruct((b, ML_HEADS, ML_HEAD_DIM), F32),
                   jax.ShapeDtypeStruct((b, ML_HEADS, 128), F32)],
        scratch_shapes=[pltpu.VMEM((ML_HEADS, ML_HEAD_DIM, ML_HEAD_DIM), F32),
                        pltpu.VMEM((ML_HEADS, ML_HEAD_DIM), F32), pltpu.VMEM((ML_HEADS, 128), F32)],
        compiler_params=_params(("parallel", "arbitrary"), V7X_VMEM_LIMIT),
    )(q, k, v, gates)


def _ml_step_kernel(q_ref, k_ref, vc_ref, g_ref, c_ref, n_ref, m_ref, h_ref, c_out, n_out, m_out):
    lane = lax.broadcasted_iota(jnp.int32, (1, 128), 1)
    for s in range(q_ref.shape[0]):
        g = g_ref[s]
        m_row = jnp.zeros((1, 128), F32)
        for h in range(ML_HEADS):
            cs = slice(h * ML_HEAD_DIM, (h + 1) * ML_HEAD_DIM)
            q = q_ref[s, :, cs]
            ks = k_ref[s, :, cs] * (ML_HEAD_DIM ** -0.5)
            v = vc_ref[s, h]
            it = g[:, h:h + 1]
            logf = _log_sigmoid(g[:, ML_HEADS + h:ML_HEADS + h + 1])
            m0 = m_ref[s, :, h:h + 1]
            c0 = c_ref[s, h]
            n0 = n_ref[s, h:h + 1, :]
            m_t = jnp.maximum(logf + m0, it)
            dm = jnp.exp(it - m_t)
            inter = jnp.exp(logf + m0 - m_t)
            w = jnp.sum(q * ks, axis=1, keepdims=True) * dm
            numer = inter * jnp.sum(c0 * q, axis=1, keepdims=True) + w * v
            denom = inter * jnp.sum(q * n0, axis=1, keepdims=True) + w
            h_ref[s, h] = numer / jnp.maximum(jnp.abs(denom), jnp.exp(-m_t))
            w_end = jnp.exp(it - m_t)
            dec = jnp.exp(logf + m0 - m_t)
            c_out[s, h] = dec * c0 + (w_end * v) * ks
            n_out[s, h:h + 1, :] = dec * n0 + w_end * ks
            m_row = jnp.where(lane == h, m_t, m_row)
        m_out[s] = m_row


def _ml_step(q, k, v, gates, state_c, state_n, state_m, layer):
    b = q.shape[0]
    vc = v.reshape(b, ML_HEADS, ML_HEAD_DIM, 1)
    m0 = state_m[:, layer].reshape(b, 1, ML_HEADS)
    ns = STEP_SEQS if b % STEP_SEQS == 0 else 1
    row = pl.BlockSpec((ns, 1, ML_WIDTH), lambda a: (a, 0, 0))
    col = pl.BlockSpec((ns, ML_HEADS, ML_HEAD_DIM, 1), lambda a: (a, 0, 0, 0))
    cspec = pl.BlockSpec((ns, ML_HEADS, ML_HEAD_DIM, ML_HEAD_DIM), lambda a: (a, 0, 0, 0))
    nspec = pl.BlockSpec((ns, ML_HEADS, ML_HEAD_DIM), lambda a: (a, 0, 0))
    return pl.pallas_call(
        _ml_step_kernel,
        grid=(b // ns,),
        in_specs=[row, row, col, pl.BlockSpec((ns, 1, 128), lambda a: (a, 0, 0)),
                  pl.BlockSpec((ns, None, ML_HEADS, ML_HEAD_DIM, ML_HEAD_DIM),
                               lambda a: (a, layer, 0, 0, 0)),
                  pl.BlockSpec((ns, None, ML_HEADS, ML_HEAD_DIM), lambda a: (a, layer, 0, 0)),
                  pl.BlockSpec((ns, 1, ML_HEADS), lambda a: (a, 0, 0))],
        out_specs=[col, cspec, nspec, pl.BlockSpec((ns, 1, 128), lambda a: (a, 0, 0))],
        out_shape=[jax.ShapeDtypeStruct((b, ML_HEADS, ML_HEAD_DIM, 1), F32),
                   jax.ShapeDtypeStruct((b, ML_HEADS, ML_HEAD_DIM, ML_HEAD_DIM), F32),
                   jax.ShapeDtypeStruct((b, ML_HEADS, ML_HEAD_DIM), F32),
                   jax.ShapeDtypeStruct((b, 1, 128), F32)],
        compiler_params=_params(("parallel",)),
    )(q, k, vc, gates, state_c, state_n, m0)


def _prep_weights(p):
    bf = lambda a: a.astype(BF16)
    w = {}
    for name in ("w_ffn_in", "w_ffn_down", "w_in", "w_hg_proj", "w_sb_proj", "w_ml_proj", "w_out", "ml_wq", "ml_wk", "ml_wv"):
        w[name] = bf(p[name])
    w["wif"] = bf(jnp.pad(p["ml_w_if"], ((0, 0), (0, 0), (0, 128 - 2 * ML_HEADS))))
    w["bif"] = jnp.pad(p["ml_b_if"], ((0, 0), (0, 128 - 2 * ML_HEADS))).reshape(DEPTH, 1, 128)
    return w


def _layer(x, ada, l, p, w, tm, past):
    ffn = lambda xx, j, s: _ffn(xx, ada, j, p["g_pre"][l, j], p["g_post"][l, j],
                                w["w_ffn_in"], w["w_ffn_down"], l, s, tm["ffn"])
    x = ffn(x, 0, 0)
    pf, pb = _mixin(x, ada, p["g_pre"][l, 1], w["w_in"], l, tm["mixin"])
    st = {}
    blk = lambda arr, c: arr[..., c * COL:(c + 1) * COL]
    if past is None:
        b, t, _ = x.shape
        o_a, st["hgrn"] = _hgrn_prompt(pf, pb, p["hg_lb_logits"], l, tm["hgrn"])
        o_b, kt, vt = _sb_prompt(pf, pb, p["sb_bias"][l])
        q, k, v, gates = _ml_pre_seq(pf, p["ml_conv_w"][l], p["ml_conv_b"][l], w["ml_wq"][l],
                                     w["ml_wk"][l], w["ml_wv"][l], w["wif"][l], w["bif"][l], tm["mlpre"])
        o_c, st["mc"], st["mn"], m_pad = _ml_prompt(q, k, v, gates)
        st["mm"] = m_pad[:, :, 0]
        st["mconv"] = blk(pf, PF_MX)[:, t - (ML_CONV - 1):]
        st["k"] = kt.reshape(b, SB_HEADS, SB_HEAD_DIM, t)
        st["v"] = vt.reshape(b, SB_HEADS, SB_HEAD_DIM, t)
    else:
        b = x.shape[1]
        rows = lambda arr, c: blk(arr, c).reshape(b, COL).astype(F32)
        mx = blk(pf, PF_MX)
        o_a, st["hgrn"] = _hgrn_step(rows(pb, PB_Q), rows(pf, PF_Z), rows(pb, PB_I), past["hgrn"],
                                     p["hg_lb_logits"], l)
        o_b = _sb_decode(rows(pb, PB_SQ), p["sb_bias"][l], past["k"], past["v"], past["page_table"], l)
        buf = past["mconv"][:, l]
        taps = [buf[:, j].reshape(1, b, ML_WIDTH) for j in range(ML_CONV - 1)]
        q, k, v, gates = _ml_pre(taps + [mx], p["ml_conv_w"][l], p["ml_conv_b"][l], w["ml_wq"][l],
                                 w["ml_wk"][l], w["ml_wv"][l], w["wif"][l], w["bif"][l], b)
        hc, st["mc"], st["mn"], m_pad = _ml_step(q.reshape(b, 1, ML_WIDTH), k.reshape(b, 1, ML_WIDTH),
                                                 v.reshape(b, 1, ML_WIDTH), gates.reshape(b, 1, 128),
                                                 past["mc"], past["mn"], past["mm"], l)
        o_a = o_a.reshape(1, b, HG_WIDTH)
        o_b = o_b.reshape(1, b, SB_WIDTH).astype(BF16)
        o_c = hc.reshape(1, b, ML_WIDTH).astype(BF16)
        st["mm"] = m_pad[:, 0, :ML_HEADS]
        st["mconv"] = jnp.concatenate([buf[:, 1:], mx.reshape(b, 1, ML_WIDTH)], axis=1)
        st["k"] = blk(pf, PF_K).reshape(b, 1, SB_HEADS, SB_HEAD_DIM)
        st["v"] = blk(pf, PF_V).reshape(b, 1, SB_HEADS, SB_HEAD_DIM)
    x = _merge(x, ada, p["g_post"][l, 1], pb, o_a, o_b, o_c, p["hg_gain"][l], p["ml_gain"][l],
               w["w_hg_proj"][l], w["w_sb_proj"][l], w["w_ml_proj"][l], w["w_out"][l], tm["merge"])
    x = ffn(x, 2, 1)
    return x, st


def _run(x, ada_all, p, w, tm, past):
    outs = []
    for l in range(DEPTH):
        x, st = _layer(x, ada_all[l], l, p, w, tm, past)
        outs.append(st)
    return x, {name: jnp.stack([o[name] for o in outs], axis=1) for name in outs[0]}


def kernel(x_prompt, x_sample, cache_sb_k, cache_sb_v, state_hgrn, state_mlstm_c, state_mlstm_n,
           state_mlstm_m, state_mlstm_conv, page_table, c_prompt, c_sample, w_ada, b_ada, g_pre, g_post,
           w_ffn_in, w_ffn_down, w_in, sb_bias, hg_lb_logits, hg_gain, ml_conv_w, ml_conv_b, ml_wq, ml_wk,
           ml_wv, ml_w_if, ml_b_if, ml_gain, w_hg_proj, w_sb_proj, w_ml_proj, w_out):
    p = dict(g_pre=g_pre, g_post=g_post, w_ffn_in=w_ffn_in, w_ffn_down=w_ffn_down, w_in=w_in,
             sb_bias=sb_bias, hg_lb_logits=hg_lb_logits, hg_gain=hg_gain, ml_conv_w=ml_conv_w,
             ml_conv_b=ml_conv_b, ml_wq=ml_wq, ml_wk=ml_wk, ml_wv=ml_wv, ml_w_if=ml_w_if,
             ml_b_if=ml_b_if, ml_gain=ml_gain, w_hg_proj=w_hg_proj, w_sb_proj=w_sb_proj,
             w_ml_proj=w_ml_proj, w_out=w_out)
    w = _prep_weights(p)
    bp, t, d = x_prompt.shape
    bs = x_sample.shape[0]
    ada = _ada(jnp.concatenate([c_prompt, c_sample], axis=0), w_ada, b_ada)
    ada_p = ada[:, :bp].reshape(DEPTH, bp, 1, N_SUB * 3 * d)
    ada_s = ada[:, bp:].reshape(DEPTH, 1, bs, N_SUB * 3 * d)

    tm_p = dict(ffn=min(t, 512), mixin=min(t, 512), merge=min(t, 256), hgrn=min(t, 512),
                mlpre=min(t, 512))
    y_p, sp = _run(x_prompt, ada_p, p, w, tm_p, None)

    past = dict(k=jnp.transpose(cache_sb_k, (0, 1, 3, 4, 2)), v=jnp.transpose(cache_sb_v, (0, 1, 3, 4, 2)),
                page_table=page_table, hgrn=state_hgrn, mc=state_mlstm_c, mn=state_mlstm_n,
                mm=state_mlstm_m, mconv=state_mlstm_conv)
    tm_s = dict(ffn=bs, mixin=bs, merge=bs)
    y_s, ss = _run(x_sample.reshape(1, bs, d), ada_s, p, w, tm_s, past)
    y_s = y_s.reshape(bs, 1, d)
    k_p = jnp.transpose(sp["k"], (0, 1, 4, 2, 3))
    v_p = jnp.transpose(sp["v"], (0, 1, 4, 2, 3))
    return (y_p, y_s, k_p, v_p, ss["k"], ss["v"], sp["hgrn"], ss["hgrn"],
            sp["mc"], ss["mc"], sp["mn"], ss["mn"], sp["mm"], ss["mm"], sp["mconv"], ss["mconv"])
```

```python
import functools

import jax
import jax.numpy as jnp
from jax import lax
from jax.experimental import pallas as pl
from jax.experimental.pallas import tpu as pltpu

F32 = jnp.float32
BF16 = jnp.bfloat16

D_MODEL = 1024
DEPTH = 2
PAGE_SIZE = 128
HG_HEADS = 4
HG_DK = 128
HG_WIDTH = 512
HG_CHUNK = 64
HG_SUB = 16
HG_SAFE_DECAY = 80.0
SB_HEADS = 8
SB_HEAD_DIM = 64
SB_WIDTH = 512
SB_TQ = 256
SB_TK = 256
SB_ROWS = 128
SB_HEAD_GROUP = 8
ML_HEADS = 4
ML_HEAD_DIM = 128
ML_WIDTH = 512
ML_CHUNK = 128
ML_CONV = 4
D_FF = 2816
FF_CHUNK = 256
FFN_RES = 0.5
N_SUB = 3
NORM_EPS = 1e-6
IN_COLS = 7680
COL = 512
PF_SRC = (1, 5, 6, 7)
PB_SRC = (0, 2, 3, 4, 8, 9, 10, 11, 12, 13, 14)
PF_COLS = len(PF_SRC) * COL
PB_COLS = len(PB_SRC) * COL
PF_Z, PF_K, PF_V, PF_MX = 0, 1, 2, 3
PB_Q, PB_I, PB_HG, PB_SQ, PB_MO, PB_GATES = 0, 1, 2, 3, 4, 5
V7X_VMEM_LIMIT = 56 * 1024 * 1024
NEG_INF = float("-inf")
LOG2E = 1.4426950408889634


def _dot(a, b):
    return jnp.dot(a, b, preferred_element_type=F32)


def _dot_nt(a, b):
    return lax.dot_general(a, b, (((1,), (1,)), ((), ())), preferred_element_type=F32)


def _split3(x):
    x1 = x.astype(BF16)
    r1 = x - x1.astype(F32)
    x2 = r1.astype(BF16)
    x3 = (r1 - x2.astype(F32)).astype(BF16)
    return x1, x2, x3


def _tri_dot(tri, x):
    x1, x2, x3 = _split3(x)
    return _dot(tri, x1) + _dot(tri, x2) + _dot(tri, x3)


def _dot_tri(x, tri):
    x1, x2, x3 = _split3(x)
    return _dot(x1, tri) + _dot(x2, tri) + _dot(x3, tri)


def _sigmoid(x):
    return 0.5 * jnp.tanh(0.5 * x) + 0.5


def _silu(x):
    return x * _sigmoid(x)


def _log_sigmoid(x):
    return jnp.minimum(x, 0.0) - jnp.log(1.0 + jnp.exp(-jnp.abs(x)))


def _logaddexp(a, b):
    amax = jnp.maximum(a, b)
    delta = a - b
    return jnp.where(delta != delta, a + b, amax + jnp.log(1.0 + jnp.exp(-jnp.abs(delta))))


def _rms(x, g):
    return x * lax.rsqrt(jnp.mean(x * x, axis=-1, keepdims=True) + NORM_EPS) * g


def _head_norm(o, gain, heads, width):
    parts = []
    for h in range(heads):
        oh = o[:, h * width:(h + 1) * width]
        parts.append(oh * lax.rsqrt(jnp.mean(oh * oh, axis=-1, keepdims=True) + NORM_EPS))
    return jnp.concatenate(parts, axis=-1) * gain


def _params(sem, vmem=None):
    return pltpu.CompilerParams(dimension_semantics=sem, vmem_limit_bytes=vmem)


def _ada_kernel(c_ref, w_ref, b_ref, o_ref):
    s = _silu(c_ref[...]).astype(BF16)
    o_ref[...] = _dot(s, w_ref[...].astype(BF16)) + b_ref[...]


def _ada(c_all, w_ada, b_ada):
    n, d = c_all.shape
    cols = w_ada.shape[-1]
    tn = 1536
    return pl.pallas_call(
        _ada_kernel,
        grid=(DEPTH, cols // tn),
        in_specs=[pl.BlockSpec((n, d), lambda l, j: (0, 0)),
                  pl.BlockSpec((None, d, tn), lambda l, j: (l, 0, j)),
                  pl.BlockSpec((None, 1, tn), lambda l, j: (l, 0, j))],
        out_specs=pl.BlockSpec((None, n, tn), lambda l, j: (l, 0, j)),
        out_shape=jax.ShapeDtypeStruct((DEPTH, n, cols), F32),
        compiler_params=_params(("parallel", "parallel"), V7X_VMEM_LIMIT),
    )(c_all, w_ada, b_ada.reshape(DEPTH, 1, cols))


def _mod_spec(ada, tm, col):
    rm = ada.shape[1]
    if rm == 1:
        return pl.BlockSpec((None, 1, D_MODEL), lambda g, i: (g, 0, col))
    return pl.BlockSpec((None, tm, D_MODEL), lambda g, i: (g, i, col))


def _ffn_kernel(x_ref, sh_ref, sc_ref, gt_ref, gpre_ref, gpost_ref, win_ref, wd_ref, o_ref, h_ref, a_ref):
    x = x_ref[...]
    h_ref[...] = (_rms(x, gpre_ref[...]) * (1.0 + sc_ref[...]) + sh_ref[...]).astype(BF16)
    for f in range(D_FF // FF_CHUNK):
        cols = slice(f * FF_CHUNK, (f + 1) * FF_CHUNK)
        h = h_ref[...]
        g = _dot(h, win_ref[:, cols])
        u = _dot(h, win_ref[:, D_FF + f * FF_CHUNK:D_FF + (f + 1) * FF_CHUNK])
        a_ref[:, cols] = (_silu(g) * u).astype(BF16)
    y = _dot(a_ref[...], wd_ref[...])
    o_ref[...] = x + FFN_RES * (1.0 + gt_ref[...]) * _rms(y, gpost_ref[...])


def _ffn(x, ada, j, gpre, gpost, w_in, w_down, l, s, tm):
    g, r, d = x.shape
    xspec = pl.BlockSpec((None, tm, d), lambda a, i: (a, i, 0))
    vec = pl.BlockSpec((1, d), lambda a, i: (0, 0))
    res = lambda shape: pl.BlockSpec((None, None) + shape, lambda a, i: (l, s, 0, 0),
                                     pipeline_mode=pl.Buffered(1))
    return pl.pallas_call(
        _ffn_kernel,
        grid=(g, r // tm),
        in_specs=[xspec, _mod_spec(ada, tm, 3 * j), _mod_spec(ada, tm, 3 * j + 1),
                  _mod_spec(ada, tm, 3 * j + 2), vec, vec, res((d, 2 * D_FF)), res((D_FF, d))],
        out_specs=xspec,
        out_shape=jax.ShapeDtypeStruct(x.shape, F32),
        scratch_shapes=[pltpu.VMEM((tm, d), BF16), pltpu.VMEM((tm, D_FF), BF16)],
        compiler_params=_params(("parallel", "parallel"), V7X_VMEM_LIMIT),
    )(x, ada, ada, ada, gpre.reshape(1, d), gpost.reshape(1, d), w_in, w_down)


def _mixin_kernel(x_ref, sh_ref, sc_ref, gpre_ref, w_ref, pf_ref, pb_ref, h_ref):
    h_ref[...] = (_rms(x_ref[...], gpre_ref[...]) * (1.0 + sc_ref[...]) + sh_ref[...]).astype(BF16)
    nf = PF_COLS // COL
    for c, src in enumerate(PF_SRC + PB_SRC):
        y = _dot(h_ref[...], w_ref[:, src * COL:(src + 1) * COL])
        if c < nf:
            pf_ref[:, c * COL:(c + 1) * COL] = y
        else:
            pb_ref[:, (c - nf) * COL:(c - nf + 1) * COL] = y.astype(BF16)


def _mixin(x, ada, gpre, w_in, l, tm):
    g, r, d = x.shape
    return pl.pallas_call(
        _mixin_kernel,
        grid=(g, r // tm),
        in_specs=[pl.BlockSpec((None, tm, d), lambda a, i: (a, i, 0)),
                  _mod_spec(ada, tm, 3), _mod_spec(ada, tm, 4),
                  pl.BlockSpec((1, d), lambda a, i: (0, 0)),
                  pl.BlockSpec((None, d, IN_COLS), lambda a, i: (l, 0, 0), pipeline_mode=pl.Buffered(1))],
        out_specs=[pl.BlockSpec((None, tm, PF_COLS), lambda a, i: (a, i, 0)),
                   pl.BlockSpec((None, tm, PB_COLS), lambda a, i: (a, i, 0))],
        out_shape=[jax.ShapeDtypeStruct((g, r, PF_COLS), F32), jax.ShapeDtypeStruct((g, r, PB_COLS), BF16)],
        scratch_shapes=[pltpu.VMEM((tm, d), BF16)],
        compiler_params=_params(("parallel", "parallel"), V7X_VMEM_LIMIT),
    )(x, ada, ada, gpre.reshape(1, d), w_in)


def _merge_kernel(x_ref, gt_ref, gpost_ref, oa_ref, hg_ref, ob_ref, oc_ref, mo_ref,
                  g0, g1, g2, g3, g4, g5, hgain_ref, mgain_ref,
                  whg_ref, wsb_ref, wml_ref, wout_ref, o_ref):
    f32 = lambda ref: ref[...].astype(F32)
    o_a = (_head_norm(oa_ref[...], hgain_ref[...], HG_HEADS, HG_DK) * _silu(f32(hg_ref))).astype(BF16)
    o_b = ob_ref[...]
    o_c = (_head_norm(f32(oc_ref), mgain_ref[...], ML_HEADS, ML_HEAD_DIM) * _sigmoid(f32(mo_ref))).astype(BF16)
    gates = ((g0, g2, g4), (g1, g3, g5))
    y = None
    for c in range(2):
        cs = slice(c * COL, (c + 1) * COL)
        ga, gb, gc = gates[c]
        m = (_sigmoid(f32(ga)) * _dot(o_a, whg_ref[:, cs])
             + _sigmoid(f32(gb)) * _dot(o_b, wsb_ref[:, cs])
             + _sigmoid(f32(gc)) * _dot(o_c, wml_ref[:, cs]))
        part = _dot(m.astype(BF16), wout_ref[cs, :])
        y = part if y is None else y + part
    o_ref[...] = x_ref[...] + (1.0 + gt_ref[...]) * _rms(y, gpost_ref[...])


def _merge(x, ada, gpost, pb, o_a, o_b, o_c, hgain, mgain, whg, wsb, wml, wout, tm):
    g, r, d = x.shape
    row = lambda col: pl.BlockSpec((None, tm, COL), lambda a, i: (a, i, col))
    vec = lambda n: pl.BlockSpec((1, n), lambda a, i: (0, 0))
    wsp = lambda shape: pl.BlockSpec(shape, lambda a, i: (0, 0))
    xspec = pl.BlockSpec((None, tm, d), lambda a, i: (a, i, 0))
    return pl.pallas_call(
        _merge_kernel,
        grid=(g, r // tm),
        in_specs=[xspec, _mod_spec(ada, tm, 5), vec(d),
                  row(0), row(PB_HG), row(0), row(0), row(PB_MO)]
                 + [row(PB_GATES + c) for c in range(6)]
                 + [vec(HG_WIDTH), vec(ML_WIDTH),
                    wsp((HG_WIDTH, d)), wsp((SB_WIDTH, d)), wsp((ML_WIDTH, d)), wsp((d, d))],
        out_specs=xspec,
        out_shape=jax.ShapeDtypeStruct(x.shape, F32),
        compiler_params=_params(("parallel", "parallel"), V7X_VMEM_LIMIT),
    )(x, ada, gpost.reshape(1, d), o_a, pb, o_b, o_c, pb,
      pb, pb, pb, pb, pb, pb,
      hgain.reshape(1, HG_WIDTH), mgain.reshape(1, ML_WIDTH), whg, wsb, wml, wout)


def _hgrn_lb(lbl, layer):
    e = jnp.exp(lbl - jnp.max(lbl, axis=0, keepdims=True))
    p = e / jnp.sum(e, axis=0, keepdims=True)
    lb = jnp.zeros_like(p[0:1])
    for r in range(1, layer + 1):
        lb = lb + p[r:r + 1]
    return lb


def _hgrn_gates(z, lb):
    logf = _logaddexp(jnp.log(lb), jnp.log1p(-lb) + _log_sigmoid(z))
    k = (1.0 - lb) * _sigmoid(-z)
    return logf, k


def _hgrn_prompt_kernel(lbl_ref, qb_ref, z_ref, ib_ref, o_ref, s_ref, st_ref, b_ref, k_ref, q_ref, i_ref,
                        *, layer):
    tc = qb_ref.shape[0]
    q_ref[...] = qb_ref[...].astype(F32)
    i_ref[...] = ib_ref[...].astype(F32)
    nchunk = tc // HG_CHUNK
    nsub = HG_CHUNK // HG_SUB

    @pl.when(pl.program_id(1) == 0)
    def _():
        st_ref[...] = jnp.zeros_like(st_ref)

    lb = _hgrn_lb(lbl_ref[...], layer)[0]
    logf, k = _hgrn_gates(z_ref[...], lb)
    k_ref[...] = k
    ri = lax.broadcasted_iota(jnp.int32, (tc, tc), 0)
    ci = lax.broadcasted_iota(jnp.int32, (tc, tc), 1)
    tri = jnp.where((ci <= ri) & (ri // HG_CHUNK == ci // HG_CHUNK), 1.0, 0.0).astype(BF16)
    b_ref[...] = _tri_dot(tri, logf)
    block_decay = -jnp.sum(logf.reshape(tc // HG_SUB, HG_SUB, HG_WIDTH), axis=1)
    safe = jnp.max(block_decay) < HG_SAFE_DECAY

    rows64 = lax.broadcasted_iota(jnp.int32, (HG_CHUNK, 1), 0)
    rows16 = lax.broadcasted_iota(jnp.int32, (HG_SUB, 1), 0)
    r64 = lax.broadcasted_iota(jnp.int32, (HG_CHUNK, HG_CHUNK), 0)
    c64 = lax.broadcasted_iota(jnp.int32, (HG_CHUNK, HG_CHUNK), 1)
    causal64 = c64 <= r64

    heads = range(HG_HEADS)
    sl = [slice(h * HG_DK, (h + 1) * HG_DK) for h in heads]

    def score_operands(q, b, kk, cs, s_i, include_diag):
        i0 = s_i * HG_SUB
        r = b[i0 - 1:i0, cs] if s_i else jnp.zeros((1, HG_DK), F32)
        hi = i0 + HG_SUB if include_diag else i0
        cap = HG_SAFE_DECAY if include_diag else 0.0
        qt = q[i0:i0 + HG_SUB, cs] * jnp.exp(b[i0:i0 + HG_SUB, cs] - r)
        kt = jnp.where(rows64 < hi, kk[:, cs] * jnp.exp(jnp.minimum(r - b[:, cs], cap)), 0.0)
        return qt.astype(BF16), kt.astype(BF16)

    def chunk(r0):
        rows = pl.ds(r0, HG_CHUNK)
        q = q_ref[rows, :]
        i = i_ref[rows, :]
        b = b_ref[rows, :]
        kk = k_ref[rows, :]
        blast = b[HG_CHUNK - 1:HG_CHUNK, :]
        eb = jnp.exp(b)
        kd = kk * jnp.exp(blast - b)
        ib = [i[:, cs].astype(BF16) for cs in sl]
        ops = [[score_operands(q, b, kk, cs, s_i, True) for s_i in range(nsub)] for cs in sl]
        p = [jnp.concatenate([_dot_nt(qt, kt) for qt, kt in ops[h]], axis=0) for h in heads]
        st = [st_ref[h] for h in heads]
        o_state = [_dot_nt((q[:, sl[h]] * eb[:, sl[h]]).astype(BF16), st[h].astype(BF16)) for h in heads]
        pm = [jnp.where(causal64 & safe, p[h], 0.0).astype(BF16) for h in heads]
        o_intra = [_dot(pm[h], ib[h]) for h in heads]
        upd = [_dot(i[:, sl[h]].T.astype(BF16), kd[:, sl[h]].astype(BF16)) for h in heads]
        o_ref[rows, :] = jnp.concatenate([o_state[h] + o_intra[h] for h in heads], axis=-1)
        for h in heads:
            st_ref[h] = st[h] * jnp.exp(blast[:, sl[h]]) + upd[h]

        @pl.when(jnp.logical_not(safe))
        def _():
            outs = []
            for h in heads:
                rows_p = [jnp.zeros((HG_SUB, HG_CHUNK), F32)]
                for s_i in range(1, nsub):
                    qt, kt = score_operands(q, b, kk, sl[h], s_i, False)
                    rows_p.append(_dot_nt(qt, kt))
                outs.append(_dot(jnp.concatenate(rows_p, axis=0).astype(BF16), ib[h]))
            o_ref[rows, :] += jnp.concatenate(outs, axis=-1)
            for s_i in range(nsub):
                base = r0 + s_i * HG_SUB
                srows = pl.ds(base, HG_SUB)
                q_i = q_ref[srows, :]
                b_i = b_ref[srows, :]

                def s_body(s, acc):
                    b_s = b_ref[pl.ds(base + s, 1), :]
                    k_s = k_ref[pl.ds(base + s, 1), :]
                    i_s = i_ref[pl.ds(base + s, 1), :]
                    e = jnp.where(rows16 >= s, jnp.exp(jnp.minimum(b_i - b_s, 0.0)), 0.0)
                    pr = q_i * e * k_s
                    parts = [jnp.sum(pr[:, cs], axis=1, keepdims=True) * i_s[:, cs] for cs in sl]
                    return acc + jnp.concatenate(parts, axis=-1)

                acc = lax.fori_loop(0, HG_SUB, s_body, jnp.zeros((HG_SUB, HG_WIDTH), F32))
                o_ref[srows, :] += acc

    def pair_body(n, carry):
        r0 = pl.multiple_of(n * (2 * HG_CHUNK), 2 * HG_CHUNK)
        chunk(r0)
        chunk(r0 + HG_CHUNK)
        return carry

    lax.fori_loop(0, nchunk // 2, pair_body, 0)

    @pl.when(pl.program_id(1) == pl.num_programs(1) - 1)
    def _():
        for h in range(HG_HEADS):
            s_ref[h] = st_ref[h].T


def _hgrn_prompt(pf, pb, lbl, layer, tc):
    b, t, _ = pf.shape
    row = lambda col: pl.BlockSpec((None, tc, COL), lambda a, i: (a, i, col))
    return pl.pallas_call(
        functools.partial(_hgrn_prompt_kernel, layer=layer),
        grid=(b, t // tc),
        in_specs=[pl.BlockSpec((DEPTH, 1, HG_WIDTH), lambda a, i: (0, 0, 0)), row(PB_Q), row(PF_Z), row(PB_I)],
        out_specs=[pl.BlockSpec((None, tc, HG_WIDTH), lambda a, i: (a, i, 0)),
                   pl.BlockSpec((None, HG_HEADS, HG_DK, HG_DK), lambda a, i: (a, 0, 0, 0))],
        out_shape=[jax.ShapeDtypeStruct((b, t, HG_WIDTH), F32),
                   jax.ShapeDtypeStruct((b, HG_HEADS, HG_DK, HG_DK), F32)],
        scratch_shapes=[pltpu.VMEM((HG_HEADS, HG_DK, HG_DK), F32)] + [pltpu.VMEM((tc, HG_WIDTH), F32)] * 4,
        compiler_params=_params(("parallel", "arbitrary"), V7X_VMEM_LIMIT),
    )(lbl.reshape(DEPTH, 1, HG_WIDTH), pb, pf, pb)


def _hgrn_step_kernel(lbl_ref, qc_ref, zc_ref, ir_ref, s_ref, o_ref, so_ref, *, layer):
    lb = _hgrn_lb(lbl_ref[...], layer)[0]
    for n in range(qc_ref.shape[0]):
        logf, k = _hgrn_gates(zc_ref[n], lb)
        outs = []
        for h in range(HG_HEADS):
            cs = slice(h * HG_DK, (h + 1) * HG_DK)
            s_new = jnp.exp(logf[h]) * s_ref[n, h] + k[h] * ir_ref[n, :, cs]
            so_ref[n, h] = s_new
            outs.append(jnp.sum(qc_ref[n, h] * s_new, axis=0, keepdims=True))
        o_ref[n] = jnp.concatenate(outs, axis=-1)


STEP_SEQS = 4


def _hgrn_step(q, z, i, state, lbl, layer):
    b = q.shape[0]
    qc = q.reshape(b, HG_HEADS, HG_DK, 1)
    zc = z.reshape(b, HG_HEADS, HG_DK, 1)
    ns = STEP_SEQS if b % STEP_SEQS == 0 else 1
    col = pl.BlockSpec((ns, HG_HEADS, HG_DK, 1), lambda a: (a, 0, 0, 0))
    return pl.pallas_call(
        functools.partial(_hgrn_step_kernel, layer=layer),
        grid=(b // ns,),
        in_specs=[pl.BlockSpec((DEPTH, HG_HEADS, HG_DK, 1), lambda a: (0, 0, 0, 0)), col, col,
                  pl.BlockSpec((ns, 1, COL), lambda a: (a, 0, 0)),
                  pl.BlockSpec((ns, None, HG_HEADS, HG_DK, HG_DK), lambda a: (a, layer, 0, 0, 0))],
        out_specs=[pl.BlockSpec((ns, 1, HG_WIDTH), lambda a: (a, 0, 0)),
                   pl.BlockSpec((ns, HG_HEADS, HG_DK, HG_DK), lambda a: (a, 0, 0, 0))],
        out_shape=[jax.ShapeDtypeStruct((b, 1, HG_WIDTH), F32),
                   jax.ShapeDtypeStruct((b, HG_HEADS, HG_DK, HG_DK), F32)],
        compiler_params=_params(("parallel",)),
    )(lbl.reshape(DEPTH, HG_HEADS, HG_DK, 1), qc, zc, i.reshape(b, 1, COL), state)


def _softplus(z):
    return jnp.maximum(z, 0.0) + jnp.log(1.0 + jnp.exp(-jnp.abs(z)))


def _sb_prompt_kernel(bias_ref, q_ref, k_ref, v_ref, o_ref, kt_out, vt_out,
                      qh_ref, kt_ref, vh_ref, acc_ref, c_ref):
    tq, tk = SB_TQ, SB_TK
    nk = k_ref.shape[0] // tk
    qi = pl.program_id(1)

    @pl.when(qi == 0)
    def _():
        for n in range(nk):
            ks = slice(n * tk, (n + 1) * tk)
            kt_full = k_ref[ks, :].T
            kt_out[:, ks] = kt_full
            vt_out[:, ks] = v_ref[ks, :].T
            for h in range(SB_HEADS):
                cs = slice(h * SB_HEAD_DIM, (h + 1) * SB_HEAD_DIM)
                kt_ref[h, n] = kt_full[cs, :].astype(BF16)
                vh_ref[h, n] = v_ref[ks, cs].astype(BF16)

    nrb = tq // SB_ROWS
    for h in range(SB_HEADS):
        cs = slice(h * SB_HEAD_DIM, (h + 1) * SB_HEAD_DIM)
        for r in range(nrb):
            rows = slice(r * SB_ROWS, (r + 1) * SB_ROWS)
            qh_ref[h, r] = (q_ref[:, cs].astype(F32)[rows] * (SB_HEAD_DIM ** -0.5 * LOG2E)).astype(BF16)
    acc_ref[...] = jnp.zeros_like(acc_ref)
    c_ref[...] = jnp.zeros_like(c_ref)

    ri = lax.broadcasted_iota(jnp.int32, (tk, tk), 0)
    ci = lax.broadcasted_iota(jnp.int32, (tk, tk), 1)
    upper = jnp.where(ri > ci, 1.0, 0.0).astype(BF16)
    assert tq == tk

    def tile(j, r, diag):
        nkeys = (r + 1) * SB_ROWS if diag else tk
        mask = None
        if diag:
            mask = (lax.broadcasted_iota(jnp.int32, (SB_ROWS, nkeys), 1)
                    < lax.broadcasted_iota(jnp.int32, (SB_ROWS, nkeys), 0) + r * SB_ROWS)
        for g0 in range(0, SB_HEADS, SB_HEAD_GROUP):
            heads = range(g0, g0 + SB_HEAD_GROUP)
            zs = [_dot(qh_ref[h, r], kt_ref[h, j, :, :nkeys]) + bias_ref[h] * LOG2E for h in heads]
            lbs, l1ms = [], []
            for z in zs:
                sp = jnp.maximum(z, 0.0) + jnp.log(1.0 + jnp.exp2(-jnp.abs(z))) * LOG2E
                l1m = -sp
                if mask is not None:
                    l1m = jnp.where(mask, l1m, 0.0)
                lbs.append(z - sp)
                l1ms.append(l1m)
            sufs = [_dot(l1m.astype(BF16), upper[:nkeys, :nkeys]) for l1m in l1ms]
            weights = []
            for h, lb, l1m, suf in zip(heads, lbs, l1ms, sufs):
                c = c_ref[h * nrb + r][:, 0:1]
                a = jnp.exp2(lb + suf + c)
                if mask is not None:
                    a = jnp.where(mask, a, 0.0)
                weights.append(a.astype(BF16))
                c_ref[h * nrb + r] = jnp.broadcast_to(c + suf[:, 0:1] + l1m[:, 0:1], (SB_ROWS, 128))
            for h, a in zip(heads, weights):
                acc_ref[h, r] += _dot(a, vh_ref[h, j, :nkeys, :])

    for r in range(nrb):
        tile(qi, r, True)

    def body(step, carry):
        for r in range(nrb):
            tile(qi - 1 - step, r, False)
        return carry

    lax.fori_loop(0, qi, body, 0)
    o_ref[...] = jnp.concatenate(
        [jnp.concatenate([acc_ref[h, r] for h in range(SB_HEADS)], axis=-1) for r in range(nrb)],
        axis=0).astype(BF16)


def _sb_prompt(pf, pb, bias):
    b, t, _ = pf.shape
    tq = min(SB_TQ, t)
    nk = t // SB_TK
    seq = lambda col: pl.BlockSpec((None, t, COL), lambda a, i: (a, 0, col))
    return pl.pallas_call(
        _sb_prompt_kernel,
        grid=(b, t // tq),
        in_specs=[pl.BlockSpec(memory_space=pltpu.SMEM),
                  pl.BlockSpec((None, tq, COL), lambda a, i: (a, i, PB_SQ)), seq(PF_K), seq(PF_V)],
        out_specs=[pl.BlockSpec((None, tq, SB_WIDTH), lambda a, i: (a, i, 0)),
                   pl.BlockSpec((None, SB_WIDTH, t), lambda a, i: (a, 0, 0)),
                   pl.BlockSpec((None, SB_WIDTH, t), lambda a, i: (a, 0, 0))],
        out_shape=[jax.ShapeDtypeStruct((b, t, SB_WIDTH), BF16),
                   jax.ShapeDtypeStruct((b, SB_WIDTH, t), F32),
                   jax.ShapeDtypeStruct((b, SB_WIDTH, t), F32)],
        scratch_shapes=[pltpu.VMEM((SB_HEADS, tq // SB_ROWS, SB_ROWS, SB_HEAD_DIM), BF16),
                        pltpu.VMEM((SB_HEADS, nk, SB_HEAD_DIM, SB_TK), BF16),
                        pltpu.VMEM((SB_HEADS, nk, SB_TK, SB_HEAD_DIM), BF16),
                        pltpu.VMEM((SB_HEADS, tq // SB_ROWS, SB_ROWS, SB_HEAD_DIM), F32),
                        pltpu.VMEM((SB_HEADS * (tq // SB_ROWS), SB_ROWS, 128), F32)],
        compiler_params=_params(("parallel", "arbitrary"), V7X_VMEM_LIMIT),
    )(bias, pb, pf, pf)


SB_PAGES_PER_STEP = 16


def _sb_decode_kernel(pt_ref, q_ref, bias_ref, *refs):
    npg = SB_PAGES_PER_STEP
    k_refs = refs[:npg]
    v_refs = refs[npg:2 * npg]
    o_ref, qb_ref, c_ref, acc_ref = refs[2 * npg:]
    g = pl.program_id(1)

    @pl.when(g == 0)
    def _():
        qb_ref[...] = jnp.broadcast_to(q_ref[...], qb_ref.shape)
        c_ref[...] = jnp.zeros_like(c_ref)
        acc_ref[...] = jnp.zeros_like(acc_ref)

    ri = lax.broadcasted_iota(jnp.int32, (PAGE_SIZE, PAGE_SIZE), 0)
    ci = lax.broadcasted_iota(jnp.int32, (PAGE_SIZE, PAGE_SIZE), 1)
    upper = jnp.where(ri > ci, 1.0, 0.0).astype(BF16)
    bias = bias_ref[...]
    c = c_ref[:, 0:1]
    weights = [None] * npg
    for p in range(npg - 1, -1, -1):
        zrows = []
        for h in range(SB_HEADS):
            part = k_refs[p][h, 0:8, :] * qb_ref[h, 0:8, :]
            for r in range(8, SB_HEAD_DIM, 8):
                part = part + k_refs[p][h, r:r + 8, :] * qb_ref[h, r:r + 8, :]
            zrows.append(jnp.sum(part, axis=0, keepdims=True))
        z = jnp.concatenate(zrows, axis=0) + bias
        sp = _softplus(z)
        l1m = -sp
        suf = _dot(l1m.astype(BF16), upper)
        weights[p] = jnp.exp(z - sp + suf + c)
        c = c + suf[:, 0:1] + l1m[:, 0:1]
    c_ref[...] = jnp.broadcast_to(c, c_ref.shape)
    for h in range(SB_HEADS):
        part = weights[0][h:h + 1, :] * v_refs[0][h]
        for p in range(1, npg):
            part = part + weights[p][h:h + 1, :] * v_refs[p][h]
        acc_ref[h] += part

    @pl.when(g == pl.num_programs(1) - 1)
    def _():
        o_ref[...] = jnp.sum(acc_ref[...], axis=2, keepdims=True)


def _sb_decode(sq, bias, cache_kt, cache_vt, page_table, layer):
    b = sq.shape[0]
    n_pages = page_table.shape[1]
    npg = SB_PAGES_PER_STEP
    ng = n_pages // npg
    qcol = (sq * (SB_HEAD_DIM ** -0.5)).reshape(b, SB_HEADS, SB_HEAD_DIM, 1)

    def page_spec(p):
        return pl.BlockSpec((None, None, SB_HEADS, SB_HEAD_DIM, PAGE_SIZE),
                            lambda a, g, pt: (pt[a, (ng - 1 - g) * npg + p], layer, 0, 0, 0))

    hd = pl.BlockSpec((None, SB_HEADS, SB_HEAD_DIM, 1), lambda a, g, pt: (a, 0, 0, 0))
    grid_spec = pltpu.PrefetchScalarGridSpec(
        num_scalar_prefetch=1,
        grid=(b, ng),
        in_specs=[hd, pl.BlockSpec((SB_HEADS, 1), lambda a, g, pt: (0, 0))]
                 + [page_spec(p) for p in range(npg)] + [page_spec(p) for p in range(npg)],
        out_specs=hd,
        scratch_shapes=[pltpu.VMEM((SB_HEADS, SB_HEAD_DIM, PAGE_SIZE), F32),
                        pltpu.VMEM((SB_HEADS, 128), F32),
                        pltpu.VMEM((SB_HEADS, SB_HEAD_DIM, PAGE_SIZE), F32)],
    )
    out = pl.pallas_call(
        _sb_decode_kernel,
        grid_spec=grid_spec,
        out_shape=jax.ShapeDtypeStruct((b, SB_HEADS, SB_HEAD_DIM, 1), F32),
        compiler_params=_params(("parallel", "arbitrary"), V7X_VMEM_LIMIT),
    )(page_table, qcol, bias.reshape(SB_HEADS, 1), *([cache_kt] * npg), *([cache_vt] * npg))
    return out.reshape(b, 1, SB_WIDTH)


def _ml_pre_kernel(x0, x1, x2, x3, *refs):
    _ml_pre_body(x0[...], x1[...], x2[...], x3[...], *refs)


def _ml_pre_seq_kernel(prev_ref, x_ref, cw_ref, cb_ref, wq_ref, wk_ref, wv_ref, wif_ref, bif_ref,
                       q_ref, k_ref, v_ref, g_ref, xs_ref):
    tm = x_ref.shape[0]
    halo = prev_ref.shape[0]
    x = x_ref[...]
    xs_ref[halo:, :] = x
    xs_ref[:halo, :] = jnp.where(pl.program_id(1) == 0, 0.0, prev_ref[...])
    taps = [xs_ref[pl.ds(halo - (ML_CONV - 1) + j, tm), :] for j in range(ML_CONV - 1)]
    _ml_pre_body(*taps, x, cw_ref, cb_ref, wq_ref, wk_ref, wv_ref, wif_ref, bif_ref,
                 q_ref, k_ref, v_ref, g_ref)


def _ml_pre_body(x0, x1, x2, x, cw_ref, cb_ref, wq_ref, wk_ref, wv_ref, wif_ref, bif_ref,
                 q_ref, k_ref, v_ref, g_ref):
    cw = cw_ref[...]
    xc = _silu(x0 * cw[0:1] + x1 * cw[1:2] + x2 * cw[2:3] + x * cw[3:4] + cb_ref[...])
    xcb = xc.astype(BF16)
    xb = x.astype(BF16)
    qs, ks, vs = [], [], []
    for h in range(ML_HEADS):
        cs = slice(h * ML_HEAD_DIM, (h + 1) * ML_HEAD_DIM)
        qs.append(_dot(xcb[:, cs], wq_ref[h]))
        ks.append(_dot(xcb[:, cs], wk_ref[h]))
        vs.append(_dot(xb[:, cs], wv_ref[h]))
    q = jnp.concatenate(qs, axis=-1)
    k = jnp.concatenate(ks, axis=-1)
    v = jnp.concatenate(vs, axis=-1)
    q_ref[...] = q
    k_ref[...] = k
    v_ref[...] = v
    g_ref[...] = (_dot(q.astype(BF16), wif_ref[0:ML_WIDTH, :])
                  + _dot(k.astype(BF16), wif_ref[ML_WIDTH:2 * ML_WIDTH, :])
                  + _dot(v.astype(BF16), wif_ref[2 * ML_WIDTH:3 * ML_WIDTH, :]) + bif_ref[...])


def _ml_pre(xs, cw, cb, wq, wk, wv, wif, bif, tm):
    g, r, w = xs[0].shape
    row = pl.BlockSpec((None, tm, w), lambda a, i: (a, i, 0))
    full = lambda shape: pl.BlockSpec(shape, lambda a, i: (0,) * len(shape))
    out = jax.ShapeDtypeStruct((g, r, w), F32)
    return pl.pallas_call(
        _ml_pre_kernel,
        grid=(g, r // tm),
        in_specs=[row, row, row, row, full((ML_CONV, w)), full((1, w)),
                  full(wq.shape), full(wk.shape), full(wv.shape), full(wif.shape), full((1, 128))],
        out_specs=[row, row, row, pl.BlockSpec((None, tm, 128), lambda a, i: (a, i, 0))],
        out_shape=[out, out, out, jax.ShapeDtypeStruct((g, r, 128), F32)],
        compiler_params=_params(("parallel", "parallel"), V7X_VMEM_LIMIT),
    )(*xs, cw, cb.reshape(1, w), wq, wk, wv, wif, bif)


ML_HALO = 8


def _ml_pre_seq(pf, cw, cb, wq, wk, wv, wif, bif, tm):
    b, t, _ = pf.shape
    w = ML_WIDTH
    per = tm // ML_HALO
    row = pl.BlockSpec((None, tm, w), lambda a, i: (a, i, 0))
    full = lambda shape: pl.BlockSpec(shape, lambda a, i: (0,) * len(shape))
    out = jax.ShapeDtypeStruct((b, t, w), F32)
    return pl.pallas_call(
        _ml_pre_seq_kernel,
        grid=(b, t // tm),
        in_specs=[pl.BlockSpec((None, ML_HALO, w), lambda a, i: (a, jnp.maximum(i * per - 1, 0), PF_MX)),
                  pl.BlockSpec((None, tm, w), lambda a, i: (a, i, PF_MX)),
                  full((ML_CONV, w)), full((1, w)),
                  full(wq.shape), full(wk.shape), full(wv.shape), full(wif.shape), full((1, 128))],
        out_specs=[row, row, row, pl.BlockSpec((None, tm, 128), lambda a, i: (a, i, 0))],
        out_shape=[out, out, out, jax.ShapeDtypeStruct((b, t, 128), F32)],
        scratch_shapes=[pltpu.VMEM((tm + ML_HALO, w), F32)],
        compiler_params=_params(("parallel", "parallel"), V7X_VMEM_LIMIT),
    )(pf, pf, cw, cb.reshape(1, w), wq, wk, wv, wif, bif)


def _ml_prompt_kernel(q_ref, k_ref, v_ref, g_ref, h_ref, c_out, n_out, m_out, c_ref, n_ref, m_ref):
    L = ML_CHUNK
    nch = q_ref.shape[0] // L

    @pl.when(pl.program_id(1) == 0)
    def _():
        c_ref[...] = jnp.zeros_like(c_ref)
        n_ref[...] = jnp.zeros_like(n_ref)
        m_ref[...] = jnp.zeros_like(m_ref)

    ri = lax.broadcasted_iota(jnp.int32, (L, L), 0)
    ci = lax.broadcasted_iota(jnp.int32, (L, L), 1)
    causal = ci <= ri
    tril = jnp.where(causal, 1.0, 0.0).astype(BF16)
    triu = jnp.where(ri <= ci, 1.0, 0.0).astype(BF16)
    heads = range(ML_HEADS)
    sl = [slice(h * ML_HEAD_DIM, (h + 1) * ML_HEAD_DIM) for h in heads]

    ones = jnp.ones((L, ML_HEAD_DIM), BF16)
    pre = []
    for j in range(nch):
        rows = slice(j * L, (j + 1) * L)
        g = g_ref[rows, :]
        gt = g.T
        b_cols = _tri_dot(tril, _log_sigmoid(g))
        b_rows = _dot_tri(_log_sigmoid(gt), triu)
        q = [q_ref[rows, cs] for cs in sl]
        ks = [k_ref[rows, cs] * (ML_HEAD_DIM ** -0.5) for cs in sl]
        qb = [x.astype(BF16) for x in q]
        ksb = [x.astype(BF16) for x in ks]
        vb = [v_ref[rows, cs].astype(BF16) for cs in sl]
        vt = [v_ref[rows, cs].T for cs in sl]
        qk = [_dot_nt(qb[h], ksb[h]) for h in heads]
        b_bc = [jnp.broadcast_to(b_cols[:, ML_HEADS + h:ML_HEADS + h + 1], (L, L)) for h in heads]
        u_row = [gt[h:h + 1, :] - b_rows[ML_HEADS + h:ML_HEADS + h + 1, :] for h in heads]
        u_max = [jnp.broadcast_to(jnp.max(jnp.where(causal, u_row[h], NEG_INF), axis=1, keepdims=True), (L, L))
                 for h in heads]
        pre.append(dict(qb=qb, ksb=ksb, vb=vb, vt=vt, qk=qk, b_bc=b_bc, u_row=u_row, u_max=u_max))

    c0 = [c_ref[h] for h in heads]
    n0 = [n_ref[h:h + 1, :] for h in heads]
    m0 = [m_ref[h:h + 1, 0:1] for h in heads]
    for j, pj in enumerate(pre):
        rows = slice(j * L, (j + 1) * L)
        qc = [_dot_nt(pj["qb"][h], c0[h].astype(BF16)) for h in heads]
        qn = [_dot_nt(pj["qb"][h], jnp.broadcast_to(n0[h], (L, ML_HEAD_DIM)).astype(BF16)) for h in heads]
        y, inter, w = [], [], []
        for h in heads:
            y_h = jnp.maximum(pj["u_max"][h], m0[h])
            y.append(y_h)
            inter.append(jnp.exp(m0[h] - y_h))
            w.append(pj["qk"][h] * jnp.where(causal, jnp.exp(pj["u_row"][h] - y_h), 0.0))
        wb = [w[h].astype(BF16) for h in heads]
        wv = [_dot(wb[h], pj["vb"][h]) for h in heads]
        wsum = [_dot(wb[h], ones) for h in heads]
        outs, dec, w_end, m_new = [], [], [], []
        for h in heads:
            m_t = pj["b_bc"][h] + y[h]
            numer = inter[h] * qc[h] + wv[h]
            denom = inter[h] * qn[h] + wsum[h]
            outs.append(numer / jnp.maximum(jnp.abs(denom), jnp.exp(-m_t)))
            m_h = m_t[L - 1:L, 0:1]
            b_last = pj["b_bc"][h][L - 1:L, 0:1]
            w_end.append(jnp.exp(pj["u_row"][h] + b_last - m_h))
            dec.append(jnp.exp(b_last + m0[h] - m_h))
            m_new.append(m_h)
        c_upd = [_dot((pj["vt"][h] * w_end[h]).astype(BF16), pj["ksb"][h]) for h in heads]
        n_upd = [_dot(w_end[h].astype(BF16), pj["ksb"][h]) for h in heads]
        c0 = [dec[h] * c0[h] + c_upd[h] for h in heads]
        n0 = [dec[h] * n0[h] + n_upd[h] for h in heads]
        m0 = m_new
        h_ref[rows, :] = jnp.concatenate(outs, axis=-1).astype(BF16)
    for h in heads:
        c_ref[h] = c0[h]
        n_ref[h:h + 1, :] = n0[h]
        m_ref[h:h + 1, :] = jnp.broadcast_to(m0[h], (1, m_ref.shape[1]))

    @pl.when(pl.program_id(1) == pl.num_programs(1) - 1)
    def _():
        c_out[...] = c_ref[...]
        n_out[...] = n_ref[...]
        m_out[...] = m_ref[...]


ML_CHUNKS_PER_STEP = 4


def _ml_prompt(q, k, v, gates):
    b, t, w = q.shape
    tr = min(t, ML_CHUNKS_PER_STEP * ML_CHUNK)
    row = pl.BlockSpec((None, tr, w), lambda a, i: (a, i, 0))
    return pl.pallas_call(
        _ml_prompt_kernel,
        grid=(b, t // tr),
        in_specs=[row, row, row, pl.BlockSpec((None, tr, 128), lambda a, i: (a, i, 0))],
        out_specs=[row,
                   pl.BlockSpec((None, ML_HEADS, ML_HEAD_DIM, ML_HEAD_DIM), lambda a, i: (a, 0, 0, 0)),
                   pl.BlockSpec((None, ML_HEADS, ML_HEAD_DIM), lambda a, i: (a, 0, 0)),
                   pl.BlockSpec((None, ML_HEADS, 128), lambda a, i: (a, 0, 0))],
        out_shape=[jax.ShapeDtypeStruct((b, t, w), BF16),
                   jax.ShapeDtypeStruct((b, ML_HEADS, ML_HEAD_DIM, ML_HEAD_DIM), F32),
                   jax.ShapeDtypeStruct((b, ML_HEADS, ML_HEAD_DIM), F32),
                   jax.ShapeDtypeStruct((b, ML_HEADS, 128), F32)],
        scratch_shapes=[pltpu.VMEM((ML_HEADS, ML_HEAD_DIM, ML_HEAD_DIM), F32),
                        pltpu.VMEM((ML_HEADS, ML_HEAD_DIM), F32), pltpu.VMEM((ML_HEADS, 128), F32)],
        compiler_params=_params(("parallel", "arbitrary"), V7X_VMEM_LIMIT),
    )(q, k, v, gates)


def _ml_step_kernel(q_ref, k_ref, vc_ref, g_ref, c_ref, n_ref, m_ref, h_ref, c_out, n_out, m_out):
    lane = lax.broadcasted_iota(jnp.int32, (1, 128), 1)
    for s in range(q_ref.shape[0]):
        g = g_ref[s]
        m_row = jnp.zeros((1, 128), F32)
        for h in range(ML_HEADS):
            cs = slice(h * ML_HEAD_DIM, (h + 1) * ML_HEAD_DIM)
            q = q_ref[s, :, cs]
            ks = k_ref[s, :, cs] * (ML_HEAD_DIM ** -0.5)
            v = vc_ref[s, h]
            it = g[:, h:h + 1]
            logf = _log_sigmoid(g[:, ML_HEADS + h:ML_HEADS + h + 1])
            m0 = m_ref[s, :, h:h + 1]
            c0 = c_ref[s, h]
            n0 = n_ref[s, h:h + 1, :]
            m_t = jnp.maximum(logf + m0, it)
            dm = jnp.exp(it - m_t)
            inter = jnp.exp(logf + m0 - m_t)
            w = jnp.sum(q * ks, axis=1, keepdims=True) * dm
            numer = inter * jnp.sum(c0 * q, axis=1, keepdims=True) + w * v
            denom = inter * jnp.sum(q * n0, axis=1, keepdims=True) + w
            h_ref[s, h] = numer / jnp.maximum(jnp.abs(denom), jnp.exp(-m_t))
            w_end = jnp.exp(it - m_t)
            dec = jnp.exp(logf + m0 - m_t)
            c_out[s, h] = dec * c0 + (w_end * v) * ks
            n_out[s, h:h + 1, :] = dec * n0 + w_end * ks
            m_row = jnp.where(lane == h, m_t, m_row)
        m_out[s] = m_row


def _ml_step(q, k, v, gates, state_c, state_n, state_m, layer):
    b = q.shape[0]
    vc = v.reshape(b, ML_HEADS, ML_HEAD_DIM, 1)
    m0 = state_m[:, layer].reshape(b, 1, ML_HEADS)
    ns = STEP_SEQS if b % STEP_SEQS == 0 else 1
    row = pl.BlockSpec((ns, 1, ML_WIDTH), lambda a: (a, 0, 0))
    col = pl.BlockSpec((ns, ML_HEADS, ML_HEAD_DIM, 1), lambda a: (a, 0, 0, 0))
    cspec = pl.BlockSpec((ns, ML_HEADS, ML_HEAD_DIM, ML_HEAD_DIM), lambda a: (a, 0, 0, 0))
    nspec = pl.BlockSpec((ns, ML_HEADS, ML_HEAD_DIM), lambda a: (a, 0, 0))
    return pl.pallas_call(
        _ml_step_kernel,
        grid=(b // ns,),
        in_specs=[row, row, col, pl.BlockSpec((ns, 1, 128), lambda a: (a, 0, 0)),
                  pl.BlockSpec((ns, None, ML_HEADS, ML_HEAD_DIM, ML_HEAD_DIM),
                               lambda a: (a, layer, 0, 0, 0)),
                  pl.BlockSpec((ns, None, ML_HEADS, ML_HEAD_DIM), lambda a: (a, layer, 0, 0)),
                  pl.BlockSpec((ns, 1, ML_HEADS), lambda a: (a, 0, 0))],
        out_specs=[col, cspec, nspec, pl.BlockSpec((ns, 1, 128), lambda a: (a, 0, 0))],
        out_shape=[jax.ShapeDtypeStruct((b, ML_HEADS, ML_HEAD_DIM, 1), F32),
                   jax.ShapeDtypeStruct((b, ML_HEADS, ML_HEAD_DIM, ML_HEAD_DIM), F32),
                   jax.ShapeDtypeStruct((b, ML_HEADS, ML_HEAD_DIM), F32),
                   jax.ShapeDtypeStruct((b, 1, 128), F32)],
        compiler_params=_params(("parallel",)),
    )(q, k, vc, gates, state_c, state_n, m0)


def _prep_weights(p):
    bf = lambda a: a.astype(BF16)
    w = {}
    for name in ("w_ffn_in", "w_ffn_down", "w_in", "w_hg_proj", "w_sb_proj", "w_ml_proj", "w_out", "ml_wq", "ml_wk", "ml_wv"):
        w[name] = bf(p[name])
    w["wif"] = bf(jnp.pad(p["ml_w_if"], ((0, 0), (0, 0), (0, 128 - 2 * ML_HEADS))))
    w["bif"] = jnp.pad(p["ml_b_if"], ((0, 0), (0, 128 - 2 * ML_HEADS))).reshape(DEPTH, 1, 128)
    return w


def _layer(x, ada, l, p, w, tm, past):
    ffn = lambda xx, j, s: _ffn(xx, ada, j, p["g_pre"][l, j], p["g_post"][l, j],
                                w["w_ffn_in"], w["w_ffn_down"], l, s, tm["ffn"])
    x = ffn(x, 0, 0)
    pf, pb = _mixin(x, ada, p["g_pre"][l, 1], w["w_in"], l, tm["mixin"])
    st = {}
    blk = lambda arr, c: arr[..., c * COL:(c + 1) * COL]
    if past is None:
        b, t, _ = x.shape
        o_a, st["hgrn"] = _hgrn_prompt(pf, pb, p["hg_lb_logits"], l, tm["hgrn"])
        o_b, kt, vt = _sb_prompt(pf, pb, p["sb_bias"][l])
        q, k, v, gates = _ml_pre_seq(pf, p["ml_conv_w"][l], p["ml_conv_b"][l], w["ml_wq"][l],
                                     w["ml_wk"][l], w["ml_wv"][l], w["wif"][l], w["bif"][l], tm["mlpre"])
        o_c, st["mc"], st["mn"], m_pad = _ml_prompt(q, k, v, gates)
        st["mm"] = m_pad[:, :, 0]
        st["mconv"] = blk(pf, PF_MX)[:, t - (ML_CONV - 1):]
        st["k"] = kt.reshape(b, SB_HEADS, SB_HEAD_DIM, t)
        st["v"] = vt.reshape(b, SB_HEADS, SB_HEAD_DIM, t)
    else:
        b = x.shape[1]
        rows = lambda arr, c: blk(arr, c).reshape(b, COL).astype(F32)
        mx = blk(pf, PF_MX)
        o_a, st["hgrn"] = _hgrn_step(rows(pb, PB_Q), rows(pf, PF_Z), rows(pb, PB_I), past["hgrn"],
                                     p["hg_lb_logits"], l)
        o_b = _sb_decode(rows(pb, PB_SQ), p["sb_bias"][l], past["k"], past["v"], past["page_table"], l)
        buf = past["mconv"][:, l]
        taps = [buf[:, j].reshape(1, b, ML_WIDTH) for j in range(ML_CONV - 1)]
        q, k, v, gates = _ml_pre(taps + [mx], p["ml_conv_w"][l], p["ml_conv_b"][l], w["ml_wq"][l],
                                 w["ml_wk"][l], w["ml_wv"][l], w["wif"][l], w["bif"][l], b)
        hc, st["mc"], st["mn"], m_pad = _ml_step(q.reshape(b, 1, ML_WIDTH), k.reshape(b, 1, ML_WIDTH),
                                                 v.reshape(b, 1, ML_WIDTH), gates.reshape(b, 1, 128),
                                                 past["mc"], past["mn"], past["mm"], l)
        o_a = o_a.reshape(1, b, HG_WIDTH)
        o_b = o_b.reshape(1, b, SB_WIDTH).astype(BF16)
        o_c = hc.reshape(1, b, ML_WIDTH).astype(BF16)
        st["mm"] = m_pad[:, 0, :ML_HEADS]
        st["mconv"] = jnp.concatenate([buf[:, 1:], mx.reshape(b, 1, ML_WIDTH)], axis=1)
        st["k"] = blk(pf, PF_K).reshape(b, 1, SB_HEADS, SB_HEAD_DIM)
        st["v"] = blk(pf, PF_V).reshape(b, 1, SB_HEADS, SB_HEAD_DIM)
    x = _merge(x, ada, p["g_post"][l, 1], pb, o_a, o_b, o_c, p["hg_gain"][l], p["ml_gain"][l],
               w["w_hg_proj"][l], w["w_sb_proj"][l], w["w_ml_proj"][l], w["w_out"][l], tm["merge"])
    x = ffn(x, 2, 1)
    return x, st


def _run(x, ada_all, p, w, tm, past):
    outs = []
    for l in range(DEPTH):
        x, st = _layer(x, ada_all[l], l, p, w, tm, past)
        outs.append(st)
    return x, {name: jnp.stack([o[name] for o in outs], axis=1) for name in outs[0]}


def kernel(x_prompt, x_sample, cache_sb_k, cache_sb_v, state_hgrn, state_mlstm_c, state_mlstm_n,
           state_mlstm_m, state_mlstm_conv, page_table, c_prompt, c_sample, w_ada, b_ada, g_pre, g_post,
           w_ffn_in, w_ffn_down, w_in, sb_bias, hg_lb_logits, hg_gain, ml_conv_w, ml_conv_b, ml_wq, ml_wk,
           ml_wv, ml_w_if, ml_b_if, ml_gain, w_hg_proj, w_sb_proj, w_ml_proj, w_out):
    p = dict(g_pre=g_pre, g_post=g_post, w_ffn_in=w_ffn_in, w_ffn_down=w_ffn_down, w_in=w_in,
             sb_bias=sb_bias, hg_lb_logits=hg_lb_logits, hg_gain=hg_gain, ml_conv_w=ml_conv_w,
             ml_conv_b=ml_conv_b, ml_wq=ml_wq, ml_wk=ml_wk, ml_wv=ml_wv, ml_w_if=ml_w_if,
             ml_b_if=ml_b_if, ml_gain=ml_gain, w_hg_proj=w_hg_proj, w_sb_proj=w_sb_proj,
             w_ml_proj=w_ml_proj, w_out=w_out)
    w = _prep_weights(p)
    bp, t, d = x_prompt.shape
    bs = x_sample.shape[0]
    ada = _ada(jnp.concatenate([c_prompt, c_sample], axis=0), w_ada, b_ada)
    ada_p = ada[:, :bp].reshape(DEPTH, bp, 1, N_SUB * 3 * d)
    ada_s = ada[:, bp:].reshape(DEPTH, 1, bs, N_SUB * 3 * d)

    tm_p = dict(ffn=min(t, 512), mixin=min(t, 512), merge=min(t, 256), hgrn=min(t, 512),
                mlpre=min(t, 512))
    y_p, sp = _run(x_prompt, ada_p, p, w, tm_p, None)

    past = dict(k=jnp.transpose(cache_sb_k, (0, 1, 3, 4, 2)), v=jnp.transpose(cache_sb_v, (0, 1, 3, 4, 2)),
                page_table=page_table, hgrn=state_hgrn, mc=state_mlstm_c, mn=state_mlstm_n,
                mm=state_mlstm_m, mconv=state_mlstm_conv)
    tm_s = dict(ffn=bs, mixin=bs, merge=bs)
    y_s, ss = _run(x_sample.reshape(1, bs, d), ada_s, p, w, tm_s, past)
    y_s = y_s.reshape(bs, 1, d)
    k_p = jnp.transpose(sp["k"], (0, 1, 4, 2, 3))
    v_p = jnp.transpose(sp["v"], (0, 1, 4, 2, 3))
    return (y_p, y_s, k_p, v_p, ss["k"], ss["v"], sp["hgrn"], ss["hgrn"],
            sp["mc"], ss["mc"], sp["mn"], ss["mn"], sp["mm"], ss["mm"], sp["mconv"], ss["mconv"])
```

```python
import functools

import jax
import jax.numpy as jnp
from jax import lax
from jax.experimental import pallas as pl
from jax.experimental.pallas import tpu as pltpu

F32 = jnp.float32
BF16 = jnp.bfloat16

D_MODEL = 1024
DEPTH = 2
PAGE_SIZE = 128
HG_HEADS = 4
HG_DK = 128
HG_WIDTH = 512
HG_CHUNK = 64
HG_SUB = 16
HG_SAFE_DECAY = 80.0
SB_HEADS = 8
SB_HEAD_DIM = 64
SB_WIDTH = 512
SB_TQ = 512
SB_TK = 256
SB_ROWS = 128
SB_HEAD_GROUP = 8
ML_HEADS = 4
ML_HEAD_DIM = 128
ML_WIDTH = 512
ML_CHUNK = 128
ML_CONV = 4
D_FF = 2816
FF_CHUNK = 256
FFN_RES = 0.5
N_SUB = 3
NORM_EPS = 1e-6
IN_COLS = 7680
COL = 512
PF_SRC = (1, 5, 6, 7)
PB_SRC = (0, 2, 3, 4, 8, 9, 10, 11, 12, 13, 14)
PF_COLS = len(PF_SRC) * COL
PB_COLS = len(PB_SRC) * COL
PF_Z, PF_K, PF_V, PF_MX = 0, 1, 2, 3
PB_Q, PB_I, PB_HG, PB_SQ, PB_MO, PB_GATES = 0, 1, 2, 3, 4, 5
V7X_VMEM_LIMIT = 56 * 1024 * 1024
NEG_INF = float("-inf")
LOG2E = 1.4426950408889634


def _dot(a, b):
    return jnp.dot(a, b, preferred_element_type=F32)


def _dot_nt(a, b):
    return lax.dot_general(a, b, (((1,), (1,)), ((), ())), preferred_element_type=F32)


def _split3(x):
    x1 = x.astype(BF16)
    r1 = x - x1.astype(F32)
    x2 = r1.astype(BF16)
    x3 = (r1 - x2.astype(F32)).astype(BF16)
    return x1, x2, x3


def _tri_dot(tri, x):
    x1, x2, x3 = _split3(x)
    return _dot(tri, x1) + _dot(tri, x2) + _dot(tri, x3)


def _dot_tri(x, tri):
    x1, x2, x3 = _split3(x)
    return _dot(x1, tri) + _dot(x2, tri) + _dot(x3, tri)


def _sigmoid(x):
    return 0.5 * jnp.tanh(0.5 * x) + 0.5


def _silu(x):
    return x * _sigmoid(x)


def _log_sigmoid(x):
    return jnp.minimum(x, 0.0) - jnp.log(1.0 + jnp.exp(-jnp.abs(x)))


def _logaddexp(a, b):
    amax = jnp.maximum(a, b)
    delta = a - b
    return jnp.where(delta != delta, a + b, amax + jnp.log(1.0 + jnp.exp(-jnp.abs(delta))))


def _rms(x, g):
    return x * lax.rsqrt(jnp.mean(x * x, axis=-1, keepdims=True) + NORM_EPS) * g


def _head_norm(o, gain, heads, width):
    parts = []
    for h in range(heads):
        oh = o[:, h * width:(h + 1) * width]
        parts.append(oh * lax.rsqrt(jnp.mean(oh * oh, axis=-1, keepdims=True) + NORM_EPS))
    return jnp.concatenate(parts, axis=-1) * gain


def _params(sem, vmem=None):
    return pltpu.CompilerParams(dimension_semantics=sem, vmem_limit_bytes=vmem)


def _ada_kernel(c_ref, w_ref, b_ref, o_ref):
    s = _silu(c_ref[...]).astype(BF16)
    o_ref[...] = _dot(s, w_ref[...].astype(BF16)) + b_ref[...]


def _ada(c_all, w_ada, b_ada):
    n, d = c_all.shape
    cols = w_ada.shape[-1]
    tn = 1536
    return pl.pallas_call(
        _ada_kernel,
        grid=(DEPTH, cols // tn),
        in_specs=[pl.BlockSpec((n, d), lambda l, j: (0, 0)),
                  pl.BlockSpec((None, d, tn), lambda l, j: (l, 0, j)),
                  pl.BlockSpec((None, 1, tn), lambda l, j: (l, 0, j))],
        out_specs=pl.BlockSpec((None, n, tn), lambda l, j: (l, 0, j)),
        out_shape=jax.ShapeDtypeStruct((DEPTH, n, cols), F32),
        compiler_params=_params(("parallel", "parallel"), V7X_VMEM_LIMIT),
    )(c_all, w_ada, b_ada.reshape(DEPTH, 1, cols))


def _mod_spec(ada, tm, col):
    rm = ada.shape[1]
    if rm == 1:
        return pl.BlockSpec((None, 1, D_MODEL), lambda g, i: (g, 0, col))
    return pl.BlockSpec((None, tm, D_MODEL), lambda g, i: (g, i, col))


def _ffn_kernel(x_ref, sh_ref, sc_ref, gt_ref, gpre_ref, gpost_ref, win_ref, wd_ref, o_ref, h_ref, a_ref):
    x = x_ref[...]
    h_ref[...] = (_rms(x, gpre_ref[...]) * (1.0 + sc_ref[...]) + sh_ref[...]).astype(BF16)
    for f in range(D_FF // FF_CHUNK):
        cols = slice(f * FF_CHUNK, (f + 1) * FF_CHUNK)
        h = h_ref[...]
        g = _dot(h, win_ref[:, cols])
        u = _dot(h, win_ref[:, D_FF + f * FF_CHUNK:D_FF + (f + 1) * FF_CHUNK])
        a_ref[:, cols] = (_silu(g) * u).astype(BF16)
    y = _dot(a_ref[...], wd_ref[...])
    o_ref[...] = x + FFN_RES * (1.0 + gt_ref[...]) * _rms(y, gpost_ref[...])


def _ffn(x, ada, j, gpre, gpost, w_in, w_down, l, s, tm):
    g, r, d = x.shape
    xspec = pl.BlockSpec((None, tm, d), lambda a, i: (a, i, 0))
    vec = pl.BlockSpec((1, d), lambda a, i: (0, 0))
    res = lambda shape: pl.BlockSpec((None, None) + shape, lambda a, i: (l, s, 0, 0),
                                     pipeline_mode=pl.Buffered(1))
    return pl.pallas_call(
        _ffn_kernel,
        grid=(g, r // tm),
        in_specs=[xspec, _mod_spec(ada, tm, 3 * j), _mod_spec(ada, tm, 3 * j + 1),
                  _mod_spec(ada, tm, 3 * j + 2), vec, vec, res((d, 2 * D_FF)), res((D_FF, d))],
        out_specs=xspec,
        out_shape=jax.ShapeDtypeStruct(x.shape, F32),
        scratch_shapes=[pltpu.VMEM((tm, d), BF16), pltpu.VMEM((tm, D_FF), BF16)],
        compiler_params=_params(("parallel", "parallel"), V7X_VMEM_LIMIT),
    )(x, ada, ada, ada, gpre.reshape(1, d), gpost.reshape(1, d), w_in, w_down)


def _mixin_kernel(x_ref, sh_ref, sc_ref, gpre_ref, w_ref, pf_ref, pb_ref, h_ref):
    h_ref[...] = (_rms(x_ref[...], gpre_ref[...]) * (1.0 + sc_ref[...]) + sh_ref[...]).astype(BF16)
    nf = PF_COLS // COL
    for c, src in enumerate(PF_SRC + PB_SRC):
        y = _dot(h_ref[...], w_ref[:, src * COL:(src + 1) * COL])
        if c < nf:
            pf_ref[:, c * COL:(c + 1) * COL] = y
        else:
            pb_ref[:, (c - nf) * COL:(c - nf + 1) * COL] = y.astype(BF16)


def _mixin(x, ada, gpre, w_in, l, tm):
    g, r, d = x.shape
    return pl.pallas_call(
        _mixin_kernel,
        grid=(g, r // tm),
        in_specs=[pl.BlockSpec((None, tm, d), lambda a, i: (a, i, 0)),
                  _mod_spec(ada, tm, 3), _mod_spec(ada, tm, 4),
                  pl.BlockSpec((1, d), lambda a, i: (0, 0)),
                  pl.BlockSpec((None, d, IN_COLS), lambda a, i: (l, 0, 0), pipeline_mode=pl.Buffered(1))],
        out_specs=[pl.BlockSpec((None, tm, PF_COLS), lambda a, i: (a, i, 0)),
                   pl.BlockSpec((None, tm, PB_COLS), lambda a, i: (a, i, 0))],
        out_shape=[jax.ShapeDtypeStruct((g, r, PF_COLS), F32), jax.ShapeDtypeStruct((g, r, PB_COLS), BF16)],
        scratch_shapes=[pltpu.VMEM((tm, d), BF16)],
        compiler_params=_params(("parallel", "parallel"), V7X_VMEM_LIMIT),
    )(x, ada, ada, gpre.reshape(1, d), w_in)


def _merge_kernel(x_ref, gt_ref, gpost_ref, oa_ref, hg_ref, ob_ref, oc_ref, mo_ref,
                  g0, g1, g2, g3, g4, g5, hgain_ref, mgain_ref,
                  whg_ref, wsb_ref, wml_ref, wout_ref, o_ref):
    f32 = lambda ref: ref[...].astype(F32)
    o_a = (_head_norm(oa_ref[...], hgain_ref[...], HG_HEADS, HG_DK) * _silu(f32(hg_ref))).astype(BF16)
    o_b = ob_ref[...]
    o_c = (_head_norm(f32(oc_ref), mgain_ref[...], ML_HEADS, ML_HEAD_DIM) * _sigmoid(f32(mo_ref))).astype(BF16)
    gates = ((g0, g2, g4), (g1, g3, g5))
    y = None
    for c in range(2):
        cs = slice(c * COL, (c + 1) * COL)
        ga, gb, gc = gates[c]
        m = (_sigmoid(f32(ga)) * _dot(o_a, whg_ref[:, cs])
             + _sigmoid(f32(gb)) * _dot(o_b, wsb_ref[:, cs])
             + _sigmoid(f32(gc)) * _dot(o_c, wml_ref[:, cs]))
        part = _dot(m.astype(BF16), wout_ref[cs, :])
        y = part if y is None else y + part
    o_ref[...] = x_ref[...] + (1.0 + gt_ref[...]) * _rms(y, gpost_ref[...])


def _merge(x, ada, gpost, pb, o_a, o_b, o_c, hgain, mgain, whg, wsb, wml, wout, tm):
    g, r, d = x.shape
    row = lambda col: pl.BlockSpec((None, tm, COL), lambda a, i: (a, i, col))
    vec = lambda n: pl.BlockSpec((1, n), lambda a, i: (0, 0))
    wsp = lambda shape: pl.BlockSpec(shape, lambda a, i: (0, 0))
    xspec = pl.BlockSpec((None, tm, d), lambda a, i: (a, i, 0))
    return pl.pallas_call(
        _merge_kernel,
        grid=(g, r // tm),
        in_specs=[xspec, _mod_spec(ada, tm, 5), vec(d),
                  row(0), row(PB_HG), row(0), row(0), row(PB_MO)]
                 + [row(PB_GATES + c) for c in range(6)]
                 + [vec(HG_WIDTH), vec(ML_WIDTH),
                    wsp((HG_WIDTH, d)), wsp((SB_WIDTH, d)), wsp((ML_WIDTH, d)), wsp((d, d))],
        out_specs=xspec,
        out_shape=jax.ShapeDtypeStruct(x.shape, F32),
        compiler_params=_params(("parallel", "parallel"), V7X_VMEM_LIMIT),
    )(x, ada, gpost.reshape(1, d), o_a, pb, o_b, o_c, pb,
      pb, pb, pb, pb, pb, pb,
      hgain.reshape(1, HG_WIDTH), mgain.reshape(1, ML_WIDTH), whg, wsb, wml, wout)


def _hgrn_lb(lbl, layer):
    e = jnp.exp(lbl - jnp.max(lbl, axis=0, keepdims=True))
    p = e / jnp.sum(e, axis=0, keepdims=True)
    lb = jnp.zeros_like(p[0:1])
    for r in range(1, layer + 1):
        lb = lb + p[r:r + 1]
    return lb


def _hgrn_gates(z, lb):
    logf = _logaddexp(jnp.log(lb), jnp.log1p(-lb) + _log_sigmoid(z))
    k = (1.0 - lb) * _sigmoid(-z)
    return logf, k


def _hgrn_prompt_kernel(lbl_ref, qb_ref, z_ref, ib_ref, o_ref, s_ref, st_ref, b_ref, k_ref, q_ref, i_ref,
                        *, layer):
    tc = qb_ref.shape[0]
    q_ref[...] = qb_ref[...].astype(F32)
    i_ref[...] = ib_ref[...].astype(F32)
    nchunk = tc // HG_CHUNK
    nsub = HG_CHUNK // HG_SUB

    @pl.when(pl.program_id(1) == 0)
    def _():
        st_ref[...] = jnp.zeros_like(st_ref)

    lb = _hgrn_lb(lbl_ref[...], layer)[0]
    logf, k = _hgrn_gates(z_ref[...], lb)
    k_ref[...] = k
    ri = lax.broadcasted_iota(jnp.int32, (tc, tc), 0)
    ci = lax.broadcasted_iota(jnp.int32, (tc, tc), 1)
    tri = jnp.where((ci <= ri) & (ri // HG_CHUNK == ci // HG_CHUNK), 1.0, 0.0).astype(BF16)
    b_ref[...] = _tri_dot(tri, logf)
    block_decay = -jnp.sum(logf.reshape(tc // HG_SUB, HG_SUB, HG_WIDTH), axis=1)
    safe = jnp.max(block_decay) < HG_SAFE_DECAY

    rows64 = lax.broadcasted_iota(jnp.int32, (HG_CHUNK, 1), 0)
    rows16 = lax.broadcasted_iota(jnp.int32, (HG_SUB, 1), 0)
    r64 = lax.broadcasted_iota(jnp.int32, (HG_CHUNK, HG_CHUNK), 0)
    c64 = lax.broadcasted_iota(jnp.int32, (HG_CHUNK, HG_CHUNK), 1)
    causal64 = c64 <= r64

    heads = range(HG_HEADS)
    sl = [slice(h * HG_DK, (h + 1) * HG_DK) for h in heads]

    def score_operands(q, b, kk, cs, s_i, include_diag):
        i0 = s_i * HG_SUB
        r = b[i0 - 1:i0, cs] if s_i else jnp.zeros((1, HG_DK), F32)
        hi = i0 + HG_SUB if include_diag else i0
        cap = HG_SAFE_DECAY if include_diag else 0.0
        qt = q[i0:i0 + HG_SUB, cs] * jnp.exp(b[i0:i0 + HG_SUB, cs] - r)
        kt = jnp.where(rows64 < hi, kk[:, cs] * jnp.exp(jnp.minimum(r - b[:, cs], cap)), 0.0)
        return qt.astype(BF16), kt.astype(BF16)

    def chunk(r0):
        rows = pl.ds(r0, HG_CHUNK)
        q = q_ref[rows, :]
        i = i_ref[rows, :]
        b = b_ref[rows, :]
        kk = k_ref[rows, :]
        blast = b[HG_CHUNK - 1:HG_CHUNK, :]
        eb = jnp.exp(b)
        kd = kk * jnp.exp(blast - b)
        ib = [i[:, cs].astype(BF16) for cs in sl]
        ops = [[score_operands(q, b, kk, cs, s_i, True) for s_i in range(nsub)] for cs in sl]
        p = [jnp.concatenate([_dot_nt(qt, kt) for qt, kt in ops[h]], axis=0) for h in heads]
        st = [st_ref[h] for h in heads]
        o_state = [_dot_nt((q[:, sl[h]] * eb[:, sl[h]]).astype(BF16), st[h].astype(BF16)) for h in heads]
        pm = [jnp.where(causal64 & safe, p[h], 0.0).astype(BF16) for h in heads]
        o_intra = [_dot(pm[h], ib[h]) for h in heads]
        upd = [_dot(i[:, sl[h]].T.astype(BF16), kd[:, sl[h]].astype(BF16)) for h in heads]
        o_ref[rows, :] = jnp.concatenate([o_state[h] + o_intra[h] for h in heads], axis=-1)
        for h in heads:
            st_ref[h] = st[h] * jnp.exp(blast[:, sl[h]]) + upd[h]

        @pl.when(jnp.logical_not(safe))
        def _():
            outs = []
            for h in heads:
                rows_p = [jnp.zeros((HG_SUB, HG_CHUNK), F32)]
                for s_i in range(1, nsub):
                    qt, kt = score_operands(q, b, kk, sl[h], s_i, False)
                    rows_p.append(_dot_nt(qt, kt))
                outs.append(_dot(jnp.concatenate(rows_p, axis=0).astype(BF16), ib[h]))
            o_ref[rows, :] += jnp.concatenate(outs, axis=-1)
            for s_i in range(nsub):
                base = r0 + s_i * HG_SUB
                srows = pl.ds(base, HG_SUB)
                q_i = q_ref[srows, :]
                b_i = b_ref[srows, :]

                def s_body(s, acc):
                    b_s = b_ref[pl.ds(base + s, 1), :]
                    k_s = k_ref[pl.ds(base + s, 1), :]
                    i_s = i_ref[pl.ds(base + s, 1), :]
                    e = jnp.where(rows16 >= s, jnp.exp(jnp.minimum(b_i - b_s, 0.0)), 0.0)
                    pr = q_i * e * k_s
                    parts = [jnp.sum(pr[:, cs], axis=1, keepdims=True) * i_s[:, cs] for cs in sl]
                    return acc + jnp.concatenate(parts, axis=-1)

                acc = lax.fori_loop(0, HG_SUB, s_body, jnp.zeros((HG_SUB, HG_WIDTH), F32))
                o_ref[srows, :] += acc

    def pair_body(n, carry):
        r0 = pl.multiple_of(n * (2 * HG_CHUNK), 2 * HG_CHUNK)
        chunk(r0)
        chunk(r0 + HG_CHUNK)
        return carry

    lax.fori_loop(0, nchunk // 2, pair_body, 0)

    @pl.when(pl.program_id(1) == pl.num_programs(1) - 1)
    def _():
        for h in range(HG_HEADS):
            s_ref[h] = st_ref[h].T


def _hgrn_prompt(pf, pb, lbl, layer, tc):
    b, t, _ = pf.shape
    row = lambda col: pl.BlockSpec((None, tc, COL), lambda a, i: (a, i, col))
    return pl.pallas_call(
        functools.partial(_hgrn_prompt_kernel, layer=layer),
        grid=(b, t // tc),
        in_specs=[pl.BlockSpec((DEPTH, 1, HG_WIDTH), lambda a, i: (0, 0, 0)), row(PB_Q), row(PF_Z), row(PB_I)],
        out_specs=[pl.BlockSpec((None, tc, HG_WIDTH), lambda a, i: (a, i, 0)),
                   pl.BlockSpec((None, HG_HEADS, HG_DK, HG_DK), lambda a, i: (a, 0, 0, 0))],
        out_shape=[jax.ShapeDtypeStruct((b, t, HG_WIDTH), F32),
                   jax.ShapeDtypeStruct((b, HG_HEADS, HG_DK, HG_DK), F32)],
        scratch_shapes=[pltpu.VMEM((HG_HEADS, HG_DK, HG_DK), F32)] + [pltpu.VMEM((tc, HG_WIDTH), F32)] * 4,
        compiler_params=_params(("parallel", "arbitrary"), V7X_VMEM_LIMIT),
    )(lbl.reshape(DEPTH, 1, HG_WIDTH), pb, pf, pb)


def _hgrn_step_kernel(lbl_ref, qc_ref, zc_ref, ir_ref, s_ref, o_ref, so_ref, *, layer):
    lb = _hgrn_lb(lbl_ref[...], layer)[0]
    for n in range(qc_ref.shape[0]):
        logf, k = _hgrn_gates(zc_ref[n], lb)
        outs = []
        for h in range(HG_HEADS):
            cs = slice(h * HG_DK, (h + 1) * HG_DK)
            s_new = jnp.exp(logf[h]) * s_ref[n, h] + k[h] * ir_ref[n, :, cs]
            so_ref[n, h] = s_new
            outs.append(jnp.sum(qc_ref[n, h] * s_new, axis=0, keepdims=True))
        o_ref[n] = jnp.concatenate(outs, axis=-1)


STEP_SEQS = 4


def _hgrn_step(q, z, i, state, lbl, layer):
    b = q.shape[0]
    qc = q.reshape(b, HG_HEADS, HG_DK, 1)
    zc = z.reshape(b, HG_HEADS, HG_DK, 1)
    ns = STEP_SEQS if b % STEP_SEQS == 0 else 1
    col = pl.BlockSpec((ns, HG_HEADS, HG_DK, 1), lambda a: (a, 0, 0, 0))
    return pl.pallas_call(
        functools.partial(_hgrn_step_kernel, layer=layer),
        grid=(b // ns,),
        in_specs=[pl.BlockSpec((DEPTH, HG_HEADS, HG_DK, 1), lambda a: (0, 0, 0, 0)), col, col,
                  pl.BlockSpec((ns, 1, COL), lambda a: (a, 0, 0)),
                  pl.BlockSpec((ns, None, HG_HEADS, HG_DK, HG_DK), lambda a: (a, layer, 0, 0, 0))],
        out_specs=[pl.BlockSpec((ns, 1, HG_WIDTH), lambda a: (a, 0, 0)),
                   pl.BlockSpec((ns, HG_HEADS, HG_DK, HG_DK), lambda a: (a, 0, 0, 0))],
        out_shape=[jax.ShapeDtypeStruct((b, 1, HG_WIDTH), F32),
                   jax.ShapeDtypeStruct((b, HG_HEADS, HG_DK, HG_DK), F32)],
        compiler_params=_params(("parallel",)),
    )(lbl.reshape(DEPTH, HG_HEADS, HG_DK, 1), qc, zc, i.reshape(b, 1, COL), state)


def _softplus(z):
    return jnp.maximum(z, 0.0) + jnp.log(1.0 + jnp.exp(-jnp.abs(z)))


def _sb_prompt_kernel(bias_ref, q_ref, k_ref, v_ref, o_ref, kt_out, vt_out,
                      qh_ref, kt_ref, vh_ref, acc_ref, c_ref):
    tq, tk = SB_TQ, SB_TK
    nk = k_ref.shape[0] // tk
    qi = pl.program_id(1)

    @pl.when(qi == 0)
    def _():
        for n in range(nk):
            ks = slice(n * tk, (n + 1) * tk)
            kt_full = k_ref[ks, :].T
            kt_out[:, ks] = kt_full
            vt_out[:, ks] = v_ref[ks, :].T
            for h in range(SB_HEADS):
                cs = slice(h * SB_HEAD_DIM, (h + 1) * SB_HEAD_DIM)
                kt_ref[h, n] = kt_full[cs, :].astype(BF16)
                vh_ref[h, n] = v_ref[ks, cs].astype(BF16)

    nrb = tq // SB_ROWS
    for h in range(SB_HEADS):
        cs = slice(h * SB_HEAD_DIM, (h + 1) * SB_HEAD_DIM)
        for r in range(nrb):
            rows = slice(r * SB_ROWS, (r + 1) * SB_ROWS)
            qh_ref[h, r] = (q_ref[:, cs].astype(F32)[rows] * (SB_HEAD_DIM ** -0.5 * LOG2E)).astype(BF16)
    acc_ref[...] = jnp.zeros_like(acc_ref)
    c_ref[...] = jnp.zeros_like(c_ref)

    ri = lax.broadcasted_iota(jnp.int32, (tk, tk), 0)
    ci = lax.broadcasted_iota(jnp.int32, (tk, tk), 1)
    upper = jnp.where(ri > ci, 1.0, 0.0).astype(BF16)
    rpt = tk // SB_ROWS
    base = qi * (tq // tk)

    def tile(j, r, diag):
        nkeys = (r % rpt + 1) * SB_ROWS if diag else tk
        mask = None
        if diag:
            mask = (lax.broadcasted_iota(jnp.int32, (SB_ROWS, nkeys), 1)
                    < lax.broadcasted_iota(jnp.int32, (SB_ROWS, nkeys), 0) + (r % rpt) * SB_ROWS)
        for g0 in range(0, SB_HEADS, SB_HEAD_GROUP):
            heads = range(g0, g0 + SB_HEAD_GROUP)
            zs = [_dot(qh_ref[h, r], kt_ref[h, j, :, :nkeys]) + bias_ref[h] * LOG2E for h in heads]
            lbs, l1ms = [], []
            for z in zs:
                sp = jnp.maximum(z, 0.0) + jnp.log(1.0 + jnp.exp2(-jnp.abs(z))) * LOG2E
                l1m = -sp
                if mask is not None:
                    l1m = jnp.where(mask, l1m, 0.0)
                lbs.append(z - sp)
                l1ms.append(l1m)
            sufs = [_dot(l1m.astype(BF16), upper[:nkeys, :nkeys]) for l1m in l1ms]
            weights = []
            for h, lb, l1m, suf in zip(heads, lbs, l1ms, sufs):
                c = c_ref[h * nrb + r][:, 0:1]
                a = jnp.exp2(lb + suf + c)
                if mask is not None:
                    a = jnp.where(mask, a, 0.0)
                weights.append(a.astype(BF16))
                c_ref[h * nrb + r] = jnp.broadcast_to(c + suf[:, 0:1] + l1m[:, 0:1], (SB_ROWS, 128))
            for h, a in zip(heads, weights):
                acc_ref[h, r] += _dot(a, vh_ref[h, j, :nkeys, :])

    for r in range(nrb):
        tile(base + r // rpt, r, True)
        for kk in range(r // rpt - 1, -1, -1):
            tile(base + kk, r, False)

    def body(step, carry):
        for r in range(nrb):
            tile(base - 1 - step, r, False)
        return carry

    lax.fori_loop(0, base, body, 0)
    o_ref[...] = jnp.concatenate(
        [jnp.concatenate([acc_ref[h, r] for h in range(SB_HEADS)], axis=-1) for r in range(nrb)],
        axis=0).astype(BF16)


def _sb_prompt(pf, pb, bias):
    b, t, _ = pf.shape
    tq = min(SB_TQ, t)
    nk = t // SB_TK
    seq = lambda col: pl.BlockSpec((None, t, COL), lambda a, i: (a, 0, col))
    return pl.pallas_call(
        _sb_prompt_kernel,
        grid=(b, t // tq),
        in_specs=[pl.BlockSpec(memory_space=pltpu.SMEM),
                  pl.BlockSpec((None, tq, COL), lambda a, i: (a, i, PB_SQ)), seq(PF_K), seq(PF_V)],
        out_specs=[pl.BlockSpec((None, tq, SB_WIDTH), lambda a, i: (a, i, 0)),
                   pl.BlockSpec((None, SB_WIDTH, t), lambda a, i: (a, 0, 0)),
                   pl.BlockSpec((None, SB_WIDTH, t), lambda a, i: (a, 0, 0))],
        out_shape=[jax.ShapeDtypeStruct((b, t, SB_WIDTH), BF16),
                   jax.ShapeDtypeStruct((b, SB_WIDTH, t), F32),
                   jax.ShapeDtypeStruct((b, SB_WIDTH, t), F32)],
        scratch_shapes=[pltpu.VMEM((SB_HEADS, tq // SB_ROWS, SB_ROWS, SB_HEAD_DIM), BF16),
                        pltpu.VMEM((SB_HEADS, nk, SB_HEAD_DIM, SB_TK), BF16),
                        pltpu.VMEM((SB_HEADS, nk, SB_TK, SB_HEAD_DIM), BF16),
                        pltpu.VMEM((SB_HEADS, tq // SB_ROWS, SB_ROWS, SB_HEAD_DIM), F32),
                        pltpu.VMEM((SB_HEADS * (tq // SB_ROWS), SB_ROWS, 128), F32)],
        compiler_params=_params(("parallel", "arbitrary"), V7X_VMEM_LIMIT),
    )(bias, pb, pf, pf)


SB_PAGES_PER_STEP = 16


def _sb_decode_kernel(pt_ref, q_ref, bias_ref, *refs):
    npg = SB_PAGES_PER_STEP
    k_refs = refs[:npg]
    v_refs = refs[npg:2 * npg]
    o_ref, qb_ref, c_ref, acc_ref = refs[2 * npg:]
    g = pl.program_id(1)

    @pl.when(g == 0)
    def _():
        qb_ref[...] = jnp.broadcast_to(q_ref[...], qb_ref.shape)
        c_ref[...] = jnp.zeros_like(c_ref)
        acc_ref[...] = jnp.zeros_like(acc_ref)

    ri = lax.broadcasted_iota(jnp.int32, (PAGE_SIZE, PAGE_SIZE), 0)
    ci = lax.broadcasted_iota(jnp.int32, (PAGE_SIZE, PAGE_SIZE), 1)
    upper = jnp.where(ri > ci, 1.0, 0.0).astype(BF16)
    bias = bias_ref[...]
    c = c_ref[:, 0:1]
    weights = [None] * npg
    for p in range(npg - 1, -1, -1):
        zrows = []
        for h in range(SB_HEADS):
            part = k_refs[p][h, 0:8, :] * qb_ref[h, 0:8, :]
            for r in range(8, SB_HEAD_DIM, 8):
                part = part + k_refs[p][h, r:r + 8, :] * qb_ref[h, r:r + 8, :]
            zrows.append(jnp.sum(part, axis=0, keepdims=True))
        z = jnp.concatenate(zrows, axis=0) + bias
        sp = _softplus(z)
        l1m = -sp
        suf = _dot(l1m.astype(BF16), upper)
        weights[p] = jnp.exp(z - sp + suf + c)
        c = c + suf[:, 0:1] + l1m[:, 0:1]
    c_ref[...] = jnp.broadcast_to(c, c_ref.shape)
    for h in range(SB_HEADS):
        part = weights[0][h:h + 1, :] * v_refs[0][h]
        for p in range(1, npg):
            part = part + weights[p][h:h + 1, :] * v_refs[p][h]
        acc_ref[h] += part

    @pl.when(g == pl.num_programs(1) - 1)
    def _():
        o_ref[...] = jnp.sum(acc_ref[...], axis=2, keepdims=True)


def _sb_decode(sq, bias, cache_kt, cache_vt, page_table, layer):
    b = sq.shape[0]
    n_pages = page_table.shape[1]
    npg = SB_PAGES_PER_STEP
    ng = n_pages // npg
    qcol = (sq * (SB_HEAD_DIM ** -0.5)).reshape(b, SB_HEADS, SB_HEAD_DIM, 1)

    def page_spec(p):
        return pl.BlockSpec((None, None, SB_HEADS, SB_HEAD_DIM, PAGE_SIZE),
                            lambda a, g, pt: (pt[a, (ng - 1 - g) * npg + p], layer, 0, 0, 0))

    hd = pl.BlockSpec((None, SB_HEADS, SB_HEAD_DIM, 1), lambda a, g, pt: (a, 0, 0, 0))
    grid_spec = pltpu.PrefetchScalarGridSpec(
        num_scalar_prefetch=1,
        grid=(b, ng),
        in_specs=[hd, pl.BlockSpec((SB_HEADS, 1), lambda a, g, pt: (0, 0))]
                 + [page_spec(p) for p in range(npg)] + [page_spec(p) for p in range(npg)],
        out_specs=hd,
        scratch_shapes=[pltpu.VMEM((SB_HEADS, SB_HEAD_DIM, PAGE_SIZE), F32),
                        pltpu.VMEM((SB_HEADS, 128), F32),
                        pltpu.VMEM((SB_HEADS, SB_HEAD_DIM, PAGE_SIZE), F32)],
    )
    out = pl.pallas_call(
        _sb_decode_kernel,
        grid_spec=grid_spec,
        out_shape=jax.ShapeDtypeStruct((b, SB_HEADS, SB_HEAD_DIM, 1), F32),
        compiler_params=_params(("parallel", "arbitrary"), V7X_VMEM_LIMIT),
    )(page_table, qcol, bias.reshape(SB_HEADS, 1), *([cache_kt] * npg), *([cache_vt] * npg))
    return out.reshape(b, 1, SB_WIDTH)


def _ml_pre_kernel(x0, x1, x2, x3, *refs):
    _ml_pre_body(x0[...], x1[...], x2[...], x3[...], *refs)


def _ml_pre_seq_kernel(prev_ref, x_ref, cw_ref, cb_ref, wq_ref, wk_ref, wv_ref, wif_ref, bif_ref,
                       q_ref, k_ref, v_ref, g_ref, xs_ref):
    tm = x_ref.shape[0]
    halo = prev_ref.shape[0]
    x = x_ref[...]
    xs_ref[halo:, :] = x
    xs_ref[:halo, :] = jnp.where(pl.program_id(1) == 0, 0.0, prev_ref[...])
    taps = [xs_ref[pl.ds(halo - (ML_CONV - 1) + j, tm), :] for j in range(ML_CONV - 1)]
    _ml_pre_body(*taps, x, cw_ref, cb_ref, wq_ref, wk_ref, wv_ref, wif_ref, bif_ref,
                 q_ref, k_ref, v_ref, g_ref)


def _ml_pre_body(x0, x1, x2, x, cw_ref, cb_ref, wq_ref, wk_ref, wv_ref, wif_ref, bif_ref,
                 q_ref, k_ref, v_ref, g_ref):
    cw = cw_ref[...]
    xc = _silu(x0 * cw[0:1] + x1 * cw[1:2] + x2 * cw[2:3] + x * cw[3:4] + cb_ref[...])
    xcb = xc.astype(BF16)
    xb = x.astype(BF16)
    qs, ks, vs = [], [], []
    for h in range(ML_HEADS):
        cs = slice(h * ML_HEAD_DIM, (h + 1) * ML_HEAD_DIM)
        qs.append(_dot(xcb[:, cs], wq_ref[h]))
        ks.append(_dot(xcb[:, cs], wk_ref[h]))
        vs.append(_dot(xb[:, cs], wv_ref[h]))
    q = jnp.concatenate(qs, axis=-1)
    k = jnp.concatenate(ks, axis=-1)
    v = jnp.concatenate(vs, axis=-1)
    q_ref[...] = q
    k_ref[...] = k
    v_ref[...] = v
    g_ref[...] = (_dot(q.astype(BF16), wif_ref[0:ML_WIDTH, :])
                  + _dot(k.astype(BF16), wif_ref[ML_WIDTH:2 * ML_WIDTH, :])
                  + _dot(v.astype(BF16), wif_ref[2 * ML_WIDTH:3 * ML_WIDTH, :]) + bif_ref[...])


def _ml_pre(xs, cw, cb, wq, wk, wv, wif, bif, tm):
    g, r, w = xs[0].shape
    row = pl.BlockSpec((None, tm, w), lambda a, i: (a, i, 0))
    full = lambda shape: pl.BlockSpec(shape, lambda a, i: (0,) * len(shape))
    out = jax.ShapeDtypeStruct((g, r, w), F32)
    return pl.pallas_call(
        _ml_pre_kernel,
        grid=(g, r // tm),
        in_specs=[row, row, row, row, full((ML_CONV, w)), full((1, w)),
                  full(wq.shape), full(wk.shape), full(wv.shape), full(wif.shape), full((1, 128))],
        out_specs=[row, row, row, pl.BlockSpec((None, tm, 128), lambda a, i: (a, i, 0))],
        out_shape=[out, out, out, jax.ShapeDtypeStruct((g, r, 128), F32)],
        compiler_params=_params(("parallel", "parallel"), V7X_VMEM_LIMIT),
    )(*xs, cw, cb.reshape(1, w), wq, wk, wv, wif, bif)


ML_HALO = 8


def _ml_pre_seq(pf, cw, cb, wq, wk, wv, wif, bif, tm):
    b, t, _ = pf.shape
    w = ML_WIDTH
    per = tm // ML_HALO
    row = pl.BlockSpec((None, tm, w), lambda a, i: (a, i, 0))
    full = lambda shape: pl.BlockSpec(shape, lambda a, i: (0,) * len(shape))
    out = jax.ShapeDtypeStruct((b, t, w), F32)
    return pl.pallas_call(
        _ml_pre_seq_kernel,
        grid=(b, t // tm),
        in_specs=[pl.BlockSpec((None, ML_HALO, w), lambda a, i: (a, jnp.maximum(i * per - 1, 0), PF_MX)),
                  pl.BlockSpec((None, tm, w), lambda a, i: (a, i, PF_MX)),
                  full((ML_CONV, w)), full((1, w)),
                  full(wq.shape), full(wk.shape), full(wv.shape), full(wif.shape), full((1, 128))],
        out_specs=[row, row, row, pl.BlockSpec((None, tm, 128), lambda a, i: (a, i, 0))],
        out_shape=[out, out, out, jax.ShapeDtypeStruct((b, t, 128), F32)],
        scratch_shapes=[pltpu.VMEM((tm + ML_HALO, w), F32)],
        compiler_params=_params(("parallel", "parallel"), V7X_VMEM_LIMIT),
    )(pf, pf, cw, cb.reshape(1, w), wq, wk, wv, wif, bif)


def _ml_prompt_kernel(q_ref, k_ref, v_ref, g_ref, h_ref, c_out, n_out, m_out, c_ref, n_ref, m_ref):
    L = ML_CHUNK
    nch = q_ref.shape[0] // L

    @pl.when(pl.program_id(1) == 0)
    def _():
        c_ref[...] = jnp.zeros_like(c_ref)
        n_ref[...] = jnp.zeros_like(n_ref)
        m_ref[...] = jnp.zeros_like(m_ref)

    ri = lax.broadcasted_iota(jnp.int32, (L, L), 0)
    ci = lax.broadcasted_iota(jnp.int32, (L, L), 1)
    causal = ci <= ri
    tril = jnp.where(causal, 1.0, 0.0).astype(BF16)
    triu = jnp.where(ri <= ci, 1.0, 0.0).astype(BF16)
    heads = range(ML_HEADS)
    sl = [slice(h * ML_HEAD_DIM, (h + 1) * ML_HEAD_DIM) for h in heads]

    ones = jnp.ones((L, ML_HEAD_DIM), BF16)
    pre = []
    for j in range(nch):
        rows = slice(j * L, (j + 1) * L)
        g = g_ref[rows, :]
        gt = g.T
        b_cols = _tri_dot(tril, _log_sigmoid(g))
        b_rows = _dot_tri(_log_sigmoid(gt), triu)
        q = [q_ref[rows, cs] for cs in sl]
        ks = [k_ref[rows, cs] * (ML_HEAD_DIM ** -0.5) for cs in sl]
        qb = [x.astype(BF16) for x in q]
        ksb = [x.astype(BF16) for x in ks]
        vb = [v_ref[rows, cs].astype(BF16) for cs in sl]
        vt = [v_ref[rows, cs].T for cs in sl]
        qk = [_dot_nt(qb[h], ksb[h]) for h in heads]
        b_bc = [jnp.broadcast_to(b_cols[:, ML_HEADS + h:ML_HEADS + h + 1], (L, L)) for h in heads]
        u_row = [gt[h:h + 1, :] - b_rows[ML_HEADS + h:ML_HEADS + h + 1, :] for h in heads]
        u_max = [jnp.broadcast_to(jnp.max(jnp.where(causal, u_row[h], NEG_INF), axis=1, keepdims=True), (L, L))
                 for h in heads]
        pre.append(dict(qb=qb, ksb=ksb, vb=vb, vt=vt, qk=qk, b_bc=b_bc, u_row=u_row, u_max=u_max))

    c0 = [c_ref[h] for h in heads]
    n0 = [n_ref[h:h + 1, :] for h in heads]
    m0 = [m_ref[h:h + 1, 0:1] for h in heads]
    for j, pj in enumerate(pre):
        rows = slice(j * L, (j + 1) * L)
        qc = [_dot_nt(pj["qb"][h], c0[h].astype(BF16)) for h in heads]
        qn = [_dot_nt(pj["qb"][h], jnp.broadcast_to(n0[h], (L, ML_HEAD_DIM)).astype(BF16)) for h in heads]
        y, inter, w = [], [], []
        for h in heads:
            y_h = jnp.maximum(pj["u_max"][h], m0[h])
            y.append(y_h)
            inter.append(jnp.exp(m0[h] - y_h))
            w.append(pj["qk"][h] * jnp.where(causal, jnp.exp(pj["u_row"][h] - y_h), 0.0))
        wb = [w[h].astype(BF16) for h in heads]
        wv = [_dot(wb[h], pj["vb"][h]) for h in heads]
        wsum = [_dot(wb[h], ones) for h in heads]
        outs, dec, w_end, m_new = [], [], [], []
        for h in heads:
            m_t = pj["b_bc"][h] + y[h]
            numer = inter[h] * qc[h] + wv[h]
            denom = inter[h] * qn[h] + wsum[h]
            outs.append(numer / jnp.maximum(jnp.abs(denom), jnp.exp(-m_t)))
            m_h = m_t[L - 1:L, 0:1]
            b_last = pj["b_bc"][h][L - 1:L, 0:1]
            w_end.append(jnp.exp(pj["u_row"][h] + b_last - m_h))
            dec.append(jnp.exp(b_last + m0[h] - m_h))
            m_new.append(m_h)
        c_upd = [_dot((pj["vt"][h] * w_end[h]).astype(BF16), pj["ksb"][h]) for h in heads]
        n_upd = [_dot(w_end[h].astype(BF16), pj["ksb"][h]) for h in heads]
        c0 = [dec[h] * c0[h] + c_upd[h] for h in heads]
        n0 = [dec[h] * n0[h] + n_upd[h] for h in heads]
        m0 = m_new
        h_ref[rows, :] = jnp.concatenate(outs, axis=-1).astype(BF16)
    for h in heads:
        c_ref[h] = c0[h]
        n_ref[h:h + 1, :] = n0[h]
        m_ref[h:h + 1, :] = jnp.broadcast_to(m0[h], (1, m_ref.shape[1]))

    @pl.when(pl.program_id(1) == pl.num_programs(1) - 1)
    def _():
        c_out[...] = c_ref[...]
        n_out[...] = n_ref[...]
        m_out[...] = m_ref[...]


ML_CHUNKS_PER_STEP = 4


def _ml_prompt(q, k, v, gates):
    b, t, w = q.shape
    tr = min(t, ML_CHUNKS_PER_STEP * ML_CHUNK)
    row = pl.BlockSpec((None, tr, w), lambda a, i: (a, i, 0))
    return pl.pallas_call(
        _ml_prompt_kernel,
        grid=(b, t // tr),
        in_specs=[row, row, row, pl.BlockSpec((None, tr, 128), lambda a, i: (a, i, 0))],
        out_specs=[row,
                   pl.BlockSpec((None, ML_HEADS, ML_HEAD_DIM, ML_HEAD_DIM), lambda a, i: (a, 0, 0, 0)),
                   pl.BlockSpec((None, ML_HEADS, ML_HEAD_DIM), lambda a, i: (a, 0, 0)),
                   pl.BlockSpec((None, ML_HEADS, 128), lambda a, i: (a, 0, 0))],
        out_shape=[jax.ShapeDtypeStruct((b, t, w), BF16),
                   jax.ShapeDtypeStruct((b, ML_HEADS, ML_HEAD_DIM, ML_HEAD_DIM), F32),
                   jax.ShapeDtypeStruct((b, ML_HEADS, ML_HEAD_DIM), F32),
                   jax.ShapeDtypeStruct((b, ML_HEADS, 128), F32)],
        scratch_shapes=[pltpu.VMEM((ML_HEADS, ML_HEAD_DIM, ML_HEAD_DIM), F32),
                        pltpu.VMEM((ML_HEADS, ML_HEAD_DIM), F32), pltpu.VMEM((ML_HEADS, 128), F32)],
        compiler_params=_params(("parallel", "arbitrary"), V7X_VMEM_LIMIT),
    )(q, k, v, gates)


def _ml_step_kernel(q_ref, k_ref, vc_ref, g_ref, c_ref, n_ref, m_ref, h_ref, c_out, n_out, m_out):
    lane = lax.broadcasted_iota(jnp.int32, (1, 128), 1)
    for s in range(q_ref.shape[0]):
        g = g_ref[s]
        m_row = jnp.zeros((1, 128), F32)
        for h in range(ML_HEADS):
            cs = slice(h * ML_HEAD_DIM, (h + 1) * ML_HEAD_DIM)
            q = q_ref[s, :, cs]
            ks = k_ref[s, :, cs] * (ML_HEAD_DIM ** -0.5)
            v = vc_ref[s, h]
            it = g[:, h:h + 1]
            logf = _log_sigmoid(g[:, ML_HEADS + h:ML_HEADS + h + 1])
            m0 = m_ref[s, :, h:h + 1]
            c0 = c_ref[s, h]
            n0 = n_ref[s, h:h + 1, :]
            m_t = jnp.maximum(logf + m0, it)
            dm = jnp.exp(it - m_t)
            inter = jnp.exp(logf + m0 - m_t)
            w = jnp.sum(q * ks, axis=1, keepdims=True) * dm
            numer = inter * jnp.sum(c0 * q, axis=1, keepdims=True) + w * v
            denom = inter * jnp.sum(q * n0, axis=1, keepdims=True) + w
            h_ref[s, h] = numer / jnp.maximum(jnp.abs(denom), jnp.exp(-m_t))
            w_end = jnp.exp(it - m_t)
            dec = jnp.exp(logf + m0 - m_t)
            c_out[s, h] = dec * c0 + (w_end * v) * ks
            n_out[s, h:h + 1, :] = dec * n0 + w_end * ks
            m_row = jnp.where(lane == h, m_t, m_row)
        m_out[s] = m_row


def _ml_step(q, k, v, gates, state_c, state_n, state_m, layer):
    b = q.shape[0]
    vc = v.reshape(b, ML_HEADS, ML_HEAD_DIM, 1)
    m0 = state_m[:, layer].reshape(b, 1, ML_HEADS)
    ns = STEP_SEQS if b % STEP_SEQS == 0 else 1
    row = pl.BlockSpec((ns, 1, ML_WIDTH), lambda a: (a, 0, 0))
    col = pl.BlockSpec((ns, ML_HEADS, ML_HEAD_DIM, 1), lambda a: (a, 0, 0, 0))
    cspec = pl.BlockSpec((ns, ML_HEADS, ML_HEAD_DIM, ML_HEAD_DIM), lambda a: (a, 0, 0, 0))
    nspec = pl.BlockSpec((ns, ML_HEADS, ML_HEAD_DIM), lambda a: (a, 0, 0))
    return pl.pallas_call(
        _ml_step_kernel,
        grid=(b // ns,),
        in_specs=[row, row, col, pl.BlockSpec((ns, 1, 128), lambda a: (a, 0, 0)),
                  pl.BlockSpec((ns, None, ML_HEADS, ML_HEAD_DIM, ML_HEAD_DIM),
                               lambda a: (a, layer, 0, 0, 0)),
                  pl.BlockSpec((ns, None, ML_HEADS, ML_HEAD_DIM), lambda a: (a, layer, 0, 0)),
                  pl.BlockSpec((ns, 1, ML_HEADS), lambda a: (a, 0, 0))],
        out_specs=[col, cspec, nspec, pl.BlockSpec((ns, 1, 128), lambda a: (a, 0, 0))],
        out_shape=[jax.ShapeDtypeStruct((b, ML_HEADS, ML_HEAD_DIM, 1), F32),
                   jax.ShapeDtypeStruct((b, ML_HEADS, ML_HEAD_DIM, ML_HEAD_DIM), F32),
                   jax.ShapeDtypeStruct((b, ML_HEADS, ML_HEAD_DIM), F32),
                   jax.ShapeDtypeStruct((b, 1, 128), F32)],
        compiler_params=_params(("parallel",)),
    )(q, k, vc, gates, state_c, state_n, m0)


def _prep_weights(p):
    bf = lambda a: a.astype(BF16)
    w = {}
    for name in ("w_ffn_in", "w_ffn_down", "w_in", "w_hg_proj", "w_sb_proj", "w_ml_proj", "w_out", "ml_wq", "ml_wk", "ml_wv"):
        w[name] = bf(p[name])
    w["wif"] = bf(jnp.pad(p["ml_w_if"], ((0, 0), (0, 0), (0, 128 - 2 * ML_HEADS))))
    w["bif"] = jnp.pad(p["ml_b_if"], ((0, 0), (0, 128 - 2 * ML_HEADS))).reshape(DEPTH, 1, 128)
    return w


def _layer(x, ada, l, p, w, tm, past):
    ffn = lambda xx, j, s: _ffn(xx, ada, j, p["g_pre"][l, j], p["g_post"][l, j],
                                w["w_ffn_in"], w["w_ffn_down"], l, s, tm["ffn"])
    x = ffn(x, 0, 0)
    pf, pb = _mixin(x, ada, p["g_pre"][l, 1], w["w_in"], l, tm["mixin"])
    st = {}
    blk = lambda arr, c: arr[..., c * COL:(c + 1) * COL]
    if past is None:
        b, t, _ = x.shape
        o_a, st["hgrn"] = _hgrn_prompt(pf, pb, p["hg_lb_logits"], l, tm["hgrn"])
        o_b, kt, vt = _sb_prompt(pf, pb, p["sb_bias"][l])
        q, k, v, gates = _ml_pre_seq(pf, p["ml_conv_w"][l], p["ml_conv_b"][l], w["ml_wq"][l],
                                     w["ml_wk"][l], w["ml_wv"][l], w["wif"][l], w["bif"][l], tm["mlpre"])
        o_c, st["mc"], st["mn"], m_pad = _ml_prompt(q, k, v, gates)
        st["mm"] = m_pad[:, :, 0]
        st["mconv"] = blk(pf, PF_MX)[:, t - (ML_CONV - 1):]
        st["k"] = kt.reshape(b, SB_HEADS, SB_HEAD_DIM, t)
        st["v"] = vt.reshape(b, SB_HEADS, SB_HEAD_DIM, t)
    else:
        b = x.shape[1]
        rows = lambda arr, c: blk(arr, c).reshape(b, COL).astype(F32)
        mx = blk(pf, PF_MX)
        o_a, st["hgrn"] = _hgrn_step(rows(pb, PB_Q), rows(pf, PF_Z), rows(pb, PB_I), past["hgrn"],
                                     p["hg_lb_logits"], l)
        o_b = _sb_decode(rows(pb, PB_SQ), p["sb_bias"][l], past["k"], past["v"], past["page_table"], l)
        buf = past["mconv"][:, l]
        taps = [buf[:, j].reshape(1, b, ML_WIDTH) for j in range(ML_CONV - 1)]
        q, k, v, gates = _ml_pre(taps + [mx], p["ml_conv_w"][l], p["ml_conv_b"][l], w["ml_wq"][l],
                                 w["ml_wk"][l], w["ml_wv"][l], w["wif"][l], w["bif"][l], b)
        hc, st["mc"], st["mn"], m_pad = _ml_step(q.reshape(b, 1, ML_WIDTH), k.reshape(b, 1, ML_WIDTH),
                                                 v.reshape(b, 1, ML_WIDTH), gates.reshape(b, 1, 128),
                                                 past["mc"], past["mn"], past["mm"], l)
        o_a = o_a.reshape(1, b, HG_WIDTH)
        o_b = o_b.reshape(1, b, SB_WIDTH).astype(BF16)
        o_c = hc.reshape(1, b, ML_WIDTH).astype(BF16)
        st["mm"] = m_pad[:, 0, :ML_HEADS]
        st["mconv"] = jnp.concatenate([buf[:, 1:], mx.reshape(b, 1, ML_WIDTH)], axis=1)
        st["k"] = blk(pf, PF_K).reshape(b, 1, SB_HEADS, SB_HEAD_DIM)
        st["v"] = blk(pf, PF_V).reshape(b, 1, SB_HEADS, SB_HEAD_DIM)
    x = _merge(x, ada, p["g_post"][l, 1], pb, o_a, o_b, o_c, p["hg_gain"][l], p["ml_gain"][l],
               w["w_hg_proj"][l], w["w_sb_proj"][l], w["w_ml_proj"][l], w["w_out"][l], tm["merge"])
    x = ffn(x, 2, 1)
    return x, st


def _run(x, ada_all, p, w, tm, past):
    outs = []
    for l in range(DEPTH):
        x, st = _layer(x, ada_all[l], l, p, w, tm, past)
        outs.append(st)
    return x, {name: jnp.stack([o[name] for o in outs], axis=1) for name in outs[0]}


def kernel(x_prompt, x_sample, cache_sb_k, cache_sb_v, state_hgrn, state_mlstm_c, state_mlstm_n,
           state_mlstm_m, state_mlstm_conv, page_table, c_prompt, c_sample, w_ada, b_ada, g_pre, g_post,
           w_ffn_in, w_ffn_down, w_in, sb_bias, hg_lb_logits, hg_gain, ml_conv_w, ml_conv_b, ml_wq, ml_wk,
           ml_wv, ml_w_if, ml_b_if, ml_gain, w_hg_proj, w_sb_proj, w_ml_proj, w_out):
    p = dict(g_pre=g_pre, g_post=g_post, w_ffn_in=w_ffn_in, w_ffn_down=w_ffn_down, w_in=w_in,
             sb_bias=sb_bias, hg_lb_logits=hg_lb_logits, hg_gain=hg_gain, ml_conv_w=ml_conv_w,
             ml_conv_b=ml_conv_b, ml_wq=ml_wq, ml_wk=ml_wk, ml_wv=ml_wv, ml_w_if=ml_w_if,
             ml_b_if=ml_b_if, ml_gain=ml_gain, w_hg_proj=w_hg_proj, w_sb_proj=w_sb_proj,
             w_ml_proj=w_ml_proj, w_out=w_out)
    w = _prep_weights(p)
    bp, t, d = x_prompt.shape
    bs = x_sample.shape[0]
    ada = _ada(jnp.concatenate([c_prompt, c_sample], axis=0), w_ada, b_ada)
    ada_p = ada[:, :bp].reshape(DEPTH, bp, 1, N_SUB * 3 * d)
    ada_s = ada[:, bp:].reshape(DEPTH, 1, bs, N_SUB * 3 * d)

    tm_p = dict(ffn=min(t, 512), mixin=min(t, 512), merge=min(t, 256), hgrn=min(t, 512),
                mlpre=min(t, 512))
    y_p, sp = _run(x_prompt, ada_p, p, w, tm_p, None)

    past = dict(k=jnp.transpose(cache_sb_k, (0, 1, 3, 4, 2)), v=jnp.transpose(cache_sb_v, (0, 1, 3, 4, 2)),
                page_table=page_table, hgrn=state_hgrn, mc=state_mlstm_c, mn=state_mlstm_n,
                mm=state_mlstm_m, mconv=state_mlstm_conv)
    tm_s = dict(ffn=bs, mixin=bs, merge=bs)
    y_s, ss = _run(x_sample.reshape(1, bs, d), ada_s, p, w, tm_s, past)
    y_s = y_s.reshape(bs, 1, d)
    k_p = jnp.transpose(sp["k"], (0, 1, 4, 2, 3))
    v_p = jnp.transpose(sp["v"], (0, 1, 4, 2, 3))
    return (y_p, y_s, k_p, v_p, ss["k"], ss["v"], sp["hgrn"], ss["hgrn"],
            sp["mc"], ss["mc"], sp["mn"], ss["mn"], sp["mm"], ss["mm"], sp["mconv"], ss["mconv"])
```

```python
import functools

import jax
import jax.numpy as jnp
from jax import lax
from jax.experimental import pallas as pl
from jax.experimental.pallas import tpu as pltpu

F32 = jnp.float32
BF16 = jnp.bfloat16

D_MODEL = 1024
DEPTH = 2
PAGE_SIZE = 128
HG_HEADS = 4
HG_DK = 128
HG_WIDTH = 512
HG_CHUNK = 64
HG_SUB = 16
HG_SAFE_DECAY = 80.0
SB_HEADS = 8
SB_HEAD_DIM = 64
SB_WIDTH = 512
SB_TQ = 512
SB_TK = 256
SB_ROWS = 128
SB_HEAD_GROUP = 8
ML_HEADS = 4
ML_HEAD_DIM = 128
ML_WIDTH = 512
ML_CHUNK = 128
ML_CONV = 4
D_FF = 2816
FF_CHUNK = 256
FFN_RES = 0.5
N_SUB = 3
NORM_EPS = 1e-6
IN_COLS = 7680
COL = 512
PF_SRC = (1, 5, 6, 7)
PB_SRC = (0, 2, 3, 4, 8, 9, 10, 11, 12, 13, 14)
PF_COLS = len(PF_SRC) * COL
PB_COLS = len(PB_SRC) * COL
PF_Z, PF_K, PF_V, PF_MX = 0, 1, 2, 3
PB_Q, PB_I, PB_HG, PB_SQ, PB_MO, PB_GATES = 0, 1, 2, 3, 4, 5
V7X_VMEM_LIMIT = 56 * 1024 * 1024
NEG_INF = float("-inf")
LOG2E = 1.4426950408889634


def _dot(a, b):
    return jnp.dot(a, b, preferred_element_type=F32)


def _dot_nt(a, b):
    return lax.dot_general(a, b, (((1,), (1,)), ((), ())), preferred_element_type=F32)


def _split3(x):
    x1 = x.astype(BF16)
    r1 = x - x1.astype(F32)
    x2 = r1.astype(BF16)
    x3 = (r1 - x2.astype(F32)).astype(BF16)
    return x1, x2, x3


def _tri_dot(tri, x):
    x1, x2, x3 = _split3(x)
    return _dot(tri, x1) + _dot(tri, x2) + _dot(tri, x3)


def _dot_tri(x, tri):
    x1, x2, x3 = _split3(x)
    return _dot(x1, tri) + _dot(x2, tri) + _dot(x3, tri)


def _sigmoid(x):
    return 0.5 * jnp.tanh(0.5 * x) + 0.5


def _silu(x):
    return x * _sigmoid(x)


def _log_sigmoid(x):
    return jnp.minimum(x, 0.0) - jnp.log(1.0 + jnp.exp(-jnp.abs(x)))


def _logaddexp(a, b):
    amax = jnp.maximum(a, b)
    delta = a - b
    return jnp.where(delta != delta, a + b, amax + jnp.log(1.0 + jnp.exp(-jnp.abs(delta))))


def _rms(x, g):
    return x * lax.rsqrt(jnp.mean(x * x, axis=-1, keepdims=True) + NORM_EPS) * g


def _head_norm(o, gain, heads, width):
    parts = []
    for h in range(heads):
        oh = o[:, h * width:(h + 1) * width]
        parts.append(oh * lax.rsqrt(jnp.mean(oh * oh, axis=-1, keepdims=True) + NORM_EPS))
    return jnp.concatenate(parts, axis=-1) * gain


def _to_col(row):
    n = row.shape[1]
    eye = lax.broadcasted_iota(jnp.int32, (n, n), 0) == lax.broadcasted_iota(jnp.int32, (n, n), 1)
    return jnp.sum(jnp.where(eye, row, 0.0), axis=1, keepdims=True)


def _to_row(col):
    n = col.shape[0]
    eye = lax.broadcasted_iota(jnp.int32, (n, n), 0) == lax.broadcasted_iota(jnp.int32, (n, n), 1)
    return jnp.sum(jnp.where(eye, col, 0.0), axis=0, keepdims=True)


def _params(sem, vmem=None):
    return pltpu.CompilerParams(dimension_semantics=sem, vmem_limit_bytes=vmem)


def _ada_kernel(c_ref, w_ref, b_ref, o_ref):
    s = _silu(c_ref[...]).astype(BF16)
    o_ref[...] = _dot(s, w_ref[...].astype(BF16)) + b_ref[...]


def _ada(c_all, w_ada, b_ada):
    n, d = c_all.shape
    cols = w_ada.shape[-1]
    tn = 1536
    return pl.pallas_call(
        _ada_kernel,
        grid=(DEPTH, cols // tn),
        in_specs=[pl.BlockSpec((n, d), lambda l, j: (0, 0)),
                  pl.BlockSpec((None, d, tn), lambda l, j: (l, 0, j)),
                  pl.BlockSpec((None, 1, tn), lambda l, j: (l, 0, j))],
        out_specs=pl.BlockSpec((None, n, tn), lambda l, j: (l, 0, j)),
        out_shape=jax.ShapeDtypeStruct((DEPTH, n, cols), F32),
        compiler_params=_params(("parallel", "parallel"), V7X_VMEM_LIMIT),
    )(c_all, w_ada, b_ada.reshape(DEPTH, 1, cols))


def _mod_spec(ada, tm, col):
    rm = ada.shape[1]
    if rm == 1:
        return pl.BlockSpec((None, 1, D_MODEL), lambda g, i: (g, 0, col))
    return pl.BlockSpec((None, tm, D_MODEL), lambda g, i: (g, i, col))


def _ffn_kernel(x_ref, sh_ref, sc_ref, gt_ref, gpre_ref, gpost_ref, win_ref, wd_ref, o_ref, h_ref, a_ref):
    x = x_ref[...]
    h_ref[...] = (_rms(x, gpre_ref[...]) * (1.0 + sc_ref[...]) + sh_ref[...]).astype(BF16)
    for f in range(D_FF // FF_CHUNK):
        cols = slice(f * FF_CHUNK, (f + 1) * FF_CHUNK)
        h = h_ref[...]
        g = _dot(h, win_ref[:, cols])
        u = _dot(h, win_ref[:, D_FF + f * FF_CHUNK:D_FF + (f + 1) * FF_CHUNK])
        a_ref[:, cols] = (_silu(g) * u).astype(BF16)
    y = _dot(a_ref[...], wd_ref[...])
    o_ref[...] = x + FFN_RES * (1.0 + gt_ref[...]) * _rms(y, gpost_ref[...])


def _ffn(x, ada, j, gpre, gpost, w_in, w_down, l, s, tm):
    g, r, d = x.shape
    xspec = pl.BlockSpec((None, tm, d), lambda a, i: (a, i, 0))
    vec = pl.BlockSpec((1, d), lambda a, i: (0, 0))
    res = lambda shape: pl.BlockSpec((None, None) + shape, lambda a, i: (l, s, 0, 0),
                                     pipeline_mode=pl.Buffered(1))
    return pl.pallas_call(
        _ffn_kernel,
        grid=(g, r // tm),
        in_specs=[xspec, _mod_spec(ada, tm, 3 * j), _mod_spec(ada, tm, 3 * j + 1),
                  _mod_spec(ada, tm, 3 * j + 2), vec, vec, res((d, 2 * D_FF)), res((D_FF, d))],
        out_specs=xspec,
        out_shape=jax.ShapeDtypeStruct(x.shape, F32),
        scratch_shapes=[pltpu.VMEM((tm, d), BF16), pltpu.VMEM((tm, D_FF), BF16)],
        compiler_params=_params(("parallel", "parallel"), V7X_VMEM_LIMIT),
    )(x, ada, ada, ada, gpre.reshape(1, d), gpost.reshape(1, d), w_in, w_down)


def _mixin_kernel(x_ref, sh_ref, sc_ref, gpre_ref, w_ref, pf_ref, pb_ref, h_ref):
    h_ref[...] = (_rms(x_ref[...], gpre_ref[...]) * (1.0 + sc_ref[...]) + sh_ref[...]).astype(BF16)
    nf = PF_COLS // COL
    for c, src in enumerate(PF_SRC + PB_SRC):
        y = _dot(h_ref[...], w_ref[:, src * COL:(src + 1) * COL])
        if c < nf:
            pf_ref[:, c * COL:(c + 1) * COL] = y
        else:
            pb_ref[:, (c - nf) * COL:(c - nf + 1) * COL] = y.astype(BF16)


def _mixin(x, ada, gpre, w_in, l, tm):
    g, r, d = x.shape
    return pl.pallas_call(
        _mixin_kernel,
        grid=(g, r // tm),
        in_specs=[pl.BlockSpec((None, tm, d), lambda a, i: (a, i, 0)),
                  _mod_spec(ada, tm, 3), _mod_spec(ada, tm, 4),
                  pl.BlockSpec((1, d), lambda a, i: (0, 0)),
                  pl.BlockSpec((None, d, IN_COLS), lambda a, i: (l, 0, 0), pipeline_mode=pl.Buffered(1))],
        out_specs=[pl.BlockSpec((None, tm, PF_COLS), lambda a, i: (a, i, 0)),
                   pl.BlockSpec((None, tm, PB_COLS), lambda a, i: (a, i, 0))],
        out_shape=[jax.ShapeDtypeStruct((g, r, PF_COLS), F32), jax.ShapeDtypeStruct((g, r, PB_COLS), BF16)],
        scratch_shapes=[pltpu.VMEM((tm, d), BF16)],
        compiler_params=_params(("parallel", "parallel"), V7X_VMEM_LIMIT),
    )(x, ada, ada, gpre.reshape(1, d), w_in)


def _merge_kernel(x_ref, gt_ref, gpost_ref, oa_ref, hg_ref, ob_ref, oc_ref, mo_ref,
                  g0, g1, g2, g3, g4, g5, hgain_ref, mgain_ref,
                  whg_ref, wsb_ref, wml_ref, wout_ref, o_ref):
    f32 = lambda ref: ref[...].astype(F32)
    o_a = (_head_norm(oa_ref[...], hgain_ref[...], HG_HEADS, HG_DK) * _silu(f32(hg_ref))).astype(BF16)
    o_b = ob_ref[...]
    o_c = (_head_norm(f32(oc_ref), mgain_ref[...], ML_HEADS, ML_HEAD_DIM) * _sigmoid(f32(mo_ref))).astype(BF16)
    gates = ((g0, g2, g4), (g1, g3, g5))
    y = None
    for c in range(2):
        cs = slice(c * COL, (c + 1) * COL)
        ga, gb, gc = gates[c]
        m = (_sigmoid(f32(ga)) * _dot(o_a, whg_ref[:, cs])
             + _sigmoid(f32(gb)) * _dot(o_b, wsb_ref[:, cs])
             + _sigmoid(f32(gc)) * _dot(o_c, wml_ref[:, cs]))
        part = _dot(m.astype(BF16), wout_ref[cs, :])
        y = part if y is None else y + part
    o_ref[...] = x_ref[...] + (1.0 + gt_ref[...]) * _rms(y, gpost_ref[...])


def _merge(x, ada, gpost, pb, o_a, o_b, o_c, hgain, mgain, whg, wsb, wml, wout, tm):
    g, r, d = x.shape
    row = lambda col: pl.BlockSpec((None, tm, COL), lambda a, i: (a, i, col))
    vec = lambda n: pl.BlockSpec((1, n), lambda a, i: (0, 0))
    wsp = lambda shape: pl.BlockSpec(shape, lambda a, i: (0, 0))
    xspec = pl.BlockSpec((None, tm, d), lambda a, i: (a, i, 0))
    return pl.pallas_call(
        _merge_kernel,
        grid=(g, r // tm),
        in_specs=[xspec, _mod_spec(ada, tm, 5), vec(d),
                  row(0), row(PB_HG), row(0), row(0), row(PB_MO)]
                 + [row(PB_GATES + c) for c in range(6)]
                 + [vec(HG_WIDTH), vec(ML_WIDTH),
                    wsp((HG_WIDTH, d)), wsp((SB_WIDTH, d)), wsp((ML_WIDTH, d)), wsp((d, d))],
        out_specs=xspec,
        out_shape=jax.ShapeDtypeStruct(x.shape, F32),
        compiler_params=_params(("parallel", "parallel"), V7X_VMEM_LIMIT),
    )(x, ada, gpost.reshape(1, d), o_a, pb, o_b, o_c, pb,
      pb, pb, pb, pb, pb, pb,
      hgain.reshape(1, HG_WIDTH), mgain.reshape(1, ML_WIDTH), whg, wsb, wml, wout)


def _hgrn_lb(lbl, layer):
    e = jnp.exp(lbl - jnp.max(lbl, axis=0, keepdims=True))
    p = e / jnp.sum(e, axis=0, keepdims=True)
    lb = jnp.zeros_like(p[0:1])
    for r in range(1, layer + 1):
        lb = lb + p[r:r + 1]
    return lb


def _hgrn_gates(z, lb):
    logf = _logaddexp(jnp.log(lb), jnp.log1p(-lb) + _log_sigmoid(z))
    k = (1.0 - lb) * _sigmoid(-z)
    return logf, k


def _hgrn_prompt_kernel(lbl_ref, qb_ref, z_ref, ib_ref, o_ref, s_ref, st_ref, b_ref, k_ref, q_ref, i_ref,
                        *, layer):
    tc = qb_ref.shape[0]
    q_ref[...] = qb_ref[...].astype(F32)
    i_ref[...] = ib_ref[...].astype(F32)
    nchunk = tc // HG_CHUNK
    nsub = HG_CHUNK // HG_SUB

    @pl.when(pl.program_id(1) == 0)
    def _():
        st_ref[...] = jnp.zeros_like(st_ref)

    lb = _hgrn_lb(lbl_ref[...], layer)[0]
    logf, k = _hgrn_gates(z_ref[...], lb)
    k_ref[...] = k
    ri = lax.broadcasted_iota(jnp.int32, (tc, tc), 0)
    ci = lax.broadcasted_iota(jnp.int32, (tc, tc), 1)
    tri = jnp.where((ci <= ri) & (ri // HG_CHUNK == ci // HG_CHUNK), 1.0, 0.0).astype(BF16)
    b_ref[...] = _tri_dot(tri, logf)
    block_decay = -jnp.sum(logf.reshape(tc // HG_SUB, HG_SUB, HG_WIDTH), axis=1)
    safe = jnp.max(block_decay) < HG_SAFE_DECAY

    rows64 = lax.broadcasted_iota(jnp.int32, (HG_CHUNK, 1), 0)
    rows16 = lax.broadcasted_iota(jnp.int32, (HG_SUB, 1), 0)
    r64 = lax.broadcasted_iota(jnp.int32, (HG_CHUNK, HG_CHUNK), 0)
    c64 = lax.broadcasted_iota(jnp.int32, (HG_CHUNK, HG_CHUNK), 1)
    causal64 = c64 <= r64

    heads = range(HG_HEADS)
    sl = [slice(h * HG_DK, (h + 1) * HG_DK) for h in heads]

    def score_operands(q, b, kk, cs, s_i, include_diag):
        i0 = s_i * HG_SUB
        r = b[i0 - 1:i0, cs] if s_i else jnp.zeros((1, HG_DK), F32)
        hi = i0 + HG_SUB if include_diag else i0
        cap = HG_SAFE_DECAY if include_diag else 0.0
        qt = q[i0:i0 + HG_SUB, cs] * jnp.exp(b[i0:i0 + HG_SUB, cs] - r)
        kt = jnp.where(rows64 < hi, kk[:, cs] * jnp.exp(jnp.minimum(r - b[:, cs], cap)), 0.0)
        return qt.astype(BF16), kt.astype(BF16)

    def chunk(r0):
        rows = pl.ds(r0, HG_CHUNK)
        q = q_ref[rows, :]
        i = i_ref[rows, :]
        b = b_ref[rows, :]
        kk = k_ref[rows, :]
        blast = b[HG_CHUNK - 1:HG_CHUNK, :]
        eb = jnp.exp(b)
        kd = kk * jnp.exp(blast - b)
        ib = [i[:, cs].astype(BF16) for cs in sl]
        ops = [[score_operands(q, b, kk, cs, s_i, True) for s_i in range(nsub)] for cs in sl]
        p = [jnp.concatenate([_dot_nt(qt, kt) for qt, kt in ops[h]], axis=0) for h in heads]
        st = [st_ref[h] for h in heads]
        o_state = [_dot_nt((q[:, sl[h]] * eb[:, sl[h]]).astype(BF16), st[h].astype(BF16)) for h in heads]
        pm = [jnp.where(causal64 & safe, p[h], 0.0).astype(BF16) for h in heads]
        o_intra = [_dot(pm[h], ib[h]) for h in heads]
        upd = [_dot(i[:, sl[h]].T.astype(BF16), kd[:, sl[h]].astype(BF16)) for h in heads]
        o_ref[rows, :] = jnp.concatenate([o_state[h] + o_intra[h] for h in heads], axis=-1)
        for h in heads:
            st_ref[h] = st[h] * jnp.exp(blast[:, sl[h]]) + upd[h]

        @pl.when(jnp.logical_not(safe))
        def _():
            outs = []
            for h in heads:
                rows_p = [jnp.zeros((HG_SUB, HG_CHUNK), F32)]
                for s_i in range(1, nsub):
                    qt, kt = score_operands(q, b, kk, sl[h], s_i, False)
                    rows_p.append(_dot_nt(qt, kt))
                outs.append(_dot(jnp.concatenate(rows_p, axis=0).astype(BF16), ib[h]))
            o_ref[rows, :] += jnp.concatenate(outs, axis=-1)
            for s_i in range(nsub):
                base = r0 + s_i * HG_SUB
                srows = pl.ds(base, HG_SUB)
                q_i = q_ref[srows, :]
                b_i = b_ref[srows, :]

                def s_body(s, acc):
                    b_s = b_ref[pl.ds(base + s, 1), :]
                    k_s = k_ref[pl.ds(base + s, 1), :]
                    i_s = i_ref[pl.ds(base + s, 1), :]
                    e = jnp.where(rows16 >= s, jnp.exp(jnp.minimum(b_i - b_s, 0.0)), 0.0)
                    pr = q_i * e * k_s
                    parts = [jnp.sum(pr[:, cs], axis=1, keepdims=True) * i_s[:, cs] for cs in sl]
                    return acc + jnp.concatenate(parts, axis=-1)

                acc = lax.fori_loop(0, HG_SUB, s_body, jnp.zeros((HG_SUB, HG_WIDTH), F32))
                o_ref[srows, :] += acc

    def pair_body(n, carry):
        r0 = pl.multiple_of(n * (2 * HG_CHUNK), 2 * HG_CHUNK)
        chunk(r0)
        chunk(r0 + HG_CHUNK)
        return carry

    lax.fori_loop(0, nchunk // 2, pair_body, 0)

    @pl.when(pl.program_id(1) == pl.num_programs(1) - 1)
    def _():
        for h in range(HG_HEADS):
            s_ref[h] = st_ref[h].T


def _hgrn_prompt(pf, pb, lbl, layer, tc):
    b, t, _ = pf.shape
    row = lambda col: pl.BlockSpec((None, tc, COL), lambda a, i: (a, i, col))
    return pl.pallas_call(
        functools.partial(_hgrn_prompt_kernel, layer=layer),
        grid=(b, t // tc),
        in_specs=[pl.BlockSpec((DEPTH, 1, HG_WIDTH), lambda a, i: (0, 0, 0)), row(PB_Q), row(PF_Z), row(PB_I)],
        out_specs=[pl.BlockSpec((None, tc, HG_WIDTH), lambda a, i: (a, i, 0)),
                   pl.BlockSpec((None, HG_HEADS, HG_DK, HG_DK), lambda a, i: (a, 0, 0, 0))],
        out_shape=[jax.ShapeDtypeStruct((b, t, HG_WIDTH), F32),
                   jax.ShapeDtypeStruct((b, HG_HEADS, HG_DK, HG_DK), F32)],
        scratch_shapes=[pltpu.VMEM((HG_HEADS, HG_DK, HG_DK), F32)] + [pltpu.VMEM((tc, HG_WIDTH), F32)] * 4,
        compiler_params=_params(("parallel", "arbitrary"), V7X_VMEM_LIMIT),
    )(lbl.reshape(DEPTH, 1, HG_WIDTH), pb, pf, pb)


def _hgrn_step_kernel(lbl_ref, qr_ref, zr_ref, ir_ref, s_ref, o_ref, so_ref, *, layer):
    lb = _hgrn_lb(lbl_ref[...], layer)[0]
    for n in range(qr_ref.shape[0]):
        outs = []
        for h in range(HG_HEADS):
            cs = slice(h * HG_DK, (h + 1) * HG_DK)
            logf, k = _hgrn_gates(_to_col(zr_ref[n, :, cs]), lb[h])
            s_new = jnp.exp(logf) * s_ref[n, h] + k * ir_ref[n, :, cs]
            so_ref[n, h] = s_new
            outs.append(jnp.sum(_to_col(qr_ref[n, :, cs]) * s_new, axis=0, keepdims=True))
        o_ref[n] = jnp.concatenate(outs, axis=-1)


STEP_SEQS = 4


def _hgrn_step(q, z, i, state, lbl, layer):
    b = q.shape[0]
    ns = STEP_SEQS if b % STEP_SEQS == 0 else 1
    row = pl.BlockSpec((ns, 1, COL), lambda a: (a, 0, 0))
    return pl.pallas_call(
        functools.partial(_hgrn_step_kernel, layer=layer),
        grid=(b // ns,),
        in_specs=[pl.BlockSpec((DEPTH, HG_HEADS, HG_DK, 1), lambda a: (0, 0, 0, 0)), row, row, row,
                  pl.BlockSpec((ns, None, HG_HEADS, HG_DK, HG_DK), lambda a: (a, layer, 0, 0, 0))],
        out_specs=[pl.BlockSpec((ns, 1, HG_WIDTH), lambda a: (a, 0, 0)),
                   pl.BlockSpec((ns, HG_HEADS, HG_DK, HG_DK), lambda a: (a, 0, 0, 0))],
        out_shape=[jax.ShapeDtypeStruct((b, 1, HG_WIDTH), F32),
                   jax.ShapeDtypeStruct((b, HG_HEADS, HG_DK, HG_DK), F32)],
        compiler_params=_params(("parallel",)),
    )(lbl.reshape(DEPTH, HG_HEADS, HG_DK, 1), q.reshape(b, 1, COL), z.reshape(b, 1, COL), i.reshape(b, 1, COL), state)


def _softplus(z):
    return jnp.maximum(z, 0.0) + jnp.log(1.0 + jnp.exp(-jnp.abs(z)))


def _sb_prompt_kernel(bias_ref, q_ref, k_ref, v_ref, o_ref, kt_out, vt_out,
                      qh_ref, kt_ref, vh_ref, acc_ref, c_ref):
    tq, tk = SB_TQ, SB_TK
    nk = k_ref.shape[0] // tk
    qi = pl.program_id(1)

    @pl.when(qi == 0)
    def _():
        for n in range(nk):
            ks = slice(n * tk, (n + 1) * tk)
            kt_full = k_ref[ks, :].T
            kt_out[:, ks] = kt_full
            vt_out[:, ks] = v_ref[ks, :].T
            for h in range(SB_HEADS):
                cs = slice(h * SB_HEAD_DIM, (h + 1) * SB_HEAD_DIM)
                kt_ref[h, n] = kt_full[cs, :].astype(BF16)
                vh_ref[h, n] = v_ref[ks, cs].astype(BF16)

    nrb = tq // SB_ROWS
    for h in range(SB_HEADS):
        cs = slice(h * SB_HEAD_DIM, (h + 1) * SB_HEAD_DIM)
        for r in range(nrb):
            rows = slice(r * SB_ROWS, (r + 1) * SB_ROWS)
            qh_ref[h, r] = (q_ref[:, cs].astype(F32)[rows] * (SB_HEAD_DIM ** -0.5 * LOG2E)).astype(BF16)
    acc_ref[...] = jnp.zeros_like(acc_ref)
    c_ref[...] = jnp.zeros_like(c_ref)

    ri = lax.broadcasted_iota(jnp.int32, (tk, tk), 0)
    ci = lax.broadcasted_iota(jnp.int32, (tk, tk), 1)
    upper = jnp.where(ri > ci, 1.0, 0.0).astype(BF16)
    rpt = tk // SB_ROWS
    base = qi * (tq // tk)

    def tile(j, r, diag):
        nkeys = (r % rpt + 1) * SB_ROWS if diag else tk
        mask = None
        if diag:
            mask = (lax.broadcasted_iota(jnp.int32, (SB_ROWS, nkeys), 1)
                    < lax.broadcasted_iota(jnp.int32, (SB_ROWS, nkeys), 0) + (r % rpt) * SB_ROWS)
        for g0 in range(0, SB_HEADS, SB_HEAD_GROUP):
            heads = range(g0, g0 + SB_HEAD_GROUP)
            zs = [_dot(qh_ref[h, r], kt_ref[h, j, :, :nkeys]) + bias_ref[h] * LOG2E for h in heads]
            lbs, l1ms = [], []
            for z in zs:
                sp = jnp.maximum(z, 0.0) + jnp.log(1.0 + jnp.exp2(-jnp.abs(z))) * LOG2E
                l1m = -sp
                if mask is not None:
                    l1m = jnp.where(mask, l1m, 0.0)
                lbs.append(z - sp)
                l1ms.append(l1m)
            sufs = [_dot(l1m.astype(BF16), upper[:nkeys, :nkeys]) for l1m in l1ms]
            weights = []
            for h, lb, l1m, suf in zip(heads, lbs, l1ms, sufs):
                c = c_ref[h * nrb + r][:, 0:1]
                a = jnp.exp2(lb + suf + c)
                if mask is not None:
                    a = jnp.where(mask, a, 0.0)
                weights.append(a.astype(BF16))
                c_ref[h * nrb + r] = jnp.broadcast_to(c + suf[:, 0:1] + l1m[:, 0:1], (SB_ROWS, 128))
            for h, a in zip(heads, weights):
                acc_ref[h, r] += _dot(a, vh_ref[h, j, :nkeys, :])

    for r in range(nrb):
        tile(base + r // rpt, r, True)
        for kk in range(r // rpt - 1, -1, -1):
            tile(base + kk, r, False)

    def body(step, carry):
        for r in range(nrb):
            tile(base - 1 - step, r, False)
        return carry

    lax.fori_loop(0, base, body, 0)
    o_ref[...] = jnp.concatenate(
        [jnp.concatenate([acc_ref[h, r] for h in range(SB_HEADS)], axis=-1) for r in range(nrb)],
        axis=0).astype(BF16)


def _sb_prompt(pf, pb, bias):
    b, t, _ = pf.shape
    tq = min(SB_TQ, t)
    nk = t // SB_TK
    seq = lambda col: pl.BlockSpec((None, t, COL), lambda a, i: (a, 0, col))
    return pl.pallas_call(
        _sb_prompt_kernel,
        grid=(b, t // tq),
        in_specs=[pl.BlockSpec(memory_space=pltpu.SMEM),
                  pl.BlockSpec((None, tq, COL), lambda a, i: (a, i, PB_SQ)), seq(PF_K), seq(PF_V)],
        out_specs=[pl.BlockSpec((None, tq, SB_WIDTH), lambda a, i: (a, i, 0)),
                   pl.BlockSpec((None, SB_WIDTH, t), lambda a, i: (a, 0, 0)),
                   pl.BlockSpec((None, SB_WIDTH, t), lambda a, i: (a, 0, 0))],
        out_shape=[jax.ShapeDtypeStruct((b, t, SB_WIDTH), BF16),
                   jax.ShapeDtypeStruct((b, SB_WIDTH, t), F32),
                   jax.ShapeDtypeStruct((b, SB_WIDTH, t), F32)],
        scratch_shapes=[pltpu.VMEM((SB_HEADS, tq // SB_ROWS, SB_ROWS, SB_HEAD_DIM), BF16),
                        pltpu.VMEM((SB_HEADS, nk, SB_HEAD_DIM, SB_TK), BF16),
                        pltpu.VMEM((SB_HEADS, nk, SB_TK, SB_HEAD_DIM), BF16),
                        pltpu.VMEM((SB_HEADS, tq // SB_ROWS, SB_ROWS, SB_HEAD_DIM), F32),
                        pltpu.VMEM((SB_HEADS * (tq // SB_ROWS), SB_ROWS, 128), F32)],
        compiler_params=_params(("parallel", "arbitrary"), V7X_VMEM_LIMIT),
    )(bias, pb, pf, pf)


SB_PAGES_PER_STEP = 16


def _sb_decode_kernel(pt_ref, q_ref, bias_ref, *refs):
    npg = SB_PAGES_PER_STEP
    k_refs = refs[:npg]
    v_refs = refs[npg:2 * npg]
    o_ref, qb_ref, c_ref, acc_ref = refs[2 * npg:]
    g = pl.program_id(1)

    @pl.when(g == 0)
    def _():
        for h in range(SB_HEADS):
            qcol = _to_col(q_ref[:, h * SB_HEAD_DIM:(h + 1) * SB_HEAD_DIM])
            qb_ref[h] = jnp.broadcast_to(qcol, qb_ref.shape[1:])
        c_ref[...] = jnp.zeros_like(c_ref)
        acc_ref[...] = jnp.zeros_like(acc_ref)

    ri = lax.broadcasted_iota(jnp.int32, (PAGE_SIZE, PAGE_SIZE), 0)
    ci = lax.broadcasted_iota(jnp.int32, (PAGE_SIZE, PAGE_SIZE), 1)
    upper = jnp.where(ri > ci, 1.0, 0.0).astype(BF16)
    bias = bias_ref[...]
    c = c_ref[:, 0:1]
    weights = [None] * npg
    for p in range(npg - 1, -1, -1):
        zrows = []
        for h in range(SB_HEADS):
            part = k_refs[p][h, 0:8, :] * qb_ref[h, 0:8, :]
            for r in range(8, SB_HEAD_DIM, 8):
                part = part + k_refs[p][h, r:r + 8, :] * qb_ref[h, r:r + 8, :]
            zrows.append(jnp.sum(part, axis=0, keepdims=True))
        z = jnp.concatenate(zrows, axis=0) + bias
        sp = _softplus(z)
        l1m = -sp
        suf = _dot(l1m.astype(BF16), upper)
        weights[p] = jnp.exp(z - sp + suf + c)
        c = c + suf[:, 0:1] + l1m[:, 0:1]
    c_ref[...] = jnp.broadcast_to(c, c_ref.shape)
    for h in range(SB_HEADS):
        part = weights[0][h:h + 1, :] * v_refs[0][h]
        for p in range(1, npg):
            part = part + weights[p][h:h + 1, :] * v_refs[p][h]
        acc_ref[h] += part

    @pl.when(g == pl.num_programs(1) - 1)
    def _():
        o_ref[...] = jnp.concatenate(
            [_to_row(jnp.sum(acc_ref[h], axis=1, keepdims=True)) for h in range(SB_HEADS)], axis=-1)


def _sb_decode(sq, bias, cache_kt, cache_vt, page_table, layer):
    b = sq.shape[0]
    n_pages = page_table.shape[1]
    npg = SB_PAGES_PER_STEP
    ng = n_pages // npg
    qrow = (sq * (SB_HEAD_DIM ** -0.5)).reshape(b, 1, SB_WIDTH)

    def page_spec(p):
        return pl.BlockSpec((None, None, SB_HEADS, SB_HEAD_DIM, PAGE_SIZE),
                            lambda a, g, pt: (pt[a, (ng - 1 - g) * npg + p], layer, 0, 0, 0))

    hd = pl.BlockSpec((None, 1, SB_WIDTH), lambda a, g, pt: (a, 0, 0))
    grid_spec = pltpu.PrefetchScalarGridSpec(
        num_scalar_prefetch=1,
        grid=(b, ng),
        in_specs=[hd, pl.BlockSpec((SB_HEADS, 1), lambda a, g, pt: (0, 0))]
                 + [page_spec(p) for p in range(npg)] + [page_spec(p) for p in range(npg)],
        out_specs=hd,
        scratch_shapes=[pltpu.VMEM((SB_HEADS, SB_HEAD_DIM, PAGE_SIZE), F32),
                        pltpu.VMEM((SB_HEADS, 128), F32),
                        pltpu.VMEM((SB_HEADS, SB_HEAD_DIM, PAGE_SIZE), F32)],
    )
    return pl.pallas_call(
        _sb_decode_kernel,
        grid_spec=grid_spec,
        out_shape=jax.ShapeDtypeStruct((b, 1, SB_WIDTH), F32),
        compiler_params=_params(("parallel", "arbitrary"), V7X_VMEM_LIMIT),
    )(page_table, qrow, bias.reshape(SB_HEADS, 1), *([cache_kt] * npg), *([cache_vt] * npg))


def _ml_pre_kernel(x0, x1, x2, x3, *refs):
    _ml_pre_body(x0[...], x1[...], x2[...], x3[...], *refs)


def _ml_pre_seq_kernel(prev_ref, x_ref, cw_ref, cb_ref, wq_ref, wk_ref, wv_ref, wif_ref, bif_ref,
                       q_ref, k_ref, v_ref, g_ref, xs_ref):
    tm = x_ref.shape[0]
    halo = prev_ref.shape[0]
    x = x_ref[...]
    xs_ref[halo:, :] = x
    xs_ref[:halo, :] = jnp.where(pl.program_id(1) == 0, 0.0, prev_ref[...])
    taps = [xs_ref[pl.ds(halo - (ML_CONV - 1) + j, tm), :] for j in range(ML_CONV - 1)]
    _ml_pre_body(*taps, x, cw_ref, cb_ref, wq_ref, wk_ref, wv_ref, wif_ref, bif_ref,
                 q_ref, k_ref, v_ref, g_ref)


def _ml_pre_body(x0, x1, x2, x, cw_ref, cb_ref, wq_ref, wk_ref, wv_ref, wif_ref, bif_ref,
                 q_ref, k_ref, v_ref, g_ref):
    cw = cw_ref[...]
    xc = _silu(x0 * cw[0:1] + x1 * cw[1:2] + x2 * cw[2:3] + x * cw[3:4] + cb_ref[...])
    xcb = xc.astype(BF16)
    xb = x.astype(BF16)
    qs, ks, vs = [], [], []
    for h in range(ML_HEADS):
        cs = slice(h * ML_HEAD_DIM, (h + 1) * ML_HEAD_DIM)
        qs.append(_dot(xcb[:, cs], wq_ref[h]))
        ks.append(_dot(xcb[:, cs], wk_ref[h]))
        vs.append(_dot(xb[:, cs], wv_ref[h]))
    q = jnp.concatenate(qs, axis=-1)
    k = jnp.concatenate(ks, axis=-1)
    v = jnp.concatenate(vs, axis=-1)
    q_ref[...] = q
    k_ref[...] = k
    v_ref[...] = v
    g_ref[...] = (_dot(q.astype(BF16), wif_ref[0:ML_WIDTH, :])
                  + _dot(k.astype(BF16), wif_ref[ML_WIDTH:2 * ML_WIDTH, :])
                  + _dot(v.astype(BF16), wif_ref[2 * ML_WIDTH:3 * ML_WIDTH, :]) + bif_ref[...])


def _ml_pre(xs, cw, cb, wq, wk, wv, wif, bif, tm):
    g, r, w = xs[0].shape
    row = pl.BlockSpec((None, tm, w), lambda a, i: (a, i, 0))
    full = lambda shape: pl.BlockSpec(shape, lambda a, i: (0,) * len(shape))
    out = jax.ShapeDtypeStruct((g, r, w), F32)
    return pl.pallas_call(
        _ml_pre_kernel,
        grid=(g, r // tm),
        in_specs=[row, row, row, row, full((ML_CONV, w)), full((1, w)),
                  full(wq.shape), full(wk.shape), full(wv.shape), full(wif.shape), full((1, 128))],
        out_specs=[row, row, row, pl.BlockSpec((None, tm, 128), lambda a, i: (a, i, 0))],
        out_shape=[out, out, out, jax.ShapeDtypeStruct((g, r, 128), F32)],
        compiler_params=_params(("parallel", "parallel"), V7X_VMEM_LIMIT),
    )(*xs, cw, cb.reshape(1, w), wq, wk, wv, wif, bif)


ML_HALO = 8


def _ml_pre_seq(pf, cw, cb, wq, wk, wv, wif, bif, tm):
    b, t, _ = pf.shape
    w = ML_WIDTH
    per = tm // ML_HALO
    row = pl.BlockSpec((None, tm, w), lambda a, i: (a, i, 0))
    full = lambda shape: pl.BlockSpec(shape, lambda a, i: (0,) * len(shape))
    out = jax.ShapeDtypeStruct((b, t, w), F32)
    return pl.pallas_call(
        _ml_pre_seq_kernel,
        grid=(b, t // tm),
        in_specs=[pl.BlockSpec((None, ML_HALO, w), lambda a, i: (a, jnp.maximum(i * per - 1, 0), PF_MX)),
                  pl.BlockSpec((None, tm, w), lambda a, i: (a, i, PF_MX)),
                  full((ML_CONV, w)), full((1, w)),
                  full(wq.shape), full(wk.shape), full(wv.shape), full(wif.shape), full((1, 128))],
        out_specs=[row, row, row, pl.BlockSpec((None, tm, 128), lambda a, i: (a, i, 0))],
        out_shape=[out, out, out, jax.ShapeDtypeStruct((b, t, 128), F32)],
        scratch_shapes=[pltpu.VMEM((tm + ML_HALO, w), F32)],
        compiler_params=_params(("parallel", "parallel"), V7X_VMEM_LIMIT),
    )(pf, pf, cw, cb.reshape(1, w), wq, wk, wv, wif, bif)


def _ml_prompt_kernel(q_ref, k_ref, v_ref, g_ref, h_ref, c_out, n_out, m_out, c_ref, n_ref, m_ref):
    L = ML_CHUNK
    nch = q_ref.shape[0] // L

    @pl.when(pl.program_id(1) == 0)
    def _():
        c_ref[...] = jnp.zeros_like(c_ref)
        n_ref[...] = jnp.zeros_like(n_ref)
        m_ref[...] = jnp.zeros_like(m_ref)

    ri = lax.broadcasted_iota(jnp.int32, (L, L), 0)
    ci = lax.broadcasted_iota(jnp.int32, (L, L), 1)
    causal = ci <= ri
    tril = jnp.where(causal, 1.0, 0.0).astype(BF16)
    triu = jnp.where(ri <= ci, 1.0, 0.0).astype(BF16)
    heads = range(ML_HEADS)
    sl = [slice(h * ML_HEAD_DIM, (h + 1) * ML_HEAD_DIM) for h in heads]

    ones = jnp.ones((L, ML_HEAD_DIM), BF16)
    pre = []
    for j in range(nch):
        rows = slice(j * L, (j + 1) * L)
        g = g_ref[rows, :]
        gt = g.T
        b_cols = _tri_dot(tril, _log_sigmoid(g))
        b_rows = _dot_tri(_log_sigmoid(gt), triu)
        q = [q_ref[rows, cs] for cs in sl]
        ks = [k_ref[rows, cs] * (ML_HEAD_DIM ** -0.5) for cs in sl]
        qb = [x.astype(BF16) for x in q]
        ksb = [x.astype(BF16) for x in ks]
        vb = [v_ref[rows, cs].astype(BF16) for cs in sl]
        vt = [v_ref[rows, cs].T for cs in sl]
        qk = [_dot_nt(qb[h], ksb[h]) for h in heads]
        b_bc = [jnp.broadcast_to(b_cols[:, ML_HEADS + h:ML_HEADS + h + 1], (L, L)) for h in heads]
        u_row = [gt[h:h + 1, :] - b_rows[ML_HEADS + h:ML_HEADS + h + 1, :] for h in heads]
        u_max = [jnp.broadcast_to(jnp.max(jnp.where(causal, u_row[h], NEG_INF), axis=1, keepdims=True), (L, L))
                 for h in heads]
        pre.append(dict(qb=qb, ksb=ksb, vb=vb, vt=vt, qk=qk, b_bc=b_bc, u_row=u_row, u_max=u_max))

    c0 = [c_ref[h] for h in heads]
    n0 = [n_ref[h:h + 1, :] for h in heads]
    m0 = [m_ref[h:h + 1, 0:1] for h in heads]
    for j, pj in enumerate(pre):
        rows = slice(j * L, (j + 1) * L)
        qc = [_dot_nt(pj["qb"][h], c0[h].astype(BF16)) for h in heads]
        qn = [_dot_nt(pj["qb"][h], jnp.broadcast_to(n0[h], (L, ML_HEAD_DIM)).astype(BF16)) for h in heads]
        y, inter, w = [], [], []
        for h in heads:
            y_h = jnp.maximum(pj["u_max"][h], m0[h])
            y.append(y_h)
            inter.append(jnp.exp(m0[h] - y_h))
            w.append(pj["qk"][h] * jnp.where(causal, jnp.exp(pj["u_row"][h] - y_h), 0.0))
        wb = [w[h].astype(BF16) for h in heads]
        wv = [_dot(wb[h], pj["vb"][h]) for h in heads]
        wsum = [_dot(wb[h], ones) for h in heads]
        outs, dec, w_end, m_new = [], [], [], []
        for h in heads:
            m_t = pj["b_bc"][h] + y[h]
            numer = inter[h] * qc[h] + wv[h]
            denom = inter[h] * qn[h] + wsum[h]
            outs.append(numer / jnp.maximum(jnp.abs(denom), jnp.exp(-m_t)))
            m_h = m_t[L - 1:L, 0:1]
            b_last = pj["b_bc"][h][L - 1:L, 0:1]
            w_end.append(jnp.exp(pj["u_row"][h] + b_last - m_h))
            dec.append(jnp.exp(b_last + m0[h] - m_h))
            m_new.append(m_h)
        c_upd = [_dot((pj["vt"][h] * w_end[h]).astype(BF16), pj["ksb"][h]) for h in heads]
        n_upd = [_dot(w_end[h].astype(BF16), pj["ksb"][h]) for h in heads]
        c0 = [dec[h] * c0[h] + c_upd[h] for h in heads]
        n0 = [dec[h] * n0[h] + n_upd[h] for h in heads]
        m0 = m_new
        h_ref[rows, :] = jnp.concatenate(outs, axis=-1).astype(BF16)
    for h in heads:
        c_ref[h] = c0[h]
        n_ref[h:h + 1, :] = n0[h]
        m_ref[h:h + 1, :] = jnp.broadcast_to(m0[h], (1, m_ref.shape[1]))

    @pl.when(pl.program_id(1) == pl.num_programs(1) - 1)
    def _():
        c_out[...] = c_ref[...]
        n_out[...] = n_ref[...]
        m_out[...] = m_ref[...]


ML_CHUNKS_PER_STEP = 4


def _ml_prompt(q, k, v, gates):
    b, t, w = q.shape
    tr = min(t, ML_CHUNKS_PER_STEP * ML_CHUNK)
    row = pl.BlockSpec((None, tr, w), lambda a, i: (a, i, 0))
    return pl.pallas_call(
        _ml_prompt_kernel,
        grid=(b, t // tr),
        in_specs=[row, row, row, pl.BlockSpec((None, tr, 128), lambda a, i: (a, i, 0))],
        out_specs=[row,
                   pl.BlockSpec((None, ML_HEADS, ML_HEAD_DIM, ML_HEAD_DIM), lambda a, i: (a, 0, 0, 0)),
                   pl.BlockSpec((None, ML_HEADS, ML_HEAD_DIM), lambda a, i: (a, 0, 0)),
                   pl.BlockSpec((None, ML_HEADS, 128), lambda a, i: (a, 0, 0))],
        out_shape=[jax.ShapeDtypeStruct((b, t, w), BF16),
                   jax.ShapeDtypeStruct((b, ML_HEADS, ML_HEAD_DIM, ML_HEAD_DIM), F32),
                   jax.ShapeDtypeStruct((b, ML_HEADS, ML_HEAD_DIM), F32),
                   jax.ShapeDtypeStruct((b, ML_HEADS, 128), F32)],
        scratch_shapes=[pltpu.VMEM((ML_HEADS, ML_HEAD_DIM, ML_HEAD_DIM), F32),
                        pltpu.VMEM((ML_HEADS, ML_HEAD_DIM), F32), pltpu.VMEM((ML_HEADS, 128), F32)],
        compiler_params=_params(("parallel", "arbitrary"), V7X_VMEM_LIMIT),
    )(q, k, v, gates)


def _ml_step_kernel(q_ref, k_ref, v_ref, g_ref, c_ref, n_ref, m_ref, h_ref, c_out, n_out, m_out):
    lane = lax.broadcasted_iota(jnp.int32, (1, 128), 1)
    for s in range(q_ref.shape[0]):
        g = g_ref[s]
        m_row = jnp.zeros((1, 128), F32)
        outs = []
        for h in range(ML_HEADS):
            cs = slice(h * ML_HEAD_DIM, (h + 1) * ML_HEAD_DIM)
            q = q_ref[s, :, cs]
            ks = k_ref[s, :, cs] * (ML_HEAD_DIM ** -0.5)
            v = _to_col(v_ref[s, :, cs])
            it = g[:, h:h + 1]
            logf = _log_sigmoid(g[:, ML_HEADS + h:ML_HEADS + h + 1])
            m0 = m_ref[s, :, h:h + 1]
            c0 = c_ref[s, h]
            n0 = n_ref[s, h:h + 1, :]
            m_t = jnp.maximum(logf + m0, it)
            dm = jnp.exp(it - m_t)
            inter = jnp.exp(logf + m0 - m_t)
            w = jnp.sum(q * ks, axis=1, keepdims=True) * dm
            numer = inter * jnp.sum(c0 * q, axis=1, keepdims=True) + w * v
            denom = inter * jnp.sum(q * n0, axis=1, keepdims=True) + w
            outs.append(_to_row(numer / jnp.maximum(jnp.abs(denom), jnp.exp(-m_t))))
            w_end = jnp.exp(it - m_t)
            dec = jnp.exp(logf + m0 - m_t)
            c_out[s, h] = dec * c0 + (w_end * v) * ks
            n_out[s, h:h + 1, :] = dec * n0 + w_end * ks
            m_row = jnp.where(lane == h, m_t, m_row)
        h_ref[s] = jnp.concatenate(outs, axis=-1)
        m_out[s] = m_row


def _ml_step(q, k, v, gates, state_c, state_n, state_m, layer):
    b = q.shape[0]
    m0 = state_m[:, layer].reshape(b, 1, ML_HEADS)
    ns = STEP_SEQS if b % STEP_SEQS == 0 else 1
    row = pl.BlockSpec((ns, 1, ML_WIDTH), lambda a: (a, 0, 0))
    cspec = pl.BlockSpec((ns, ML_HEADS, ML_HEAD_DIM, ML_HEAD_DIM), lambda a: (a, 0, 0, 0))
    nspec = pl.BlockSpec((ns, ML_HEADS, ML_HEAD_DIM), lambda a: (a, 0, 0))
    return pl.pallas_call(
        _ml_step_kernel,
        grid=(b // ns,),
        in_specs=[row, row, row, pl.BlockSpec((ns, 1, 128), lambda a: (a, 0, 0)),
                  pl.BlockSpec((ns, None, ML_HEADS, ML_HEAD_DIM, ML_HEAD_DIM),
                               lambda a: (a, layer, 0, 0, 0)),
                  pl.BlockSpec((ns, None, ML_HEADS, ML_HEAD_DIM), lambda a: (a, layer, 0, 0)),
                  pl.BlockSpec((ns, 1, ML_HEADS), lambda a: (a, 0, 0))],
        out_specs=[row, cspec, nspec, pl.BlockSpec((ns, 1, 128), lambda a: (a, 0, 0))],
        out_shape=[jax.ShapeDtypeStruct((b, 1, ML_WIDTH), F32),
                   jax.ShapeDtypeStruct((b, ML_HEADS, ML_HEAD_DIM, ML_HEAD_DIM), F32),
                   jax.ShapeDtypeStruct((b, ML_HEADS, ML_HEAD_DIM), F32),
                   jax.ShapeDtypeStruct((b, 1, 128), F32)],
        compiler_params=_params(("parallel",)),
    )(q, k, v, gates, state_c, state_n, m0)


def _prep_weights(p):
    bf = lambda a: a.astype(BF16)
    w = {}
    for name in ("w_ffn_in", "w_ffn_down", "w_in", "w_hg_proj", "w_sb_proj", "w_ml_proj", "w_out", "ml_wq", "ml_wk", "ml_wv"):
        w[name] = bf(p[name])
    w["wif"] = bf(jnp.pad(p["ml_w_if"], ((0, 0), (0, 0), (0, 128 - 2 * ML_HEADS))))
    w["bif"] = jnp.pad(p["ml_b_if"], ((0, 0), (0, 128 - 2 * ML_HEADS))).reshape(DEPTH, 1, 128)
    return w


def _layer(x, ada, l, p, w, tm, past):
    ffn = lambda xx, j, s: _ffn(xx, ada, j, p["g_pre"][l, j], p["g_post"][l, j],
                                w["w_ffn_in"], w["w_ffn_down"], l, s, tm["ffn"])
    x = ffn(x, 0, 0)
    pf, pb = _mixin(x, ada, p["g_pre"][l, 1], w["w_in"], l, tm["mixin"])
    st = {}
    blk = lambda arr, c: arr[..., c * COL:(c + 1) * COL]
    if past is None:
        b, t, _ = x.shape
        o_a, st["hgrn"] = _hgrn_prompt(pf, pb, p["hg_lb_logits"], l, tm["hgrn"])
        o_b, kt, vt = _sb_prompt(pf, pb, p["sb_bias"][l])
        q, k, v, gates = _ml_pre_seq(pf, p["ml_conv_w"][l], p["ml_conv_b"][l], w["ml_wq"][l],
                                     w["ml_wk"][l], w["ml_wv"][l], w["wif"][l], w["bif"][l], tm["mlpre"])
        o_c, st["mc"], st["mn"], m_pad = _ml_prompt(q, k, v, gates)
        st["mm"] = m_pad[:, :, 0]
        st["mconv"] = blk(pf, PF_MX)[:, t - (ML_CONV - 1):]
        st["k"] = kt.reshape(b, SB_HEADS, SB_HEAD_DIM, t)
        st["v"] = vt.reshape(b, SB_HEADS, SB_HEAD_DIM, t)
    else:
        b = x.shape[1]
        rows = lambda arr, c: blk(arr, c).reshape(b, COL).astype(F32)
        mx = blk(pf, PF_MX)
        o_a, st["hgrn"] = _hgrn_step(rows(pb, PB_Q), rows(pf, PF_Z), rows(pb, PB_I), past["hgrn"],
                                     p["hg_lb_logits"], l)
        o_b = _sb_decode(rows(pb, PB_SQ), p["sb_bias"][l], past["k"], past["v"], past["page_table"], l)
        buf = past["mconv"][:, l]
        taps = [buf[:, j].reshape(1, b, ML_WIDTH) for j in range(ML_CONV - 1)]
        q, k, v, gates = _ml_pre(taps + [mx], p["ml_conv_w"][l], p["ml_conv_b"][l], w["ml_wq"][l],
                                 w["ml_wk"][l], w["ml_wv"][l], w["wif"][l], w["bif"][l], b)
        hc, st["mc"], st["mn"], m_pad = _ml_step(q.reshape(b, 1, ML_WIDTH), k.reshape(b, 1, ML_WIDTH),
                                                 v.reshape(b, 1, ML_WIDTH), gates.reshape(b, 1, 128),
                                                 past["mc"], past["mn"], past["mm"], l)
        o_a = o_a.reshape(1, b, HG_WIDTH)
        o_b = o_b.reshape(1, b, SB_WIDTH).astype(BF16)
        o_c = hc.reshape(1, b, ML_WIDTH).astype(BF16)
        st["mm"] = m_pad[:, 0, :ML_HEADS]
        st["mconv"] = jnp.concatenate([buf[:, 1:], mx.reshape(b, 1, ML_WIDTH)], axis=1)
        st["k"] = blk(pf, PF_K).reshape(b, 1, SB_HEADS, SB_HEAD_DIM)
        st["v"] = blk(pf, PF_V).reshape(b, 1, SB_HEADS, SB_HEAD_DIM)
    x = _merge(x, ada, p["g_post"][l, 1], pb, o_a, o_b, o_c, p["hg_gain"][l], p["ml_gain"][l],
               w["w_hg_proj"][l], w["w_sb_proj"][l], w["w_ml_proj"][l], w["w_out"][l], tm["merge"])
    x = ffn(x, 2, 1)
    return x, st


def _run(x, ada_all, p, w, tm, past):
    outs = []
    for l in range(DEPTH):
        x, st = _layer(x, ada_all[l], l, p, w, tm, past)
        outs.append(st)
    return x, {name: jnp.stack([o[name] for o in outs], axis=1) for name in outs[0]}


def kernel(x_prompt, x_sample, cache_sb_k, cache_sb_v, state_hgrn, state_mlstm_c, state_mlstm_n,
           state_mlstm_m, state_mlstm_conv, page_table, c_prompt, c_sample, w_ada, b_ada, g_pre, g_post,
           w_ffn_in, w_ffn_down, w_in, sb_bias, hg_lb_logits, hg_gain, ml_conv_w, ml_conv_b, ml_wq, ml_wk,
           ml_wv, ml_w_if, ml_b_if, ml_gain, w_hg_proj, w_sb_proj, w_ml_proj, w_out):
    p = dict(g_pre=g_pre, g_post=g_post, w_ffn_in=w_ffn_in, w_ffn_down=w_ffn_down, w_in=w_in,
             sb_bias=sb_bias, hg_lb_logits=hg_lb_logits, hg_gain=hg_gain, ml_conv_w=ml_conv_w,
             ml_conv_b=ml_conv_b, ml_wq=ml_wq, ml_wk=ml_wk, ml_wv=ml_wv, ml_w_if=ml_w_if,
             ml_b_if=ml_b_if, ml_gain=ml_gain, w_hg_proj=w_hg_proj, w_sb_proj=w_sb_proj,
             w_ml_proj=w_ml_proj, w_out=w_out)
    w = _prep_weights(p)
    bp, t, d = x_prompt.shape
    bs = x_sample.shape[0]
    ada = _ada(jnp.concatenate([c_prompt, c_sample], axis=0), w_ada, b_ada)
    ada_p = ada[:, :bp].reshape(DEPTH, bp, 1, N_SUB * 3 * d)
    ada_s = ada[:, bp:].reshape(DEPTH, 1, bs, N_SUB * 3 * d)

    tm_p = dict(ffn=min(t, 512), mixin=min(t, 512), merge=min(t, 256), hgrn=min(t, 512),
                mlpre=min(t, 512))
    y_p, sp = _run(x_prompt, ada_p, p, w, tm_p, None)

    past = dict(k=jnp.transpose(cache_sb_k, (0, 1, 3, 4, 2)), v=jnp.transpose(cache_sb_v, (0, 1, 3, 4, 2)),
                page_table=page_table, hgrn=state_hgrn, mc=state_mlstm_c, mn=state_mlstm_n,
                mm=state_mlstm_m, mconv=state_mlstm_conv)
    tm_s = dict(ffn=bs, mixin=bs, merge=bs)
    y_s, ss = _run(x_sample.reshape(1, bs, d), ada_s, p, w, tm_s, past)
    y_s = y_s.reshape(bs, 1, d)
    k_p = jnp.transpose(sp["k"], (0, 1, 4, 2, 3))
    v_p = jnp.transpose(sp["v"], (0, 1, 4, 2, 3))
    return (y_p, y_s, k_p, v_p, ss["k"], ss["v"], sp["hgrn"], ss["hgrn"],
            sp["mc"], ss["mc"], sp["mn"], ss["mn"], sp["mm"], ss["mm"], sp["mconv"], ss["mconv"])
```

```python
import functools

import jax
import jax.numpy as jnp
from jax import lax
from jax.experimental import pallas as pl
from jax.experimental.pallas import tpu as pltpu

F32 = jnp.float32
BF16 = jnp.bfloat16

D_MODEL = 1024
DEPTH = 2
PAGE_SIZE = 128
HG_HEADS = 4
HG_DK = 128
HG_WIDTH = 512
HG_CHUNK = 64
HG_SUB = 16
HG_SAFE_DECAY = 80.0
SB_HEADS = 8
SB_HEAD_DIM = 64
SB_WIDTH = 512
SB_TQ = 512
SB_TK = 256
SB_ROWS = 128
SB_HEAD_GROUP = 8
ML_HEADS = 4
ML_HEAD_DIM = 128
ML_WIDTH = 512
ML_CHUNK = 128
ML_CONV = 4
D_FF = 2816
FF_CHUNK = 256
FFN_RES = 0.5
N_SUB = 3
NORM_EPS = 1e-6
IN_COLS = 7680
COL = 512
PF_SRC = (1, 5, 6, 7)
PB_SRC = (0, 2, 3, 4, 8, 9, 10, 11, 12, 13, 14)
PF_COLS = len(PF_SRC) * COL
PB_COLS = len(PB_SRC) * COL
PF_Z, PF_K, PF_V, PF_MX = 0, 1, 2, 3
PB_Q, PB_I, PB_HG, PB_SQ, PB_MO, PB_GATES = 0, 1, 2, 3, 4, 5
V7X_VMEM_LIMIT = 56 * 1024 * 1024
NEG_INF = float("-inf")
LOG2E = 1.4426950408889634


def _dot(a, b):
    return jnp.dot(a, b, preferred_element_type=F32)


def _dot_nt(a, b):
    return lax.dot_general(a, b, (((1,), (1,)), ((), ())), preferred_element_type=F32)


def _split3(x):
    x1 = x.astype(BF16)
    r1 = x - x1.astype(F32)
    x2 = r1.astype(BF16)
    x3 = (r1 - x2.astype(F32)).astype(BF16)
    return x1, x2, x3


def _tri_dot(tri, x):
    x1, x2, x3 = _split3(x)
    return _dot(tri, x1) + _dot(tri, x2) + _dot(tri, x3)


def _dot_tri(x, tri):
    x1, x2, x3 = _split3(x)
    return _dot(x1, tri) + _dot(x2, tri) + _dot(x3, tri)


def _sigmoid(x):
    return 0.5 * jnp.tanh(0.5 * x) + 0.5


def _silu(x):
    return x * _sigmoid(x)


def _log_sigmoid(x):
    return jnp.minimum(x, 0.0) - jnp.log(1.0 + jnp.exp(-jnp.abs(x)))


def _logaddexp(a, b):
    amax = jnp.maximum(a, b)
    delta = a - b
    return jnp.where(delta != delta, a + b, amax + jnp.log(1.0 + jnp.exp(-jnp.abs(delta))))


def _rms(x, g):
    return x * lax.rsqrt(jnp.mean(x * x, axis=-1, keepdims=True) + NORM_EPS) * g


def _head_norm(o, gain, heads, width):
    parts = []
    for h in range(heads):
        oh = o[:, h * width:(h + 1) * width]
        parts.append(oh * lax.rsqrt(jnp.mean(oh * oh, axis=-1, keepdims=True) + NORM_EPS))
    return jnp.concatenate(parts, axis=-1) * gain


def _to_col(row):
    n = row.shape[1]
    eye = lax.broadcasted_iota(jnp.int32, (n, n), 0) == lax.broadcasted_iota(jnp.int32, (n, n), 1)
    return jnp.sum(jnp.where(eye, row, 0.0), axis=1, keepdims=True)


def _to_row(col):
    n = col.shape[0]
    eye = lax.broadcasted_iota(jnp.int32, (n, n), 0) == lax.broadcasted_iota(jnp.int32, (n, n), 1)
    return jnp.sum(jnp.where(eye, col, 0.0), axis=0, keepdims=True)


def _params(sem, vmem=None):
    return pltpu.CompilerParams(dimension_semantics=sem, vmem_limit_bytes=vmem)


def _ada_kernel(c_ref, w_ref, b_ref, o_ref):
    s = _silu(c_ref[...]).astype(BF16)
    o_ref[...] = _dot(s, w_ref[...].astype(BF16)) + b_ref[...]


def _ada(c_all, w_ada, b_ada):
    n, d = c_all.shape
    cols = w_ada.shape[-1]
    tn = 1536
    return pl.pallas_call(
        _ada_kernel,
        grid=(DEPTH, cols // tn),
        in_specs=[pl.BlockSpec((n, d), lambda l, j: (0, 0)),
                  pl.BlockSpec((None, d, tn), lambda l, j: (l, 0, j)),
                  pl.BlockSpec((None, 1, tn), lambda l, j: (l, 0, j))],
        out_specs=pl.BlockSpec((None, n, tn), lambda l, j: (l, 0, j)),
        out_shape=jax.ShapeDtypeStruct((DEPTH, n, cols), F32),
        compiler_params=_params(("parallel", "parallel"), V7X_VMEM_LIMIT),
    )(c_all, w_ada, b_ada.reshape(DEPTH, 1, cols))


def _mod_spec(ada, tm, col):
    rm = ada.shape[1]
    if rm == 1:
        return pl.BlockSpec((None, 1, D_MODEL), lambda g, i: (g, 0, col))
    return pl.BlockSpec((None, tm, D_MODEL), lambda g, i: (g, i, col))


def _ffn_kernel(x_ref, sh_ref, sc_ref, gt_ref, gpre_ref, gpost_ref, win_ref, wd_ref, o_ref, h_ref, a_ref):
    x = x_ref[...]
    h_ref[...] = (_rms(x, gpre_ref[...]) * (1.0 + sc_ref[...]) + sh_ref[...]).astype(BF16)
    for f in range(D_FF // FF_CHUNK):
        cols = slice(f * FF_CHUNK, (f + 1) * FF_CHUNK)
        h = h_ref[...]
        g = _dot(h, win_ref[:, cols])
        u = _dot(h, win_ref[:, D_FF + f * FF_CHUNK:D_FF + (f + 1) * FF_CHUNK])
        a_ref[:, cols] = (_silu(g) * u).astype(BF16)
    y = _dot(a_ref[...], wd_ref[...])
    o_ref[...] = x + FFN_RES * (1.0 + gt_ref[...]) * _rms(y, gpost_ref[...])


def _ffn(x, ada, j, gpre, gpost, w_in, w_down, l, s, tm):
    g, r, d = x.shape
    xspec = pl.BlockSpec((None, tm, d), lambda a, i: (a, i, 0))
    vec = pl.BlockSpec((1, d), lambda a, i: (0, 0))
    res = lambda shape: pl.BlockSpec((None, None) + shape, lambda a, i: (l, s, 0, 0),
                                     pipeline_mode=pl.Buffered(1))
    return pl.pallas_call(
        _ffn_kernel,
        grid=(g, r // tm),
        in_specs=[xspec, _mod_spec(ada, tm, 3 * j), _mod_spec(ada, tm, 3 * j + 1),
                  _mod_spec(ada, tm, 3 * j + 2), vec, vec, res((d, 2 * D_FF)), res((D_FF, d))],
        out_specs=xspec,
        out_shape=jax.ShapeDtypeStruct(x.shape, F32),
        scratch_shapes=[pltpu.VMEM((tm, d), BF16), pltpu.VMEM((tm, D_FF), BF16)],
        compiler_params=_params(("parallel", "parallel"), V7X_VMEM_LIMIT),
    )(x, ada, ada, ada, gpre.reshape(1, d), gpost.reshape(1, d), w_in, w_down)


def _mixin_kernel(x_ref, sh_ref, sc_ref, gpre_ref, w_ref, pf_ref, pb_ref, h_ref):
    h_ref[...] = (_rms(x_ref[...], gpre_ref[...]) * (1.0 + sc_ref[...]) + sh_ref[...]).astype(BF16)
    nf = PF_COLS // COL
    for c, src in enumerate(PF_SRC + PB_SRC):
        y = _dot(h_ref[...], w_ref[:, src * COL:(src + 1) * COL])
        if c < nf:
            pf_ref[:, c * COL:(c + 1) * COL] = y
        else:
            pb_ref[:, (c - nf) * COL:(c - nf + 1) * COL] = y.astype(BF16)


def _mixin(x, ada, gpre, w_in, l, tm):
    g, r, d = x.shape
    return pl.pallas_call(
        _mixin_kernel,
        grid=(g, r // tm),
        in_specs=[pl.BlockSpec((None, tm, d), lambda a, i: (a, i, 0)),
                  _mod_spec(ada, tm, 3), _mod_spec(ada, tm, 4),
                  pl.BlockSpec((1, d), lambda a, i: (0, 0)),
                  pl.BlockSpec((None, d, IN_COLS), lambda a, i: (l, 0, 0), pipeline_mode=pl.Buffered(1))],
        out_specs=[pl.BlockSpec((None, tm, PF_COLS), lambda a, i: (a, i, 0)),
                   pl.BlockSpec((None, tm, PB_COLS), lambda a, i: (a, i, 0))],
        out_shape=[jax.ShapeDtypeStruct((g, r, PF_COLS), F32), jax.ShapeDtypeStruct((g, r, PB_COLS), BF16)],
        scratch_shapes=[pltpu.VMEM((tm, d), BF16)],
        compiler_params=_params(("parallel", "parallel"), V7X_VMEM_LIMIT),
    )(x, ada, ada, gpre.reshape(1, d), w_in)


def _merge_kernel(x_ref, gt_ref, gpost_ref, oa_ref, hg_ref, ob_ref, oc_ref, mo_ref,
                  g0, g1, g2, g3, g4, g5, hgain_ref, mgain_ref,
                  whg_ref, wsb_ref, wml_ref, wout_ref, o_ref):
    f32 = lambda ref: ref[...].astype(F32)
    o_a = (_head_norm(oa_ref[...], hgain_ref[...], HG_HEADS, HG_DK) * _silu(f32(hg_ref))).astype(BF16)
    o_b = ob_ref[...]
    o_c = (_head_norm(f32(oc_ref), mgain_ref[...], ML_HEADS, ML_HEAD_DIM) * _sigmoid(f32(mo_ref))).astype(BF16)
    gates = ((g0, g2, g4), (g1, g3, g5))
    y = None
    for c in range(2):
        cs = slice(c * COL, (c + 1) * COL)
        ga, gb, gc = gates[c]
        m = (_sigmoid(f32(ga)) * _dot(o_a, whg_ref[:, cs])
             + _sigmoid(f32(gb)) * _dot(o_b, wsb_ref[:, cs])
             + _sigmoid(f32(gc)) * _dot(o_c, wml_ref[:, cs]))
        part = _dot(m.astype(BF16), wout_ref[cs, :])
        y = part if y is None else y + part
    o_ref[...] = x_ref[...] + (1.0 + gt_ref[...]) * _rms(y, gpost_ref[...])


def _merge(x, ada, gpost, pb, o_a, o_b, o_c, hgain, mgain, whg, wsb, wml, wout, tm):
    g, r, d = x.shape
    row = lambda col: pl.BlockSpec((None, tm, COL), lambda a, i: (a, i, col))
    vec = lambda n: pl.BlockSpec((1, n), lambda a, i: (0, 0))
    wsp = lambda shape: pl.BlockSpec(shape, lambda a, i: (0, 0))
    xspec = pl.BlockSpec((None, tm, d), lambda a, i: (a, i, 0))
    return pl.pallas_call(
        _merge_kernel,
        grid=(g, r // tm),
        in_specs=[xspec, _mod_spec(ada, tm, 5), vec(d),
                  row(0), row(PB_HG), row(0), row(0), row(PB_MO)]
                 + [row(PB_GATES + c) for c in range(6)]
                 + [vec(HG_WIDTH), vec(ML_WIDTH),
                    wsp((HG_WIDTH, d)), wsp((SB_WIDTH, d)), wsp((ML_WIDTH, d)), wsp((d, d))],
        out_specs=xspec,
        out_shape=jax.ShapeDtypeStruct(x.shape, F32),
        compiler_params=_params(("parallel", "parallel"), V7X_VMEM_LIMIT),
    )(x, ada, gpost.reshape(1, d), o_a, pb, o_b, o_c, pb,
      pb, pb, pb, pb, pb, pb,
      hgain.reshape(1, HG_WIDTH), mgain.reshape(1, ML_WIDTH), whg, wsb, wml, wout)


def _hgrn_lb(lbl, layer):
    e = jnp.exp(lbl - jnp.max(lbl, axis=0, keepdims=True))
    p = e / jnp.sum(e, axis=0, keepdims=True)
    lb = jnp.zeros_like(p[0:1])
    for r in range(1, layer + 1):
        lb = lb + p[r:r + 1]
    return lb


def _hgrn_gates(z, lb):
    logf = _logaddexp(jnp.log(lb), jnp.log1p(-lb) + _log_sigmoid(z))
    k = (1.0 - lb) * _sigmoid(-z)
    return logf, k


def _hgrn_prompt_kernel(lbl_ref, qb_ref, z_ref, ib_ref, o_ref, s_ref, st_ref, b_ref, k_ref, q_ref, i_ref,
                        *, layer):
    tc = qb_ref.shape[0]
    q_ref[...] = qb_ref[...].astype(F32)
    i_ref[...] = ib_ref[...].astype(F32)
    nchunk = tc // HG_CHUNK
    nsub = HG_CHUNK // HG_SUB

    @pl.when(pl.program_id(1) == 0)
    def _():
        st_ref[...] = jnp.zeros_like(st_ref)

    lb = _hgrn_lb(lbl_ref[...], layer)[0]
    logf, k = _hgrn_gates(z_ref[...], lb)
    k_ref[...] = k
    ri = lax.broadcasted_iota(jnp.int32, (tc, tc), 0)
    ci = lax.broadcasted_iota(jnp.int32, (tc, tc), 1)
    tri = jnp.where((ci <= ri) & (ri // HG_CHUNK == ci // HG_CHUNK), 1.0, 0.0).astype(BF16)
    b_ref[...] = _tri_dot(tri, logf)
    block_decay = -jnp.sum(logf.reshape(tc // HG_SUB, HG_SUB, HG_WIDTH), axis=1)
    safe = jnp.max(block_decay) < HG_SAFE_DECAY

    rows64 = lax.broadcasted_iota(jnp.int32, (HG_CHUNK, 1), 0)
    rows16 = lax.broadcasted_iota(jnp.int32, (HG_SUB, 1), 0)
    r64 = lax.broadcasted_iota(jnp.int32, (HG_CHUNK, HG_CHUNK), 0)
    c64 = lax.broadcasted_iota(jnp.int32, (HG_CHUNK, HG_CHUNK), 1)
    causal64 = c64 <= r64

    heads = range(HG_HEADS)
    sl = [slice(h * HG_DK, (h + 1) * HG_DK) for h in heads]

    def score_operands(q, b, kk, cs, s_i, include_diag):
        i0 = s_i * HG_SUB
        r = b[i0 - 1:i0, cs] if s_i else jnp.zeros((1, HG_DK), F32)
        hi = i0 + HG_SUB if include_diag else i0
        cap = HG_SAFE_DECAY if include_diag else 0.0
        qt = q[i0:i0 + HG_SUB, cs] * jnp.exp(b[i0:i0 + HG_SUB, cs] - r)
        kt = jnp.where(rows64 < hi, kk[:, cs] * jnp.exp(jnp.minimum(r - b[:, cs], cap)), 0.0)
        return qt.astype(BF16), kt.astype(BF16)

    def chunk(r0):
        rows = pl.ds(r0, HG_CHUNK)
        q = q_ref[rows, :]
        i = i_ref[rows, :]
        b = b_ref[rows, :]
        kk = k_ref[rows, :]
        blast = b[HG_CHUNK - 1:HG_CHUNK, :]
        eb = jnp.exp(b)
        kd = kk * jnp.exp(blast - b)
        ib = [i[:, cs].astype(BF16) for cs in sl]
        ops = [[score_operands(q, b, kk, cs, s_i, True) for s_i in range(nsub)] for cs in sl]
        p = [jnp.concatenate([_dot_nt(qt, kt) for qt, kt in ops[h]], axis=0) for h in heads]
        st = [st_ref[h] for h in heads]
        o_state = [_dot_nt((q[:, sl[h]] * eb[:, sl[h]]).astype(BF16), st[h].astype(BF16)) for h in heads]
        pm = [jnp.where(causal64 & safe, p[h], 0.0).astype(BF16) for h in heads]
        o_intra = [_dot(pm[h], ib[h]) for h in heads]
        upd = [_dot(i[:, sl[h]].T.astype(BF16), kd[:, sl[h]].astype(BF16)) for h in heads]
        o_ref[rows, :] = jnp.concatenate([o_state[h] + o_intra[h] for h in heads], axis=-1)
        for h in heads:
            st_ref[h] = st[h] * jnp.exp(blast[:, sl[h]]) + upd[h]

        @pl.when(jnp.logical_not(safe))
        def _():
            outs = []
            for h in heads:
                rows_p = [jnp.zeros((HG_SUB, HG_CHUNK), F32)]
                for s_i in range(1, nsub):
                    qt, kt = score_operands(q, b, kk, sl[h], s_i, False)
                    rows_p.append(_dot_nt(qt, kt))
                outs.append(_dot(jnp.concatenate(rows_p, axis=0).astype(BF16), ib[h]))
            o_ref[rows, :] += jnp.concatenate(outs, axis=-1)
            for s_i in range(nsub):
                base = r0 + s_i * HG_SUB
                srows = pl.ds(base, HG_SUB)
                q_i = q_ref[srows, :]
                b_i = b_ref[srows, :]

                def s_body(s, acc):
                    b_s = b_ref[pl.ds(base + s, 1), :]
                    k_s = k_ref[pl.ds(base + s, 1), :]
                    i_s = i_ref[pl.ds(base + s, 1), :]
                    e = jnp.where(rows16 >= s, jnp.exp(jnp.minimum(b_i - b_s, 0.0)), 0.0)
                    pr = q_i * e * k_s
                    parts = [jnp.sum(pr[:, cs], axis=1, keepdims=True) * i_s[:, cs] for cs in sl]
                    return acc + jnp.concatenate(parts, axis=-1)

                acc = lax.fori_loop(0, HG_SUB, s_body, jnp.zeros((HG_SUB, HG_WIDTH), F32))
                o_ref[srows, :] += acc

    def pair_body(n, carry):
        r0 = pl.multiple_of(n * (2 * HG_CHUNK), 2 * HG_CHUNK)
        chunk(r0)
        chunk(r0 + HG_CHUNK)
        return carry

    lax.fori_loop(0, nchunk // 2, pair_body, 0)

    @pl.when(pl.program_id(1) == pl.num_programs(1) - 1)
    def _():
        for h in range(HG_HEADS):
            s_ref[h] = st_ref[h].T


def _hgrn_prompt(pf, pb, lbl, layer, tc):
    b, t, _ = pf.shape
    row = lambda col: pl.BlockSpec((None, tc, COL), lambda a, i: (a, i, col))
    return pl.pallas_call(
        functools.partial(_hgrn_prompt_kernel, layer=layer),
        grid=(b, t // tc),
        in_specs=[pl.BlockSpec((DEPTH, 1, HG_WIDTH), lambda a, i: (0, 0, 0)), row(PB_Q), row(PF_Z), row(PB_I)],
        out_specs=[pl.BlockSpec((None, tc, HG_WIDTH), lambda a, i: (a, i, 0)),
                   pl.BlockSpec((None, HG_HEADS, HG_DK, HG_DK), lambda a, i: (a, 0, 0, 0))],
        out_shape=[jax.ShapeDtypeStruct((b, t, HG_WIDTH), F32),
                   jax.ShapeDtypeStruct((b, HG_HEADS, HG_DK, HG_DK), F32)],
        scratch_shapes=[pltpu.VMEM((HG_HEADS, HG_DK, HG_DK), F32)] + [pltpu.VMEM((tc, HG_WIDTH), F32)] * 4,
        compiler_params=_params(("parallel", "arbitrary"), V7X_VMEM_LIMIT),
    )(lbl.reshape(DEPTH, 1, HG_WIDTH), pb, pf, pb)


def _hgrn_step_kernel(lbl_ref, qr_ref, zr_ref, ir_ref, s_ref, o_ref, so_ref, *, layer):
    lb = _hgrn_lb(lbl_ref[...], layer)[0]
    for n in range(qr_ref.shape[0]):
        outs = []
        for h in range(HG_HEADS):
            cs = slice(h * HG_DK, (h + 1) * HG_DK)
            logf, k = _hgrn_gates(_to_col(zr_ref[n, :, cs]), lb[h])
            s_new = jnp.exp(logf) * s_ref[n, h] + k * ir_ref[n, :, cs]
            so_ref[n, h] = s_new
            outs.append(jnp.sum(_to_col(qr_ref[n, :, cs]) * s_new, axis=0, keepdims=True))
        o_ref[n] = jnp.concatenate(outs, axis=-1)


STEP_SEQS = 4


def _hgrn_step(q, z, i, state, lbl, layer):
    b = q.shape[0]
    ns = STEP_SEQS if b % STEP_SEQS == 0 else 1
    row = pl.BlockSpec((ns, 1, COL), lambda a: (a, 0, 0))
    return pl.pallas_call(
        functools.partial(_hgrn_step_kernel, layer=layer),
        grid=(b // ns,),
        in_specs=[pl.BlockSpec((DEPTH, HG_HEADS, HG_DK, 1), lambda a: (0, 0, 0, 0)), row, row, row,
                  pl.BlockSpec((ns, None, HG_HEADS, HG_DK, HG_DK), lambda a: (a, layer, 0, 0, 0))],
        out_specs=[pl.BlockSpec((ns, 1, HG_WIDTH), lambda a: (a, 0, 0)),
                   pl.BlockSpec((ns, HG_HEADS, HG_DK, HG_DK), lambda a: (a, 0, 0, 0))],
        out_shape=[jax.ShapeDtypeStruct((b, 1, HG_WIDTH), F32),
                   jax.ShapeDtypeStruct((b, HG_HEADS, HG_DK, HG_DK), F32)],
        compiler_params=_params(("parallel",)),
    )(lbl.reshape(DEPTH, HG_HEADS, HG_DK, 1), q.reshape(b, 1, COL), z.reshape(b, 1, COL), i.reshape(b, 1, COL), state)


def _softplus(z):
    return jnp.maximum(z, 0.0) + jnp.log(1.0 + jnp.exp(-jnp.abs(z)))


def _sb_prompt_kernel(bias_ref, q_ref, k_ref, v_ref, o_ref, kt_out, vt_out,
                      qh_ref, kt_ref, vh_ref, acc_ref, c_ref):
    tq, tk = SB_TQ, SB_TK
    nk = k_ref.shape[0] // tk
    qi = pl.program_id(1)

    @pl.when(qi == 0)
    def _():
        for n in range(nk):
            ks = slice(n * tk, (n + 1) * tk)
            kt_full = k_ref[ks, :].T
            kt_out[:, ks] = kt_full
            vt_out[:, ks] = v_ref[ks, :].T
            for h in range(SB_HEADS):
                cs = slice(h * SB_HEAD_DIM, (h + 1) * SB_HEAD_DIM)
                kt_ref[h, n] = kt_full[cs, :].astype(BF16)
                vh_ref[h, n] = v_ref[ks, cs].astype(BF16)

    nrb = tq // SB_ROWS
    for h in range(SB_HEADS):
        cs = slice(h * SB_HEAD_DIM, (h + 1) * SB_HEAD_DIM)
        for r in range(nrb):
            rows = slice(r * SB_ROWS, (r + 1) * SB_ROWS)
            qh_ref[h, r] = (q_ref[:, cs].astype(F32)[rows] * (SB_HEAD_DIM ** -0.5 * LOG2E)).astype(BF16)
    acc_ref[...] = jnp.zeros_like(acc_ref)
    c_ref[...] = jnp.zeros_like(c_ref)

    ri = lax.broadcasted_iota(jnp.int32, (tk, tk), 0)
    ci = lax.broadcasted_iota(jnp.int32, (tk, tk), 1)
    upper = jnp.where(ri > ci, 1.0, 0.0).astype(BF16)
    rpt = tk // SB_ROWS
    base = qi * (tq // tk)

    def tile(j, r, diag):
        nkeys = (r % rpt + 1) * SB_ROWS if diag else tk
        mask = None
        if diag:
            mask = (lax.broadcasted_iota(jnp.int32, (SB_ROWS, nkeys), 1)
                    < lax.broadcasted_iota(jnp.int32, (SB_ROWS, nkeys), 0) + (r % rpt) * SB_ROWS)
        for g0 in range(0, SB_HEADS, SB_HEAD_GROUP):
            heads = range(g0, g0 + SB_HEAD_GROUP)
            zs = [_dot(qh_ref[h, r], kt_ref[h, j, :, :nkeys]) + bias_ref[h] * LOG2E for h in heads]
            lbs, l1ms = [], []
            for z in zs:
                sp = jnp.maximum(z, 0.0) + jnp.log(1.0 + jnp.exp2(-jnp.abs(z))) * LOG2E
                l1m = -sp
                if mask is not None:
                    l1m = jnp.where(mask, l1m, 0.0)
                lbs.append(z - sp)
                l1ms.append(l1m)
            sufs = [_dot(l1m.astype(BF16), upper[:nkeys, :nkeys]) for l1m in l1ms]
            weights = []
            for h, lb, l1m, suf in zip(heads, lbs, l1ms, sufs):
                c = c_ref[h * nrb + r][:, 0:1]
                a = jnp.exp2(lb + suf + c)
                if mask is not None:
                    a = jnp.where(mask, a, 0.0)
                weights.append(a.astype(BF16))
                c_ref[h * nrb + r] = jnp.broadcast_to(c + suf[:, 0:1] + l1m[:, 0:1], (SB_ROWS, 128))
            for h, a in zip(heads, weights):
                acc_ref[h, r] += _dot(a, vh_ref[h, j, :nkeys, :])

    for r in range(nrb):
        tile(base + r // rpt, r, True)
        for kk in range(r // rpt - 1, -1, -1):
            tile(base + kk, r, False)

    def body(step, carry):
        for r in range(nrb):
            tile(base - 1 - step, r, False)
        return carry

    lax.fori_loop(0, base, body, 0)
    o_ref[...] = jnp.concatenate(
        [jnp.concatenate([acc_ref[h, r] for h in range(SB_HEADS)], axis=-1) for r in range(nrb)],
        axis=0).astype(BF16)


def _sb_prompt(pf, pb, bias):
    b, t, _ = pf.shape
    tq = min(SB_TQ, t)
    nk = t // SB_TK
    seq = lambda col: pl.BlockSpec((None, t, COL), lambda a, i: (a, 0, col))
    return pl.pallas_call(
        _sb_prompt_kernel,
        grid=(b, t // tq),
        in_specs=[pl.BlockSpec(memory_space=pltpu.SMEM),
                  pl.BlockSpec((None, tq, COL), lambda a, i: (a, i, PB_SQ)), seq(PF_K), seq(PF_V)],
        out_specs=[pl.BlockSpec((None, tq, SB_WIDTH), lambda a, i: (a, i, 0)),
                   pl.BlockSpec((None, SB_WIDTH, t), lambda a, i: (a, 0, 0)),
                   pl.BlockSpec((None, SB_WIDTH, t), lambda a, i: (a, 0, 0))],
        out_shape=[jax.ShapeDtypeStruct((b, t, SB_WIDTH), BF16),
                   jax.ShapeDtypeStruct((b, SB_WIDTH, t), F32),
                   jax.ShapeDtypeStruct((b, SB_WIDTH, t), F32)],
        scratch_shapes=[pltpu.VMEM((SB_HEADS, tq // SB_ROWS, SB_ROWS, SB_HEAD_DIM), BF16),
                        pltpu.VMEM((SB_HEADS, nk, SB_HEAD_DIM, SB_TK), BF16),
                        pltpu.VMEM((SB_HEADS, nk, SB_TK, SB_HEAD_DIM), BF16),
                        pltpu.VMEM((SB_HEADS, tq // SB_ROWS, SB_ROWS, SB_HEAD_DIM), F32),
                        pltpu.VMEM((SB_HEADS * (tq // SB_ROWS), SB_ROWS, 128), F32)],
        compiler_params=_params(("parallel", "arbitrary"), V7X_VMEM_LIMIT),
    )(bias, pb, pf, pf)


SB_PAGES_PER_STEP = 16


def _sb_decode_kernel(pt_ref, q_ref, bias_ref, *refs):
    npg = SB_PAGES_PER_STEP
    k_refs = refs[:npg]
    v_refs = refs[npg:2 * npg]
    o_ref, qb_ref, c_ref, acc_ref = refs[2 * npg:]
    g = pl.program_id(1)

    @pl.when(g == 0)
    def _():
        for h in range(SB_HEADS):
            qcol = _to_col(q_ref[:, h * SB_HEAD_DIM:(h + 1) * SB_HEAD_DIM])
            qb_ref[h] = jnp.broadcast_to(qcol, qb_ref.shape[1:])
        c_ref[...] = jnp.zeros_like(c_ref)
        acc_ref[...] = jnp.zeros_like(acc_ref)

    ri = lax.broadcasted_iota(jnp.int32, (PAGE_SIZE, PAGE_SIZE), 0)
    ci = lax.broadcasted_iota(jnp.int32, (PAGE_SIZE, PAGE_SIZE), 1)
    upper = jnp.where(ri > ci, 1.0, 0.0).astype(BF16)
    bias = bias_ref[...]
    c = c_ref[:, 0:1]
    weights = [None] * npg
    for p in range(npg - 1, -1, -1):
        zrows = []
        for h in range(SB_HEADS):
            part = k_refs[p][h, 0:8, :] * qb_ref[h, 0:8, :]
            for r in range(8, SB_HEAD_DIM, 8):
                part = part + k_refs[p][h, r:r + 8, :] * qb_ref[h, r:r + 8, :]
            zrows.append(jnp.sum(part, axis=0, keepdims=True))
        z = jnp.concatenate(zrows, axis=0) + bias
        sp = _softplus(z)
        l1m = -sp
        suf = _dot(l1m.astype(BF16), upper)
        weights[p] = jnp.exp(z - sp + suf + c)
        c = c + suf[:, 0:1] + l1m[:, 0:1]
    c_ref[...] = jnp.broadcast_to(c, c_ref.shape)
    for h in range(SB_HEADS):
        part = weights[0][h:h + 1, :] * v_refs[0][h]
        for p in range(1, npg):
            part = part + weights[p][h:h + 1, :] * v_refs[p][h]
        acc_ref[h] += part

    @pl.when(g == pl.num_programs(1) - 1)
    def _():
        o_ref[...] = jnp.concatenate(
            [_to_row(jnp.sum(acc_ref[h], axis=1, keepdims=True)) for h in range(SB_HEADS)], axis=-1)


def _sb_decode(sq, bias, cache_kt, cache_vt, page_table, layer):
    b = sq.shape[0]
    n_pages = page_table.shape[1]
    npg = SB_PAGES_PER_STEP
    ng = n_pages // npg
    qrow = (sq * (SB_HEAD_DIM ** -0.5)).reshape(b, 1, SB_WIDTH)

    def page_spec(p):
        return pl.BlockSpec((None, None, SB_HEADS, SB_HEAD_DIM, PAGE_SIZE),
                            lambda a, g, pt: (pt[a, (ng - 1 - g) * npg + p], layer, 0, 0, 0))

    hd = pl.BlockSpec((None, 1, SB_WIDTH), lambda a, g, pt: (a, 0, 0))
    grid_spec = pltpu.PrefetchScalarGridSpec(
        num_scalar_prefetch=1,
        grid=(b, ng),
        in_specs=[hd, pl.BlockSpec((SB_HEADS, 1), lambda a, g, pt: (0, 0))]
                 + [page_spec(p) for p in range(npg)] + [page_spec(p) for p in range(npg)],
        out_specs=hd,
        scratch_shapes=[pltpu.VMEM((SB_HEADS, SB_HEAD_DIM, PAGE_SIZE), F32),
                        pltpu.VMEM((SB_HEADS, 128), F32),
                        pltpu.VMEM((SB_HEADS, SB_HEAD_DIM, PAGE_SIZE), F32)],
    )
    return pl.pallas_call(
        _sb_decode_kernel,
        grid_spec=grid_spec,
        out_shape=jax.ShapeDtypeStruct((b, 1, SB_WIDTH), F32),
        compiler_params=_params(("parallel", "arbitrary"), V7X_VMEM_LIMIT),
    )(page_table, qrow, bias.reshape(SB_HEADS, 1), *([cache_kt] * npg), *([cache_vt] * npg))


def _ml_pre_kernel(x0, x1, x2, x3, *refs):
    _ml_pre_body(x0[...], x1[...], x2[...], x3[...], *refs)


def _ml_pre_seq_kernel(prev_ref, x_ref, cw_ref, cb_ref, wq_ref, wk_ref, wv_ref, wif_ref, bif_ref,
                       q_ref, k_ref, v_ref, g_ref, xs_ref):
    tm = x_ref.shape[0]
    halo = prev_ref.shape[0]
    x = x_ref[...]
    xs_ref[halo:, :] = x
    xs_ref[:halo, :] = jnp.where(pl.program_id(1) == 0, 0.0, prev_ref[...])
    taps = [xs_ref[pl.ds(halo - (ML_CONV - 1) + j, tm), :] for j in range(ML_CONV - 1)]
    _ml_pre_body(*taps, x, cw_ref, cb_ref, wq_ref, wk_ref, wv_ref, wif_ref, bif_ref,
                 q_ref, k_ref, v_ref, g_ref)


def _ml_pre_body(x0, x1, x2, x, cw_ref, cb_ref, wq_ref, wk_ref, wv_ref, wif_ref, bif_ref,
                 q_ref, k_ref, v_ref, g_ref):
    cw = cw_ref[...]
    xc = _silu(x0 * cw[0:1] + x1 * cw[1:2] + x2 * cw[2:3] + x * cw[3:4] + cb_ref[...])
    xcb = xc.astype(BF16)
    xb = x.astype(BF16)
    qs, ks, vs = [], [], []
    for h in range(ML_HEADS):
        cs = slice(h * ML_HEAD_DIM, (h + 1) * ML_HEAD_DIM)
        qs.append(_dot(xcb[:, cs], wq_ref[h]))
        ks.append(_dot(xcb[:, cs], wk_ref[h]))
        vs.append(_dot(xb[:, cs], wv_ref[h]))
    q = jnp.concatenate(qs, axis=-1)
    k = jnp.concatenate(ks, axis=-1)
    v = jnp.concatenate(vs, axis=-1)
    q_ref[...] = q
    k_ref[...] = k
    v_ref[...] = v
    g_ref[...] = (_dot(q.astype(BF16), wif_ref[0:ML_WIDTH, :])
                  + _dot(k.astype(BF16), wif_ref[ML_WIDTH:2 * ML_WIDTH, :])
                  + _dot(v.astype(BF16), wif_ref[2 * ML_WIDTH:3 * ML_WIDTH, :]) + bif_ref[...])


def _ml_pre(xs, cw, cb, wq, wk, wv, wif, bif, tm):
    g, r, w = xs[0].shape
    row = pl.BlockSpec((None, tm, w), lambda a, i: (a, i, 0))
    full = lambda shape: pl.BlockSpec(shape, lambda a, i: (0,) * len(shape))
    out = jax.ShapeDtypeStruct((g, r, w), F32)
    return pl.pallas_call(
        _ml_pre_kernel,
        grid=(g, r // tm),
        in_specs=[row, row, row, row, full((ML_CONV, w)), full((1, w)),
                  full(wq.shape), full(wk.shape), full(wv.shape), full(wif.shape), full((1, 128))],
        out_specs=[row, row, row, pl.BlockSpec((None, tm, 128), lambda a, i: (a, i, 0))],
        out_shape=[out, out, out, jax.ShapeDtypeStruct((g, r, 128), F32)],
        compiler_params=_params(("parallel", "parallel"), V7X_VMEM_LIMIT),
    )(*xs, cw, cb.reshape(1, w), wq, wk, wv, wif, bif)


ML_HALO = 8


def _ml_pre_seq(pf, cw, cb, wq, wk, wv, wif, bif, tm):
    b, t, _ = pf.shape
    w = ML_WIDTH
    per = tm // ML_HALO
    row = pl.BlockSpec((None, tm, w), lambda a, i: (a, i, 0))
    full = lambda shape: pl.BlockSpec(shape, lambda a, i: (0,) * len(shape))
    out = jax.ShapeDtypeStruct((b, t, w), F32)
    return pl.pallas_call(
        _ml_pre_seq_kernel,
        grid=(b, t // tm),
        in_specs=[pl.BlockSpec((None, ML_HALO, w), lambda a, i: (a, jnp.maximum(i * per - 1, 0), PF_MX)),
                  pl.BlockSpec((None, tm, w), lambda a, i: (a, i, PF_MX)),
                  full((ML_CONV, w)), full((1, w)),
                  full(wq.shape), full(wk.shape), full(wv.shape), full(wif.shape), full((1, 128))],
        out_specs=[row, row, row, pl.BlockSpec((None, tm, 128), lambda a, i: (a, i, 0))],
        out_shape=[out, out, out, jax.ShapeDtypeStruct((b, t, 128), F32)],
        scratch_shapes=[pltpu.VMEM((tm + ML_HALO, w), F32)],
        compiler_params=_params(("parallel", "parallel"), V7X_VMEM_LIMIT),
    )(pf, pf, cw, cb.reshape(1, w), wq, wk, wv, wif, bif)


def _ml_prompt_kernel(q_ref, k_ref, v_ref, g_ref, h_ref, c_out, n_out, m_out, c_ref, n_ref, m_ref):
    L = ML_CHUNK
    nch = q_ref.shape[0] // L

    @pl.when(pl.program_id(1) == 0)
    def _():
        c_ref[...] = jnp.zeros_like(c_ref)
        n_ref[...] = jnp.zeros_like(n_ref)
        m_ref[...] = jnp.zeros_like(m_ref)

    ri = lax.broadcasted_iota(jnp.int32, (L, L), 0)
    ci = lax.broadcasted_iota(jnp.int32, (L, L), 1)
    causal = ci <= ri
    tril = jnp.where(causal, 1.0, 0.0).astype(BF16)
    triu = jnp.where(ri <= ci, 1.0, 0.0).astype(BF16)
    heads = range(ML_HEADS)
    sl = [slice(h * ML_HEAD_DIM, (h + 1) * ML_HEAD_DIM) for h in heads]

    ones = jnp.ones((L, ML_HEAD_DIM), BF16)
    pre = []
    for j in range(nch):
        rows = slice(j * L, (j + 1) * L)
        g = g_ref[rows, :]
        gt = g.T
        b_cols = _tri_dot(tril, _log_sigmoid(g))
        b_rows = _dot_tri(_log_sigmoid(gt), triu)
        q = [q_ref[rows, cs] for cs in sl]
        ks = [k_ref[rows, cs] * (ML_HEAD_DIM ** -0.5) for cs in sl]
        qb = [x.astype(BF16) for x in q]
        ksb = [x.astype(BF16) for x in ks]
        vb = [v_ref[rows, cs].astype(BF16) for cs in sl]
        vt = [v_ref[rows, cs].T for cs in sl]
        qk = [_dot_nt(qb[h], ksb[h]) for h in heads]
        b_bc = [jnp.broadcast_to(b_cols[:, ML_HEADS + h:ML_HEADS + h + 1], (L, L)) for h in heads]
        u_row = [gt[h:h + 1, :] - b_rows[ML_HEADS + h:ML_HEADS + h + 1, :] for h in heads]
        u_max = [jnp.broadcast_to(jnp.max(jnp.where(causal, u_row[h], NEG_INF), axis=1, keepdims=True), (L, L))
                 for h in heads]
        pre.append(dict(qb=qb, ksb=ksb, vb=vb, vt=vt, qk=qk, b_bc=b_bc, u_row=u_row, u_max=u_max))

    c0 = [c_ref[h] for h in heads]
    n0 = [n_ref[h:h + 1, :] for h in heads]
    m0 = [m_ref[h:h + 1, 0:1] for h in heads]
    for j, pj in enumerate(pre):
        rows = slice(j * L, (j + 1) * L)
        qc = [_dot_nt(pj["qb"][h], c0[h].astype(BF16)) for h in heads]
        qn = [_dot_nt(pj["qb"][h], jnp.broadcast_to(n0[h], (L, ML_HEAD_DIM)).astype(BF16)) for h in heads]
        y, inter, w = [], [], []
        for h in heads:
            y_h = jnp.maximum(pj["u_max"][h], m0[h])
            y.append(y_h)
            inter.append(jnp.exp(m0[h] - y_h))
            w.append(pj["qk"][h] * jnp.where(causal, jnp.exp(pj["u_row"][h] - y_h), 0.0))
        wb = [w[h].astype(BF16) for h in heads]
        wv = [_dot(wb[h], pj["vb"][h]) for h in heads]
        wsum = [_dot(wb[h], ones) for h in heads]
        outs, dec, w_end, m_new = [], [], [], []
        for h in heads:
            m_t = pj["b_bc"][h] + y[h]
            numer = inter[h] * qc[h] + wv[h]
            denom = inter[h] * qn[h] + wsum[h]
            outs.append(numer / jnp.maximum(jnp.abs(denom), jnp.exp(-m_t)))
            m_h = m_t[L - 1:L, 0:1]
            b_last = pj["b_bc"][h][L - 1:L, 0:1]
            w_end.append(jnp.exp(pj["u_row"][h] + b_last - m_h))
            dec.append(jnp.exp(b_last + m0[h] - m_h))
            m_new.append(m_h)
        c_upd = [_dot((pj["vt"][h] * w_end[h]).astype(BF16), pj["ksb"][h]) for h in heads]
        n_upd = [_dot(w_end[h].astype(BF16), pj["ksb"][h]) for h in heads]
        c0 = [dec[h] * c0[h] + c_upd[h] for h in heads]
        n0 = [dec[h] * n0[h] + n_upd[h] for h in heads]
        m0 = m_new
        h_ref[rows, :] = jnp.concatenate(outs, axis=-1).astype(BF16)
    for h in heads:
        c_ref[h] = c0[h]
        n_ref[h:h + 1, :] = n0[h]
        m_ref[h:h + 1, :] = jnp.broadcast_to(m0[h], (1, m_ref.shape[1]))

    @pl.when(pl.program_id(1) == pl.num_programs(1) - 1)
    def _():
        c_out[...] = c_ref[...]
        n_out[...] = n_ref[...]
        m_out[...] = m_ref[...]


ML_CHUNKS_PER_STEP = 4


def _ml_prompt(q, k, v, gates):
    b, t, w = q.shape
    tr = min(t, ML_CHUNKS_PER_STEP * ML_CHUNK)
    row = pl.BlockSpec((None, tr, w), lambda a, i: (a, i, 0))
    return pl.pallas_call(
        _ml_prompt_kernel,
        grid=(b, t // tr),
        in_specs=[row, row, row, pl.BlockSpec((None, tr, 128), lambda a, i: (a, i, 0))],
        out_specs=[row,
                   pl.BlockSpec((None, ML_HEADS, ML_HEAD_DIM, ML_HEAD_DIM), lambda a, i: (a, 0, 0, 0)),
                   pl.BlockSpec((None, ML_HEADS, ML_HEAD_DIM), lambda a, i: (a, 0, 0)),
                   pl.BlockSpec((None, ML_HEADS, 128), lambda a, i: (a, 0, 0))],
        out_shape=[jax.ShapeDtypeStruct((b, t, w), BF16),
                   jax.ShapeDtypeStruct((b, ML_HEADS, ML_HEAD_DIM, ML_HEAD_DIM), F32),
                   jax.ShapeDtypeStruct((b, ML_HEADS, ML_HEAD_DIM), F32),
                   jax.ShapeDtypeStruct((b, ML_HEADS, 128), F32)],
        scratch_shapes=[pltpu.VMEM((ML_HEADS, ML_HEAD_DIM, ML_HEAD_DIM), F32),
                        pltpu.VMEM((ML_HEADS, ML_HEAD_DIM), F32), pltpu.VMEM((ML_HEADS, 128), F32)],
        compiler_params=_params(("parallel", "arbitrary"), V7X_VMEM_LIMIT),
    )(q, k, v, gates)


def _ml_step_kernel(q_ref, k_ref, v_ref, g_ref, c_ref, n_ref, m_ref, h_ref, c_out, n_out, m_out):
    lane = lax.broadcasted_iota(jnp.int32, (1, 128), 1)
    for s in range(q_ref.shape[0]):
        g = g_ref[s]
        m_row = jnp.zeros((1, 128), F32)
        outs = []
        for h in range(ML_HEADS):
            cs = slice(h * ML_HEAD_DIM, (h + 1) * ML_HEAD_DIM)
            q = q_ref[s, :, cs]
            ks = k_ref[s, :, cs] * (ML_HEAD_DIM ** -0.5)
            v = _to_col(v_ref[s, :, cs])
            it = g[:, h:h + 1]
            logf = _log_sigmoid(g[:, ML_HEADS + h:ML_HEADS + h + 1])
            m0 = m_ref[s, :, h:h + 1]
            c0 = c_ref[s, h]
            n0 = n_ref[s, h:h + 1, :]
            m_t = jnp.maximum(logf + m0, it)
            dm = jnp.exp(it - m_t)
            inter = jnp.exp(logf + m0 - m_t)
            w = jnp.sum(q * ks, axis=1, keepdims=True) * dm
            numer = inter * jnp.sum(c0 * q, axis=1, keepdims=True) + w * v
            denom = inter * jnp.sum(q * n0, axis=1, keepdims=True) + w
            outs.append(_to_row(numer / jnp.maximum(jnp.abs(denom), jnp.exp(-m_t))))
            w_end = jnp.exp(it - m_t)
            dec = jnp.exp(logf + m0 - m_t)
            c_out[s, h] = dec * c0 + (w_end * v) * ks
            n_out[s, h:h + 1, :] = dec * n0 + w_end * ks
            m_row = jnp.where(lane == h, m_t, m_row)
        h_ref[s] = jnp.concatenate(outs, axis=-1)
        m_out[s] = m_row


def _ml_step(q, k, v, gates, state_c, state_n, state_m, layer):
    b = q.shape[0]
    m0 = state_m[:, layer].reshape(b, 1, ML_HEADS)
    ns = STEP_SEQS if b % STEP_SEQS == 0 else 1
    row = pl.BlockSpec((ns, 1, ML_WIDTH), lambda a: (a, 0, 0))
    cspec = pl.BlockSpec((ns, ML_HEADS, ML_HEAD_DIM, ML_HEAD_DIM), lambda a: (a, 0, 0, 0))
    nspec = pl.BlockSpec((ns, ML_HEADS, ML_HEAD_DIM), lambda a: (a, 0, 0))
    return pl.pallas_call(
        _ml_step_kernel,
        grid=(b // ns,),
        in_specs=[row, row, row, pl.BlockSpec((ns, 1, 128), lambda a: (a, 0, 0)),
                  pl.BlockSpec((ns, None, ML_HEADS, ML_HEAD_DIM, ML_HEAD_DIM),
                               lambda a: (a, layer, 0, 0, 0)),
                  pl.BlockSpec((ns, None, ML_HEADS, ML_HEAD_DIM), lambda a: (a, layer, 0, 0)),
                  pl.BlockSpec((ns, 1, ML_HEADS), lambda a: (a, 0, 0))],
        out_specs=[row, cspec, nspec, pl.BlockSpec((ns, 1, 128), lambda a: (a, 0, 0))],
        out_shape=[jax.ShapeDtypeStruct((b, 1, ML_WIDTH), F32),
                   jax.ShapeDtypeStruct((b, ML_HEADS, ML_HEAD_DIM, ML_HEAD_DIM), F32),
                   jax.ShapeDtypeStruct((b, ML_HEADS, ML_HEAD_DIM), F32),
                   jax.ShapeDtypeStruct((b, 1, 128), F32)],
        compiler_params=_params(("parallel",)),
    )(q, k, v, gates, state_c, state_n, m0)


def _prep_weights(p):
    bf = lambda a: a.astype(BF16)
    w = {}
    for name in ("w_ffn_in", "w_ffn_down", "w_in", "w_hg_proj", "w_sb_proj", "w_ml_proj", "w_out", "ml_wq", "ml_wk", "ml_wv"):
        w[name] = bf(p[name])
    w["wif"] = bf(jnp.pad(p["ml_w_if"], ((0, 0), (0, 0), (0, 128 - 2 * ML_HEADS))))
    w["bif"] = jnp.pad(p["ml_b_if"], ((0, 0), (0, 128 - 2 * ML_HEADS))).reshape(DEPTH, 1, 128)
    return w


def _layer(x, ada, l, p, w, tm, past):
    ffn = lambda xx, j, s: _ffn(xx, ada, j, p["g_pre"][l, j], p["g_post"][l, j],
                                w["w_ffn_in"], w["w_ffn_down"], l, s, tm["ffn"])
    x = ffn(x, 0, 0)
    pf, pb = _mixin(x, ada, p["g_pre"][l, 1], w["w_in"], l, tm["mixin"])
    st = {}
    blk = lambda arr, c: arr[..., c * COL:(c + 1) * COL]
    if past is None:
        b, t, _ = x.shape
        o_a, st["hgrn"] = _hgrn_prompt(pf, pb, p["hg_lb_logits"], l, tm["hgrn"])
        o_b, kt, vt = _sb_prompt(pf, pb, p["sb_bias"][l])
        q, k, v, gates = _ml_pre_seq(pf, p["ml_conv_w"][l], p["ml_conv_b"][l], w["ml_wq"][l],
                                     w["ml_wk"][l], w["ml_wv"][l], w["wif"][l], w["bif"][l], tm["mlpre"])
        o_c, st["mc"], st["mn"], m_pad = _ml_prompt(q, k, v, gates)
        st["mm"] = m_pad[:, :, 0]
        st["mconv"] = blk(pf, PF_MX)[:, t - (ML_CONV - 1):]
        st["k"] = kt.reshape(b, SB_HEADS, SB_HEAD_DIM, t)
        st["v"] = vt.reshape(b, SB_HEADS, SB_HEAD_DIM, t)
    else:
        b = x.shape[1]
        rows = lambda arr, c: blk(arr, c).reshape(b, COL).astype(F32)
        mx = blk(pf, PF_MX)
        o_a, st["hgrn"] = _hgrn_step(rows(pb, PB_Q), rows(pf, PF_Z), rows(pb, PB_I), past["hgrn"],
                                     p["hg_lb_logits"], l)
        o_b = _sb_decode(rows(pb, PB_SQ), p["sb_bias"][l], past["k"], past["v"], past["page_table"], l)
        buf = past["mconv"][:, l]
        taps = [buf[:, j].reshape(1, b, ML_WIDTH) for j in range(ML_CONV - 1)]
        q, k, v, gates = _ml_pre(taps + [mx], p["ml_conv_w"][l], p["ml_conv_b"][l], w["ml_wq"][l],
                                 w["ml_wk"][l], w["ml_wv"][l], w["wif"][l], w["bif"][l], b)
        hc, st["mc"], st["mn"], m_pad = _ml_step(q.reshape(b, 1, ML_WIDTH), k.reshape(b, 1, ML_WIDTH),
                                                 v.reshape(b, 1, ML_WIDTH), gates.reshape(b, 1, 128),
                                                 past["mc"], past["mn"], past["mm"], l)
        o_a = o_a.reshape(1, b, HG_WIDTH)
        o_b = o_b.reshape(1, b, SB_WIDTH).astype(BF16)
        o_c = hc.reshape(1, b, ML_WIDTH).astype(BF16)
        st["mm"] = m_pad[:, 0, :ML_HEADS]
        st["mconv"] = jnp.concatenate([buf[:, 1:], mx.reshape(b, 1, ML_WIDTH)], axis=1)
        st["k"] = blk(pf, PF_K).reshape(b, 1, SB_HEADS, SB_HEAD_DIM)
        st["v"] = blk(pf, PF_V).reshape(b, 1, SB_HEADS, SB_HEAD_DIM)
    x = _merge(x, ada, p["g_post"][l, 1], pb, o_a, o_b, o_c, p["hg_gain"][l], p["ml_gain"][l],
               w["w_hg_proj"][l], w["w_sb_proj"][l], w["w_ml_proj"][l], w["w_out"][l], tm["merge"])
    x = ffn(x, 2, 1)
    return x, st


def _run(x, ada_all, p, w, tm, past):
    outs = []
    for l in range(DEPTH):
        x, st = _layer(x, ada_all[l], l, p, w, tm, past)
        outs.append(st)
    return x, {name: jnp.stack([o[name] for o in outs], axis=1) for name in outs[0]}


def kernel(x_prompt, x_sample, cache_sb_k, cache_sb_v, state_hgrn, state_mlstm_c, state_mlstm_n,
           state_mlstm_m, state_mlstm_conv, page_table, c_prompt, c_sample, w_ada, b_ada, g_pre, g_post,
           w_ffn_in, w_ffn_down, w_in, sb_bias, hg_lb_logits, hg_gain, ml_conv_w, ml_conv_b, ml_wq, ml_wk,
           ml_wv, ml_w_if, ml_b_if, ml_gain, w_hg_proj, w_sb_proj, w_ml_proj, w_out):
    p = dict(g_pre=g_pre, g_post=g_post, w_ffn_in=w_ffn_in, w_ffn_down=w_ffn_down, w_in=w_in,
             sb_bias=sb_bias, hg_lb_logits=hg_lb_logits, hg_gain=hg_gain, ml_conv_w=ml_conv_w,
             ml_conv_b=ml_conv_b, ml_wq=ml_wq, ml_wk=ml_wk, ml_wv=ml_wv, ml_w_if=ml_w_if,
             ml_b_if=ml_b_if, ml_gain=ml_gain, w_hg_proj=w_hg_proj, w_sb_proj=w_sb_proj,
             w_ml_proj=w_ml_proj, w_out=w_out)
    w = _prep_weights(p)
    bp, t, d = x_prompt.shape
    bs = x_sample.shape[0]
    ada = _ada(jnp.concatenate([c_prompt, c_sample], axis=0), w_ada, b_ada)
    ada_p = ada[:, :bp].reshape(DEPTH, bp, 1, N_SUB * 3 * d)
    ada_s = ada[:, bp:].reshape(DEPTH, 1, bs, N_SUB * 3 * d)

    tm_p = dict(ffn=min(t, 1024), mixin=min(t, 512), merge=min(t, 512), hgrn=min(t, 512),
                mlpre=min(t, 512))
    y_p, sp = _run(x_prompt, ada_p, p, w, tm_p, None)

    past = dict(k=jnp.transpose(cache_sb_k, (0, 1, 3, 4, 2)), v=jnp.transpose(cache_sb_v, (0, 1, 3, 4, 2)),
                page_table=page_table, hgrn=state_hgrn, mc=state_mlstm_c, mn=state_mlstm_n,
                mm=state_mlstm_m, mconv=state_mlstm_conv)
    tm_s = dict(ffn=bs, mixin=bs, merge=bs)
    y_s, ss = _run(x_sample.reshape(1, bs, d), ada_s, p, w, tm_s, past)
    y_s = y_s.reshape(bs, 1, d)
    k_p = jnp.transpose(sp["k"], (0, 1, 4, 2, 3))
    v_p = jnp.transpose(sp["v"], (0, 1, 4, 2, 3))
    return (y_p, y_s, k_p, v_p, ss["k"], ss["v"], sp["hgrn"], ss["hgrn"],
            sp["mc"], ss["mc"], sp["mn"], ss["mn"], sp["mm"], ss["mm"], sp["mconv"], ss["mconv"])
```

```python
import functools

import jax
import jax.numpy as jnp
from jax import lax
from jax.experimental import pallas as pl
from jax.experimental.pallas import tpu as pltpu

F32 = jnp.float32
BF16 = jnp.bfloat16

D_MODEL = 1024
DEPTH = 2
PAGE_SIZE = 128
HG_HEADS = 4
HG_DK = 128
HG_WIDTH = 512
HG_CHUNK = 64
HG_SUB = 16
HG_SAFE_DECAY = 80.0
SB_HEADS = 8
SB_HEAD_DIM = 64
SB_WIDTH = 512
SB_TQ = 512
SB_TK = 256
SB_ROWS = 128
SB_HEAD_GROUP = 8
ML_HEADS = 4
ML_HEAD_DIM = 128
ML_WIDTH = 512
ML_CHUNK = 128
ML_CONV = 4
D_FF = 2816
FF_CHUNK = 256
FFN_RES = 0.5
N_SUB = 3
NORM_EPS = 1e-6
IN_COLS = 7680
COL = 512
PF_SRC = (1, 5, 6, 7)
PB_SRC = (0, 2, 3, 4, 8, 9, 10, 11, 12, 13, 14)
PF_COLS = len(PF_SRC) * COL
PB_COLS = len(PB_SRC) * COL
PF_Z, PF_K, PF_V, PF_MX = 0, 1, 2, 3
PB_Q, PB_I, PB_HG, PB_SQ, PB_MO, PB_GATES = 0, 1, 2, 3, 4, 5
V7X_VMEM_LIMIT = 56 * 1024 * 1024
NEG_INF = float("-inf")
LOG2E = 1.4426950408889634


def _dot(a, b):
    return jnp.dot(a, b, preferred_element_type=F32)


def _dot_nt(a, b):
    return lax.dot_general(a, b, (((1,), (1,)), ((), ())), preferred_element_type=F32)


def _split3(x):
    x1 = x.astype(BF16)
    r1 = x - x1.astype(F32)
    x2 = r1.astype(BF16)
    x3 = (r1 - x2.astype(F32)).astype(BF16)
    return x1, x2, x3


def _tri_dot(tri, x):
    x1, x2, x3 = _split3(x)
    return _dot(tri, x1) + _dot(tri, x2) + _dot(tri, x3)


def _dot_tri(x, tri):
    x1, x2, x3 = _split3(x)
    return _dot(x1, tri) + _dot(x2, tri) + _dot(x3, tri)


def _sigmoid(x):
    return 0.5 * jnp.tanh(0.5 * x) + 0.5


def _silu(x):
    return x * _sigmoid(x)


def _log_sigmoid(x):
    return jnp.minimum(x, 0.0) - jnp.log(1.0 + jnp.exp(-jnp.abs(x)))


def _logaddexp(a, b):
    amax = jnp.maximum(a, b)
    delta = a - b
    return jnp.where(delta != delta, a + b, amax + jnp.log(1.0 + jnp.exp(-jnp.abs(delta))))


def _rms(x, g):
    return x * lax.rsqrt(jnp.mean(x * x, axis=-1, keepdims=True) + NORM_EPS) * g


def _head_norm(o, gain, heads, width):
    parts = []
    for h in range(heads):
        oh = o[:, h * width:(h + 1) * width]
        parts.append(oh * lax.rsqrt(jnp.mean(oh * oh, axis=-1, keepdims=True) + NORM_EPS))
    return jnp.concatenate(parts, axis=-1) * gain


def _to_col(row):
    n = row.shape[1]
    eye = lax.broadcasted_iota(jnp.int32, (n, n), 0) == lax.broadcasted_iota(jnp.int32, (n, n), 1)
    return jnp.sum(jnp.where(eye, row, 0.0), axis=1, keepdims=True)


def _to_row(col):
    n = col.shape[0]
    eye = lax.broadcasted_iota(jnp.int32, (n, n), 0) == lax.broadcasted_iota(jnp.int32, (n, n), 1)
    return jnp.sum(jnp.where(eye, col, 0.0), axis=0, keepdims=True)


def _params(sem, vmem=None):
    return pltpu.CompilerParams(dimension_semantics=sem, vmem_limit_bytes=vmem)


def _ada_kernel(c_ref, w_ref, b_ref, o_ref):
    s = _silu(c_ref[...]).astype(BF16)
    o_ref[...] = _dot(s, w_ref[...].astype(BF16)) + b_ref[...]


def _ada(c_all, w_ada, b_ada):
    n, d = c_all.shape
    cols = w_ada.shape[-1]
    tn = 1536
    return pl.pallas_call(
        _ada_kernel,
        grid=(DEPTH, cols // tn),
        in_specs=[pl.BlockSpec((n, d), lambda l, j: (0, 0)),
                  pl.BlockSpec((None, d, tn), lambda l, j: (l, 0, j)),
                  pl.BlockSpec((None, 1, tn), lambda l, j: (l, 0, j))],
        out_specs=pl.BlockSpec((None, n, tn), lambda l, j: (l, 0, j)),
        out_shape=jax.ShapeDtypeStruct((DEPTH, n, cols), F32),
        compiler_params=_params(("parallel", "parallel"), V7X_VMEM_LIMIT),
    )(c_all, w_ada, b_ada.reshape(DEPTH, 1, cols))


def _mod_spec(ada, tm, col):
    rm = ada.shape[1]
    if rm == 1:
        return pl.BlockSpec((None, 1, D_MODEL), lambda g, i: (g, 0, col))
    return pl.BlockSpec((None, tm, D_MODEL), lambda g, i: (g, i, col))


def _ffn_kernel(x_ref, sh_ref, sc_ref, gt_ref, gpre_ref, gpost_ref, win_ref, wd_ref, o_ref, h_ref, a_ref):
    x = x_ref[...]
    h_ref[...] = (_rms(x, gpre_ref[...]) * (1.0 + sc_ref[...]) + sh_ref[...]).astype(BF16)
    for f in range(D_FF // FF_CHUNK):
        cols = slice(f * FF_CHUNK, (f + 1) * FF_CHUNK)
        h = h_ref[...]
        g = _dot(h, win_ref[:, cols])
        u = _dot(h, win_ref[:, D_FF + f * FF_CHUNK:D_FF + (f + 1) * FF_CHUNK])
        a_ref[:, cols] = (_silu(g) * u).astype(BF16)
    y = _dot(a_ref[...], wd_ref[...])
    o_ref[...] = x + FFN_RES * (1.0 + gt_ref[...]) * _rms(y, gpost_ref[...])


def _ffn(x, ada, j, gpre, gpost, w_in, w_down, l, s, tm):
    g, r, d = x.shape
    xspec = pl.BlockSpec((None, tm, d), lambda a, i: (a, i, 0))
    vec = pl.BlockSpec((1, d), lambda a, i: (0, 0))
    res = lambda shape: pl.BlockSpec((None, None) + shape, lambda a, i: (l, s, 0, 0),
                                     pipeline_mode=pl.Buffered(1))
    return pl.pallas_call(
        _ffn_kernel,
        grid=(g, r // tm),
        in_specs=[xspec, _mod_spec(ada, tm, 3 * j), _mod_spec(ada, tm, 3 * j + 1),
                  _mod_spec(ada, tm, 3 * j + 2), vec, vec, res((d, 2 * D_FF)), res((D_FF, d))],
        out_specs=xspec,
        out_shape=jax.ShapeDtypeStruct(x.shape, F32),
        scratch_shapes=[pltpu.VMEM((tm, d), BF16), pltpu.VMEM((tm, D_FF), BF16)],
        compiler_params=_params(("parallel", "parallel"), V7X_VMEM_LIMIT),
    )(x, ada, ada, ada, gpre.reshape(1, d), gpost.reshape(1, d), w_in, w_down)


def _mixin_kernel(x_ref, sh_ref, sc_ref, gpre_ref, w_ref, pf_ref, pb_ref, h_ref):
    h_ref[...] = (_rms(x_ref[...], gpre_ref[...]) * (1.0 + sc_ref[...]) + sh_ref[...]).astype(BF16)
    nf = PF_COLS // COL
    for c, src in enumerate(PF_SRC + PB_SRC):
        y = _dot(h_ref[...], w_ref[:, src * COL:(src + 1) * COL])
        if c < nf:
            pf_ref[:, c * COL:(c + 1) * COL] = y
        else:
            pb_ref[:, (c - nf) * COL:(c - nf + 1) * COL] = y.astype(BF16)


def _mixin(x, ada, gpre, w_in, l, tm):
    g, r, d = x.shape
    return pl.pallas_call(
        _mixin_kernel,
        grid=(g, r // tm),
        in_specs=[pl.BlockSpec((None, tm, d), lambda a, i: (a, i, 0)),
                  _mod_spec(ada, tm, 3), _mod_spec(ada, tm, 4),
                  pl.BlockSpec((1, d), lambda a, i: (0, 0)),
                  pl.BlockSpec((None, d, IN_COLS), lambda a, i: (l, 0, 0), pipeline_mode=pl.Buffered(1))],
        out_specs=[pl.BlockSpec((None, tm, PF_COLS), lambda a, i: (a, i, 0)),
                   pl.BlockSpec((None, tm, PB_COLS), lambda a, i: (a, i, 0))],
        out_shape=[jax.ShapeDtypeStruct((g, r, PF_COLS), F32), jax.ShapeDtypeStruct((g, r, PB_COLS), BF16)],
        scratch_shapes=[pltpu.VMEM((tm, d), BF16)],
        compiler_params=_params(("parallel", "parallel"), V7X_VMEM_LIMIT),
    )(x, ada, ada, gpre.reshape(1, d), w_in)


def _merge_kernel(x_ref, gt_ref, gpost_ref, oa_ref, hg_ref, ob_ref, oc_ref, mo_ref,
                  g0, g1, g2, g3, g4, g5, hgain_ref, mgain_ref,
                  whg_ref, wsb_ref, wml_ref, wout_ref, o_ref):
    f32 = lambda ref: ref[...].astype(F32)
    o_a = (_head_norm(oa_ref[...], hgain_ref[...], HG_HEADS, HG_DK) * _silu(f32(hg_ref))).astype(BF16)
    o_b = ob_ref[...]
    o_c = (_head_norm(f32(oc_ref), mgain_ref[...], ML_HEADS, ML_HEAD_DIM) * _sigmoid(f32(mo_ref))).astype(BF16)
    gates = ((g0, g2, g4), (g1, g3, g5))
    y = None
    for c in range(2):
        cs = slice(c * COL, (c + 1) * COL)
        ga, gb, gc = gates[c]
        m = (_sigmoid(f32(ga)) * _dot(o_a, whg_ref[:, cs])
             + _sigmoid(f32(gb)) * _dot(o_b, wsb_ref[:, cs])
             + _sigmoid(f32(gc)) * _dot(o_c, wml_ref[:, cs]))
        part = _dot(m.astype(BF16), wout_ref[cs, :])
        y = part if y is None else y + part
    o_ref[...] = x_ref[...] + (1.0 + gt_ref[...]) * _rms(y, gpost_ref[...])


def _merge(x, ada, gpost, pb, o_a, o_b, o_c, hgain, mgain, whg, wsb, wml, wout, tm):
    g, r, d = x.shape
    row = lambda col: pl.BlockSpec((None, tm, COL), lambda a, i: (a, i, col))
    vec = lambda n: pl.BlockSpec((1, n), lambda a, i: (0, 0))
    wsp = lambda shape: pl.BlockSpec(shape, lambda a, i: (0, 0))
    xspec = pl.BlockSpec((None, tm, d), lambda a, i: (a, i, 0))
    return pl.pallas_call(
        _merge_kernel,
        grid=(g, r // tm),
        in_specs=[xspec, _mod_spec(ada, tm, 5), vec(d),
                  row(0), row(PB_HG), row(0), row(0), row(PB_MO)]
                 + [row(PB_GATES + c) for c in range(6)]
                 + [vec(HG_WIDTH), vec(ML_WIDTH),
                    wsp((HG_WIDTH, d)), wsp((SB_WIDTH, d)), wsp((ML_WIDTH, d)), wsp((d, d))],
        out_specs=xspec,
        out_shape=jax.ShapeDtypeStruct(x.shape, F32),
        compiler_params=_params(("parallel", "parallel"), V7X_VMEM_LIMIT),
    )(x, ada, gpost.reshape(1, d), o_a, pb, o_b, o_c, pb,
      pb, pb, pb, pb, pb, pb,
      hgain.reshape(1, HG_WIDTH), mgain.reshape(1, ML_WIDTH), whg, wsb, wml, wout)


def _hgrn_lb(lbl, layer):
    e = jnp.exp(lbl - jnp.max(lbl, axis=0, keepdims=True))
    p = e / jnp.sum(e, axis=0, keepdims=True)
    lb = jnp.zeros_like(p[0:1])
    for r in range(1, layer + 1):
        lb = lb + p[r:r + 1]
    return lb


def _hgrn_gates(z, lb):
    logf = _logaddexp(jnp.log(lb), jnp.log1p(-lb) + _log_sigmoid(z))
    k = (1.0 - lb) * _sigmoid(-z)
    return logf, k


def _hgrn_prompt_kernel(lbl_ref, qb_ref, z_ref, ib_ref, o_ref, s_ref, st_ref, b_ref, k_ref, q_ref, i_ref,
                        *, layer):
    tc = qb_ref.shape[0]
    q_ref[...] = qb_ref[...].astype(F32)
    i_ref[...] = ib_ref[...].astype(F32)
    nchunk = tc // HG_CHUNK
    nsub = HG_CHUNK // HG_SUB

    @pl.when(pl.program_id(1) == 0)
    def _():
        st_ref[...] = jnp.zeros_like(st_ref)

    lb = _hgrn_lb(lbl_ref[...], layer)[0]
    logf, k = _hgrn_gates(z_ref[...], lb)
    k_ref[...] = k
    ri = lax.broadcasted_iota(jnp.int32, (tc, tc), 0)
    ci = lax.broadcasted_iota(jnp.int32, (tc, tc), 1)
    tri = jnp.where((ci <= ri) & (ri // HG_CHUNK == ci // HG_CHUNK), 1.0, 0.0).astype(BF16)
    b_ref[...] = _tri_dot(tri, logf)
    block_decay = -jnp.sum(logf.reshape(tc // HG_SUB, HG_SUB, HG_WIDTH), axis=1)
    safe = jnp.max(block_decay) < HG_SAFE_DECAY

    rows64 = lax.broadcasted_iota(jnp.int32, (HG_CHUNK, 1), 0)
    rows16 = lax.broadcasted_iota(jnp.int32, (HG_SUB, 1), 0)
    r64 = lax.broadcasted_iota(jnp.int32, (HG_CHUNK, HG_CHUNK), 0)
    c64 = lax.broadcasted_iota(jnp.int32, (HG_CHUNK, HG_CHUNK), 1)
    causal64 = c64 <= r64

    heads = range(HG_HEADS)
    sl = [slice(h * HG_DK, (h + 1) * HG_DK) for h in heads]

    def score_operands(q, b, kk, cs, s_i, include_diag):
        i0 = s_i * HG_SUB
        r = b[i0 - 1:i0, cs] if s_i else jnp.zeros((1, HG_DK), F32)
        hi = i0 + HG_SUB if include_diag else i0
        cap = HG_SAFE_DECAY if include_diag else 0.0
        qt = q[i0:i0 + HG_SUB, cs] * jnp.exp(b[i0:i0 + HG_SUB, cs] - r)
        kt = jnp.where(rows64 < hi, kk[:, cs] * jnp.exp(jnp.minimum(r - b[:, cs], cap)), 0.0)
        return qt.astype(BF16), kt.astype(BF16)

    def chunk(r0):
        rows = pl.ds(r0, HG_CHUNK)
        q = q_ref[rows, :]
        i = i_ref[rows, :]
        b = b_ref[rows, :]
        kk = k_ref[rows, :]
        blast = b[HG_CHUNK - 1:HG_CHUNK, :]
        eb = jnp.exp(b)
        kd = kk * jnp.exp(blast - b)
        ib = [i[:, cs].astype(BF16) for cs in sl]
        ops = [[score_operands(q, b, kk, cs, s_i, True) for s_i in range(nsub)] for cs in sl]
        p = [jnp.concatenate([_dot_nt(qt, kt) for qt, kt in ops[h]], axis=0) for h in heads]
        st = [st_ref[h] for h in heads]
        o_state = [_dot_nt((q[:, sl[h]] * eb[:, sl[h]]).astype(BF16), st[h].astype(BF16)) for h in heads]
        pm = [jnp.where(causal64 & safe, p[h], 0.0).astype(BF16) for h in heads]
        o_intra = [_dot(pm[h], ib[h]) for h in heads]
        upd = [_dot(i[:, sl[h]].T.astype(BF16), kd[:, sl[h]].astype(BF16)) for h in heads]
        o_ref[rows, :] = jnp.concatenate([o_state[h] + o_intra[h] for h in heads], axis=-1)
        for h in heads:
            st_ref[h] = st[h] * jnp.exp(blast[:, sl[h]]) + upd[h]

        @pl.when(jnp.logical_not(safe))
        def _():
            outs = []
            for h in heads:
                rows_p = [jnp.zeros((HG_SUB, HG_CHUNK), F32)]
                for s_i in range(1, nsub):
                    qt, kt = score_operands(q, b, kk, sl[h], s_i, False)
                    rows_p.append(_dot_nt(qt, kt))
                outs.append(_dot(jnp.concatenate(rows_p, axis=0).astype(BF16), ib[h]))
            o_ref[rows, :] += jnp.concatenate(outs, axis=-1)
            for s_i in range(nsub):
                base = r0 + s_i * HG_SUB
                srows = pl.ds(base, HG_SUB)
                q_i = q_ref[srows, :]
                b_i = b_ref[srows, :]

                def s_body(s, acc):
                    b_s = b_ref[pl.ds(base + s, 1), :]
                    k_s = k_ref[pl.ds(base + s, 1), :]
                    i_s = i_ref[pl.ds(base + s, 1), :]
                    e = jnp.where(rows16 >= s, jnp.exp(jnp.minimum(b_i - b_s, 0.0)), 0.0)
                    pr = q_i * e * k_s
                    parts = [jnp.sum(pr[:, cs], axis=1, keepdims=True) * i_s[:, cs] for cs in sl]
                    return acc + jnp.concatenate(parts, axis=-1)

                acc = lax.fori_loop(0, HG_SUB, s_body, jnp.zeros((HG_SUB, HG_WIDTH), F32))
                o_ref[srows, :] += acc

    def pair_body(n, carry):
        r0 = pl.multiple_of(n * (2 * HG_CHUNK), 2 * HG_CHUNK)
        chunk(r0)
        chunk(r0 + HG_CHUNK)
        return carry

    lax.fori_loop(0, nchunk // 2, pair_body, 0)

    @pl.when(pl.program_id(1) == pl.num_programs(1) - 1)
    def _():
        for h in range(HG_HEADS):
            s_ref[h] = st_ref[h].T


def _hgrn_prompt(pf, pb, lbl, layer, tc):
    b, t, _ = pf.shape
    row = lambda col: pl.BlockSpec((None, tc, COL), lambda a, i: (a, i, col))
    return pl.pallas_call(
        functools.partial(_hgrn_prompt_kernel, layer=layer),
        grid=(b, t // tc),
        in_specs=[pl.BlockSpec((DEPTH, 1, HG_WIDTH), lambda a, i: (0, 0, 0)), row(PB_Q), row(PF_Z), row(PB_I)],
        out_specs=[pl.BlockSpec((None, tc, HG_WIDTH), lambda a, i: (a, i, 0)),
                   pl.BlockSpec((None, HG_HEADS, HG_DK, HG_DK), lambda a, i: (a, 0, 0, 0))],
        out_shape=[jax.ShapeDtypeStruct((b, t, HG_WIDTH), F32),
                   jax.ShapeDtypeStruct((b, HG_HEADS, HG_DK, HG_DK), F32)],
        scratch_shapes=[pltpu.VMEM((HG_HEADS, HG_DK, HG_DK), F32)] + [pltpu.VMEM((tc, HG_WIDTH), F32)] * 4,
        compiler_params=_params(("parallel", "arbitrary"), V7X_VMEM_LIMIT),
    )(lbl.reshape(DEPTH, 1, HG_WIDTH), pb, pf, pb)


def _hgrn_step_kernel(lbl_ref, qr_ref, zr_ref, ir_ref, s_ref, o_ref, so_ref, *, layer):
    lb = _hgrn_lb(lbl_ref[...], layer)[0]
    for n in range(qr_ref.shape[0]):
        outs = []
        for h in range(HG_HEADS):
            cs = slice(h * HG_DK, (h + 1) * HG_DK)
            logf, k = _hgrn_gates(_to_col(zr_ref[n, :, cs]), lb[h])
            s_new = jnp.exp(logf) * s_ref[n, h] + k * ir_ref[n, :, cs]
            so_ref[n, h] = s_new
            outs.append(jnp.sum(_to_col(qr_ref[n, :, cs]) * s_new, axis=0, keepdims=True))
        o_ref[n] = jnp.concatenate(outs, axis=-1)


STEP_SEQS = 4


def _hgrn_step(q, z, i, state, lbl, layer):
    b = q.shape[0]
    ns = STEP_SEQS if b % STEP_SEQS == 0 else 1
    row = pl.BlockSpec((ns, 1, COL), lambda a: (a, 0, 0))
    return pl.pallas_call(
        functools.partial(_hgrn_step_kernel, layer=layer),
        grid=(b // ns,),
        in_specs=[pl.BlockSpec((DEPTH, HG_HEADS, HG_DK, 1), lambda a: (0, 0, 0, 0)), row, row, row,
                  pl.BlockSpec((ns, None, HG_HEADS, HG_DK, HG_DK), lambda a: (a, layer, 0, 0, 0))],
        out_specs=[pl.BlockSpec((ns, 1, HG_WIDTH), lambda a: (a, 0, 0)),
                   pl.BlockSpec((ns, HG_HEADS, HG_DK, HG_DK), lambda a: (a, 0, 0, 0))],
        out_shape=[jax.ShapeDtypeStruct((b, 1, HG_WIDTH), F32),
                   jax.ShapeDtypeStruct((b, HG_HEADS, HG_DK, HG_DK), F32)],
        compiler_params=_params(("parallel",)),
    )(lbl.reshape(DEPTH, HG_HEADS, HG_DK, 1), q.reshape(b, 1, COL), z.reshape(b, 1, COL), i.reshape(b, 1, COL), state)


def _softplus(z):
    return jnp.maximum(z, 0.0) + jnp.log(1.0 + jnp.exp(-jnp.abs(z)))


def _sb_prompt_kernel(bias_ref, q_ref, k_ref, v_ref, o_ref, kt_out, vt_out,
                      qh_ref, kt_ref, vh_ref, acc_ref, c_ref):
    tq, tk = SB_TQ, SB_TK
    nk = k_ref.shape[0] // tk
    qi = pl.program_id(1)

    @pl.when(qi == 0)
    def _():
        for n in range(nk):
            ks = slice(n * tk, (n + 1) * tk)
            kt_full = k_ref[ks, :].T
            kt_out[:, ks] = kt_full
            vt_out[:, ks] = v_ref[ks, :].T
            for h in range(SB_HEADS):
                cs = slice(h * SB_HEAD_DIM, (h + 1) * SB_HEAD_DIM)
                kt_ref[h, n] = kt_full[cs, :].astype(BF16)
                vh_ref[h, n] = v_ref[ks, cs].astype(BF16)

    nrb = tq // SB_ROWS
    for h in range(SB_HEADS):
        cs = slice(h * SB_HEAD_DIM, (h + 1) * SB_HEAD_DIM)
        for r in range(nrb):
            rows = slice(r * SB_ROWS, (r + 1) * SB_ROWS)
            qh_ref[h, r] = (q_ref[:, cs].astype(F32)[rows] * (SB_HEAD_DIM ** -0.5 * LOG2E)).astype(BF16)
    acc_ref[...] = jnp.zeros_like(acc_ref)
    c_ref[...] = jnp.zeros_like(c_ref)

    ri = lax.broadcasted_iota(jnp.int32, (tk, tk), 0)
    ci = lax.broadcasted_iota(jnp.int32, (tk, tk), 1)
    upper = jnp.where(ri > ci, 1.0, 0.0).astype(BF16)
    rpt = tk // SB_ROWS
    base = qi * (tq // tk)

    def tile(j, r, diag):
        nkeys = (r % rpt + 1) * SB_ROWS if diag else tk
        mask = None
        if diag:
            mask = (lax.broadcasted_iota(jnp.int32, (SB_ROWS, nkeys), 1)
                    < lax.broadcasted_iota(jnp.int32, (SB_ROWS, nkeys), 0) + (r % rpt) * SB_ROWS)
        for g0 in range(0, SB_HEADS, SB_HEAD_GROUP):
            heads = range(g0, g0 + SB_HEAD_GROUP)
            zs = [_dot(qh_ref[h, r], kt_ref[h, j, :, :nkeys]) + bias_ref[h] * LOG2E for h in heads]
            lbs, l1ms = [], []
            for z in zs:
                sp = jnp.maximum(z, 0.0) + jnp.log(1.0 + jnp.exp2(-jnp.abs(z))) * LOG2E
                l1m = -sp
                if mask is not None:
                    l1m = jnp.where(mask, l1m, 0.0)
                lbs.append(z - sp)
                l1ms.append(l1m)
            sufs = [_dot(l1m.astype(BF16), upper[:nkeys, :nkeys]) for l1m in l1ms]
            weights = []
            for h, lb, l1m, suf in zip(heads, lbs, l1ms, sufs):
                c = c_ref[h * nrb + r][:, 0:1]
                a = jnp.exp2(lb + suf + c)
                if mask is not None:
                    a = jnp.where(mask, a, 0.0)
                weights.append(a.astype(BF16))
                c_ref[h * nrb + r] = jnp.broadcast_to(c + suf[:, 0:1] + l1m[:, 0:1], (SB_ROWS, 128))
            for h, a in zip(heads, weights):
                acc_ref[h, r] += _dot(a, vh_ref[h, j, :nkeys, :])

    for r in range(nrb):
        tile(base + r // rpt, r, True)
        for kk in range(r // rpt - 1, -1, -1):
            tile(base + kk, r, False)

    def body(step, carry):
        for r in range(nrb):
            tile(base - 1 - step, r, False)
        return carry

    lax.fori_loop(0, base, body, 0)
    o_ref[...] = jnp.concatenate(
        [jnp.concatenate([acc_ref[h, r] for h in range(SB_HEADS)], axis=-1) for r in range(nrb)],
        axis=0).astype(BF16)


def _sb_prompt(pf, pb, bias):
    b, t, _ = pf.shape
    tq = min(SB_TQ, t)
    nk = t // SB_TK
    seq = lambda col: pl.BlockSpec((None, t, COL), lambda a, i: (a, 0, col))
    return pl.pallas_call(
        _sb_prompt_kernel,
        grid=(b, t // tq),
        in_specs=[pl.BlockSpec(memory_space=pltpu.SMEM),
                  pl.BlockSpec((None, tq, COL), lambda a, i: (a, i, PB_SQ)), seq(PF_K), seq(PF_V)],
        out_specs=[pl.BlockSpec((None, tq, SB_WIDTH), lambda a, i: (a, i, 0)),
                   pl.BlockSpec((None, SB_WIDTH, t), lambda a, i: (a, 0, 0)),
                   pl.BlockSpec((None, SB_WIDTH, t), lambda a, i: (a, 0, 0))],
        out_shape=[jax.ShapeDtypeStruct((b, t, SB_WIDTH), BF16),
                   jax.ShapeDtypeStruct((b, SB_WIDTH, t), F32),
                   jax.ShapeDtypeStruct((b, SB_WIDTH, t), F32)],
        scratch_shapes=[pltpu.VMEM((SB_HEADS, tq // SB_ROWS, SB_ROWS, SB_HEAD_DIM), BF16),
                        pltpu.VMEM((SB_HEADS, nk, SB_HEAD_DIM, SB_TK), BF16),
                        pltpu.VMEM((SB_HEADS, nk, SB_TK, SB_HEAD_DIM), BF16),
                        pltpu.VMEM((SB_HEADS, tq // SB_ROWS, SB_ROWS, SB_HEAD_DIM), F32),
                        pltpu.VMEM((SB_HEADS * (tq // SB_ROWS), SB_ROWS, 128), F32)],
        compiler_params=_params(("parallel", "arbitrary"), V7X_VMEM_LIMIT),
    )(bias, pb, pf, pf)


SB_PAGES_PER_STEP = 16


def _sb_decode_kernel(pt_ref, q_ref, bias_ref, *refs):
    npg = SB_PAGES_PER_STEP
    k_refs = refs[:npg]
    v_refs = refs[npg:2 * npg]
    o_ref, qb_ref, c_ref, acc_ref = refs[2 * npg:]
    g = pl.program_id(1)

    @pl.when(g == 0)
    def _():
        for h in range(SB_HEADS):
            qcol = _to_col(q_ref[:, h * SB_HEAD_DIM:(h + 1) * SB_HEAD_DIM])
            qb_ref[h] = jnp.broadcast_to(qcol, qb_ref.shape[1:])
        c_ref[...] = jnp.zeros_like(c_ref)
        acc_ref[...] = jnp.zeros_like(acc_ref)

    ri = lax.broadcasted_iota(jnp.int32, (PAGE_SIZE, PAGE_SIZE), 0)
    ci = lax.broadcasted_iota(jnp.int32, (PAGE_SIZE, PAGE_SIZE), 1)
    upper = jnp.where(ri > ci, 1.0, 0.0).astype(BF16)
    bias = bias_ref[...]
    c = c_ref[:, 0:1]
    weights = [None] * npg
    for p in range(npg - 1, -1, -1):
        zrows = []
        for h in range(SB_HEADS):
            part = k_refs[p][h, 0:8, :] * qb_ref[h, 0:8, :]
            for r in range(8, SB_HEAD_DIM, 8):
                part = part + k_refs[p][h, r:r + 8, :] * qb_ref[h, r:r + 8, :]
            zrows.append(jnp.sum(part, axis=0, keepdims=True))
        z = jnp.concatenate(zrows, axis=0) + bias
        sp = _softplus(z)
        l1m = -sp
        suf = _dot(l1m.astype(BF16), upper)
        weights[p] = jnp.exp(z - sp + suf + c)
        c = c + suf[:, 0:1] + l1m[:, 0:1]
    c_ref[...] = jnp.broadcast_to(c, c_ref.shape)
    for h in range(SB_HEADS):
        part = weights[0][h:h + 1, :] * v_refs[0][h]
        for p in range(1, npg):
            part = part + weights[p][h:h + 1, :] * v_refs[p][h]
        acc_ref[h] += part

    @pl.when(g == pl.num_programs(1) - 1)
    def _():
        o_ref[...] = jnp.concatenate(
            [_to_row(jnp.sum(acc_ref[h], axis=1, keepdims=True)) for h in range(SB_HEADS)], axis=-1)


def _sb_decode(sq, bias, cache_kt, cache_vt, page_table, layer):
    b = sq.shape[0]
    n_pages = page_table.shape[1]
    npg = SB_PAGES_PER_STEP
    ng = n_pages // npg
    qrow = (sq * (SB_HEAD_DIM ** -0.5)).reshape(b, 1, SB_WIDTH)

    def page_spec(p):
        return pl.BlockSpec((None, None, SB_HEADS, SB_HEAD_DIM, PAGE_SIZE),
                            lambda a, g, pt: (pt[a, (ng - 1 - g) * npg + p], layer, 0, 0, 0))

    hd = pl.BlockSpec((None, 1, SB_WIDTH), lambda a, g, pt: (a, 0, 0))
    grid_spec = pltpu.PrefetchScalarGridSpec(
        num_scalar_prefetch=1,
        grid=(b, ng),
        in_specs=[hd, pl.BlockSpec((SB_HEADS, 1), lambda a, g, pt: (0, 0))]
                 + [page_spec(p) for p in range(npg)] + [page_spec(p) for p in range(npg)],
        out_specs=hd,
        scratch_shapes=[pltpu.VMEM((SB_HEADS, SB_HEAD_DIM, PAGE_SIZE), F32),
                        pltpu.VMEM((SB_HEADS, 128), F32),
                        pltpu.VMEM((SB_HEADS, SB_HEAD_DIM, PAGE_SIZE), F32)],
    )
    return pl.pallas_call(
        _sb_decode_kernel,
        grid_spec=grid_spec,
        out_shape=jax.ShapeDtypeStruct((b, 1, SB_WIDTH), F32),
        compiler_params=_params(("parallel", "arbitrary"), V7X_VMEM_LIMIT),
    )(page_table, qrow, bias.reshape(SB_HEADS, 1), *([cache_kt] * npg), *([cache_vt] * npg))


def _ml_pre_kernel(x0, x1, x2, x3, *refs):
    _ml_pre_body(x0[...], x1[...], x2[...], x3[...], *refs)


def _ml_pre_seq_kernel(prev_ref, x_ref, cw_ref, cb_ref, wq_ref, wk_ref, wv_ref, wif_ref, bif_ref,
                       q_ref, k_ref, v_ref, g_ref, xs_ref):
    tm = x_ref.shape[0]
    halo = prev_ref.shape[0]
    x = x_ref[...]
    xs_ref[halo:, :] = x
    xs_ref[:halo, :] = jnp.where(pl.program_id(1) == 0, 0.0, prev_ref[...])
    taps = [xs_ref[pl.ds(halo - (ML_CONV - 1) + j, tm), :] for j in range(ML_CONV - 1)]
    _ml_pre_body(*taps, x, cw_ref, cb_ref, wq_ref, wk_ref, wv_ref, wif_ref, bif_ref,
                 q_ref, k_ref, v_ref, g_ref)


def _ml_pre_body(x0, x1, x2, x, cw_ref, cb_ref, wq_ref, wk_ref, wv_ref, wif_ref, bif_ref,
                 q_ref, k_ref, v_ref, g_ref):
    cw = cw_ref[...]
    xc = _silu(x0 * cw[0:1] + x1 * cw[1:2] + x2 * cw[2:3] + x * cw[3:4] + cb_ref[...])
    xcb = xc.astype(BF16)
    xb = x.astype(BF16)
    qs, ks, vs = [], [], []
    for h in range(ML_HEADS):
        cs = slice(h * ML_HEAD_DIM, (h + 1) * ML_HEAD_DIM)
        qs.append(_dot(xcb[:, cs], wq_ref[h]))
        ks.append(_dot(xcb[:, cs], wk_ref[h]))
        vs.append(_dot(xb[:, cs], wv_ref[h]))
    q = jnp.concatenate(qs, axis=-1)
    k = jnp.concatenate(ks, axis=-1)
    v = jnp.concatenate(vs, axis=-1)
    q_ref[...] = q
    k_ref[...] = k
    v_ref[...] = v
    g_ref[...] = (_dot(q.astype(BF16), wif_ref[0:ML_WIDTH, :])
                  + _dot(k.astype(BF16), wif_ref[ML_WIDTH:2 * ML_WIDTH, :])
                  + _dot(v.astype(BF16), wif_ref[2 * ML_WIDTH:3 * ML_WIDTH, :]) + bif_ref[...])


def _ml_pre(xs, cw, cb, wq, wk, wv, wif, bif, tm):
    g, r, w = xs[0].shape
    row = pl.BlockSpec((None, tm, w), lambda a, i: (a, i, 0))
    full = lambda shape: pl.BlockSpec(shape, lambda a, i: (0,) * len(shape))
    out = jax.ShapeDtypeStruct((g, r, w), F32)
    return pl.pallas_call(
        _ml_pre_kernel,
        grid=(g, r // tm),
        in_specs=[row, row, row, row, full((ML_CONV, w)), full((1, w)),
                  full(wq.shape), full(wk.shape), full(wv.shape), full(wif.shape), full((1, 128))],
        out_specs=[row, row, row, pl.BlockSpec((None, tm, 128), lambda a, i: (a, i, 0))],
        out_shape=[out, out, out, jax.ShapeDtypeStruct((g, r, 128), F32)],
        compiler_params=_params(("parallel", "parallel"), V7X_VMEM_LIMIT),
    )(*xs, cw, cb.reshape(1, w), wq, wk, wv, wif, bif)


ML_HALO = 8


def _ml_pre_seq(pf, cw, cb, wq, wk, wv, wif, bif, tm):
    b, t, _ = pf.shape
    w = ML_WIDTH
    per = tm // ML_HALO
    row = pl.BlockSpec((None, tm, w), lambda a, i: (a, i, 0))
    full = lambda shape: pl.BlockSpec(shape, lambda a, i: (0,) * len(shape))
    out = jax.ShapeDtypeStruct((b, t, w), F32)
    return pl.pallas_call(
        _ml_pre_seq_kernel,
        grid=(b, t // tm),
        in_specs=[pl.BlockSpec((None, ML_HALO, w), lambda a, i: (a, jnp.maximum(i * per - 1, 0), PF_MX)),
                  pl.BlockSpec((None, tm, w), lambda a, i: (a, i, PF_MX)),
                  full((ML_CONV, w)), full((1, w)),
                  full(wq.shape), full(wk.shape), full(wv.shape), full(wif.shape), full((1, 128))],
        out_specs=[row, row, row, pl.BlockSpec((None, tm, 128), lambda a, i: (a, i, 0))],
        out_shape=[out, out, out, jax.ShapeDtypeStruct((b, t, 128), F32)],
        scratch_shapes=[pltpu.VMEM((tm + ML_HALO, w), F32)],
        compiler_params=_params(("parallel", "parallel"), V7X_VMEM_LIMIT),
    )(pf, pf, cw, cb.reshape(1, w), wq, wk, wv, wif, bif)


def _ml_prompt_kernel(q_ref, k_ref, v_ref, g_ref, h_ref, c_out, n_out, m_out, c_ref, n_ref, m_ref):
    L = ML_CHUNK
    nch = q_ref.shape[0] // L

    @pl.when(pl.program_id(1) == 0)
    def _():
        c_ref[...] = jnp.zeros_like(c_ref)
        n_ref[...] = jnp.zeros_like(n_ref)
        m_ref[...] = jnp.zeros_like(m_ref)

    ri = lax.broadcasted_iota(jnp.int32, (L, L), 0)
    ci = lax.broadcasted_iota(jnp.int32, (L, L), 1)
    causal = ci <= ri
    tril = jnp.where(causal, 1.0, 0.0).astype(BF16)
    triu = jnp.where(ri <= ci, 1.0, 0.0).astype(BF16)
    heads = range(ML_HEADS)
    sl = [slice(h * ML_HEAD_DIM, (h + 1) * ML_HEAD_DIM) for h in heads]

    ones = jnp.ones((L, ML_HEAD_DIM), BF16)
    pre = []
    for j in range(nch):
        rows = slice(j * L, (j + 1) * L)
        g = g_ref[rows, :]
        gt = g.T
        b_cols = _tri_dot(tril, _log_sigmoid(g))
        b_rows = _dot_tri(_log_sigmoid(gt), triu)
        q = [q_ref[rows, cs] for cs in sl]
        ks = [k_ref[rows, cs] * (ML_HEAD_DIM ** -0.5) for cs in sl]
        qb = [x.astype(BF16) for x in q]
        ksb = [x.astype(BF16) for x in ks]
        vb = [v_ref[rows, cs].astype(BF16) for cs in sl]
        vt = [v_ref[rows, cs].T for cs in sl]
        qk = [_dot_nt(qb[h], ksb[h]) for h in heads]
        b_bc = [jnp.broadcast_to(b_cols[:, ML_HEADS + h:ML_HEADS + h + 1], (L, L)) for h in heads]
        u_row = [gt[h:h + 1, :] - b_rows[ML_HEADS + h:ML_HEADS + h + 1, :] for h in heads]
        u_max = [jnp.broadcast_to(jnp.max(jnp.where(causal, u_row[h], NEG_INF), axis=1, keepdims=True), (L, L))
                 for h in heads]
        pre.append(dict(qb=qb, ksb=ksb, vb=vb, vt=vt, qk=qk, b_bc=b_bc, u_row=u_row, u_max=u_max))

    c0 = [c_ref[h] for h in heads]
    n0 = [n_ref[h:h + 1, :] for h in heads]
    m0 = [m_ref[h:h + 1, 0:1] for h in heads]
    for j, pj in enumerate(pre):
        rows = slice(j * L, (j + 1) * L)
        qc = [_dot_nt(pj["qb"][h], c0[h].astype(BF16)) for h in heads]
        qn = [_dot_nt(pj["qb"][h], jnp.broadcast_to(n0[h], (L, ML_HEAD_DIM)).astype(BF16)) for h in heads]
        y, inter, w = [], [], []
        for h in heads:
            y_h = jnp.maximum(pj["u_max"][h], m0[h])
            y.append(y_h)
            inter.append(jnp.exp(m0[h] - y_h))
            w.append(pj["qk"][h] * jnp.where(causal, jnp.exp(pj["u_row"][h] - y_h), 0.0))
        wb = [w[h].astype(BF16) for h in heads]
        wv = [_dot(wb[h], pj["vb"][h]) for h in heads]
        wsum = [_dot(wb[h], ones) for h in heads]
        outs, dec, w_end, m_new = [], [], [], []
        for h in heads:
            m_t = pj["b_bc"][h] + y[h]
            numer = inter[h] * qc[h] + wv[h]
            denom = inter[h] * qn[h] + wsum[h]
            outs.append(numer / jnp.maximum(jnp.abs(denom), jnp.exp(-m_t)))
            m_h = m_t[L - 1:L, 0:1]
            b_last = pj["b_bc"][h][L - 1:L, 0:1]
            w_end.append(jnp.exp(pj["u_row"][h] + b_last - m_h))
            dec.append(jnp.exp(b_last + m0[h] - m_h))
            m_new.append(m_h)
        c_upd = [_dot((pj["vt"][h] * w_end[h]).astype(BF16), pj["ksb"][h]) for h in heads]
        n_upd = [_dot(w_end[h].astype(BF16), pj["ksb"][h]) for h in heads]
        c0 = [dec[h] * c0[h] + c_upd[h] for h in heads]
        n0 = [dec[h] * n0[h] + n_upd[h] for h in heads]
        m0 = m_new
        h_ref[rows, :] = jnp.concatenate(outs, axis=-1).astype(BF16)
    for h in heads:
        c_ref[h] = c0[h]
        n_ref[h:h + 1, :] = n0[h]
        m_ref[h:h + 1, :] = jnp.broadcast_to(m0[h], (1, m_ref.shape[1]))

    @pl.when(pl.program_id(1) == pl.num_programs(1) - 1)
    def _():
        c_out[...] = c_ref[...]
        n_out[...] = n_ref[...]
        m_out[...] = m_ref[...]


ML_CHUNKS_PER_STEP = 8


def _ml_prompt(q, k, v, gates):
    b, t, w = q.shape
    tr = min(t, ML_CHUNKS_PER_STEP * ML_CHUNK)
    row = pl.BlockSpec((None, tr, w), lambda a, i: (a, i, 0))
    return pl.pallas_call(
        _ml_prompt_kernel,
        grid=(b, t // tr),
        in_specs=[row, row, row, pl.BlockSpec((None, tr, 128), lambda a, i: (a, i, 0))],
        out_specs=[row,
                   pl.BlockSpec((None, ML_HEADS, ML_HEAD_DIM, ML_HEAD_DIM), lambda a, i: (a, 0, 0, 0)),
                   pl.BlockSpec((None, ML_HEADS, ML_HEAD_DIM), lambda a, i: (a, 0, 0)),
                   pl.BlockSpec((None, ML_HEADS, 128), lambda a, i: (a, 0, 0))],
        out_shape=[jax.ShapeDtypeStruct((b, t, w), BF16),
                   jax.ShapeDtypeStruct((b, ML_HEADS, ML_HEAD_DIM, ML_HEAD_DIM), F32),
                   jax.ShapeDtypeStruct((b, ML_HEADS, ML_HEAD_DIM), F32),
                   jax.ShapeDtypeStruct((b, ML_HEADS, 128), F32)],
        scratch_shapes=[pltpu.VMEM((ML_HEADS, ML_HEAD_DIM, ML_HEAD_DIM), F32),
                        pltpu.VMEM((ML_HEADS, ML_HEAD_DIM), F32), pltpu.VMEM((ML_HEADS, 128), F32)],
        compiler_params=_params(("parallel", "arbitrary"), V7X_VMEM_LIMIT),
    )(q, k, v, gates)


def _ml_step_kernel(q_ref, k_ref, v_ref, g_ref, c_ref, n_ref, m_ref, h_ref, c_out, n_out, m_out):
    lane = lax.broadcasted_iota(jnp.int32, (1, 128), 1)
    for s in range(q_ref.shape[0]):
        g = g_ref[s]
        m_row = jnp.zeros((1, 128), F32)
        outs = []
        for h in range(ML_HEADS):
            cs = slice(h * ML_HEAD_DIM, (h + 1) * ML_HEAD_DIM)
            q = q_ref[s, :, cs]
            ks = k_ref[s, :, cs] * (ML_HEAD_DIM ** -0.5)
            v = _to_col(v_ref[s, :, cs])
            it = g[:, h:h + 1]
            logf = _log_sigmoid(g[:, ML_HEADS + h:ML_HEADS + h + 1])
            m0 = m_ref[s, :, h:h + 1]
            c0 = c_ref[s, h]
            n0 = n_ref[s, h:h + 1, :]
            m_t = jnp.maximum(logf + m0, it)
            dm = jnp.exp(it - m_t)
            inter = jnp.exp(logf + m0 - m_t)
            w = jnp.sum(q * ks, axis=1, keepdims=True) * dm
            numer = inter * jnp.sum(c0 * q, axis=1, keepdims=True) + w * v
            denom = inter * jnp.sum(q * n0, axis=1, keepdims=True) + w
            outs.append(_to_row(numer / jnp.maximum(jnp.abs(denom), jnp.exp(-m_t))))
            w_end = jnp.exp(it - m_t)
            dec = jnp.exp(logf + m0 - m_t)
            c_out[s, h] = dec * c0 + (w_end * v) * ks
            n_out[s, h:h + 1, :] = dec * n0 + w_end * ks
            m_row = jnp.where(lane == h, m_t, m_row)
        h_ref[s] = jnp.concatenate(outs, axis=-1)
        m_out[s] = m_row


def _ml_step(q, k, v, gates, state_c, state_n, state_m, layer):
    b = q.shape[0]
    m0 = state_m[:, layer].reshape(b, 1, ML_HEADS)
    ns = STEP_SEQS if b % STEP_SEQS == 0 else 1
    row = pl.BlockSpec((ns, 1, ML_WIDTH), lambda a: (a, 0, 0))
    cspec = pl.BlockSpec((ns, ML_HEADS, ML_HEAD_DIM, ML_HEAD_DIM), lambda a: (a, 0, 0, 0))
    nspec = pl.BlockSpec((ns, ML_HEADS, ML_HEAD_DIM), lambda a: (a, 0, 0))
    return pl.pallas_call(
        _ml_step_kernel,
        grid=(b // ns,),
        in_specs=[row, row, row, pl.BlockSpec((ns, 1, 128), lambda a: (a, 0, 0)),
                  pl.BlockSpec((ns, None, ML_HEADS, ML_HEAD_DIM, ML_HEAD_DIM),
                               lambda a: (a, layer, 0, 0, 0)),
                  pl.BlockSpec((ns, None, ML_HEADS, ML_HEAD_DIM), lambda a: (a, layer, 0, 0)),
                  pl.BlockSpec((ns, 1, ML_HEADS), lambda a: (a, 0, 0))],
        out_specs=[row, cspec, nspec, pl.BlockSpec((ns, 1, 128), lambda a: (a, 0, 0))],
        out_shape=[jax.ShapeDtypeStruct((b, 1, ML_WIDTH), F32),
                   jax.ShapeDtypeStruct((b, ML_HEADS, ML_HEAD_DIM, ML_HEAD_DIM), F32),
                   jax.ShapeDtypeStruct((b, ML_HEADS, ML_HEAD_DIM), F32),
                   jax.ShapeDtypeStruct((b, 1, 128), F32)],
        compiler_params=_params(("parallel",)),
    )(q, k, v, gates, state_c, state_n, m0)


def _prep_weights(p):
    bf = lambda a: a.astype(BF16)
    w = {}
    for name in ("w_ffn_in", "w_ffn_down", "w_in", "w_hg_proj", "w_sb_proj", "w_ml_proj", "w_out", "ml_wq", "ml_wk", "ml_wv"):
        w[name] = bf(p[name])
    w["wif"] = bf(jnp.pad(p["ml_w_if"], ((0, 0), (0, 0), (0, 128 - 2 * ML_HEADS))))
    w["bif"] = jnp.pad(p["ml_b_if"], ((0, 0), (0, 128 - 2 * ML_HEADS))).reshape(DEPTH, 1, 128)
    return w


def _layer(x, ada, l, p, w, tm, past):
    ffn = lambda xx, j, s: _ffn(xx, ada, j, p["g_pre"][l, j], p["g_post"][l, j],
                                w["w_ffn_in"], w["w_ffn_down"], l, s, tm["ffn"])
    x = ffn(x, 0, 0)
    pf, pb = _mixin(x, ada, p["g_pre"][l, 1], w["w_in"], l, tm["mixin"])
    st = {}
    blk = lambda arr, c: arr[..., c * COL:(c + 1) * COL]
    if past is None:
        b, t, _ = x.shape
        o_a, st["hgrn"] = _hgrn_prompt(pf, pb, p["hg_lb_logits"], l, tm["hgrn"])
        o_b, kt, vt = _sb_prompt(pf, pb, p["sb_bias"][l])
        q, k, v, gates = _ml_pre_seq(pf, p["ml_conv_w"][l], p["ml_conv_b"][l], w["ml_wq"][l],
                                     w["ml_wk"][l], w["ml_wv"][l], w["wif"][l], w["bif"][l], tm["mlpre"])
        o_c, st["mc"], st["mn"], m_pad = _ml_prompt(q, k, v, gates)
        st["mm"] = m_pad[:, :, 0]
        st["mconv"] = blk(pf, PF_MX)[:, t - (ML_CONV - 1):]
        st["k"] = kt.reshape(b, SB_HEADS, SB_HEAD_DIM, t)
        st["v"] = vt.reshape(b, SB_HEADS, SB_HEAD_DIM, t)
    else:
        b = x.shape[1]
        rows = lambda arr, c: blk(arr, c).reshape(b, COL).astype(F32)
        mx = blk(pf, PF_MX)
        o_a, st["hgrn"] = _hgrn_step(rows(pb, PB_Q), rows(pf, PF_Z), rows(pb, PB_I), past["hgrn"],
                                     p["hg_lb_logits"], l)
        o_b = _sb_decode(rows(pb, PB_SQ), p["sb_bias"][l], past["k"], past["v"], past["page_table"], l)
        buf = past["mconv"][:, l]
        taps = [buf[:, j].reshape(1, b, ML_WIDTH) for j in range(ML_CONV - 1)]
        q, k, v, gates = _ml_pre(taps + [mx], p["ml_conv_w"][l], p["ml_conv_b"][l], w["ml_wq"][l],
                                 w["ml_wk"][l], w["ml_wv"][l], w["wif"][l], w["bif"][l], b)
        hc, st["mc"], st["mn"], m_pad = _ml_step(q.reshape(b, 1, ML_WIDTH), k.reshape(b, 1, ML_WIDTH),
                                                 v.reshape(b, 1, ML_WIDTH), gates.reshape(b, 1, 128),
                                                 past["mc"], past["mn"], past["mm"], l)
        o_a = o_a.reshape(1, b, HG_WIDTH)
        o_b = o_b.reshape(1, b, SB_WIDTH).astype(BF16)
        o_c = hc.reshape(1, b, ML_WIDTH).astype(BF16)
        st["mm"] = m_pad[:, 0, :ML_HEADS]
        st["mconv"] = jnp.concatenate([buf[:, 1:], mx.reshape(b, 1, ML_WIDTH)], axis=1)
        st["k"] = blk(pf, PF_K).reshape(b, 1, SB_HEADS, SB_HEAD_DIM)
        st["v"] = blk(pf, PF_V).reshape(b, 1, SB_HEADS, SB_HEAD_DIM)
    x = _merge(x, ada, p["g_post"][l, 1], pb, o_a, o_b, o_c, p["hg_gain"][l], p["ml_gain"][l],
               w["w_hg_proj"][l], w["w_sb_proj"][l], w["w_ml_proj"][l], w["w_out"][l], tm["merge"])
    x = ffn(x, 2, 1)
    return x, st


def _run(x, ada_all, p, w, tm, past):
    outs = []
    for l in range(DEPTH):
        x, st = _layer(x, ada_all[l], l, p, w, tm, past)
        outs.append(st)
    return x, {name: jnp.stack([o[name] for o in outs], axis=1) for name in outs[0]}


def kernel(x_prompt, x_sample, cache_sb_k, cache_sb_v, state_hgrn, state_mlstm_c, state_mlstm_n,
           state_mlstm_m, state_mlstm_conv, page_table, c_prompt, c_sample, w_ada, b_ada, g_pre, g_post,
           w_ffn_in, w_ffn_down, w_in, sb_bias, hg_lb_logits, hg_gain, ml_conv_w, ml_conv_b, ml_wq, ml_wk,
           ml_wv, ml_w_if, ml_b_if, ml_gain, w_hg_proj, w_sb_proj, w_ml_proj, w_out):
    p = dict(g_pre=g_pre, g_post=g_post, w_ffn_in=w_ffn_in, w_ffn_down=w_ffn_down, w_in=w_in,
             sb_bias=sb_bias, hg_lb_logits=hg_lb_logits, hg_gain=hg_gain, ml_conv_w=ml_conv_w,
             ml_conv_b=ml_conv_b, ml_wq=ml_wq, ml_wk=ml_wk, ml_wv=ml_wv, ml_w_if=ml_w_if,
             ml_b_if=ml_b_if, ml_gain=ml_gain, w_hg_proj=w_hg_proj, w_sb_proj=w_sb_proj,
             w_ml_proj=w_ml_proj, w_out=w_out)
    w = _prep_weights(p)
    bp, t, d = x_prompt.shape
    bs = x_sample.shape[0]
    ada = _ada(jnp.concatenate([c_prompt, c_sample], axis=0), w_ada, b_ada)
    ada_p = ada[:, :bp].reshape(DEPTH, bp, 1, N_SUB * 3 * d)
    ada_s = ada[:, bp:].reshape(DEPTH, 1, bs, N_SUB * 3 * d)

    tm_p = dict(ffn=min(t, 1024), mixin=min(t, 512), merge=min(t, 512), hgrn=min(t, 512),
                mlpre=min(t, 512))
    y_p, sp = _run(x_prompt, ada_p, p, w, tm_p, None)

    past = dict(k=jnp.transpose(cache_sb_k, (0, 1, 3, 4, 2)), v=jnp.transpose(cache_sb_v, (0, 1, 3, 4, 2)),
                page_table=page_table, hgrn=state_hgrn, mc=state_mlstm_c, mn=state_mlstm_n,
                mm=state_mlstm_m, mconv=state_mlstm_conv)
    tm_s = dict(ffn=bs, mixin=bs, merge=bs)
    y_s, ss = _run(x_sample.reshape(1, bs, d), ada_s, p, w, tm_s, past)
    y_s = y_s.reshape(bs, 1, d)
    k_p = jnp.transpose(sp["k"], (0, 1, 4, 2, 3))
    v_p = jnp.transpose(sp["v"], (0, 1, 4, 2, 3))
    return (y_p, y_s, k_p, v_p, ss["k"], ss["v"], sp["hgrn"], ss["hgrn"],
            sp["mc"], ss["mc"], sp["mn"], ss["mn"], sp["mm"], ss["mm"], sp["mconv"], ss["mconv"])
```
